```python
import math
import jax, jax.numpy as jnp
from jax import lax
import numpy as np

D_MODEL = 2048
BATCH = 16
SEQ = 2048
DEPTH = 1
DEC_BATCH = 32
DEC_SEQ = 8
PAST_LEN = 16384
PAGE_SIZE = 128

MIX_WIDTH = D_MODEL
RET_HEADS = 4
RET_DK = MIX_WIDTH // (4 * RET_HEADS)
RET_DV = 2 * RET_DK
RET_CHUNK = 128
NSA_HEADS = 8
NSA_KV_HEADS = 2
NSA_HD = MIX_WIDTH // (2 * NSA_HEADS)
NSA_GROUP = NSA_HEADS // NSA_KV_HEADS
CMP_BLOCK = 32
CMP_STRIDE = 16
CMP_HIDDEN = 2 * NSA_HD
SLC_BLOCK = 64
N_SELECT = 16
WINDOW = 512
NSA_QBLOCK = 32
N_BRANCH = 3
D_FF = ((8 * D_MODEL // 3 + 255) // 256) * 256
ROPE_THETA = 10000.0
LN_EPS = 1e-5
ALPHA = (2.0 * DEPTH) ** 0.25
BETA = (8.0 * DEPTH) ** -0.25

RET_Q = RET_HEADS * RET_DK
RET_V = RET_HEADS * RET_DV
NSA_Q = NSA_HEADS * NSA_HD
NSA_KV = NSA_KV_HEADS * NSA_HD

kernel_name = 'hymba_retnet_nsa_macaron_decoder'

F32 = jnp.float32


def split_sizes():
    return [RET_Q, RET_Q, RET_V, RET_V, NSA_Q, 6 * NSA_KV, N_BRANCH * NSA_HEADS]


def layer_norm(x, g, b):
    xf = x.astype(F32)
    mu = xf.mean(-1, keepdims=True)
    var = jnp.square(xf - mu).mean(-1, keepdims=True)
    return ((xf - mu) * lax.rsqrt(var + LN_EPS) * g + b).astype(x.dtype)


def swiglu(x, w_up, w_down):
    a, b = jnp.split(x @ w_up, 2, axis=-1)
    return (jax.nn.silu(a) * b) @ w_down


def rope(x, pos):
    half = x.shape[-1] // 2
    inv = ROPE_THETA ** (-jnp.arange(half, dtype=F32) / half)
    ang = pos.astype(F32)[:, None] * inv[None, :]
    cos = jnp.cos(ang)[None, :, None, :]
    sin = jnp.sin(ang)[None, :, None, :]
    xf = x.astype(F32)
    x1, x2 = xf[..., :half], xf[..., half:]
    return jnp.concatenate([x1 * cos - x2 * sin, x1 * sin + x2 * cos], -1).astype(x.dtype)


def masked_softmax(s, mask):
    s = jnp.where(mask, s, -jnp.inf)
    m = jnp.max(s, -1, keepdims=True)
    m = jnp.where(jnp.isfinite(m), m, 0.0)
    e = jnp.exp(s - m)
    d = e.sum(-1, keepdims=True)
    return e / jnp.where(d > 0, d, 1.0)


def retention(q, k, v, s0):
    B, L, H, DK = q.shape
    DV = v.shape[-1]
    C = RET_CHUNK if L % RET_CHUNK == 0 else L
    nc = L // C
    lg = jnp.log1p(-jnp.exp2(-5.0 - jnp.arange(H, dtype=F32)))
    i = jnp.arange(C, dtype=F32)
    diff = i[:, None] - i[None, :]
    dmask = jnp.where(diff >= 0, jnp.exp(lg[:, None, None] * jnp.maximum(diff, 0.0)), 0.0)
    in_decay = jnp.exp(lg[:, None] * (i + 1.0))
    st_decay = jnp.exp(lg[:, None] * (C - 1.0 - i))
    chunk_decay = jnp.exp(lg * C)

    def to_chunks(t):
        return t.astype(F32).reshape(B, nc, C, H, -1).transpose(1, 0, 3, 2, 4)

    def step(S, inp):
        qc, kc, vc = inp
        a = jnp.einsum('bhid,bhjd->bhij', qc, kc) * dmask
        o = (jnp.einsum('bhij,bhje->bhie', a, vc)
             + jnp.einsum('bhid,bhde->bhie', qc * in_decay[:, :, None], S))
        S = S * chunk_decay[:, None, None] + jnp.einsum('bhjd,bhje->bhde', kc * st_decay[:, :, None], vc)
        return S, o

    S, o = lax.scan(step, s0.astype(F32), (to_chunks(q), to_chunks(k), to_chunks(v)))
    return o.transpose(1, 0, 3, 2, 4).reshape(B, L, H, DV), S


def compress(x, pos_emb, w1, w2):
    B, Lk = x.shape[:2]
    R = CMP_BLOCK // CMP_STRIDE
    n_chunks = Lk // CMP_STRIDE
    n_cmp = n_chunks - R + 1
    chunks = x[:, :n_chunks * CMP_STRIDE].astype(F32).reshape(B, n_chunks, CMP_STRIDE, NSA_KV_HEADS, NSA_HD)
    w1f = w1.astype(F32)
    w1r = w1f.reshape(R, CMP_STRIDE, NSA_HD, CMP_HIDDEN)
    h = jnp.einsum('pd,pdf->f', pos_emb.astype(F32), w1f)
    for r in range(R):
        h = h + jnp.einsum('bnskd,sdf->bnkf', chunks, w1r[r])[:, r:r + n_cmp]
    return jnp.einsum('bnkf,fd->bnkd', jax.nn.gelu(h), w2.astype(F32))


def nsa(q, k_cmp, v_cmp, k_slc, v_slc, k_win_src, v_win_src, gates, q_off, p):
    B, Lq, H, HD = q.shape
    Lk = k_slc.shape[1]
    scale = HD ** -0.5
    kc = compress(k_cmp, p['cmp_pos_k'], p['cmp_w1_k'], p['cmp_w2_k'])
    vc = compress(v_cmp, p['cmp_pos_v'], p['cmp_w1_v'], p['cmp_w2_v'])
    n_cmp = kc.shape[1]
    cmp_end = jnp.arange(n_cmp) * CMP_STRIDE + CMP_BLOCK - 1
    n_slc = -(-Lk // SLC_BLOCK)
    pad = n_slc * SLC_BLOCK - Lk

    def blocks(t):
        t = jnp.pad(t, ((0, 0), (0, pad), (0, 0), (0, 0)))
        return t.reshape(B, n_slc, SLC_BLOCK, NSA_KV_HEADS, HD).transpose(0, 3, 1, 2, 4)

    kb, vb = blocks(k_slc), blocks(v_slc)
    c_i = jnp.arange(n_cmp)[:, None]
    n_i = jnp.arange(n_slc)[None, :]
    cover = jnp.clip(jnp.minimum(c_i * CMP_STRIDE + CMP_BLOCK, (n_i + 1) * SLC_BLOCK)
                     - jnp.maximum(c_i * CMP_STRIDE, n_i * SLC_BLOCK), 0, None).astype(F32) / CMP_BLOCK
    n_sel = min(N_SELECT, n_slc)
    QB = NSA_QBLOCK if Lq % NSA_QBLOCK == 0 else Lq
    WL = WINDOW + QB - 1
    bi = jnp.arange(B)[:, None, None, None]
    hi = jnp.arange(NSA_KV_HEADS)[None, None, :, None]
    blk = jnp.arange(n_slc)

    def one_block(qs):
        qq = lax.dynamic_slice_in_dim(q, qs, QB, 1).astype(F32).reshape(B, QB, NSA_KV_HEADS, NSA_GROUP, HD)
        t = q_off + qs + jnp.arange(QB)
        s = jnp.einsum('bqkgd,bckd->bqkgc', qq, kc) * scale
        p_cmp = masked_softmax(s, (cmp_end[None, :] <= t[:, None])[None, :, None, None, :])
        o_cmp = jnp.einsum('bqkgc,bckd->bqkgd', p_cmp, vc)
        imp = jnp.einsum('bqkgc,cn->bqkn', p_cmp, cover)
        valid = blk[None, :] * SLC_BLOCK <= t[:, None]
        cur = t[:, None] // SLC_BLOCK
        forced = (blk[None, :] == 0) | (blk[None, :] == cur) | (blk[None, :] == cur - 1)
        prio = jnp.where(forced[None, :, None, :], jnp.inf,
                         jnp.where(valid[None, :, None, :], imp, -jnp.inf))
        _, idx = lax.top_k(prio, n_sel)
        sel_ok = jnp.take_along_axis(jnp.broadcast_to(valid[None, :, None, :], prio.shape), idx, -1)
        ks = kb[bi, hi, idx].astype(F32)
        vs = vb[bi, hi, idx].astype(F32)
        pos = idx[..., None] * SLC_BLOCK + jnp.arange(SLC_BLOCK)
        msel = (sel_ok[..., None] & (pos <= t[None, :, None, None, None])).reshape(B, QB, NSA_KV_HEADS, 1, n_sel * SLC_BLOCK)
        s = jnp.einsum('bqkgd,bqknrd->bqkgnr', qq, ks).reshape(B, QB, NSA_KV_HEADS, NSA_GROUP, n_sel * SLC_BLOCK) * scale
        p_slc = masked_softmax(s, msel)
        o_slc = jnp.einsum('bqkgm,bqkmd->bqkgd', p_slc, vs.reshape(B, QB, NSA_KV_HEADS, n_sel * SLC_BLOCK, HD))
        kw = lax.dynamic_slice_in_dim(k_win_src, qs + 1, WL, 1).astype(F32)
        vw = lax.dynamic_slice_in_dim(v_win_src, qs + 1, WL, 1).astype(F32)
        sp = q_off - WINDOW + qs + 1 + jnp.arange(WL)
        dlt = t[:, None] - sp[None, :]
        mw = ((sp[None, :] >= 0) & (dlt >= 0) & (dlt < WINDOW))[None, :, None, None, :]
        s = jnp.einsum('bqkgd,bskd->bqkgs', qq, kw) * scale
        o_win = jnp.einsum('bqkgs,bskd->bqkgd', masked_softmax(s, mw), vw)
        g = lax.dynamic_slice_in_dim(gates, qs, QB, 1).reshape(B, QB, NSA_KV_HEADS, NSA_GROUP, N_BRANCH)
        o = g[..., 0:1] * o_cmp + g[..., 1:2] * o_slc + g[..., 2:3] * o_win
        return o.reshape(B, QB, H, HD)

    out = lax.map(one_block, jnp.arange(Lq // QB) * QB)
    return out.transpose(1, 0, 2, 3, 4).reshape(B, Lq, H, HD).astype(q.dtype)


def mixer(h, q_off, ret_s0, past_kv, win_past, p):
    B, L, _ = h.shape
    cuts = [int(c) for c in np.cumsum(split_sizes())[:-1]]
    rq, rk, rv, rg, nq, nkv, ng = jnp.split(h @ p['w_in'], cuts, axis=-1)
    pos = q_off + jnp.arange(L, dtype=jnp.int32)
    rq = rope(rq.reshape(B, L, RET_HEADS, RET_DK), pos)
    rk = rope(rk.reshape(B, L, RET_HEADS, RET_DK), pos) * (RET_DK ** -0.5)
    rv = rv.reshape(B, L, RET_HEADS, RET_DV)
    ro, ret_s = retention(rq, rk, rv, ret_s0)
    mu = ro.mean(-1, keepdims=True)
    var = jnp.square(ro - mu).mean(-1, keepdims=True)
    ro = ((ro - mu) * lax.rsqrt(var + LN_EPS)).reshape(B, L, RET_V) * p['ret_gn_g'] + p['ret_gn_b']
    ro = jax.nn.silu(rg.astype(F32)) * ro
    nq = rope(nq.reshape(B, L, NSA_HEADS, NSA_HD), pos)
    nkv = nkv.reshape(B, L, 6, NSA_KV_HEADS, NSA_HD)
    kv_rows = jnp.stack([rope(nkv[:, :, 0], pos), nkv[:, :, 1], rope(nkv[:, :, 2], pos), nkv[:, :, 3]], axis=2)
    win_rows = jnp.stack([rope(nkv[:, :, 4], pos), nkv[:, :, 5]], axis=2)
    full = kv_rows if past_kv is None else jnp.concatenate([past_kv, kv_rows], axis=1)
    wsrc = jnp.concatenate([win_past, win_rows], axis=1)
    gates = jax.nn.sigmoid(ng.astype(F32)).reshape(B, L, NSA_HEADS, N_BRANCH)
    no = nsa(nq, full[:, :, 0], full[:, :, 1], full[:, :, 2], full[:, :, 3],
             wsrc[:, :, 0], wsrc[:, :, 1], gates, q_off, p)
    mixed = jnp.concatenate([ro.astype(h.dtype), no.reshape(B, L, NSA_Q)], axis=-1)
    return mixed @ p['w_out'], ret_s, kv_rows, win_rows


def decoder_layer(x, q_off, ret_s0, past_kv, win_past, p):
    x = layer_norm(ALPHA * x + 0.5 * swiglu(x, p['ffn1_w_up'], p['ffn1_w_down']), p['ln1_g'], p['ln1_b'])
    m, ret_s, kv_rows, win_rows = mixer(x, q_off, ret_s0, past_kv, win_past, p)
    x = layer_norm(ALPHA * x + m, p['ln2_g'], p['ln2_b'])
    x = layer_norm(ALPHA * x + 0.5 * swiglu(x, p['ffn2_w_up'], p['ffn2_w_down']), p['ln3_g'], p['ln3_b'])
    return x, ret_s, kv_rows, win_rows


def setup_inputs(seed: int = 0) -> dict:
    key = jax.random.key(seed)
    ks = iter(jax.random.split(key, 40))

    def nrm(shape, scale):
        return jax.random.normal(next(ks), shape, F32) * scale

    n_pages = PAST_LEN // PAGE_SIZE
    n_pool = (5 * DEC_BATCH * n_pages + 3) // 4
    w_buf = min(WINDOW, PAST_LEN)
    n_in = sum(split_sizes())
    page_table = jax.random.permutation(next(ks), n_pool)[:DEC_BATCH * n_pages].reshape(DEC_BATCH, n_pages).astype(jnp.int32)
    return {
        'x_prompt': nrm((BATCH, SEQ, D_MODEL), 1.0),
        'x_sample': nrm((DEC_BATCH, DEC_SEQ, D_MODEL), 1.0),
        'state_ret': nrm((DEPTH, DEC_BATCH, RET_HEADS, RET_DK, RET_DV), 0.5),
        'cache_nsa_kv': nrm((DEPTH, n_pool, PAGE_SIZE, 4, NSA_KV_HEADS, NSA_HD), 1.0),
        'cache_win': nrm((DEPTH, DEC_BATCH, w_buf, 2, NSA_KV_HEADS, NSA_HD), 1.0),
        'page_table': page_table,
        'ffn1_w_up': nrm((DEPTH, D_MODEL, 2 * D_FF), D_MODEL ** -0.5),
        'ffn1_w_down': nrm((DEPTH, D_FF, D_MODEL), BETA * D_FF ** -0.5),
        'ln1_g': 1.0 + nrm((DEPTH, D_MODEL), 0.02),
        'ln1_b': nrm((DEPTH, D_MODEL), 0.02),
        'w_in': nrm((DEPTH, D_MODEL, n_in), D_MODEL ** -0.5),
        'w_out': nrm((DEPTH, MIX_WIDTH, D_MODEL), BETA * MIX_WIDTH ** -0.5),
        'ret_gn_g': 1.0 + nrm((DEPTH, RET_V), 0.02),
        'ret_gn_b': nrm((DEPTH, RET_V), 0.02),
        'cmp_pos_k': nrm((DEPTH, CMP_BLOCK, NSA_HD), 0.1),
        'cmp_w1_k': nrm((DEPTH, CMP_BLOCK, NSA_HD, CMP_HIDDEN), (CMP_BLOCK * NSA_HD) ** -0.5),
        'cmp_w2_k': nrm((DEPTH, CMP_HIDDEN, NSA_HD), CMP_HIDDEN ** -0.5),
        'cmp_pos_v': nrm((DEPTH, CMP_BLOCK, NSA_HD), 0.1),
        'cmp_w1_v': nrm((DEPTH, CMP_BLOCK, NSA_HD, CMP_HIDDEN), (CMP_BLOCK * NSA_HD) ** -0.5),
        'cmp_w2_v': nrm((DEPTH, CMP_HIDDEN, NSA_HD), CMP_HIDDEN ** -0.5),
        'ln2_g': 1.0 + nrm((DEPTH, D_MODEL), 0.02),
        'ln2_b': nrm((DEPTH, D_MODEL), 0.02),
        'ffn2_w_up': nrm((DEPTH, D_MODEL, 2 * D_FF), D_MODEL ** -0.5),
        'ffn2_w_down': nrm((DEPTH, D_FF, D_MODEL), BETA * D_FF ** -0.5),
        'ln3_g': 1.0 + nrm((DEPTH, D_MODEL), 0.02),
        'ln3_b': nrm((DEPTH, D_MODEL), 0.02),
    }


def reference(x_prompt, x_sample, state_ret, cache_nsa_kv, cache_win, page_table,
              ffn1_w_up, ffn1_w_down, ln1_g, ln1_b, w_in, w_out, ret_gn_g, ret_gn_b,
              cmp_pos_k, cmp_w1_k, cmp_w2_k, cmp_pos_v, cmp_w1_v, cmp_w2_v,
              ln2_g, ln2_b, ffn2_w_up, ffn2_w_down, ln3_g, ln3_b):
    B, L_p, _ = x_prompt.shape
    DB = x_sample.shape[0]
    n_pages = page_table.shape[1]
    past_len = n_pages * cache_nsa_kv.shape[2]
    w_buf = cache_win.shape[2]
    yp, ys = x_prompt, x_sample
    rs_p_l, rs_s_l, kv_p_l, kv_s_l, win_p_l, win_s_l = [], [], [], [], [], []
    for l in range(DEPTH):
        p = {
            'ffn1_w_up': ffn1_w_up[l], 'ffn1_w_down': ffn1_w_down[l], 'ln1_g': ln1_g[l], 'ln1_b': ln1_b[l],
            'w_in': w_in[l], 'w_out': w_out[l], 'ret_gn_g': ret_gn_g[l], 'ret_gn_b': ret_gn_b[l],
            'cmp_pos_k': cmp_pos_k[l], 'cmp_w1_k': cmp_w1_k[l], 'cmp_w2_k': cmp_w2_k[l],
            'cmp_pos_v': cmp_pos_v[l], 'cmp_w1_v': cmp_w1_v[l], 'cmp_w2_v': cmp_w2_v[l],
            'ln2_g': ln2_g[l], 'ln2_b': ln2_b[l], 'ffn2_w_up': ffn2_w_up[l], 'ffn2_w_down': ffn2_w_down[l],
            'ln3_g': ln3_g[l], 'ln3_b': ln3_b[l],
        }
        s0 = jnp.zeros((B, RET_HEADS, RET_DK, RET_DV), F32)
        win0 = jnp.zeros((B, WINDOW, 2, NSA_KV_HEADS, NSA_HD), x_prompt.dtype)
        yp, rs_p, kv_p, win_p = decoder_layer(yp, 0, s0, None, win0, p)
        past = cache_nsa_kv[l][page_table].reshape(DB, past_len, 4, NSA_KV_HEADS, NSA_HD)
        win_past = jnp.pad(cache_win[l], ((0, 0), (WINDOW - w_buf, 0), (0, 0), (0, 0), (0, 0)))
        ys, rs_s, kv_s, win_s = decoder_layer(ys, past_len, state_ret[l], past, win_past, p)
        rs_p_l.append(rs_p)
        rs_s_l.append(rs_s)
        kv_p_l.append(kv_p)
        kv_s_l.append(kv_s)
        win_p_l.append(win_p[:, L_p - min(WINDOW, L_p):])
        win_s_l.append(jnp.concatenate([cache_win[l], win_s], axis=1)[:, -w_buf:])
    ret_state_prompt = jnp.stack(rs_p_l)
    ret_state_sample = jnp.stack(rs_s_l)
    nsa_kv_prompt = jnp.stack(kv_p_l)
    nsa_kv_sample = jnp.stack(kv_s_l)
    win_kv_prompt = jnp.stack(win_p_l)
    win_kv_sample = jnp.stack(win_s_l)
    return (yp, ys, ret_state_prompt, ret_state_sample, nsa_kv_prompt, nsa_kv_sample, win_kv_prompt, win_kv_sample)
```

```python
import functools

import numpy as np
import jax
import jax.numpy as jnp
from jax import lax
from jax.experimental import pallas as pl
from jax.experimental.pallas import tpu as pltpu

F32 = jnp.float32
BF16 = jnp.bfloat16

LANES = 128
SUBLANES = 8
VMEM_LIMIT_BYTES = 56 * 1024 * 1024

RET_HEADS = 4
RET_DK = 128
RET_DV = 256
RET_CHUNK = 128
NSA_HEADS = 8
NSA_KV_HEADS = 2
NSA_HD = 128
NSA_GROUP = NSA_HEADS // NSA_KV_HEADS
CMP_BLOCK = 32
CMP_STRIDE = 16
CMP_HIDDEN = 2 * NSA_HD
SLC_BLOCK = 64
SLC_SHIFT = 6
N_SELECT = 16
WINDOW = 512
N_BRANCH = 3
ROPE_THETA = 10000.0
LN_EPS = 1e-5
NEG_BIG = -1e30

N_IN = 5656
N_IN_PAD = 5760
ROPE_CHUNKS = tuple(range(0, 8)) + tuple(range(24, 32)) + (32, 33, 36, 37, 40, 41)
KSCALE_CHUNKS = tuple(range(4, 8))


def _cparams(sem):
    return pltpu.CompilerParams(dimension_semantics=sem, vmem_limit_bytes=VMEM_LIMIT_BYTES)


def _pick_tile(n, candidates):
    for c in candidates:
        if n % c == 0:
            return c
    return n


def _round_up(n, m):
    return (n + m - 1) // m * m


def _layer_norm(z, g, b):
    mu = jnp.mean(z, axis=-1, keepdims=True)
    zc = z - mu
    var = jnp.mean(zc * zc, axis=-1, keepdims=True)
    return zc * lax.rsqrt(var + LN_EPS) * g + b


def _dot(a, b):
    return jnp.dot(a, b, preferred_element_type=F32)


def _dot_nt(a, b):
    return lax.dot_general(a, b, (((1,), (1,)), ((), ())), preferred_element_type=F32)


def _ffn_ln_body(x_ref, wa_ref, wb_ref, wd_ref, g_ref, b_ref, *rest, nj, alpha, emit_bf16):
    if emit_bf16:
        o_ref, obf_ref, xbf_sc, acc_sc = rest
    else:
        o_ref, xbf_sc, acc_sc = rest
    j = pl.program_id(1)

    @pl.when(j == 0)
    def _init():
        xbf_sc[...] = x_ref[...].astype(BF16)
        acc_sc[...] = jnp.zeros_like(acc_sc)

    xb = xbf_sc[...]
    a = _dot(xb, wa_ref[...])
    b = _dot(xb, wb_ref[...])
    h = (a * jax.nn.sigmoid(a)) * b
    acc_sc[...] += _dot(h.astype(BF16), wd_ref[...])

    @pl.when(j == nj - 1)
    def _finish():
        z = alpha * x_ref[...] + 0.5 * acc_sc[...]
        y = _layer_norm(z, g_ref[...], b_ref[...])
        o_ref[...] = y
        if emit_bf16:
            obf_ref[...] = y.astype(BF16)


def _ffn_ln(x, w_up, w_down, g, b, alpha, emit_bf16):
    T, D = x.shape
    F = w_down.shape[0]
    tm = _pick_tile(T, (512, 256, 128, 64, 32, 16, 8))
    tf = _pick_tile(F, (512, 256, 128))
    nj = F // tf
    out_shape = [jax.ShapeDtypeStruct((T, D), F32)]
    out_specs = [pl.BlockSpec((tm, D), lambda i, j: (i, 0))]
    if emit_bf16:
        out_shape.append(jax.ShapeDtypeStruct((T, D), BF16))
        out_specs.append(pl.BlockSpec((tm, D), lambda i, j: (i, 0)))
    res = pl.pallas_call(
        functools.partial(_ffn_ln_body, nj=nj, alpha=alpha, emit_bf16=emit_bf16),
        grid=(T // tm, nj),
        in_specs=[
            pl.BlockSpec((tm, D), lambda i, j: (i, 0)),
            pl.BlockSpec((D, tf), lambda i, j: (0, j)),
            pl.BlockSpec((D, tf), lambda i, j: (0, nj + j)),
            pl.BlockSpec((tf, D), lambda i, j: (j, 0)),
            pl.BlockSpec((1, D), lambda i, j: (0, 0)),
            pl.BlockSpec((1, D), lambda i, j: (0, 0)),
        ],
        out_specs=out_specs,
        out_shape=out_shape,
        scratch_shapes=[pltpu.VMEM((tm, D), BF16), pltpu.VMEM((tm, D), F32)],
        compiler_params=_cparams(("parallel", "arbitrary")),
        name="ffn_ln",
    )(x, w_up, w_up, w_down, g.reshape(1, D), b.reshape(1, D))
    return res if emit_bf16 else (res[0], None)


def _proj_body(flag_ref, x_ref, w_ref, cos_ref, sin_ref, o_ref, *, n_chunk, kscale):
    j = pl.program_id(1)
    y = _dot(x_ref[...], w_ref[...])
    cos = cos_ref[...]
    sin = sin_ref[...]
    for c in range(n_chunk):
        f = flag_ref[j * n_chunk + c]
        sl = slice(c * LANES, (c + 1) * LANES)
        yc = y[:, sl]

        @pl.when(f == 0)
        def _plain():
            o_ref[:, sl] = yc

        @pl.when(f != 0)
        def _rope():
            roped = yc * cos + pltpu.roll(yc, NSA_HD // 2, 1) * sin
            sc = jnp.where(f == 2, kscale, 1.0).astype(F32)
            o_ref[:, sl] = roped * sc


def _proj(x_bf, w_bf, flags, cos, sin, rows_per_seq):
    T, D = x_bf.shape
    N = w_bf.shape[1]
    tm = _pick_tile(T, (1024, 512, 256, 128, 64, 32, 16, 8))
    tn = 640
    n_chunk = tn // LANES
    if rows_per_seq >= tm:
        assert rows_per_seq % tm == 0
        n_tab = rows_per_seq // tm
    else:
        assert tm % rows_per_seq == 0
        cos = jnp.tile(cos, (tm // rows_per_seq, 1))
        sin = jnp.tile(sin, (tm // rows_per_seq, 1))
        n_tab = 1
    grid_spec = pltpu.PrefetchScalarGridSpec(
        num_scalar_prefetch=1,
        grid=(T // tm, N // tn),
        in_specs=[
            pl.BlockSpec((tm, D), lambda i, j, fl: (i, 0)),
            pl.BlockSpec((D, tn), lambda i, j, fl: (0, j)),
            pl.BlockSpec((tm, LANES), lambda i, j, fl: (i % n_tab, 0)),
            pl.BlockSpec((tm, LANES), lambda i, j, fl: (i % n_tab, 0)),
        ],
        out_specs=pl.BlockSpec((tm, tn), lambda i, j, fl: (i, j)),
    )
    return pl.pallas_call(
        functools.partial(_proj_body, n_chunk=n_chunk, kscale=RET_DK ** -0.5),
        grid_spec=grid_spec,
        out_shape=jax.ShapeDtypeStruct((T, N), F32),
        compiler_params=_cparams(("parallel", "arbitrary")),
        name="proj_rope",
    )(flags, x_bf, w_bf, cos, sin)


def _ret_body(q_ref, k_ref, v_ref, g_ref, s0_ref, dm_ref, ind_ref, std_ref, cd_ref, gng_ref, gnb_ref,
              o_ref, sout_ref, s_sc, *, nc, rows, rows_pad):
    c = pl.program_id(1)

    @pl.when(c == 0)
    def _load_state():
        s_sc[...] = s0_ref[...]

    pad = rows_pad - rows
    for h in range(RET_HEADS):
        ks = slice(h * RET_DK, (h + 1) * RET_DK)
        vs = slice(h * RET_DV, (h + 1) * RET_DV)
        q = q_ref[:, ks]
        k = k_ref[:, ks]
        v = v_ref[:, vs]
        kd = k * std_ref[h]
        if pad:
            k = jnp.concatenate([k, jnp.zeros((pad, RET_DK), F32)], axis=0)
            kd = jnp.concatenate([kd, jnp.zeros((pad, RET_DK), F32)], axis=0)
            v = jnp.concatenate([v, jnp.zeros((pad, RET_DV), F32)], axis=0)
        s_old = s_sc[h]
        vb = v.astype(BF16)
        a = _dot_nt(q.astype(BF16), k.astype(BF16)) * dm_ref[h]
        o = _dot(a.astype(BF16), vb) + _dot((q * ind_ref[h]).astype(BF16), s_old.astype(BF16))
        s_sc[h] = s_old * cd_ref[h] + _dot(kd.T.astype(BF16), vb)
        mu = jnp.mean(o, axis=-1, keepdims=True)
        oc = o - mu
        var = jnp.mean(oc * oc, axis=-1, keepdims=True)
        on = oc * lax.rsqrt(var + LN_EPS) * gng_ref[:, vs] + gnb_ref[:, vs]
        gate = g_ref[:, vs]
        o_ref[:, vs] = (gate * jax.nn.sigmoid(gate)) * on

    @pl.when(c == nc - 1)
    def _store_state():
        sout_ref[...] = s_sc[...]


def _retention(P, s0, gn_g, gn_b, B, L):
    C = RET_CHUNK if L % RET_CHUNK == 0 else L
    nc = L // C
    CP = max(C, LANES)
    lg = jnp.log1p(-jnp.exp2(-5.0 - jnp.arange(RET_HEADS, dtype=F32)))
    i = jnp.arange(C, dtype=F32)
    diff = i[:, None] - i[None, :]
    dmask = jnp.where(diff >= 0, jnp.exp(lg[:, None, None] * jnp.maximum(diff, 0.0)), 0.0)
    dmask = jnp.pad(dmask, ((0, 0), (0, 0), (0, CP - C)))
    in_decay = jnp.broadcast_to(jnp.exp(lg[:, None] * (i + 1.0))[:, :, None], (RET_HEADS, C, RET_DK))
    st_decay = jnp.broadcast_to(jnp.exp(lg[:, None] * (C - 1.0 - i))[:, :, None], (RET_HEADS, C, RET_DK))
    chunk_decay = jnp.broadcast_to(jnp.exp(lg * C)[:, None, None], (RET_HEADS, 1, RET_DV))
    qw = RET_HEADS * RET_DK
    vw = RET_HEADS * RET_DV
    const3 = lambda b, c: (0, 0, 0)
    return pl.pallas_call(
        functools.partial(_ret_body, nc=nc, rows=C, rows_pad=CP),
        grid=(B, nc),
        in_specs=[
            pl.BlockSpec((C, qw), lambda b, c: (b * nc + c, 0)),
            pl.BlockSpec((C, qw), lambda b, c: (b * nc + c, 1)),
            pl.BlockSpec((C, vw), lambda b, c: (b * nc + c, 1)),
            pl.BlockSpec((C, vw), lambda b, c: (b * nc + c, 2)),
            pl.BlockSpec((None, RET_HEADS, RET_DK, RET_DV), lambda b, c: (b, 0, 0, 0)),
            pl.BlockSpec((RET_HEADS, C, CP), const3),
            pl.BlockSpec((RET_HEADS, C, RET_DK), const3),
            pl.BlockSpec((RET_HEADS, C, RET_DK), const3),
            pl.BlockSpec((RET_HEADS, 1, RET_DV), const3),
            pl.BlockSpec((1, vw), lambda b, c: (0, 0)),
            pl.BlockSpec((1, vw), lambda b, c: (0, 0)),
        ],
        out_specs=[
            pl.BlockSpec((C, vw), lambda b, c: (b * nc + c, 0)),
            pl.BlockSpec((None, RET_HEADS, RET_DK, RET_DV), lambda b, c: (b, 0, 0, 0)),
        ],
        out_shape=[
            jax.ShapeDtypeStruct((B * L, vw), F32),
            jax.ShapeDtypeStruct((B, RET_HEADS, RET_DK, RET_DV), F32),
        ],
        scratch_shapes=[pltpu.VMEM((RET_HEADS, RET_DK, RET_DV), F32)],
        compiler_params=_cparams(("parallel", "arbitrary")),
        name="retention",
    )(P, P, P, P, s0, dmask, in_decay, st_decay, chunk_decay, gn_g.reshape(1, vw), gn_b.reshape(1, vw))


def _cmp_body(*refs, n_in, rows, n_grp, n_cmp, paged):
    if paged:
        refs = refs[1:]
    x_refs = refs[:NSA_KV_HEADS * n_in]
    w1_ref, pos_ref, w2_ref, o_ref, carry_sc = refs[NSA_KV_HEADS * n_in:]
    g = pl.program_id(2)
    grp = n_grp - 1 - g
    cpi = rows // CMP_STRIDE
    M = n_in * cpi

    @pl.when(g == 0)
    def _init():
        carry_sc[...] = jnp.zeros_like(carry_sc)

    w1 = w1_ref[...]
    gp = _dot(pos_ref[...], w1)
    posterm = gp[0:1, :CMP_HIDDEN] + gp[1:2, CMP_HIDDEN:]
    row = lax.broadcasted_iota(jnp.int32, (M, 1), 0)
    for hd in range(NSA_KV_HEADS):
        blocks = []
        for x_ref in x_refs[hd * n_in:(hd + 1) * n_in]:
            blocks.append(jnp.concatenate(
                [x_ref[pl.ds(s, cpi, stride=CMP_STRIDE), :] for s in range(CMP_STRIDE)], axis=1))
        xc = blocks[0] if n_in == 1 else jnp.concatenate(blocks, axis=0)
        gg = _dot(xc.astype(BF16), w1)
        g0 = gg[:, :CMP_HIDDEN]
        g1 = gg[:, CMP_HIDDEN:]
        nxt = pltpu.roll(g1, M - 1, 0)
        nxt = jnp.where(row == M - 1, carry_sc[hd][0:1, :], nxt)
        carry_sc[hd] = g1[0:SUBLANES, :]
        hid = g0 + nxt + posterm
        out = _dot(jax.nn.gelu(hid).astype(BF16), w2_ref[...])
        o_ref[hd] = jnp.where(grp * M + row < n_cmp, out, 0.0)


def _compress(srcs, src_specs, n_in, rows, n_grp, n_cmp, B, w1, pos, w2, page_table=None):
    M = n_in * rows // CMP_STRIDE
    nch = n_grp * M
    paged = page_table is not None
    extra = (lambda *a: a[:3]) if paged else (lambda *a: a)
    in_specs = list(src_specs) + [
        pl.BlockSpec((None, CMP_STRIDE * NSA_HD, 2 * CMP_HIDDEN), lambda *a: (extra(*a)[1], 0, 0)),
        pl.BlockSpec((None, SUBLANES, CMP_STRIDE * NSA_HD), lambda *a: (extra(*a)[1], 0, 0)),
        pl.BlockSpec((None, CMP_HIDDEN, NSA_HD), lambda *a: (extra(*a)[1], 0, 0)),
    ]
    out_spec = pl.BlockSpec((None, None, NSA_KV_HEADS, M, NSA_HD),
                            lambda *a: (extra(*a)[0], extra(*a)[1], 0, n_grp - 1 - extra(*a)[2], 0))
    body = functools.partial(_cmp_body, n_in=n_in, rows=rows, n_grp=n_grp, n_cmp=n_cmp, paged=paged)
    out_shape = jax.ShapeDtypeStruct((B, 2, NSA_KV_HEADS, nch, NSA_HD), F32)
    scratch = [pltpu.VMEM((NSA_KV_HEADS, SUBLANES, CMP_HIDDEN), F32)]
    sem = ("parallel", "arbitrary", "arbitrary")
    if paged:
        grid_spec = pltpu.PrefetchScalarGridSpec(
            num_scalar_prefetch=1, grid=(B, 2, n_grp), in_specs=in_specs, out_specs=out_spec,
            scratch_shapes=scratch)
        return pl.pallas_call(body, grid_spec=grid_spec, out_shape=out_shape,
                              compiler_params=_cparams(sem), name="nsa_compress_paged")(
            page_table, *srcs, w1, pos, w2)
    return pl.pallas_call(body, grid=(B, 2, n_grp), in_specs=in_specs, out_specs=out_spec,
                          out_shape=out_shape, scratch_shapes=scratch,
                          compiler_params=_cparams(sem), name="nsa_compress")(*srcs, w1, pos, w2)


def _cover_matrix(n_cmp, n_slc, rows, cols):
    c_i = np.arange(n_cmp)[:, None]
    n_i = np.arange(n_slc)[None, :]
    cov = np.clip(np.minimum(c_i * CMP_STRIDE + CMP_BLOCK, (n_i + 1) * SLC_BLOCK)
                  - np.maximum(c_i * CMP_STRIDE, n_i * SLC_BLOCK), 0, None).astype(np.float32) / CMP_BLOCK
    out = np.zeros((rows, cols), np.float32)
    out[:n_cmp, :n_slc] = cov
    return out


def _nsa_prompt_body(q_ref, kc_ref, vc_ref, ks_ref, vs_ref, kw_ref, vw_ref, gate_ref, covt_ref, o_ref,
                     kcb_sc, vct_sc, ksb_sc, vst_sc, kwb_sc, vwt_sc, prio_sc, sel_sc, gt_sc,
                     m_sc, l_sc, acc_sc, out_sc, *, tq, L, n_cmp, n_slc, n_sel, nbp):
    kvh = pl.program_id(1)
    qi = pl.program_id(2)
    tk = tq
    nch = kc_ref.shape[0]
    scale = NSA_HD ** -0.5

    @pl.when(qi == 0)
    def _stage_kv():
        kcb_sc[...] = kc_ref[...].astype(BF16)
        vct_sc[...] = vc_ref[...].T.astype(BF16)
        ksb_sc[...] = ks_ref[...].astype(BF16)
        kwb_sc[...] = kw_ref[...].astype(BF16)
        for i in range(L // tk):
            vst_sc[i] = vs_ref[i * tk:(i + 1) * tk, :].T.astype(BF16)
            vwt_sc[i] = vw_ref[i * tk:(i + 1) * tk, :].T.astype(BF16)

    t0 = qi * tq
    t_row = t0 + lax.broadcasted_iota(jnp.int32, (1, tq), 1)
    qs = [(q_ref[:, g * NSA_HD:(g + 1) * NSA_HD] * scale).astype(BF16) for g in range(NSA_GROUP)]
    gt_sc[...] = jax.nn.sigmoid(gate_ref[...]).T

    def gate_row(g, br):
        return gt_sc[pl.ds((kvh * NSA_GROUP + g) * N_BRANCH + br, 1), :]

    c_i = lax.broadcasted_iota(jnp.int32, (nch, 1), 0)
    cmask = (c_i * CMP_STRIDE + (CMP_BLOCK - 1) <= t_row) & (c_i < n_cmp)
    kcb = kcb_sc[...]
    vct = vct_sc[...]
    psum = jnp.zeros((nch, tq), F32)
    for g in range(NSA_GROUP):
        s = jnp.where(cmask, _dot_nt(kcb, qs[g]), -jnp.inf)
        m = jnp.max(s, axis=0, keepdims=True)
        m = jnp.where(m > -jnp.inf, m, 0.0)
        e = jnp.exp(s - m)
        d = jnp.sum(e, axis=0, keepdims=True)
        p = e / jnp.where(d > 0, d, 1.0)
        psum = psum + p
        out_sc[g] = gate_row(g, 0) * _dot(vct, p.astype(BF16))

    imp = jnp.dot(covt_ref[...], psum, preferred_element_type=F32, precision=lax.Precision.HIGHEST)
    n_i = lax.broadcasted_iota(jnp.int32, (nbp, 1), 0)
    valid = (n_i * SLC_BLOCK <= t_row) & (n_i < n_slc)
    cur = t_row >> SLC_SHIFT
    forced = (n_i == 0) | (n_i == cur) | (n_i == cur - 1)
    prio = jnp.where(forced, jnp.inf, jnp.where(valid, imp, -jnp.inf))
    prio_sc[...] = prio
    cnt = jnp.zeros((nbp, tq), jnp.int32)
    for mm in range(n_slc):
        pm = prio_sc[mm:mm + 1, :]
        beats = (pm > prio) | ((pm == prio) & (n_i > mm))
        cnt = cnt + beats.astype(jnp.int32)
    sel_sc[...] = jnp.where((cnt < n_sel) & valid, 1.0, 0.0)

    def attend(kb_sc, vt_sc, kt_lo, kt_hi, mask_fn, br):
        for g in range(NSA_GROUP):
            m_sc[g] = jnp.full((1, tq), NEG_BIG, F32)
            l_sc[g] = jnp.zeros((1, tq), F32)
            acc_sc[g] = jnp.zeros((NSA_HD, tq), F32)

        def body(kt, carry):
            k0 = pl.multiple_of(kt * tk, tk)
            kp = k0 + lax.broadcasted_iota(jnp.int32, (tk, 1), 0)
            mask = mask_fn(kt, kp)
            kb = kb_sc[pl.ds(k0, tk), :]
            vt = vt_sc[kt]
            for g in range(NSA_GROUP):
                s = jnp.where(mask, _dot_nt(kb, qs[g]), NEG_BIG)
                m_old = m_sc[g]
                m_new = jnp.maximum(m_old, jnp.max(s, axis=0, keepdims=True))
                p = jnp.where(mask, jnp.exp(s - m_new), 0.0)
                alpha = jnp.exp(m_old - m_new)
                l_sc[g] = alpha * l_sc[g] + jnp.sum(p, axis=0, keepdims=True)
                acc_sc[g] = alpha * acc_sc[g] + _dot(vt, p.astype(BF16))
                m_sc[g] = m_new
            return carry

        lax.fori_loop(kt_lo, kt_hi, body, 0)
        for g in range(NSA_GROUP):
            out_sc[g] += gate_row(g, br) * (acc_sc[g] / l_sc[g])

    def slc_mask(kt, kp):
        per_tile = tk // SLC_BLOCK
        rows = [jnp.broadcast_to(sel_sc[pl.ds(kt * per_tile + r, 1), :], (SLC_BLOCK, tq))
                for r in range(per_tile)]
        return (jnp.concatenate(rows, axis=0) > 0.5) & (kp <= t_row)

    def win_mask(kt, kp):
        dlt = t_row - kp
        return (dlt >= 0) & (dlt < WINDOW)

    attend(ksb_sc, vst_sc, 0, qi + 1, slc_mask, 1)
    attend(kwb_sc, vwt_sc, jnp.maximum(qi - WINDOW // tk, 0), qi + 1, win_mask, 2)
    for g in range(NSA_GROUP):
        o_ref[:, g * NSA_HD:(g + 1) * NSA_HD] = out_sc[g].T


def _nsa_prompt(P, CMP, B, L):
    tq = _pick_tile(L, (256, 128))
    assert L % tq == 0 and tq % SLC_BLOCK == 0 and WINDOW % tq == 0
    nq = L // tq
    nch = CMP.shape[3]
    n_cmp = L // CMP_STRIDE - CMP_BLOCK // CMP_STRIDE + 1
    n_slc = -(-L // SLC_BLOCK)
    n_sel = min(N_SELECT, n_slc)
    nbp = _round_up(n_slc, SUBLANES)
    covt = jnp.asarray(_cover_matrix(n_cmp, n_slc, nch, nbp).T)
    gw = NSA_GROUP * NSA_HD
    col = lambda c: (lambda b, k, i: (b, c + k))
    body = functools.partial(_nsa_prompt_body, tq=tq, L=L, n_cmp=n_cmp, n_slc=n_slc, n_sel=n_sel, nbp=nbp)
    return pl.pallas_call(
        body,
        grid=(B, NSA_KV_HEADS, nq),
        in_specs=[
            pl.BlockSpec((tq, gw), lambda b, k, i: (b * nq + i, 6 + k)),
            pl.BlockSpec((None, None, None, nch, NSA_HD), lambda b, k, i: (b, 0, k, 0, 0)),
            pl.BlockSpec((None, None, None, nch, NSA_HD), lambda b, k, i: (b, 1, k, 0, 0)),
            pl.BlockSpec((L, NSA_HD), col(36)),
            pl.BlockSpec((L, NSA_HD), col(38)),
            pl.BlockSpec((L, NSA_HD), col(40)),
            pl.BlockSpec((L, NSA_HD), col(42)),
            pl.BlockSpec((tq, LANES), lambda b, k, i: (b * nq + i, 44)),
            pl.BlockSpec((nbp, nch), lambda b, k, i: (0, 0)),
        ],
        out_specs=pl.BlockSpec((tq, gw), lambda b, k, i: (b * nq + i, k)),
        out_shape=jax.ShapeDtypeStruct((B * L, NSA_HEADS * NSA_HD), F32),
        scratch_shapes=[
            pltpu.VMEM((nch, NSA_HD), BF16),
            pltpu.VMEM((NSA_HD, nch), BF16),
            pltpu.VMEM((L, NSA_HD), BF16),
            pltpu.VMEM((L // tq, NSA_HD, tq), BF16),
            pltpu.VMEM((L, NSA_HD), BF16),
            pltpu.VMEM((L // tq, NSA_HD, tq), BF16),
            pltpu.VMEM((nbp, tq), F32),
            pltpu.VMEM((nbp, tq), F32),
            pltpu.VMEM((LANES, tq), F32),
            pltpu.VMEM((NSA_GROUP, 1, tq), F32),
            pltpu.VMEM((NSA_GROUP, 1, tq), F32),
            pltpu.VMEM((NSA_GROUP, NSA_HD, tq), F32),
            pltpu.VMEM((NSA_GROUP, NSA_HD, tq), F32),
        ],
        compiler_params=_cparams(("parallel", "arbitrary", "arbitrary")),
        name="nsa_prompt",
    )(P, CMP, CMP, P, P, P, P, P, covt)


def _nsa_sample_body(*refs, pg, n_steps, dl, q_off, w_buf, n_cmp, n_slc, n_sel, nbl, page):
    refs = refs[1:]
    q_ref, kvn_ref, wn_ref, gate_ref, kc_ref, vc_ref, cov_ref, cw_ref = refs[:8]
    kp_refs = refs[8:8 + pg]
    vp_refs = refs[8 + pg:8 + 2 * pg]
    o_ref, qb_sc, sel_sc, m_sc, l_sc, acc_sc, ocmp_sc = refs[8 + 2 * pg:]
    step = pl.program_id(1)
    rows_h = NSA_GROUP * dl
    rows = NSA_KV_HEADS * rows_h
    scale = NSA_HD ** -0.5
    nch = kc_ref.shape[1]

    def tok_pos(n):
        r = lax.broadcasted_iota(jnp.int32, (n, 1), 0)
        return q_off + r % dl

    @pl.when(step == 0)
    def _select():
        for k in range(NSA_KV_HEADS):
            for g in range(NSA_GROUP):
                h = k * NSA_GROUP + g
                qb_sc[pl.ds(h * dl, dl), :] = (q_ref[:, h * NSA_HD:(h + 1) * NSA_HD] * scale).astype(BF16)
        t_h = tok_pos(rows_h)
        c_i = lax.broadcasted_iota(jnp.int32, (1, nch), 1)
        cmask = (c_i * CMP_STRIDE + (CMP_BLOCK - 1) <= t_h) & (c_i < n_cmp)
        imps = []
        for k in range(NSA_KV_HEADS):
            qk = qb_sc[pl.ds(k * rows_h, rows_h), :]
            s = jnp.where(cmask, _dot_nt(qk, kc_ref[k].astype(BF16)), -jnp.inf)
            m = jnp.max(s, axis=1, keepdims=True)
            m = jnp.where(m > -jnp.inf, m, 0.0)
            e = jnp.exp(s - m)
            d = jnp.sum(e, axis=1, keepdims=True)
            p = e / jnp.where(d > 0, d, 1.0)
            ocmp_sc[pl.ds(k * rows_h, rows_h), :] = _dot(p.astype(BF16), vc_ref[k].astype(BF16))
            psum = p[0:dl]
            for g in range(1, NSA_GROUP):
                psum = psum + p[g * dl:(g + 1) * dl]
            imps.append(jnp.dot(psum, cov_ref[...], preferred_element_type=F32,
                                precision=lax.Precision.HIGHEST))
        imp = jnp.concatenate(imps, axis=0)
        t_s = tok_pos(NSA_KV_HEADS * dl)
        n_i = lax.broadcasted_iota(jnp.int32, (1, nbl), 1)
        valid = (n_i * SLC_BLOCK <= t_s) & (n_i < n_slc)
        cur = t_s >> SLC_SHIFT
        forced = (n_i == 0) | (n_i == cur) | (n_i == cur - 1)
        prio = jnp.where(forced, jnp.inf, jnp.where(valid, imp, -jnp.inf))
        n_f = n_i.astype(F32)
        alive = jnp.broadcast_to(jnp.where(n_i < n_slc, 1.0, 0.0), prio.shape)
        sel = jnp.zeros(prio.shape, F32)
        for _ in range(n_sel):
            mx = jnp.max(jnp.where(alive > 0.5, prio, -jnp.inf), axis=1, keepdims=True)
            cand = (alive > 0.5) & (prio == mx)
            first = jnp.min(jnp.where(cand, n_f, float(nbl)), axis=1, keepdims=True)
            pick = n_f == first
            sel = jnp.where(pick, 1.0, sel)
            alive = jnp.where(pick, 0.0, alive)
        sel = jnp.where(valid, sel, 0.0)
        for k in range(NSA_KV_HEADS):
            for g in range(NSA_GROUP):
                sel_sc[pl.ds((k * NSA_GROUP + g) * dl, dl), :] = sel[k * dl:(k + 1) * dl]
        m_sc[...] = jnp.full(m_sc.shape, NEG_BIG, F32)
        l_sc[...] = jnp.zeros_like(l_sc)
        acc_sc[...] = jnp.zeros_like(acc_sc)

    selb = sel_sc[...].astype(BF16)
    n_col = lax.broadcasted_iota(jnp.int32, (nbl, 1), 0)

    def online_update(k, s, mask, v_tiles):
        rs = pl.ds(k * rows_h, rows_h)
        m_old = m_sc[rs, :]
        m_new = jnp.maximum(m_old, jnp.max(s, axis=1, keepdims=True))
        p = jnp.where(mask, jnp.exp(s - m_new), 0.0)
        alpha = jnp.exp(m_old - m_new)
        l_sc[rs, :] = alpha * l_sc[rs, :] + jnp.sum(p, axis=1, keepdims=True)
        pv = jnp.zeros((rows_h, NSA_HD), F32)
        for i, vt in enumerate(v_tiles):
            pv = pv + _dot(p[:, i * LANES:(i + 1) * LANES].astype(BF16), vt)
        acc_sc[rs, :] = alpha * acc_sc[rs, :] + pv
        m_sc[rs, :] = m_new

    t_h = tok_pos(rows_h)

    def selected(kpos):
        blk = jnp.where(n_col == (kpos >> SLC_SHIFT), 1.0, 0.0).astype(BF16)
        return _dot(selb, blk)

    kpos = step * (pg * page) + lax.broadcasted_iota(jnp.int32, (1, pg * page), 1)
    sel_all = selected(kpos)
    for k in range(NSA_KV_HEADS):
        cs = slice(k * NSA_HD, (k + 1) * NSA_HD)
        qk = qb_sc[pl.ds(k * rows_h, rows_h), :]
        mask = (sel_all[k * rows_h:(k + 1) * rows_h] > 0.5) & (kpos <= t_h)
        s = jnp.concatenate([_dot_nt(qk, r[:, cs].astype(BF16)) for r in kp_refs], axis=1)
        s = jnp.where(mask, s, NEG_BIG)
        online_update(k, s, mask, [r[:, cs].astype(BF16) for r in vp_refs])

    @pl.when(step == n_steps - 1)
    def _finish():
        padn = LANES - dl
        j_new = lax.broadcasted_iota(jnp.int32, (1, LANES), 1)
        kpos_n = q_off + j_new
        sel_n = selected(kpos_n)
        for k in range(NSA_KV_HEADS):
            kn = jnp.concatenate([kvn_ref[:, k * NSA_HD:(k + 1) * NSA_HD], jnp.zeros((padn, NSA_HD), F32)], axis=0)
            vn = jnp.concatenate([kvn_ref[:, (2 + k) * NSA_HD:(3 + k) * NSA_HD], jnp.zeros((padn, NSA_HD), F32)],
                                 axis=0)
            qk = qb_sc[pl.ds(k * rows_h, rows_h), :]
            mask = (sel_n[k * rows_h:(k + 1) * rows_h] > 0.5) & (kpos_n <= t_h) & (j_new < dl)
            s = jnp.where(mask, _dot_nt(qk, kn.astype(BF16)), NEG_BIG)
            online_update(k, s, mask, [vn.astype(BF16)])
        j_w = lax.broadcasted_iota(jnp.int32, (1, w_buf + LANES), 1)
        pos_w = q_off - w_buf + j_w
        dlt = t_h - pos_w
        wmask = (j_w < w_buf + dl) & (pos_w >= 0) & (dlt >= 0) & (dlt < WINDOW)
        gates = jax.nn.sigmoid(gate_ref[...])
        for k in range(NSA_KV_HEADS):
            kw = jnp.concatenate([cw_ref[:, k * NSA_HD:(k + 1) * NSA_HD],
                                  wn_ref[:, k * NSA_HD:(k + 1) * NSA_HD], jnp.zeros((padn, NSA_HD), F32)], axis=0)
            vw = jnp.concatenate([cw_ref[:, (2 + k) * NSA_HD:(3 + k) * NSA_HD],
                                  wn_ref[:, (2 + k) * NSA_HD:(3 + k) * NSA_HD], jnp.zeros((padn, NSA_HD), F32)],
                                 axis=0)
            qk = qb_sc[pl.ds(k * rows_h, rows_h), :]
            s = jnp.where(wmask, _dot_nt(qk, kw.astype(BF16)), -jnp.inf)
            m = jnp.max(s, axis=1, keepdims=True)
            m = jnp.where(m > -jnp.inf, m, 0.0)
            e = jnp.exp(s - m)
            d = jnp.sum(e, axis=1, keepdims=True)
            o_win = _dot((e / jnp.where(d > 0, d, 1.0)).astype(BF16), vw.astype(BF16))
            rs = pl.ds(k * rows_h, rows_h)
            o_slc = acc_sc[rs, :] / l_sc[rs, :]
            o_cmp = ocmp_sc[rs, :]
            for g in range(NSA_GROUP):
                h = k * NSA_GROUP + g
                r = slice(g * dl, (g + 1) * dl)
                gc = gates[:, h * N_BRANCH:h * N_BRANCH + 1]
                gs = gates[:, h * N_BRANCH + 1:h * N_BRANCH + 2]
                gw = gates[:, h * N_BRANCH + 2:h * N_BRANCH + 3]
                o_ref[:, h * NSA_HD:(h + 1) * NSA_HD] = gc * o_cmp[r] + gs * o_slc[r] + gw * o_win[r]


def _nsa_sample(P, CMP, cache_pages, cache_win, page_table, DB, DL, past_len, page):
    n_pages = page_table.shape[1]
    w_buf = cache_win.shape[1]
    lk = past_len + DL
    nch = CMP.shape[3]
    n_cmp = lk // CMP_STRIDE - CMP_BLOCK // CMP_STRIDE + 1
    n_slc = -(-lk // SLC_BLOCK)
    n_sel = min(N_SELECT, n_slc)
    nbl = _round_up(n_slc, LANES)
    pg = _pick_tile(n_pages, (16, 8, 4, 2, 1))
    n_steps = n_pages // pg
    assert DL % SUBLANES == 0 and DL <= LANES and page == LANES and past_len == n_pages * page
    cov = jnp.asarray(_cover_matrix(n_cmp, n_slc, nch, nbl))
    kvw = NSA_KV_HEADS * NSA_HD
    page_spec = lambda i, c: pl.BlockSpec((None, page, kvw), lambda b, s, pt: (pt[b, s * pg + i], 0, c))
    in_specs = [
        pl.BlockSpec((DL, NSA_HEADS * NSA_HD), lambda b, s, pt: (b, 3)),
        pl.BlockSpec((DL, 2 * kvw), lambda b, s, pt: (b, 9)),
        pl.BlockSpec((DL, 2 * kvw), lambda b, s, pt: (b, 10)),
        pl.BlockSpec((DL, LANES), lambda b, s, pt: (b, 44)),
        pl.BlockSpec((None, None, NSA_KV_HEADS, nch, NSA_HD), lambda b, s, pt: (b, 0, 0, 0, 0)),
        pl.BlockSpec((None, None, NSA_KV_HEADS, nch, NSA_HD), lambda b, s, pt: (b, 1, 0, 0, 0)),
        pl.BlockSpec((nch, nbl), lambda b, s, pt: (0, 0)),
        pl.BlockSpec((None, w_buf, 2 * kvw), lambda b, s, pt: (b, 0, 0)),
    ] + [page_spec(i, 2) for i in range(pg)] + [page_spec(i, 3) for i in range(pg)]
    rows = NSA_HEADS * DL
    grid_spec = pltpu.PrefetchScalarGridSpec(
        num_scalar_prefetch=1,
        grid=(DB, n_steps),
        in_specs=in_specs,
        out_specs=pl.BlockSpec((DL, NSA_HEADS * NSA_HD), lambda b, s, pt: (b, 0)),
        scratch_shapes=[
            pltpu.VMEM((rows, NSA_HD), BF16),
            pltpu.VMEM((rows, nbl), F32),
            pltpu.VMEM((rows, 1), F32),
            pltpu.VMEM((rows, 1), F32),
            pltpu.VMEM((rows, NSA_HD), F32),
            pltpu.VMEM((rows, NSA_HD), F32),
        ],
    )
    body = functools.partial(_nsa_sample_body, pg=pg, n_steps=n_steps, dl=DL, q_off=past_len, w_buf=w_buf,
                             n_cmp=n_cmp, n_slc=n_slc, n_sel=n_sel, nbl=nbl, page=page)
    return pl.pallas_call(
        body, grid_spec=grid_spec,
        out_shape=jax.ShapeDtypeStruct((DB * DL, NSA_HEADS * NSA_HD), F32),
        compiler_params=_cparams(("parallel", "arbitrary")),
        name="nsa_sample",
    )(page_table, P, P, P, P, CMP, CMP, cov, cache_win, *([cache_pages] * (2 * pg)))


def _out_ln_body(h_ref, ro_ref, no_ref, wr_ref, wn_ref, g_ref, b_ref, o_ref, *, alpha):
    m = _dot(ro_ref[...].astype(BF16), wr_ref[...]) + _dot(no_ref[...].astype(BF16), wn_ref[...])
    o_ref[...] = _layer_norm(alpha * h_ref[...] + m, g_ref[...], b_ref[...])


def _out_ln(h, ro, no, w_out, g, b, alpha):
    T, D = h.shape
    kr = ro.shape[1]
    kn = no.shape[1]
    tm = _pick_tile(T, (512, 256, 128, 64, 32, 16, 8))
    return pl.pallas_call(
        functools.partial(_out_ln_body, alpha=alpha),
        grid=(T // tm,),
        in_specs=[
            pl.BlockSpec((tm, D), lambda i: (i, 0)),
            pl.BlockSpec((tm, kr), lambda i: (i, 0)),
            pl.BlockSpec((tm, kn), lambda i: (i, 0)),
            pl.BlockSpec((kr, D), lambda i: (0, 0)),
            pl.BlockSpec((kn, D), lambda i: (1, 0)),
            pl.BlockSpec((1, D), lambda i: (0, 0)),
            pl.BlockSpec((1, D), lambda i: (0, 0)),
        ],
        out_specs=pl.BlockSpec((tm, D), lambda i: (i, 0)),
        out_shape=jax.ShapeDtypeStruct((T, D), F32),
        compiler_params=_cparams(("parallel",)),
        name="out_ln",
    )(h, ro, no, w_out, w_out, g.reshape(1, D), b.reshape(1, D))


def _rope_tables(pos):
    half = NSA_HD // 2
    inv = ROPE_THETA ** (-jnp.arange(half, dtype=F32) / half)
    ang = pos.astype(F32)[:, None] * inv[None, :]
    cos = jnp.cos(ang)
    sin = jnp.sin(ang)
    return jnp.concatenate([cos, cos], -1), jnp.concatenate([-sin, sin], -1)


def _cmp_weights(w1, pos, w2):
    r = CMP_BLOCK // CMP_STRIDE
    w1r = w1.reshape(r, CMP_STRIDE * NSA_HD, CMP_HIDDEN)
    w1c = jnp.concatenate([w1r[i] for i in range(r)], axis=1).astype(BF16)
    posr = jnp.pad(pos.reshape(r, CMP_STRIDE * NSA_HD), ((0, SUBLANES - r), (0, 0))).astype(BF16)
    return w1c, posr, w2.astype(BF16)


def _decoder_layer(x, B, L, q_off, s0, p, sample_ctx):
    alpha = p['alpha']
    h1, h1_bf = _ffn_ln(x, p['ffn1_w_up'], p['ffn1_w_down'], p['ln1_g'], p['ln1_b'], alpha, True)
    cos, sin = _rope_tables(q_off + jnp.arange(L, dtype=jnp.int32))
    P = _proj(h1_bf, p['w_in'], p['flags'], cos, sin, L)
    ro, ret_s = _retention(P, s0, p['ret_gn_g'], p['ret_gn_b'], B, L)
    if sample_ctx is None:
        assert L % CMP_STRIDE == 0
        n_cmp = L // CMP_STRIDE - CMP_BLOCK // CMP_STRIDE + 1
        specs = [pl.BlockSpec((L, NSA_HD), (lambda hd: (lambda b, t, g: (b, 32 + NSA_KV_HEADS * t + hd)))(hd))
                 for hd in range(NSA_KV_HEADS)]
        CMP = _compress([P] * NSA_KV_HEADS, specs, 1, L, 1, n_cmp, B, p['cmp_w1'], p['cmp_pos'], p['cmp_w2'])
        no = _nsa_prompt(P, CMP, B, L)
    else:
        cache_pages, cache_win, page_table, past_len, page = sample_ctx
        n_pages = page_table.shape[1]
        lk = past_len + L
        assert (lk // CMP_STRIDE) * CMP_STRIDE <= past_len, "compression blocks must lie in the paged past"
        n_cmp = lk // CMP_STRIDE - CMP_BLOCK // CMP_STRIDE + 1
        n_in = _pick_tile(n_pages, (32, 16, 8, 4, 2, 1))
        n_grp = n_pages // n_in
        specs = [pl.BlockSpec((None, page, NSA_HD),
                              (lambda i, hd: (lambda b, t, g, pt: (pt[b, (n_grp - 1 - g) * n_in + i], 0,
                                                                   NSA_KV_HEADS * t + hd)))(i, hd))
                 for hd in range(NSA_KV_HEADS) for i in range(n_in)]
        CMP = _compress([cache_pages] * (NSA_KV_HEADS * n_in), specs, n_in, page, n_grp, n_cmp, B,
                        p['cmp_w1'], p['cmp_pos'], p['cmp_w2'], page_table=page_table)
        no = _nsa_sample(P, CMP, cache_pages, cache_win, page_table, B, L, past_len, page)
    x2 = _out_ln(h1, ro, no, p['w_out'], p['ln2_g'], p['ln2_b'], alpha)
    y, _ = _ffn_ln(x2, p['ffn2_w_up'], p['ffn2_w_down'], p['ln3_g'], p['ln3_b'], alpha, False)
    return y, ret_s, P


def kernel(x_prompt, x_sample, state_ret, cache_nsa_kv, cache_win, page_table, ffn1_w_up, ffn1_w_down, ln1_g, ln1_b, w_in, w_out, ret_gn_g, ret_gn_b, cmp_pos_k, cmp_w1_k, cmp_w2_k, cmp_pos_v, cmp_w1_v, cmp_w2_v, ln2_g, ln2_b, ffn2_w_up, ffn2_w_down, ln3_g, ln3_b):
    B, L, D = x_prompt.shape
    DB, DL, _ = x_sample.shape
    depth = w_in.shape[0]
    n_pool, page = cache_nsa_kv.shape[1], cache_nsa_kv.shape[2]
    n_pages = page_table.shape[1]
    past_len = n_pages * page
    w_buf = cache_win.shape[2]
    alpha = (2.0 * depth) ** 0.25
    kv_cols = 4 * NSA_KV_HEADS * NSA_HD
    win_cols = 2 * NSA_KV_HEADS * NSA_HD
    flags_np = np.zeros((N_IN_PAD // LANES,), np.int32)
    flags_np[list(ROPE_CHUNKS)] = 1
    flags_np[list(KSCALE_CHUNKS)] = 2
    flags = jnp.asarray(flags_np)

    yp = x_prompt.reshape(B * L, D)
    ys = x_sample.reshape(DB * DL, D)
    outs = [[] for _ in range(6)]
    for l in range(depth):
        k1, p1, k2 = _cmp_weights(cmp_w1_k[l], cmp_pos_k[l], cmp_w2_k[l])
        v1, q1, v2 = _cmp_weights(cmp_w1_v[l], cmp_pos_v[l], cmp_w2_v[l])
        p = {
            'alpha': alpha, 'flags': flags,
            'ffn1_w_up': ffn1_w_up[l].astype(BF16), 'ffn1_w_down': ffn1_w_down[l].astype(BF16),
            'ln1_g': ln1_g[l], 'ln1_b': ln1_b[l],
            'w_in': jnp.pad(w_in[l], ((0, 0), (0, N_IN_PAD - N_IN))).astype(BF16),
            'w_out': w_out[l].astype(BF16),
            'ret_gn_g': ret_gn_g[l], 'ret_gn_b': ret_gn_b[l],
            'cmp_w1': jnp.stack([k1, v1]), 'cmp_pos': jnp.stack([p1, q1]), 'cmp_w2': jnp.stack([k2, v2]),
            'ln2_g': ln2_g[l], 'ln2_b': ln2_b[l],
            'ffn2_w_up': ffn2_w_up[l].astype(BF16), 'ffn2_w_down': ffn2_w_down[l].astype(BF16),
            'ln3_g': ln3_g[l], 'ln3_b': ln3_b[l],
        }
        s0 = jnp.zeros((B, RET_HEADS, RET_DK, RET_DV), F32)
        yp, rs_p, P_p = _decoder_layer(yp, B, L, 0, s0, p, None)
        ctx = (cache_nsa_kv[l].reshape(n_pool, page, kv_cols), cache_win[l].reshape(DB, w_buf, win_cols),
               page_table, past_len, page)
        ys, rs_s, P_s = _decoder_layer(ys, DB, DL, past_len, state_ret[l], p, ctx)
        kv0 = 32 * LANES
        w0 = 40 * LANES
        P_p3 = P_p.reshape(B, L, N_IN_PAD)
        P_s3 = P_s.reshape(DB, DL, N_IN_PAD)
        wl = min(WINDOW, L)
        outs[0].append(rs_p)
        outs[1].append(rs_s)
        outs[2].append(P_p3[:, :, kv0:kv0 + kv_cols].reshape(B, L, 4, NSA_KV_HEADS, NSA_HD))
        outs[3].append(P_s3[:, :, kv0:kv0 + kv_cols].reshape(DB, DL, 4, NSA_KV_HEADS, NSA_HD))
        outs[4].append(P_p3[:, L - wl:, w0:w0 + win_cols].reshape(B, wl, 2, NSA_KV_HEADS, NSA_HD))
        win_s = P_s3[:, :, w0:w0 + win_cols].reshape(DB, DL, 2, NSA_KV_HEADS, NSA_HD)
        outs[5].append(jnp.concatenate([cache_win[l], win_s], axis=1)[:, -w_buf:])
    return (yp.reshape(B, L, D), ys.reshape(DB, DL, D), jnp.stack(outs[0]), jnp.stack(outs[1]),
            jnp.stack(outs[2]), jnp.stack(outs[3]), jnp.stack(outs[4]), jnp.stack(outs[5]))
```

```python
import functools

import numpy as np
import jax
import jax.numpy as jnp
from jax import lax
from jax.experimental import pallas as pl
from jax.experimental.pallas import tpu as pltpu

F32 = jnp.float32
BF16 = jnp.bfloat16

LANES = 128
SUBLANES = 8
VMEM_LIMIT_BYTES = 56 * 1024 * 1024

RET_HEADS = 4
RET_DK = 128
RET_DV = 256
RET_CHUNK = 128
NSA_HEADS = 8
NSA_KV_HEADS = 2
NSA_HD = 128
NSA_GROUP = NSA_HEADS // NSA_KV_HEADS
CMP_BLOCK = 32
CMP_STRIDE = 16
CMP_HIDDEN = 2 * NSA_HD
SLC_BLOCK = 64
SLC_SHIFT = 6
N_SELECT = 16
WINDOW = 512
N_BRANCH = 3
ROPE_THETA = 10000.0
LN_EPS = 1e-5
NEG_BIG = -1e30
LOG2_E = 1.4426950408889634

N_IN = 5656
N_IN_PAD = 5760
ROPE_CHUNKS = tuple(range(0, 8)) + tuple(range(24, 32)) + (32, 33, 36, 37, 40, 41)
KSCALE_CHUNKS = tuple(range(4, 8))


def _cparams(sem):
    return pltpu.CompilerParams(dimension_semantics=sem, vmem_limit_bytes=VMEM_LIMIT_BYTES)


def _pick_tile(n, candidates):
    for c in candidates:
        if n % c == 0:
            return c
    return n


def _round_up(n, m):
    return (n + m - 1) // m * m


def _layer_norm(z, g, b):
    mu = jnp.mean(z, axis=-1, keepdims=True)
    zc = z - mu
    var = jnp.mean(zc * zc, axis=-1, keepdims=True)
    return zc * lax.rsqrt(var + LN_EPS) * g + b


def _dot(a, b):
    return jnp.dot(a, b, preferred_element_type=F32)


def _dot_nt(a, b):
    return lax.dot_general(a, b, (((1,), (1,)), ((), ())), preferred_element_type=F32)


def _ffn_ln_body(x_ref, wa_ref, wb_ref, wd_ref, g_ref, b_ref, *rest, nj, alpha, emit_bf16):
    if emit_bf16:
        o_ref, obf_ref, xbf_sc, acc_sc = rest
    else:
        o_ref, xbf_sc, acc_sc = rest
    j = pl.program_id(1)

    @pl.when(j == 0)
    def _init():
        xbf_sc[...] = x_ref[...].astype(BF16)
        acc_sc[...] = jnp.zeros_like(acc_sc)

    xb = xbf_sc[...]
    a = _dot(xb, wa_ref[...])
    b = _dot(xb, wb_ref[...])
    h = (a * jax.nn.sigmoid(a)) * b
    acc_sc[...] += _dot(h.astype(BF16), wd_ref[...])

    @pl.when(j == nj - 1)
    def _finish():
        z = alpha * x_ref[...] + 0.5 * acc_sc[...]
        y = _layer_norm(z, g_ref[...], b_ref[...])
        o_ref[...] = y
        if emit_bf16:
            obf_ref[...] = y.astype(BF16)


def _ffn_ln(x, w_up, w_down, g, b, alpha, emit_bf16):
    T, D = x.shape
    F = w_down.shape[0]
    tm = _pick_tile(T, (512, 256, 128, 64, 32, 16, 8))
    tf = _pick_tile(F, (512, 256, 128))
    nj = F // tf
    out_shape = [jax.ShapeDtypeStruct((T, D), F32)]
    out_specs = [pl.BlockSpec((tm, D), lambda i, j: (i, 0))]
    if emit_bf16:
        out_shape.append(jax.ShapeDtypeStruct((T, D), BF16))
        out_specs.append(pl.BlockSpec((tm, D), lambda i, j: (i, 0)))
    res = pl.pallas_call(
        functools.partial(_ffn_ln_body, nj=nj, alpha=alpha, emit_bf16=emit_bf16),
        grid=(T // tm, nj),
        in_specs=[
            pl.BlockSpec((tm, D), lambda i, j: (i, 0)),
            pl.BlockSpec((D, tf), lambda i, j: (0, j)),
            pl.BlockSpec((D, tf), lambda i, j: (0, nj + j)),
            pl.BlockSpec((tf, D), lambda i, j: (j, 0)),
            pl.BlockSpec((1, D), lambda i, j: (0, 0)),
            pl.BlockSpec((1, D), lambda i, j: (0, 0)),
        ],
        out_specs=out_specs,
        out_shape=out_shape,
        scratch_shapes=[pltpu.VMEM((tm, D), BF16), pltpu.VMEM((tm, D), F32)],
        compiler_params=_cparams(("parallel", "arbitrary")),
        name="ffn_ln",
    )(x, w_up, w_up, w_down, g.reshape(1, D), b.reshape(1, D))
    return res if emit_bf16 else (res[0], None)


def _proj_body(x_ref, w_ref, cos_ref, sin_ref, rope_ref, scale_ref, o_ref, *, n_chunk):
    y = _dot(x_ref[...], w_ref[...])
    cos = cos_ref[...]
    sin = sin_ref[...]
    for c in range(n_chunk):
        sl = slice(c * LANES, (c + 1) * LANES)
        yc = y[:, sl]
        roped = yc * cos + pltpu.roll(yc, NSA_HD // 2, 1) * sin
        o_ref[:, sl] = jnp.where(rope_ref[:, sl] > 0.5, roped, yc) * scale_ref[:, sl]


def _proj(x_bf, w_bf, rope_cols, scale_cols, cos, sin, rows_per_seq):
    T, D = x_bf.shape
    N = w_bf.shape[1]
    tm = _pick_tile(T, (512, 256, 128, 64, 32, 16, 8))
    tn = 1920
    assert N % tn == 0
    n_chunk = tn // LANES
    if rows_per_seq >= tm:
        assert rows_per_seq % tm == 0
        n_tab = rows_per_seq // tm
    else:
        assert tm % rows_per_seq == 0
        cos = jnp.tile(cos, (tm // rows_per_seq, 1))
        sin = jnp.tile(sin, (tm // rows_per_seq, 1))
        n_tab = 1
    return pl.pallas_call(
        functools.partial(_proj_body, n_chunk=n_chunk),
        grid=(T // tm, N // tn),
        in_specs=[
            pl.BlockSpec((tm, D), lambda i, j: (i, 0)),
            pl.BlockSpec((D, tn), lambda i, j: (0, j)),
            pl.BlockSpec((tm, LANES), lambda i, j: (i % n_tab, 0)),
            pl.BlockSpec((tm, LANES), lambda i, j: (i % n_tab, 0)),
            pl.BlockSpec((1, tn), lambda i, j: (0, j)),
            pl.BlockSpec((1, tn), lambda i, j: (0, j)),
        ],
        out_specs=pl.BlockSpec((tm, tn), lambda i, j: (i, j)),
        out_shape=jax.ShapeDtypeStruct((T, N), F32),
        compiler_params=_cparams(("parallel", "arbitrary")),
        name="proj_rope",
    )(x_bf, w_bf, cos, sin, rope_cols, scale_cols)


def _ret_body(q_ref, k_ref, v_ref, g_ref, s0_ref, dm_ref, ind_ref, std_ref, cd_ref, gng_ref, gnb_ref,
              o_ref, sout_ref, s_sc, *, nc, rows, rows_pad):
    c = pl.program_id(1)

    @pl.when(c == 0)
    def _load_state():
        s_sc[...] = s0_ref[...]

    pad = rows_pad - rows
    for h in range(RET_HEADS):
        ks = slice(h * RET_DK, (h + 1) * RET_DK)
        vs = slice(h * RET_DV, (h + 1) * RET_DV)
        q = q_ref[:, ks]
        k = k_ref[:, ks]
        v = v_ref[:, vs]
        kd = k * std_ref[h]
        if pad:
            k = jnp.concatenate([k, jnp.zeros((pad, RET_DK), F32)], axis=0)
            kd = jnp.concatenate([kd, jnp.zeros((pad, RET_DK), F32)], axis=0)
            v = jnp.concatenate([v, jnp.zeros((pad, RET_DV), F32)], axis=0)
        s_old = s_sc[h]
        vb = v.astype(BF16)
        a = _dot_nt(q.astype(BF16), k.astype(BF16)) * dm_ref[h]
        o = _dot(a.astype(BF16), vb) + _dot((q * ind_ref[h]).astype(BF16), s_old.astype(BF16))
        s_sc[h] = s_old * cd_ref[h] + _dot(kd.T.astype(BF16), vb)
        mu = jnp.mean(o, axis=-1, keepdims=True)
        oc = o - mu
        var = jnp.mean(oc * oc, axis=-1, keepdims=True)
        on = oc * lax.rsqrt(var + LN_EPS) * gng_ref[:, vs] + gnb_ref[:, vs]
        gate = g_ref[:, vs]
        o_ref[:, vs] = (gate * jax.nn.sigmoid(gate)) * on

    @pl.when(c == nc - 1)
    def _store_state():
        sout_ref[...] = s_sc[...]


def _retention(P, s0, gn_g, gn_b, B, L):
    C = RET_CHUNK if L % RET_CHUNK == 0 else L
    nc = L // C
    CP = max(C, LANES)
    lg = jnp.log1p(-jnp.exp2(-5.0 - jnp.arange(RET_HEADS, dtype=F32)))
    i = jnp.arange(C, dtype=F32)
    diff = i[:, None] - i[None, :]
    dmask = jnp.where(diff >= 0, jnp.exp(lg[:, None, None] * jnp.maximum(diff, 0.0)), 0.0)
    dmask = jnp.pad(dmask, ((0, 0), (0, 0), (0, CP - C)))
    in_decay = jnp.broadcast_to(jnp.exp(lg[:, None] * (i + 1.0))[:, :, None], (RET_HEADS, C, RET_DK))
    st_decay = jnp.broadcast_to(jnp.exp(lg[:, None] * (C - 1.0 - i))[:, :, None], (RET_HEADS, C, RET_DK))
    chunk_decay = jnp.broadcast_to(jnp.exp(lg * C)[:, None, None], (RET_HEADS, 1, RET_DV))
    qw = RET_HEADS * RET_DK
    vw = RET_HEADS * RET_DV
    const3 = lambda b, c: (0, 0, 0)
    return pl.pallas_call(
        functools.partial(_ret_body, nc=nc, rows=C, rows_pad=CP),
        grid=(B, nc),
        in_specs=[
            pl.BlockSpec((C, qw), lambda b, c: (b * nc + c, 0)),
            pl.BlockSpec((C, qw), lambda b, c: (b * nc + c, 1)),
            pl.BlockSpec((C, vw), lambda b, c: (b * nc + c, 1)),
            pl.BlockSpec((C, vw), lambda b, c: (b * nc + c, 2)),
            pl.BlockSpec((None, RET_HEADS, RET_DK, RET_DV), lambda b, c: (b, 0, 0, 0)),
            pl.BlockSpec((RET_HEADS, C, CP), const3),
            pl.BlockSpec((RET_HEADS, C, RET_DK), const3),
            pl.BlockSpec((RET_HEADS, C, RET_DK), const3),
            pl.BlockSpec((RET_HEADS, 1, RET_DV), const3),
            pl.BlockSpec((1, vw), lambda b, c: (0, 0)),
            pl.BlockSpec((1, vw), lambda b, c: (0, 0)),
        ],
        out_specs=[
            pl.BlockSpec((C, vw), lambda b, c: (b * nc + c, 0)),
            pl.BlockSpec((None, RET_HEADS, RET_DK, RET_DV), lambda b, c: (b, 0, 0, 0)),
        ],
        out_shape=[
            jax.ShapeDtypeStruct((B * L, vw), F32),
            jax.ShapeDtypeStruct((B, RET_HEADS, RET_DK, RET_DV), F32),
        ],
        scratch_shapes=[pltpu.VMEM((RET_HEADS, RET_DK, RET_DV), F32)],
        compiler_params=_cparams(("parallel", "arbitrary")),
        name="retention",
    )(P, P, P, P, s0, dmask, in_decay, st_decay, chunk_decay, gn_g.reshape(1, vw), gn_b.reshape(1, vw))


def _cmp_body(*refs, n_in, rows, n_grp, n_cmp, paged):
    if paged:
        refs = refs[1:]
    x_refs = refs[:NSA_KV_HEADS * n_in]
    w1_ref, pos_ref, w2_ref, o_ref, carry_sc = refs[NSA_KV_HEADS * n_in:]
    g = pl.program_id(2)
    grp = n_grp - 1 - g
    cpi = rows // CMP_STRIDE
    M = n_in * cpi

    @pl.when(g == 0)
    def _init():
        carry_sc[...] = jnp.zeros_like(carry_sc)

    w1 = w1_ref[...]
    gp = _dot(pos_ref[...], w1)
    posterm = gp[0:1, :CMP_HIDDEN] + gp[1:2, CMP_HIDDEN:]
    row = lax.broadcasted_iota(jnp.int32, (M, 1), 0)
    for hd in range(NSA_KV_HEADS):
        blocks = []
        for x_ref in x_refs[hd * n_in:(hd + 1) * n_in]:
            blocks.append(jnp.concatenate(
                [x_ref[pl.ds(s, cpi, stride=CMP_STRIDE), :] for s in range(CMP_STRIDE)], axis=1))
        xc = blocks[0] if n_in == 1 else jnp.concatenate(blocks, axis=0)
        gg = _dot(xc.astype(BF16), w1)
        g0 = gg[:, :CMP_HIDDEN]
        g1 = gg[:, CMP_HIDDEN:]
        nxt = pltpu.roll(g1, M - 1, 0)
        nxt = jnp.where(row == M - 1, carry_sc[hd][0:1, :], nxt)
        carry_sc[hd] = g1[0:SUBLANES, :]
        hid = g0 + nxt + posterm
        out = _dot(jax.nn.gelu(hid).astype(BF16), w2_ref[...])
        o_ref[hd] = jnp.where(grp * M + row < n_cmp, out, 0.0)


def _compress(srcs, src_specs, n_in, rows, n_grp, n_cmp, B, w1, pos, w2, page_table=None):
    M = n_in * rows // CMP_STRIDE
    nch = n_grp * M
    paged = page_table is not None
    extra = (lambda *a: a[:3]) if paged else (lambda *a: a)
    in_specs = list(src_specs) + [
        pl.BlockSpec((None, CMP_STRIDE * NSA_HD, 2 * CMP_HIDDEN), lambda *a: (extra(*a)[1], 0, 0)),
        pl.BlockSpec((None, SUBLANES, CMP_STRIDE * NSA_HD), lambda *a: (extra(*a)[1], 0, 0)),
        pl.BlockSpec((None, CMP_HIDDEN, NSA_HD), lambda *a: (extra(*a)[1], 0, 0)),
    ]
    out_spec = pl.BlockSpec((None, None, NSA_KV_HEADS, M, NSA_HD),
                            lambda *a: (extra(*a)[0], extra(*a)[1], 0, n_grp - 1 - extra(*a)[2], 0))
    body = functools.partial(_cmp_body, n_in=n_in, rows=rows, n_grp=n_grp, n_cmp=n_cmp, paged=paged)
    out_shape = jax.ShapeDtypeStruct((B, 2, NSA_KV_HEADS, nch, NSA_HD), F32)
    scratch = [pltpu.VMEM((NSA_KV_HEADS, SUBLANES, CMP_HIDDEN), F32)]
    sem = ("parallel", "arbitrary", "arbitrary")
    if paged:
        grid_spec = pltpu.PrefetchScalarGridSpec(
            num_scalar_prefetch=1, grid=(B, 2, n_grp), in_specs=in_specs, out_specs=out_spec,
            scratch_shapes=scratch)
        return pl.pallas_call(body, grid_spec=grid_spec, out_shape=out_shape,
                              compiler_params=_cparams(sem), name="nsa_compress_paged")(
            page_table, *srcs, w1, pos, w2)
    return pl.pallas_call(body, grid=(B, 2, n_grp), in_specs=in_specs, out_specs=out_spec,
                          out_shape=out_shape, scratch_shapes=scratch,
                          compiler_params=_cparams(sem), name="nsa_compress")(*srcs, w1, pos, w2)


def _cover_matrix(n_cmp, n_slc, rows, cols):
    c_i = np.arange(n_cmp)[:, None]
    n_i = np.arange(n_slc)[None, :]
    cov = np.clip(np.minimum(c_i * CMP_STRIDE + CMP_BLOCK, (n_i + 1) * SLC_BLOCK)
                  - np.maximum(c_i * CMP_STRIDE, n_i * SLC_BLOCK), 0, None).astype(np.float32) / CMP_BLOCK
    out = np.zeros((rows, cols), np.float32)
    out[:n_cmp, :n_slc] = cov
    return out


def _nsa_prompt_body(q_ref, kc_ref, vc_ref, ks_ref, vs_ref, kw_ref, vw_ref, gate_ref, covt_ref, o_ref,
                     kcb_sc, vct_sc, ksb_sc, vst_sc, kwb_sc, vwt_sc, prio_sc, sel_sc, gt_sc,
                     m_sc, l_sc, acc_sc, out_sc, *, tq, L, n_cmp, n_slc, n_sel, nbp):
    kvh = pl.program_id(1)
    qi = pl.program_id(2)
    tk = tq
    nch = kc_ref.shape[0]
    scale = NSA_HD ** -0.5

    @pl.when(qi == 0)
    def _stage_kv():
        kcb_sc[...] = kc_ref[...].astype(BF16)
        vct_sc[...] = vc_ref[...].T.astype(BF16)
        ksb_sc[...] = ks_ref[...].astype(BF16)
        kwb_sc[...] = kw_ref[...].astype(BF16)
        for i in range(L // tk):
            vst_sc[i] = vs_ref[i * tk:(i + 1) * tk, :].T.astype(BF16)
            vwt_sc[i] = vw_ref[i * tk:(i + 1) * tk, :].T.astype(BF16)

    G = NSA_GROUP
    W = G * tq
    t0 = qi * tq
    t_row = t0 + lax.broadcasted_iota(jnp.int32, (1, tq), 1)
    lane = lax.broadcasted_iota(jnp.int32, (1, W), 1)
    t_all = t0 + (lane & (tq - 1))
    qcat = jnp.concatenate([q_ref[:, g * NSA_HD:(g + 1) * NSA_HD] for g in range(G)], axis=0)
    qcat = (qcat * (scale * LOG2_E)).astype(BF16)
    gt_sc[...] = jax.nn.sigmoid(gate_ref[...]).T

    def gate_row(br):
        return jnp.concatenate(
            [gt_sc[pl.ds((kvh * G + g) * N_BRANCH + br, 1), :] for g in range(G)], axis=1)

    c_i = lax.broadcasted_iota(jnp.int32, (nch, 1), 0)
    cmask = (c_i * CMP_STRIDE + (CMP_BLOCK - 1) <= t_all) & (c_i < n_cmp)
    s = jnp.where(cmask, _dot_nt(kcb_sc[...], qcat), -jnp.inf)
    m = jnp.max(s, axis=0, keepdims=True)
    m = jnp.where(m > -jnp.inf, m, 0.0)
    e = jnp.exp2(s - m)
    d = jnp.sum(e, axis=0, keepdims=True)
    p = e / jnp.where(d > 0, d, 1.0)
    out_sc[...] = gate_row(0) * _dot(vct_sc[...], p.astype(BF16))
    psum = p[:, 0:tq]
    for g in range(1, G):
        psum = psum + p[:, g * tq:(g + 1) * tq]

    imp = jnp.dot(covt_ref[...], psum, preferred_element_type=F32, precision=lax.Precision.HIGHEST)
    n_i = lax.broadcasted_iota(jnp.int32, (nbp, 1), 0)
    valid = (n_i * SLC_BLOCK <= t_row) & (n_i < n_slc)
    cur = t_row >> SLC_SHIFT
    forced = (n_i == 0) | (n_i == cur) | (n_i == cur - 1)
    prio = jnp.where(forced, jnp.inf, jnp.where(valid, imp, -jnp.inf))
    prio_sc[...] = prio
    cnt = jnp.zeros((nbp, tq), jnp.int32)
    for mm in range(n_slc):
        pm = prio_sc[mm:mm + 1, :]
        beats = (pm > prio) | ((pm == prio) & (n_i > mm))
        cnt = cnt + beats.astype(jnp.int32)
    selbias = jnp.where((cnt < n_sel) & valid, 0.0, NEG_BIG)
    sel_sc[...] = jnp.concatenate([selbias] * G, axis=1)

    kp_l = lax.broadcasted_iota(jnp.int32, (tk, 1), 0)
    t_l = lane & (tq - 1)
    causal_bias = jnp.where(kp_l <= t_l, 0.0, NEG_BIG)
    far_bias = jnp.where(kp_l > t_l, 0.0, NEG_BIG)

    def reset():
        m_sc[...] = jnp.full((1, W), NEG_BIG, F32)
        l_sc[...] = jnp.zeros((1, W), F32)
        acc_sc[...] = jnp.zeros((NSA_HD, W), F32)

    def tile_update(kb_sc, vt_sc, kt, bias):
        k0 = pl.multiple_of(kt * tk, tk)
        s = _dot_nt(kb_sc[pl.ds(k0, tk), :], qcat)
        if bias is not None:
            s = s + bias
        m_old = m_sc[...]
        m_new = jnp.maximum(m_old, jnp.max(s, axis=0, keepdims=True))
        p = jnp.exp2(s - m_new)
        alpha = jnp.exp2(m_old - m_new)
        l_sc[...] = alpha * l_sc[...] + jnp.sum(p, axis=0, keepdims=True)
        acc_sc[...] = alpha * acc_sc[...] + _dot(vt_sc[kt], p.astype(BF16))
        m_sc[...] = m_new

    def sel_bias(kt):
        per_tile = tk // SLC_BLOCK
        rows = [jnp.broadcast_to(sel_sc[pl.ds(kt * per_tile + r, 1), :], (SLC_BLOCK, W))
                for r in range(per_tile)]
        return jnp.concatenate(rows, axis=0)

    def flush(br):
        out_sc[...] += (gate_row(br) / l_sc[...]) * acc_sc[...]

    reset()

    def slc_body(kt, carry):
        tile_update(ksb_sc, vst_sc, kt, sel_bias(kt))
        return carry

    lax.fori_loop(0, qi, slc_body, 0)
    tile_update(ksb_sc, vst_sc, qi, sel_bias(qi) + causal_bias)
    flush(1)

    reset()
    far = WINDOW // tk

    @pl.when(qi >= far)
    def _far():
        tile_update(kwb_sc, vwt_sc, qi - far, far_bias)

    for back in range(far - 1, 0, -1):
        @pl.when(qi >= back)
        def _mid():
            tile_update(kwb_sc, vwt_sc, qi - back, None)

    tile_update(kwb_sc, vwt_sc, qi, causal_bias)
    flush(2)
    for g in range(G):
        o_ref[:, g * NSA_HD:(g + 1) * NSA_HD] = out_sc[:, g * tq:(g + 1) * tq].T


def _nsa_prompt(P, CMP, B, L):
    tq = _pick_tile(L, (256, 128))
    assert L % tq == 0 and tq % SLC_BLOCK == 0 and WINDOW % tq == 0 and tq & (tq - 1) == 0
    nq = L // tq
    nch = CMP.shape[3]
    n_cmp = L // CMP_STRIDE - CMP_BLOCK // CMP_STRIDE + 1
    n_slc = -(-L // SLC_BLOCK)
    n_sel = min(N_SELECT, n_slc)
    nbp = _round_up(n_slc, SUBLANES)
    covt = jnp.asarray(_cover_matrix(n_cmp, n_slc, nch, nbp).T)
    gw = NSA_GROUP * NSA_HD
    col = lambda c: (lambda b, k, i: (b, c + k))
    body = functools.partial(_nsa_prompt_body, tq=tq, L=L, n_cmp=n_cmp, n_slc=n_slc, n_sel=n_sel, nbp=nbp)
    return pl.pallas_call(
        body,
        grid=(B, NSA_KV_HEADS, nq),
        in_specs=[
            pl.BlockSpec((tq, gw), lambda b, k, i: (b * nq + i, 6 + k)),
            pl.BlockSpec((None, None, None, nch, NSA_HD), lambda b, k, i: (b, 0, k, 0, 0)),
            pl.BlockSpec((None, None, None, nch, NSA_HD), lambda b, k, i: (b, 1, k, 0, 0)),
            pl.BlockSpec((L, NSA_HD), col(36)),
            pl.BlockSpec((L, NSA_HD), col(38)),
            pl.BlockSpec((L, NSA_HD), col(40)),
            pl.BlockSpec((L, NSA_HD), col(42)),
            pl.BlockSpec((tq, LANES), lambda b, k, i: (b * nq + i, 44)),
            pl.BlockSpec((nbp, nch), lambda b, k, i: (0, 0)),
        ],
        out_specs=pl.BlockSpec((tq, gw), lambda b, k, i: (b * nq + i, k)),
        out_shape=jax.ShapeDtypeStruct((B * L, NSA_HEADS * NSA_HD), F32),
        scratch_shapes=[
            pltpu.VMEM((nch, NSA_HD), BF16),
            pltpu.VMEM((NSA_HD, nch), BF16),
            pltpu.VMEM((L, NSA_HD), BF16),
            pltpu.VMEM((L // tq, NSA_HD, tq), BF16),
            pltpu.VMEM((L, NSA_HD), BF16),
            pltpu.VMEM((L // tq, NSA_HD, tq), BF16),
            pltpu.VMEM((nbp, tq), F32),
            pltpu.VMEM((nbp, NSA_GROUP * tq), F32),
            pltpu.VMEM((LANES, tq), F32),
            pltpu.VMEM((1, NSA_GROUP * tq), F32),
            pltpu.VMEM((1, NSA_GROUP * tq), F32),
            pltpu.VMEM((NSA_HD, NSA_GROUP * tq), F32),
            pltpu.VMEM((NSA_HD, NSA_GROUP * tq), F32),
        ],
        compiler_params=_cparams(("parallel", "arbitrary", "arbitrary")),
        name="nsa_prompt",
    )(P, CMP, CMP, P, P, P, P, P, covt)


def _nsa_sample_body(*refs, pg, n_steps, dl, q_off, w_buf, n_cmp, n_slc, n_sel, nbl, page):
    refs = refs[1:]
    q_ref, kvn_ref, wn_ref, gate_ref, kc_ref, vc_ref, cov_ref, cw_ref = refs[:8]
    kp_refs = refs[8:8 + pg]
    vp_refs = refs[8 + pg:8 + 2 * pg]
    o_ref, qb_sc, sel_sc, m_sc, l_sc, acc_sc, ocmp_sc = refs[8 + 2 * pg:]
    step = pl.program_id(1)
    rows_h = NSA_GROUP * dl
    rows = NSA_KV_HEADS * rows_h
    scale = NSA_HD ** -0.5
    nch = kc_ref.shape[1]

    def tok_pos(n):
        r = lax.broadcasted_iota(jnp.int32, (n, 1), 0)
        return q_off + r % dl

    @pl.when(step == 0)
    def _select():
        for k in range(NSA_KV_HEADS):
            for g in range(NSA_GROUP):
                h = k * NSA_GROUP + g
                qb_sc[pl.ds(h * dl, dl), :] = (q_ref[:, h * NSA_HD:(h + 1) * NSA_HD] * scale).astype(BF16)
        t_h = tok_pos(rows_h)
        c_i = lax.broadcasted_iota(jnp.int32, (1, nch), 1)
        cmask = (c_i * CMP_STRIDE + (CMP_BLOCK - 1) <= t_h) & (c_i < n_cmp)
        imps = []
        for k in range(NSA_KV_HEADS):
            qk = qb_sc[pl.ds(k * rows_h, rows_h), :]
            s = jnp.where(cmask, _dot_nt(qk, kc_ref[k].astype(BF16)), -jnp.inf)
            m = jnp.max(s, axis=1, keepdims=True)
            m = jnp.where(m > -jnp.inf, m, 0.0)
            e = jnp.exp(s - m)
            d = jnp.sum(e, axis=1, keepdims=True)
            p = e / jnp.where(d > 0, d, 1.0)
            ocmp_sc[pl.ds(k * rows_h, rows_h), :] = _dot(p.astype(BF16), vc_ref[k].astype(BF16))
            psum = p[0:dl]
            for g in range(1, NSA_GROUP):
                psum = psum + p[g * dl:(g + 1) * dl]
            imps.append(jnp.dot(psum, cov_ref[...], preferred_element_type=F32,
                                precision=lax.Precision.HIGHEST))
        imp = jnp.concatenate(imps, axis=0)
        t_s = tok_pos(NSA_KV_HEADS * dl)
        n_i = lax.broadcasted_iota(jnp.int32, (1, nbl), 1)
        valid = (n_i * SLC_BLOCK <= t_s) & (n_i < n_slc)
        cur = t_s >> SLC_SHIFT
        forced = (n_i == 0) | (n_i == cur) | (n_i == cur - 1)
        prio = jnp.where(forced, jnp.inf, jnp.where(valid, imp, -jnp.inf))
        n_f = n_i.astype(F32)
        alive = jnp.broadcast_to(jnp.where(n_i < n_slc, 1.0, 0.0), prio.shape)
        sel = jnp.zeros(prio.shape, F32)
        for _ in range(n_sel):
            mx = jnp.max(jnp.where(alive > 0.5, prio, -jnp.inf), axis=1, keepdims=True)
            cand = (alive > 0.5) & (prio == mx)
            first = jnp.min(jnp.where(cand, n_f, float(nbl)), axis=1, keepdims=True)
            pick = n_f == first
            sel = jnp.where(pick, 1.0, sel)
            alive = jnp.where(pick, 0.0, alive)
        sel = jnp.where(valid, sel, 0.0)
        for k in range(NSA_KV_HEADS):
            for g in range(NSA_GROUP):
                sel_sc[pl.ds((k * NSA_GROUP + g) * dl, dl), :] = sel[k * dl:(k + 1) * dl]
        m_sc[...] = jnp.full(m_sc.shape, NEG_BIG, F32)
        l_sc[...] = jnp.zeros_like(l_sc)
        acc_sc[...] = jnp.zeros_like(acc_sc)

    selb = sel_sc[...].astype(BF16)
    n_col = lax.broadcasted_iota(jnp.int32, (nbl, 1), 0)

    def online_update(k, s, mask, v_tiles):
        rs = pl.ds(k * rows_h, rows_h)
        m_old = m_sc[rs, :]
        m_new = jnp.maximum(m_old, jnp.max(s, axis=1, keepdims=True))
        p = jnp.where(mask, jnp.exp(s - m_new), 0.0)
        alpha = jnp.exp(m_old - m_new)
        l_sc[rs, :] = alpha * l_sc[rs, :] + jnp.sum(p, axis=1, keepdims=True)
        pv = jnp.zeros((rows_h, NSA_HD), F32)
        for i, vt in enumerate(v_tiles):
            pv = pv + _dot(p[:, i * LANES:(i + 1) * LANES].astype(BF16), vt)
        acc_sc[rs, :] = alpha * acc_sc[rs, :] + pv
        m_sc[rs, :] = m_new

    t_h = tok_pos(rows_h)

    def selected(kpos):
        blk = jnp.where(n_col == (kpos >> SLC_SHIFT), 1.0, 0.0).astype(BF16)
        return _dot(selb, blk)

    kpos = step * (pg * page) + lax.broadcasted_iota(jnp.int32, (1, pg * page), 1)
    sel_all = selected(kpos)
    for k in range(NSA_KV_HEADS):
        cs = slice(k * NSA_HD, (k + 1) * NSA_HD)
        qk = qb_sc[pl.ds(k * rows_h, rows_h), :]
        mask = (sel_all[k * rows_h:(k + 1) * rows_h] > 0.5) & (kpos <= t_h)
        s = jnp.concatenate([_dot_nt(qk, r[:, cs].astype(BF16)) for r in kp_refs], axis=1)
        s = jnp.where(mask, s, NEG_BIG)
        online_update(k, s, mask, [r[:, cs].astype(BF16) for r in vp_refs])

    @pl.when(step == n_steps - 1)
    def _finish():
        padn = LANES - dl
        j_new = lax.broadcasted_iota(jnp.int32, (1, LANES), 1)
        kpos_n = q_off + j_new
        sel_n = selected(kpos_n)
        for k in range(NSA_KV_HEADS):
            kn = jnp.concatenate([kvn_ref[:, k * NSA_HD:(k + 1) * NSA_HD], jnp.zeros((padn, NSA_HD), F32)], axis=0)
            vn = jnp.concatenate([kvn_ref[:, (2 + k) * NSA_HD:(3 + k) * NSA_HD], jnp.zeros((padn, NSA_HD), F32)],
                                 axis=0)
            qk = qb_sc[pl.ds(k * rows_h, rows_h), :]
            mask = (sel_n[k * rows_h:(k + 1) * rows_h] > 0.5) & (kpos_n <= t_h) & (j_new < dl)
            s = jnp.where(mask, _dot_nt(qk, kn.astype(BF16)), NEG_BIG)
            online_update(k, s, mask, [vn.astype(BF16)])
        j_w = lax.broadcasted_iota(jnp.int32, (1, w_buf + LANES), 1)
        pos_w = q_off - w_buf + j_w
        dlt = t_h - pos_w
        wmask = (j_w < w_buf + dl) & (pos_w >= 0) & (dlt >= 0) & (dlt < WINDOW)
        gates = jax.nn.sigmoid(gate_ref[...])
        for k in range(NSA_KV_HEADS):
            kw = jnp.concatenate([cw_ref[:, k * NSA_HD:(k + 1) * NSA_HD],
                                  wn_ref[:, k * NSA_HD:(k + 1) * NSA_HD], jnp.zeros((padn, NSA_HD), F32)], axis=0)
            vw = jnp.concatenate([cw_ref[:, (2 + k) * NSA_HD:(3 + k) * NSA_HD],
                                  wn_ref[:, (2 + k) * NSA_HD:(3 + k) * NSA_HD], jnp.zeros((padn, NSA_HD), F32)],
                                 axis=0)
            qk = qb_sc[pl.ds(k * rows_h, rows_h), :]
            s = jnp.where(wmask, _dot_nt(qk, kw.astype(BF16)), -jnp.inf)
            m = jnp.max(s, axis=1, keepdims=True)
            m = jnp.where(m > -jnp.inf, m, 0.0)
            e = jnp.exp(s - m)
            d = jnp.sum(e, axis=1, keepdims=True)
            o_win = _dot((e / jnp.where(d > 0, d, 1.0)).astype(BF16), vw.astype(BF16))
            rs = pl.ds(k * rows_h, rows_h)
            o_slc = acc_sc[rs, :] / l_sc[rs, :]
            o_cmp = ocmp_sc[rs, :]
            for g in range(NSA_GROUP):
                h = k * NSA_GROUP + g
                r = slice(g * dl, (g + 1) * dl)
                gc = gates[:, h * N_BRANCH:h * N_BRANCH + 1]
                gs = gates[:, h * N_BRANCH + 1:h * N_BRANCH + 2]
                gw = gates[:, h * N_BRANCH + 2:h * N_BRANCH + 3]
                o_ref[:, h * NSA_HD:(h + 1) * NSA_HD] = gc * o_cmp[r] + gs * o_slc[r] + gw * o_win[r]


def _nsa_sample(P, CMP, cache_pages, cache_win, page_table, DB, DL, past_len, page):
    n_pages = page_table.shape[1]
    w_buf = cache_win.shape[1]
    lk = past_len + DL
    nch = CMP.shape[3]
    n_cmp = lk // CMP_STRIDE - CMP_BLOCK // CMP_STRIDE + 1
    n_slc = -(-lk // SLC_BLOCK)
    n_sel = min(N_SELECT, n_slc)
    nbl = _round_up(n_slc, LANES)
    pg = _pick_tile(n_pages, (16, 8, 4, 2, 1))
    n_steps = n_pages // pg
    assert DL % SUBLANES == 0 and DL <= LANES and page == LANES and past_len == n_pages * page
    cov = jnp.asarray(_cover_matrix(n_cmp, n_slc, nch, nbl))
    kvw = NSA_KV_HEADS * NSA_HD
    page_spec = lambda i, c: pl.BlockSpec((None, page, kvw), lambda b, s, pt: (pt[b, s * pg + i], 0, c))
    in_specs = [
        pl.BlockSpec((DL, NSA_HEADS * NSA_HD), lambda b, s, pt: (b, 3)),
        pl.BlockSpec((DL, 2 * kvw), lambda b, s, pt: (b, 9)),
        pl.BlockSpec((DL, 2 * kvw), lambda b, s, pt: (b, 10)),
        pl.BlockSpec((DL, LANES), lambda b, s, pt: (b, 44)),
        pl.BlockSpec((None, None, NSA_KV_HEADS, nch, NSA_HD), lambda b, s, pt: (b, 0, 0, 0, 0)),
        pl.BlockSpec((None, None, NSA_KV_HEADS, nch, NSA_HD), lambda b, s, pt: (b, 1, 0, 0, 0)),
        pl.BlockSpec((nch, nbl), lambda b, s, pt: (0, 0)),
        pl.BlockSpec((None, w_buf, 2 * kvw), lambda b, s, pt: (b, 0, 0)),
    ] + [page_spec(i, 2) for i in range(pg)] + [page_spec(i, 3) for i in range(pg)]
    rows = NSA_HEADS * DL
    grid_spec = pltpu.PrefetchScalarGridSpec(
        num_scalar_prefetch=1,
        grid=(DB, n_steps),
        in_specs=in_specs,
        out_specs=pl.BlockSpec((DL, NSA_HEADS * NSA_HD), lambda b, s, pt: (b, 0)),
        scratch_shapes=[
            pltpu.VMEM((rows, NSA_HD), BF16),
            pltpu.VMEM((rows, nbl), F32),
            pltpu.VMEM((rows, 1), F32),
            pltpu.VMEM((rows, 1), F32),
            pltpu.VMEM((rows, NSA_HD), F32),
            pltpu.VMEM((rows, NSA_HD), F32),
        ],
    )
    body = functools.partial(_nsa_sample_body, pg=pg, n_steps=n_steps, dl=DL, q_off=past_len, w_buf=w_buf,
                             n_cmp=n_cmp, n_slc=n_slc, n_sel=n_sel, nbl=nbl, page=page)
    return pl.pallas_call(
        body, grid_spec=grid_spec,
        out_shape=jax.ShapeDtypeStruct((DB * DL, NSA_HEADS * NSA_HD), F32),
        compiler_params=_cparams(("parallel", "arbitrary")),
        name="nsa_sample",
    )(page_table, P, P, P, P, CMP, CMP, cov, cache_win, *([cache_pages] * (2 * pg)))


def _out_ln_body(h_ref, ro_ref, no_ref, wr_ref, wn_ref, g_ref, b_ref, o_ref, *, alpha):
    m = _dot(ro_ref[...].astype(BF16), wr_ref[...]) + _dot(no_ref[...].astype(BF16), wn_ref[...])
    o_ref[...] = _layer_norm(alpha * h_ref[...] + m, g_ref[...], b_ref[...])


def _out_ln(h, ro, no, w_out, g, b, alpha):
    T, D = h.shape
    kr = ro.shape[1]
    kn = no.shape[1]
    tm = _pick_tile(T, (512, 256, 128, 64, 32, 16, 8))
    return pl.pallas_call(
        functools.partial(_out_ln_body, alpha=alpha),
        grid=(T // tm,),
        in_specs=[
            pl.BlockSpec((tm, D), lambda i: (i, 0)),
            pl.BlockSpec((tm, kr), lambda i: (i, 0)),
            pl.BlockSpec((tm, kn), lambda i: (i, 0)),
            pl.BlockSpec((kr, D), lambda i: (0, 0)),
            pl.BlockSpec((kn, D), lambda i: (1, 0)),
            pl.BlockSpec((1, D), lambda i: (0, 0)),
            pl.BlockSpec((1, D), lambda i: (0, 0)),
        ],
        out_specs=pl.BlockSpec((tm, D), lambda i: (i, 0)),
        out_shape=jax.ShapeDtypeStruct((T, D), F32),
        compiler_params=_cparams(("parallel",)),
        name="out_ln",
    )(h, ro, no, w_out, w_out, g.reshape(1, D), b.reshape(1, D))


def _rope_tables(pos):
    half = NSA_HD // 2
    inv = ROPE_THETA ** (-jnp.arange(half, dtype=F32) / half)
    ang = pos.astype(F32)[:, None] * inv[None, :]
    cos = jnp.cos(ang)
    sin = jnp.sin(ang)
    return jnp.concatenate([cos, cos], -1), jnp.concatenate([-sin, sin], -1)


def _cmp_weights(w1, pos, w2):
    r = CMP_BLOCK // CMP_STRIDE
    w1r = w1.reshape(r, CMP_STRIDE * NSA_HD, CMP_HIDDEN)
    w1c = jnp.concatenate([w1r[i] for i in range(r)], axis=1).astype(BF16)
    posr = jnp.pad(pos.reshape(r, CMP_STRIDE * NSA_HD), ((0, SUBLANES - r), (0, 0))).astype(BF16)
    return w1c, posr, w2.astype(BF16)


def _layer_view(arr, l, shape):
    return arr.reshape(shape) if arr.shape[0] == 1 else arr[l].reshape(shape)


def _decoder_layer(x, B, L, q_off, s0, p, sample_ctx):
    alpha = p['alpha']
    h1, h1_bf = _ffn_ln(x, p['ffn1_w_up'], p['ffn1_w_down'], p['ln1_g'], p['ln1_b'], alpha, True)
    cos, sin = _rope_tables(q_off + jnp.arange(L, dtype=jnp.int32))
    P = _proj(h1_bf, p['w_in'], p['rope_cols'], p['scale_cols'], cos, sin, L)
    ro, ret_s = _retention(P, s0, p['ret_gn_g'], p['ret_gn_b'], B, L)
    if sample_ctx is None:
        assert L % CMP_STRIDE == 0
        n_cmp = L // CMP_STRIDE - CMP_BLOCK // CMP_STRIDE + 1
        specs = [pl.BlockSpec((L, NSA_HD), (lambda hd: (lambda b, t, g: (b, 32 + NSA_KV_HEADS * t + hd)))(hd))
                 for hd in range(NSA_KV_HEADS)]
        CMP = _compress([P] * NSA_KV_HEADS, specs, 1, L, 1, n_cmp, B, p['cmp_w1'], p['cmp_pos'], p['cmp_w2'])
        no = _nsa_prompt(P, CMP, B, L)
    else:
        cache_pages, cache_win, page_table, past_len, page = sample_ctx
        n_pages = page_table.shape[1]
        lk = past_len + L
        assert (lk // CMP_STRIDE) * CMP_STRIDE <= past_len, "compression blocks must lie in the paged past"
        n_cmp = lk // CMP_STRIDE - CMP_BLOCK // CMP_STRIDE + 1
        n_in = _pick_tile(n_pages, (32, 16, 8, 4, 2, 1))
        n_grp = n_pages // n_in
        specs = [pl.BlockSpec((None, page, NSA_HD),
                              (lambda i, hd: (lambda b, t, g, pt: (pt[b, (n_grp - 1 - g) * n_in + i], 0,
                                                                   NSA_KV_HEADS * t + hd)))(i, hd))
                 for hd in range(NSA_KV_HEADS) for i in range(n_in)]
        CMP = _compress([cache_pages] * (NSA_KV_HEADS * n_in), specs, n_in, page, n_grp, n_cmp, B,
                        p['cmp_w1'], p['cmp_pos'], p['cmp_w2'], page_table=page_table)
        no = _nsa_sample(P, CMP, cache_pages, cache_win, page_table, B, L, past_len, page)
    x2 = _out_ln(h1, ro, no, p['w_out'], p['ln2_g'], p['ln2_b'], alpha)
    y, _ = _ffn_ln(x2, p['ffn2_w_up'], p['ffn2_w_down'], p['ln3_g'], p['ln3_b'], alpha, False)
    return y, ret_s, P


def kernel(x_prompt, x_sample, state_ret, cache_nsa_kv, cache_win, page_table, ffn1_w_up, ffn1_w_down, ln1_g, ln1_b, w_in, w_out, ret_gn_g, ret_gn_b, cmp_pos_k, cmp_w1_k, cmp_w2_k, cmp_pos_v, cmp_w1_v, cmp_w2_v, ln2_g, ln2_b, ffn2_w_up, ffn2_w_down, ln3_g, ln3_b):
    B, L, D = x_prompt.shape
    DB, DL, _ = x_sample.shape
    depth = w_in.shape[0]
    n_pool, page = cache_nsa_kv.shape[1], cache_nsa_kv.shape[2]
    n_pages = page_table.shape[1]
    past_len = n_pages * page
    w_buf = cache_win.shape[2]
    alpha = (2.0 * depth) ** 0.25
    kv_cols = 4 * NSA_KV_HEADS * NSA_HD
    win_cols = 2 * NSA_KV_HEADS * NSA_HD
    rope_np = np.zeros((N_IN_PAD // LANES, LANES), np.float32)
    rope_np[list(ROPE_CHUNKS)] = 1.0
    scale_np = np.ones((N_IN_PAD // LANES, LANES), np.float32)
    scale_np[list(KSCALE_CHUNKS)] = RET_DK ** -0.5
    rope_cols = jnp.asarray(rope_np.reshape(1, N_IN_PAD))
    scale_cols = jnp.asarray(scale_np.reshape(1, N_IN_PAD))

    yp = x_prompt.reshape(B * L, D)
    ys = x_sample.reshape(DB * DL, D)
    outs = [[] for _ in range(6)]
    for l in range(depth):
        k1, p1, k2 = _cmp_weights(cmp_w1_k[l], cmp_pos_k[l], cmp_w2_k[l])
        v1, q1, v2 = _cmp_weights(cmp_w1_v[l], cmp_pos_v[l], cmp_w2_v[l])
        p = {
            'alpha': alpha, 'rope_cols': rope_cols, 'scale_cols': scale_cols,
            'ffn1_w_up': ffn1_w_up[l].astype(BF16), 'ffn1_w_down': ffn1_w_down[l].astype(BF16),
            'ln1_g': ln1_g[l], 'ln1_b': ln1_b[l],
            'w_in': jnp.pad(w_in[l], ((0, 0), (0, N_IN_PAD - N_IN))).astype(BF16),
            'w_out': w_out[l].astype(BF16),
            'ret_gn_g': ret_gn_g[l], 'ret_gn_b': ret_gn_b[l],
            'cmp_w1': jnp.stack([k1, v1]), 'cmp_pos': jnp.stack([p1, q1]), 'cmp_w2': jnp.stack([k2, v2]),
            'ln2_g': ln2_g[l], 'ln2_b': ln2_b[l],
            'ffn2_w_up': ffn2_w_up[l].astype(BF16), 'ffn2_w_down': ffn2_w_down[l].astype(BF16),
            'ln3_g': ln3_g[l], 'ln3_b': ln3_b[l],
        }
        s0 = jnp.zeros((B, RET_HEADS, RET_DK, RET_DV), F32)
        yp, rs_p, P_p = _decoder_layer(yp, B, L, 0, s0, p, None)
        ctx = (cache_nsa_kv.reshape(depth * n_pool, page, kv_cols),
               _layer_view(cache_win, l, (DB, w_buf, win_cols)), page_table + l * n_pool, past_len, page)
        ys, rs_s, P_s = _decoder_layer(ys, DB, DL, past_len, _layer_view(state_ret, l, state_ret.shape[1:]), p,
                                       ctx)
        kv0 = 32 * LANES
        w0 = 40 * LANES
        P_p3 = P_p.reshape(B, L, N_IN_PAD)
        P_s3 = P_s.reshape(DB, DL, N_IN_PAD)
        wl = min(WINDOW, L)
        outs[0].append(rs_p)
        outs[1].append(rs_s)
        outs[2].append(P_p3[:, :, kv0:kv0 + kv_cols].reshape(B, L, 4, NSA_KV_HEADS, NSA_HD))
        outs[3].append(P_s3[:, :, kv0:kv0 + kv_cols].reshape(DB, DL, 4, NSA_KV_HEADS, NSA_HD))
        outs[4].append(P_p3[:, L - wl:, w0:w0 + win_cols].reshape(B, wl, 2, NSA_KV_HEADS, NSA_HD))
        win_s = P_s3[:, :, w0:w0 + win_cols].reshape(DB, DL, 2, NSA_KV_HEADS, NSA_HD)
        outs[5].append(jnp.concatenate([cache_win[l], win_s], axis=1)[:, -w_buf:])
    return (yp.reshape(B, L, D), ys.reshape(DB, DL, D), jnp.stack(outs[0]), jnp.stack(outs[1]),
            jnp.stack(outs[2]), jnp.stack(outs[3]), jnp.stack(outs[4]), jnp.stack(outs[5]))
```

```python
import functools

import numpy as np
import jax
import jax.numpy as jnp
from jax import lax
from jax.experimental import pallas as pl
from jax.experimental.pallas import tpu as pltpu

F32 = jnp.float32
BF16 = jnp.bfloat16

LANES = 128
SUBLANES = 8
VMEM_LIMIT_BYTES = 56 * 1024 * 1024

RET_HEADS = 4
RET_DK = 128
RET_DV = 256
RET_CHUNK = 128
NSA_HEADS = 8
NSA_KV_HEADS = 2
NSA_HD = 128
NSA_GROUP = NSA_HEADS // NSA_KV_HEADS
CMP_BLOCK = 32
CMP_STRIDE = 16
CMP_HIDDEN = 2 * NSA_HD
SLC_BLOCK = 64
SLC_SHIFT = 6
N_SELECT = 16
WINDOW = 512
N_BRANCH = 3
ROPE_THETA = 10000.0
LN_EPS = 1e-5
NEG_BIG = -1e30
LOG2_E = 1.4426950408889634

N_IN = 5656
N_IN_PAD = 5760
ROPE_CHUNKS = tuple(range(0, 8)) + tuple(range(24, 32)) + (32, 33, 36, 37, 40, 41)
KSCALE_CHUNKS = tuple(range(4, 8))


def _cparams(sem):
    return pltpu.CompilerParams(dimension_semantics=sem, vmem_limit_bytes=VMEM_LIMIT_BYTES)


def _pick_tile(n, candidates):
    for c in candidates:
        if n % c == 0:
            return c
    return n


def _round_up(n, m):
    return (n + m - 1) // m * m


def _layer_norm(z, g, b):
    mu = jnp.mean(z, axis=-1, keepdims=True)
    zc = z - mu
    var = jnp.mean(zc * zc, axis=-1, keepdims=True)
    return zc * lax.rsqrt(var + LN_EPS) * g + b


def _dot(a, b):
    return jnp.dot(a, b, preferred_element_type=F32)


def _dot_nt(a, b):
    return lax.dot_general(a, b, (((1,), (1,)), ((), ())), preferred_element_type=F32)


def _ffn_ln_body(x_ref, wa_ref, wb_ref, wd_ref, g_ref, b_ref, *rest, nj, alpha, emit_bf16):
    if emit_bf16:
        o_ref, obf_ref, xbf_sc, acc_sc = rest
    else:
        o_ref, xbf_sc, acc_sc = rest
    j = pl.program_id(1)

    @pl.when(j == 0)
    def _init():
        xbf_sc[...] = x_ref[...].astype(BF16)
        acc_sc[...] = jnp.zeros_like(acc_sc)

    xb = xbf_sc[...]
    a = _dot(xb, wa_ref[...])
    b = _dot(xb, wb_ref[...])
    h = (a * jax.nn.sigmoid(a)) * b
    acc_sc[...] += _dot(h.astype(BF16), wd_ref[...])

    @pl.when(j == nj - 1)
    def _finish():
        z = alpha * x_ref[...] + 0.5 * acc_sc[...]
        y = _layer_norm(z, g_ref[...], b_ref[...])
        o_ref[...] = y
        if emit_bf16:
            obf_ref[...] = y.astype(BF16)


def _ffn_ln(x, w_up, w_down, g, b, alpha, emit_bf16):
    T, D = x.shape
    F = w_down.shape[0]
    tm = _pick_tile(T, (512, 256, 128, 64, 32, 16, 8))
    tf = _pick_tile(F, (512, 256, 128))
    nj = F // tf
    out_shape = [jax.ShapeDtypeStruct((T, D), F32)]
    out_specs = [pl.BlockSpec((tm, D), lambda i, j: (i, 0))]
    if emit_bf16:
        out_shape.append(jax.ShapeDtypeStruct((T, D), BF16))
        out_specs.append(pl.BlockSpec((tm, D), lambda i, j: (i, 0)))
    res = pl.pallas_call(
        functools.partial(_ffn_ln_body, nj=nj, alpha=alpha, emit_bf16=emit_bf16),
        grid=(T // tm, nj),
        in_specs=[
            pl.BlockSpec((tm, D), lambda i, j: (i, 0)),
            pl.BlockSpec((D, tf), lambda i, j: (0, j)),
            pl.BlockSpec((D, tf), lambda i, j: (0, nj + j)),
            pl.BlockSpec((tf, D), lambda i, j: (j, 0)),
            pl.BlockSpec((1, D), lambda i, j: (0, 0)),
            pl.BlockSpec((1, D), lambda i, j: (0, 0)),
        ],
        out_specs=out_specs,
        out_shape=out_shape,
        scratch_shapes=[pltpu.VMEM((tm, D), BF16), pltpu.VMEM((tm, D), F32)],
        compiler_params=_cparams(("parallel", "arbitrary")),
        name="ffn_ln",
    )(x, w_up, w_up, w_down, g.reshape(1, D), b.reshape(1, D))
    return res if emit_bf16 else (res[0], None)


def _proj_body(x_ref, w_ref, cos_ref, sin_ref, rope_ref, scale_ref, o_ref, *, n_chunk):
    y = _dot(x_ref[...], w_ref[...])
    cos = cos_ref[...]
    sin = sin_ref[...]
    for c in range(n_chunk):
        sl = slice(c * LANES, (c + 1) * LANES)
        yc = y[:, sl]
        roped = yc * cos + pltpu.roll(yc, NSA_HD // 2, 1) * sin
        o_ref[:, sl] = jnp.where(rope_ref[:, sl] > 0.5, roped, yc) * scale_ref[:, sl]


def _proj(x_bf, w_bf, rope_cols, scale_cols, cos, sin, rows_per_seq):
    T, D = x_bf.shape
    N = w_bf.shape[1]
    tm = _pick_tile(T, (512, 256, 128, 64, 32, 16, 8))
    tn = 1920
    assert N % tn == 0
    n_chunk = tn // LANES
    if rows_per_seq >= tm:
        assert rows_per_seq % tm == 0
        n_tab = rows_per_seq // tm
    else:
        assert tm % rows_per_seq == 0
        cos = jnp.tile(cos, (tm // rows_per_seq, 1))
        sin = jnp.tile(sin, (tm // rows_per_seq, 1))
        n_tab = 1
    return pl.pallas_call(
        functools.partial(_proj_body, n_chunk=n_chunk),
        grid=(T // tm, N // tn),
        in_specs=[
            pl.BlockSpec((tm, D), lambda i, j: (i, 0)),
            pl.BlockSpec((D, tn), lambda i, j: (0, j)),
            pl.BlockSpec((tm, LANES), lambda i, j: (i % n_tab, 0)),
            pl.BlockSpec((tm, LANES), lambda i, j: (i % n_tab, 0)),
            pl.BlockSpec((1, tn), lambda i, j: (0, j)),
            pl.BlockSpec((1, tn), lambda i, j: (0, j)),
        ],
        out_specs=pl.BlockSpec((tm, tn), lambda i, j: (i, j)),
        out_shape=jax.ShapeDtypeStruct((T, N), F32),
        compiler_params=_cparams(("parallel", "arbitrary")),
        name="proj_rope",
    )(x_bf, w_bf, cos, sin, rope_cols, scale_cols)


def _ret_body(q_ref, k_ref, v_ref, g_ref, s0_ref, dm_ref, ind_ref, std_ref, cd_ref, gng_ref, gnb_ref,
              o_ref, sout_ref, s_sc, *, nc, rows, rows_pad):
    c = pl.program_id(1)

    @pl.when(c == 0)
    def _load_state():
        s_sc[...] = s0_ref[...]

    pad = rows_pad - rows
    for h in range(RET_HEADS):
        ks = slice(h * RET_DK, (h + 1) * RET_DK)
        vs = slice(h * RET_DV, (h + 1) * RET_DV)
        q = q_ref[:, ks]
        k = k_ref[:, ks]
        v = v_ref[:, vs]
        kd = k * std_ref[h]
        if pad:
            k = jnp.concatenate([k, jnp.zeros((pad, RET_DK), F32)], axis=0)
            kd = jnp.concatenate([kd, jnp.zeros((pad, RET_DK), F32)], axis=0)
            v = jnp.concatenate([v, jnp.zeros((pad, RET_DV), F32)], axis=0)
        s_old = s_sc[h]
        vb = v.astype(BF16)
        a = _dot_nt(q.astype(BF16), k.astype(BF16)) * dm_ref[h]
        o = _dot(a.astype(BF16), vb) + _dot((q * ind_ref[h]).astype(BF16), s_old.astype(BF16))
        s_sc[h] = s_old * cd_ref[h] + _dot(kd.T.astype(BF16), vb)
        mu = jnp.mean(o, axis=-1, keepdims=True)
        oc = o - mu
        var = jnp.mean(oc * oc, axis=-1, keepdims=True)
        on = oc * lax.rsqrt(var + LN_EPS) * gng_ref[:, vs] + gnb_ref[:, vs]
        gate = g_ref[:, vs]
        o_ref[:, vs] = (gate * jax.nn.sigmoid(gate)) * on

    @pl.when(c == nc - 1)
    def _store_state():
        sout_ref[...] = s_sc[...]


def _retention(P, s0, gn_g, gn_b, B, L):
    C = RET_CHUNK if L % RET_CHUNK == 0 else L
    nc = L // C
    CP = max(C, LANES)
    lg = jnp.log1p(-jnp.exp2(-5.0 - jnp.arange(RET_HEADS, dtype=F32)))
    i = jnp.arange(C, dtype=F32)
    diff = i[:, None] - i[None, :]
    dmask = jnp.where(diff >= 0, jnp.exp(lg[:, None, None] * jnp.maximum(diff, 0.0)), 0.0)
    dmask = jnp.pad(dmask, ((0, 0), (0, 0), (0, CP - C)))
    in_decay = jnp.broadcast_to(jnp.exp(lg[:, None] * (i + 1.0))[:, :, None], (RET_HEADS, C, RET_DK))
    st_decay = jnp.broadcast_to(jnp.exp(lg[:, None] * (C - 1.0 - i))[:, :, None], (RET_HEADS, C, RET_DK))
    chunk_decay = jnp.broadcast_to(jnp.exp(lg * C)[:, None, None], (RET_HEADS, 1, RET_DV))
    qw = RET_HEADS * RET_DK
    vw = RET_HEADS * RET_DV
    const3 = lambda b, c: (0, 0, 0)
    return pl.pallas_call(
        functools.partial(_ret_body, nc=nc, rows=C, rows_pad=CP),
        grid=(B, nc),
        in_specs=[
            pl.BlockSpec((C, qw), lambda b, c: (b * nc + c, 0)),
            pl.BlockSpec((C, qw), lambda b, c: (b * nc + c, 1)),
            pl.BlockSpec((C, vw), lambda b, c: (b * nc + c, 1)),
            pl.BlockSpec((C, vw), lambda b, c: (b * nc + c, 2)),
            pl.BlockSpec((None, RET_HEADS, RET_DK, RET_DV), lambda b, c: (b, 0, 0, 0)),
            pl.BlockSpec((RET_HEADS, C, CP), const3),
            pl.BlockSpec((RET_HEADS, C, RET_DK), const3),
            pl.BlockSpec((RET_HEADS, C, RET_DK), const3),
            pl.BlockSpec((RET_HEADS, 1, RET_DV), const3),
            pl.BlockSpec((1, vw), lambda b, c: (0, 0)),
            pl.BlockSpec((1, vw), lambda b, c: (0, 0)),
        ],
        out_specs=[
            pl.BlockSpec((C, vw), lambda b, c: (b * nc + c, 0)),
            pl.BlockSpec((None, RET_HEADS, RET_DK, RET_DV), lambda b, c: (b, 0, 0, 0)),
        ],
        out_shape=[
            jax.ShapeDtypeStruct((B * L, vw), F32),
            jax.ShapeDtypeStruct((B, RET_HEADS, RET_DK, RET_DV), F32),
        ],
        scratch_shapes=[pltpu.VMEM((RET_HEADS, RET_DK, RET_DV), F32)],
        compiler_params=_cparams(("parallel", "arbitrary")),
        name="retention",
    )(P, P, P, P, s0, dmask, in_decay, st_decay, chunk_decay, gn_g.reshape(1, vw), gn_b.reshape(1, vw))


def _cmp_body(*refs, n_in, rows, n_grp, n_cmp, paged):
    if paged:
        refs = refs[1:]
    n_src = n_in if paged else NSA_KV_HEADS * n_in
    x_refs = refs[:n_src]
    w1_ref, pos_ref, w2_ref, o_ref, carry_sc = refs[n_src:]
    g = pl.program_id(2)
    grp = n_grp - 1 - g
    cpi = rows // CMP_STRIDE
    M = n_in * cpi

    @pl.when(g == 0)
    def _init():
        carry_sc[...] = jnp.zeros_like(carry_sc)

    w1 = w1_ref[...]
    gp = _dot(pos_ref[...], w1)
    posterm = gp[0:1, :CMP_HIDDEN] + gp[1:2, CMP_HIDDEN:]
    row = lax.broadcasted_iota(jnp.int32, (M, 1), 0)
    for hd in range(NSA_KV_HEADS):
        blocks = []
        if paged:
            for x_ref in x_refs:
                blocks.append(jnp.concatenate(
                    [x_ref[pl.ds(s, cpi, stride=CMP_STRIDE), hd, :] for s in range(CMP_STRIDE)], axis=1))
        else:
            for x_ref in x_refs[hd * n_in:(hd + 1) * n_in]:
                blocks.append(jnp.concatenate(
                    [x_ref[pl.ds(s, cpi, stride=CMP_STRIDE), :] for s in range(CMP_STRIDE)], axis=1))
        xc = blocks[0] if n_in == 1 else jnp.concatenate(blocks, axis=0)
        gg = _dot(xc.astype(BF16), w1)
        g0 = gg[:, :CMP_HIDDEN]
        g1 = gg[:, CMP_HIDDEN:]
        nxt = pltpu.roll(g1, M - 1, 0)
        nxt = jnp.where(row == M - 1, carry_sc[hd][0:1, :], nxt)
        carry_sc[hd] = g1[0:SUBLANES, :]
        hid = g0 + nxt + posterm
        out = _dot(jax.nn.gelu(hid).astype(BF16), w2_ref[...])
        o_ref[hd] = jnp.where(grp * M + row < n_cmp, out, 0.0)


def _compress(srcs, src_specs, n_in, rows, n_grp, n_cmp, B, w1, pos, w2, page_table=None):
    M = n_in * rows // CMP_STRIDE
    nch = n_grp * M
    paged = page_table is not None
    extra = (lambda *a: a[:3]) if paged else (lambda *a: a)
    in_specs = list(src_specs) + [
        pl.BlockSpec((None, CMP_STRIDE * NSA_HD, 2 * CMP_HIDDEN), lambda *a: (extra(*a)[1], 0, 0)),
        pl.BlockSpec((None, SUBLANES, CMP_STRIDE * NSA_HD), lambda *a: (extra(*a)[1], 0, 0)),
        pl.BlockSpec((None, CMP_HIDDEN, NSA_HD), lambda *a: (extra(*a)[1], 0, 0)),
    ]
    out_spec = pl.BlockSpec((None, None, NSA_KV_HEADS, M, NSA_HD),
                            lambda *a: (extra(*a)[0], extra(*a)[1], 0, n_grp - 1 - extra(*a)[2], 0))
    body = functools.partial(_cmp_body, n_in=n_in, rows=rows, n_grp=n_grp, n_cmp=n_cmp, paged=paged)
    out_shape = jax.ShapeDtypeStruct((B, 2, NSA_KV_HEADS, nch, NSA_HD), F32)
    scratch = [pltpu.VMEM((NSA_KV_HEADS, SUBLANES, CMP_HIDDEN), F32)]
    sem = ("parallel", "arbitrary", "arbitrary")
    if paged:
        grid_spec = pltpu.PrefetchScalarGridSpec(
            num_scalar_prefetch=1, grid=(B, 2, n_grp), in_specs=in_specs, out_specs=out_spec,
            scratch_shapes=scratch)
        return pl.pallas_call(body, grid_spec=grid_spec, out_shape=out_shape,
                              compiler_params=_cparams(sem), name="nsa_compress_paged")(
            page_table, *srcs, w1, pos, w2)
    return pl.pallas_call(body, grid=(B, 2, n_grp), in_specs=in_specs, out_specs=out_spec,
                          out_shape=out_shape, scratch_shapes=scratch,
                          compiler_params=_cparams(sem), name="nsa_compress")(*srcs, w1, pos, w2)


def _cover_matrix(n_cmp, n_slc, rows, cols):
    c_i = np.arange(n_cmp)[:, None]
    n_i = np.arange(n_slc)[None, :]
    cov = np.clip(np.minimum(c_i * CMP_STRIDE + CMP_BLOCK, (n_i + 1) * SLC_BLOCK)
                  - np.maximum(c_i * CMP_STRIDE, n_i * SLC_BLOCK), 0, None).astype(np.float32) / CMP_BLOCK
    out = np.zeros((rows, cols), np.float32)
    out[:n_cmp, :n_slc] = cov
    return out


def _nsa_prompt_body(q_ref, kc_ref, vc_ref, ks_ref, vs_ref, kw_ref, vw_ref, gate_ref, covt_ref, o_ref,
                     kcb_sc, vct_sc, ksb_sc, vst_sc, kwb_sc, vwt_sc, prio_sc, sel_sc, gt_sc,
                     m_sc, l_sc, acc_sc, out_sc, s_sc, *, tq, L, n_cmp, n_slc, n_sel, nbp):
    kvh = pl.program_id(1)
    qi = pl.program_id(2)
    tk = tq
    nch = kc_ref.shape[0]
    scale = NSA_HD ** -0.5

    @pl.when(qi == 0)
    def _stage_kv():
        kcb_sc[...] = kc_ref[...].astype(BF16)
        vct_sc[...] = vc_ref[...].T.astype(BF16)
        ksb_sc[...] = ks_ref[...].astype(BF16)
        kwb_sc[...] = kw_ref[...].astype(BF16)
        for i in range(L // tk):
            vst_sc[i] = vs_ref[i * tk:(i + 1) * tk, :].T.astype(BF16)
            vwt_sc[i] = vw_ref[i * tk:(i + 1) * tk, :].T.astype(BF16)

    G = NSA_GROUP
    W = G * tq
    t0 = qi * tq
    t_row = t0 + lax.broadcasted_iota(jnp.int32, (1, tq), 1)
    lane = lax.broadcasted_iota(jnp.int32, (1, W), 1)
    t_all = t0 + (lane & (tq - 1))
    qcat = jnp.concatenate([q_ref[:, g * NSA_HD:(g + 1) * NSA_HD] for g in range(G)], axis=0)
    qcat = (qcat * (scale * LOG2_E)).astype(BF16)
    gt_sc[...] = jax.nn.sigmoid(gate_ref[...]).T

    def gate_row(br):
        return jnp.concatenate(
            [gt_sc[pl.ds((kvh * G + g) * N_BRANCH + br, 1), :] for g in range(G)], axis=1)

    c_i = lax.broadcasted_iota(jnp.int32, (nch, 1), 0)
    cmask = (c_i * CMP_STRIDE + (CMP_BLOCK - 1) <= t_all) & (c_i < n_cmp)
    s = jnp.where(cmask, _dot_nt(kcb_sc[...], qcat), -jnp.inf)
    m = jnp.max(s, axis=0, keepdims=True)
    m = jnp.where(m > -jnp.inf, m, 0.0)
    e = jnp.exp2(s - m)
    d = jnp.sum(e, axis=0, keepdims=True)
    p = e / jnp.where(d > 0, d, 1.0)
    out_sc[...] = gate_row(0) * _dot(vct_sc[...], p.astype(BF16))
    psum = p[:, 0:tq]
    for g in range(1, G):
        psum = psum + p[:, g * tq:(g + 1) * tq]

    imp = jnp.dot(covt_ref[...], psum, preferred_element_type=F32, precision=lax.Precision.HIGHEST)
    n_i = lax.broadcasted_iota(jnp.int32, (nbp, 1), 0)
    valid = (n_i * SLC_BLOCK <= t_row) & (n_i < n_slc)
    cur = t_row >> SLC_SHIFT
    forced = (n_i == 0) | (n_i == cur) | (n_i == cur - 1)
    prio = jnp.where(forced, jnp.inf, jnp.where(valid, imp, -jnp.inf))
    prio_sc[...] = prio
    cnt = jnp.zeros((nbp, tq), jnp.int32)
    for mm in range(n_slc):
        pm = prio_sc[mm:mm + 1, :]
        beats = (pm > prio) | ((pm == prio) & (n_i > mm))
        cnt = cnt + beats.astype(jnp.int32)
    selbias = jnp.where((cnt < n_sel) & valid, 0.0, NEG_BIG)
    sel_sc[...] = jnp.concatenate([selbias] * G, axis=1)

    kp_l = lax.broadcasted_iota(jnp.int32, (tk, 1), 0)
    t_l = lane & (tq - 1)
    causal_bias = jnp.where(kp_l <= t_l, 0.0, NEG_BIG)
    far_bias = jnp.where(kp_l > t_l, 0.0, NEG_BIG)

    def reset():
        m_sc[...] = jnp.full((1, W), NEG_BIG, F32)
        l_sc[...] = jnp.zeros((1, W), F32)
        acc_sc[...] = jnp.zeros((NSA_HD, W), F32)

    def score(kb_sc, kt, bias):
        k0 = pl.multiple_of(kt * tk, tk)
        s = _dot_nt(kb_sc[pl.ds(k0, tk), :], qcat)
        return s if bias is None else s + bias

    def update(vt_sc, kt, s):
        m_old = m_sc[...]
        m_new = jnp.maximum(m_old, jnp.max(s, axis=0, keepdims=True))
        p = jnp.exp2(s - m_new)
        alpha = jnp.exp2(m_old - m_new)
        l_sc[...] = alpha * l_sc[...] + jnp.sum(p, axis=0, keepdims=True)
        acc_sc[...] = alpha * acc_sc[...] + _dot(vt_sc[kt], p.astype(BF16))
        m_sc[...] = m_new

    def sel_bias(kt):
        per_tile = tk // SLC_BLOCK
        rows = [jnp.broadcast_to(sel_sc[pl.ds(kt * per_tile + r, 1), :], (SLC_BLOCK, W))
                for r in range(per_tile)]
        return jnp.concatenate(rows, axis=0)

    def flush(br):
        out_sc[...] += (gate_row(br) / l_sc[...]) * acc_sc[...]

    reset()
    s_sc[...] = score(ksb_sc, 0, sel_bias(0))

    def slc_body(kt, carry):
        s_next = score(ksb_sc, kt + 1, sel_bias(kt + 1))
        update(vst_sc, kt, s_sc[...])
        s_sc[...] = s_next
        return carry

    lax.fori_loop(0, qi, slc_body, 0)
    update(vst_sc, qi, s_sc[...] + causal_bias)
    flush(1)

    reset()
    far = WINDOW // tk
    tiles = []
    for back in range(far, -1, -1):
        bias = far_bias if back == far else (causal_bias if back == 0 else None)
        if back > 0:
            off = jnp.where(qi >= back, 0.0, NEG_BIG).astype(F32)
            bias = off if bias is None else bias + off
        kt = jnp.maximum(qi - back, 0)
        tiles.append((kt, score(kwb_sc, kt, bias)))
    for kt, s_w in tiles:
        update(vwt_sc, kt, s_w)
    flush(2)
    for g in range(G):
        o_ref[:, g * NSA_HD:(g + 1) * NSA_HD] = out_sc[:, g * tq:(g + 1) * tq].T


def _nsa_prompt(P, CMP, B, L):
    tq = _pick_tile(L, (256, 128))
    assert L % tq == 0 and tq % SLC_BLOCK == 0 and WINDOW % tq == 0 and tq & (tq - 1) == 0
    nq = L // tq
    nch = CMP.shape[3]
    n_cmp = L // CMP_STRIDE - CMP_BLOCK // CMP_STRIDE + 1
    n_slc = -(-L // SLC_BLOCK)
    n_sel = min(N_SELECT, n_slc)
    nbp = _round_up(n_slc, SUBLANES)
    covt = jnp.asarray(_cover_matrix(n_cmp, n_slc, nch, nbp).T)
    gw = NSA_GROUP * NSA_HD
    col = lambda c: (lambda b, k, i: (b, c + k))
    body = functools.partial(_nsa_prompt_body, tq=tq, L=L, n_cmp=n_cmp, n_slc=n_slc, n_sel=n_sel, nbp=nbp)
    return pl.pallas_call(
        body,
        grid=(B, NSA_KV_HEADS, nq),
        in_specs=[
            pl.BlockSpec((tq, gw), lambda b, k, i: (b * nq + i, 6 + k)),
            pl.BlockSpec((None, None, None, nch, NSA_HD), lambda b, k, i: (b, 0, k, 0, 0)),
            pl.BlockSpec((None, None, None, nch, NSA_HD), lambda b, k, i: (b, 1, k, 0, 0)),
            pl.BlockSpec((L, NSA_HD), col(36)),
            pl.BlockSpec((L, NSA_HD), col(38)),
            pl.BlockSpec((L, NSA_HD), col(40)),
            pl.BlockSpec((L, NSA_HD), col(42)),
            pl.BlockSpec((tq, LANES), lambda b, k, i: (b * nq + i, 44)),
            pl.BlockSpec((nbp, nch), lambda b, k, i: (0, 0)),
        ],
        out_specs=pl.BlockSpec((tq, gw), lambda b, k, i: (b * nq + i, k)),
        out_shape=jax.ShapeDtypeStruct((B * L, NSA_HEADS * NSA_HD), F32),
        scratch_shapes=[
            pltpu.VMEM((nch, NSA_HD), BF16),
            pltpu.VMEM((NSA_HD, nch), BF16),
            pltpu.VMEM((L, NSA_HD), BF16),
            pltpu.VMEM((L // tq, NSA_HD, tq), BF16),
            pltpu.VMEM((L, NSA_HD), BF16),
            pltpu.VMEM((L // tq, NSA_HD, tq), BF16),
            pltpu.VMEM((nbp, tq), F32),
            pltpu.VMEM((nbp, NSA_GROUP * tq), F32),
            pltpu.VMEM((LANES, tq), F32),
            pltpu.VMEM((1, NSA_GROUP * tq), F32),
            pltpu.VMEM((1, NSA_GROUP * tq), F32),
            pltpu.VMEM((NSA_HD, NSA_GROUP * tq), F32),
            pltpu.VMEM((NSA_HD, NSA_GROUP * tq), F32),
            pltpu.VMEM((tq, NSA_GROUP * tq), F32),
        ],
        compiler_params=_cparams(("parallel", "arbitrary", "arbitrary")),
        name="nsa_prompt",
    )(P, CMP, CMP, P, P, P, P, P, covt)


def _head_rows(ref, j):
    rows = ref.shape[0]
    n = int(np.prod(ref.shape[1:-1]))
    return ref.reshape(n * rows, ref.shape[-1])[pl.ds(j, rows, stride=n), :]


def _nsa_sample_body(*refs, pg, n_steps, dl, q_off, w_buf, n_cmp, n_slc, n_sel, nbl, page):
    refs = refs[1:]
    q_ref, kvn_ref, wn_ref, gate_ref, kc_ref, vc_ref, cov_ref, cw_ref = refs[:8]
    kp_refs = refs[8:8 + pg]
    vp_refs = refs[8 + pg:8 + 2 * pg]
    o_ref, qb_sc, sel_sc, m_sc, l_sc, acc_sc, ocmp_sc = refs[8 + 2 * pg:]
    step = pl.program_id(1)
    rows_h = NSA_GROUP * dl
    rows = NSA_KV_HEADS * rows_h
    scale = NSA_HD ** -0.5
    nch = kc_ref.shape[1]

    def tok_pos(n):
        r = lax.broadcasted_iota(jnp.int32, (n, 1), 0)
        return q_off + r % dl

    @pl.when(step == 0)
    def _select():
        for k in range(NSA_KV_HEADS):
            for g in range(NSA_GROUP):
                h = k * NSA_GROUP + g
                qb_sc[pl.ds(h * dl, dl), :] = (q_ref[:, h * NSA_HD:(h + 1) * NSA_HD] * scale).astype(BF16)
        t_h = tok_pos(rows_h)
        c_i = lax.broadcasted_iota(jnp.int32, (1, nch), 1)
        cmask = (c_i * CMP_STRIDE + (CMP_BLOCK - 1) <= t_h) & (c_i < n_cmp)
        imps = []
        for k in range(NSA_KV_HEADS):
            qk = qb_sc[pl.ds(k * rows_h, rows_h), :]
            s = jnp.where(cmask, _dot_nt(qk, kc_ref[k].astype(BF16)), -jnp.inf)
            m = jnp.max(s, axis=1, keepdims=True)
            m = jnp.where(m > -jnp.inf, m, 0.0)
            e = jnp.exp(s - m)
            d = jnp.sum(e, axis=1, keepdims=True)
            p = e / jnp.where(d > 0, d, 1.0)
            ocmp_sc[pl.ds(k * rows_h, rows_h), :] = _dot(p.astype(BF16), vc_ref[k].astype(BF16))
            psum = p[0:dl]
            for g in range(1, NSA_GROUP):
                psum = psum + p[g * dl:(g + 1) * dl]
            imps.append(jnp.dot(psum, cov_ref[...], preferred_element_type=F32,
                                precision=lax.Precision.HIGHEST))
        imp = jnp.concatenate(imps, axis=0)
        t_s = tok_pos(NSA_KV_HEADS * dl)
        n_i = lax.broadcasted_iota(jnp.int32, (1, nbl), 1)
        valid = (n_i * SLC_BLOCK <= t_s) & (n_i < n_slc)
        cur = t_s >> SLC_SHIFT
        forced = (n_i == 0) | (n_i == cur) | (n_i == cur - 1)
        prio = jnp.where(forced, jnp.inf, jnp.where(valid, imp, -jnp.inf))
        n_f = n_i.astype(F32)
        alive = jnp.broadcast_to(jnp.where(n_i < n_slc, 1.0, 0.0), prio.shape)
        sel = jnp.zeros(prio.shape, F32)
        for _ in range(n_sel):
            mx = jnp.max(jnp.where(alive > 0.5, prio, -jnp.inf), axis=1, keepdims=True)
            cand = (alive > 0.5) & (prio == mx)
            first = jnp.min(jnp.where(cand, n_f, float(nbl)), axis=1, keepdims=True)
            pick = n_f == first
            sel = jnp.where(pick, 1.0, sel)
            alive = jnp.where(pick, 0.0, alive)
        sel = jnp.where(valid, sel, 0.0)
        for k in range(NSA_KV_HEADS):
            for g in range(NSA_GROUP):
                sel_sc[pl.ds((k * NSA_GROUP + g) * dl, dl), :] = sel[k * dl:(k + 1) * dl]
        m_sc[...] = jnp.full(m_sc.shape, NEG_BIG, F32)
        l_sc[...] = jnp.zeros_like(l_sc)
        acc_sc[...] = jnp.zeros_like(acc_sc)

    selb = sel_sc[...].astype(BF16)
    n_col = lax.broadcasted_iota(jnp.int32, (nbl, 1), 0)

    def online_update(k, s, mask, v_tiles):
        rs = pl.ds(k * rows_h, rows_h)
        m_old = m_sc[rs, :]
        m_new = jnp.maximum(m_old, jnp.max(s, axis=1, keepdims=True))
        p = jnp.where(mask, jnp.exp(s - m_new), 0.0)
        alpha = jnp.exp(m_old - m_new)
        l_sc[rs, :] = alpha * l_sc[rs, :] + jnp.sum(p, axis=1, keepdims=True)
        pv = jnp.zeros((rows_h, NSA_HD), F32)
        for i, vt in enumerate(v_tiles):
            pv = pv + _dot(p[:, i * LANES:(i + 1) * LANES].astype(BF16), vt)
        acc_sc[rs, :] = alpha * acc_sc[rs, :] + pv
        m_sc[rs, :] = m_new

    t_h = tok_pos(rows_h)

    def selected(kpos):
        blk = jnp.where(n_col == (kpos >> SLC_SHIFT), 1.0, 0.0).astype(BF16)
        return _dot(selb, blk)

    kpos = step * (pg * page) + lax.broadcasted_iota(jnp.int32, (1, pg * page), 1)
    sel_all = selected(kpos)
    for k in range(NSA_KV_HEADS):
        qk = qb_sc[pl.ds(k * rows_h, rows_h), :]
        mask = (sel_all[k * rows_h:(k + 1) * rows_h] > 0.5) & (kpos <= t_h)
        s = jnp.concatenate([_dot_nt(qk, _head_rows(r, k).astype(BF16)) for r in kp_refs], axis=1)
        s = jnp.where(mask, s, NEG_BIG)
        online_update(k, s, mask, [_head_rows(r, k).astype(BF16) for r in vp_refs])

    @pl.when(step == n_steps - 1)
    def _finish():
        padn = LANES - dl
        j_new = lax.broadcasted_iota(jnp.int32, (1, LANES), 1)
        kpos_n = q_off + j_new
        sel_n = selected(kpos_n)
        for k in range(NSA_KV_HEADS):
            kn = jnp.concatenate([kvn_ref[:, k * NSA_HD:(k + 1) * NSA_HD], jnp.zeros((padn, NSA_HD), F32)], axis=0)
            vn = jnp.concatenate([kvn_ref[:, (2 + k) * NSA_HD:(3 + k) * NSA_HD], jnp.zeros((padn, NSA_HD), F32)],
                                 axis=0)
            qk = qb_sc[pl.ds(k * rows_h, rows_h), :]
            mask = (sel_n[k * rows_h:(k + 1) * rows_h] > 0.5) & (kpos_n <= t_h) & (j_new < dl)
            s = jnp.where(mask, _dot_nt(qk, kn.astype(BF16)), NEG_BIG)
            online_update(k, s, mask, [vn.astype(BF16)])
        j_w = lax.broadcasted_iota(jnp.int32, (1, w_buf + LANES), 1)
        pos_w = q_off - w_buf + j_w
        dlt = t_h - pos_w
        wmask = (j_w < w_buf + dl) & (pos_w >= 0) & (dlt >= 0) & (dlt < WINDOW)
        gates = jax.nn.sigmoid(gate_ref[...])
        for k in range(NSA_KV_HEADS):
            kw = jnp.concatenate([_head_rows(cw_ref, k),
                                  wn_ref[:, k * NSA_HD:(k + 1) * NSA_HD], jnp.zeros((padn, NSA_HD), F32)], axis=0)
            vw = jnp.concatenate([_head_rows(cw_ref, NSA_KV_HEADS + k),
                                  wn_ref[:, (2 + k) * NSA_HD:(3 + k) * NSA_HD], jnp.zeros((padn, NSA_HD), F32)],
                                 axis=0)
            qk = qb_sc[pl.ds(k * rows_h, rows_h), :]
            s = jnp.where(wmask, _dot_nt(qk, kw.astype(BF16)), -jnp.inf)
            m = jnp.max(s, axis=1, keepdims=True)
            m = jnp.where(m > -jnp.inf, m, 0.0)
            e = jnp.exp(s - m)
            d = jnp.sum(e, axis=1, keepdims=True)
            o_win = _dot((e / jnp.where(d > 0, d, 1.0)).astype(BF16), vw.astype(BF16))
            rs = pl.ds(k * rows_h, rows_h)
            o_slc = acc_sc[rs, :] / l_sc[rs, :]
            o_cmp = ocmp_sc[rs, :]
            for g in range(NSA_GROUP):
                h = k * NSA_GROUP + g
                r = slice(g * dl, (g + 1) * dl)
                gc = gates[:, h * N_BRANCH:h * N_BRANCH + 1]
                gs = gates[:, h * N_BRANCH + 1:h * N_BRANCH + 2]
                gw = gates[:, h * N_BRANCH + 2:h * N_BRANCH + 3]
                o_ref[:, h * NSA_HD:(h + 1) * NSA_HD] = gc * o_cmp[r] + gs * o_slc[r] + gw * o_win[r]


def _nsa_sample(P, CMP, cache_kv, cache_win, layer, page_table, DB, DL, past_len, page):
    n_pages = page_table.shape[1]
    w_buf = cache_win.shape[2]
    lk = past_len + DL
    nch = CMP.shape[3]
    n_cmp = lk // CMP_STRIDE - CMP_BLOCK // CMP_STRIDE + 1
    n_slc = -(-lk // SLC_BLOCK)
    n_sel = min(N_SELECT, n_slc)
    nbl = _round_up(n_slc, LANES)
    pg = _pick_tile(n_pages, (16, 8, 4, 2, 1))
    n_steps = n_pages // pg
    assert DL % SUBLANES == 0 and DL <= LANES and page == LANES and past_len == n_pages * page
    cov = jnp.asarray(_cover_matrix(n_cmp, n_slc, nch, nbl))
    kvw = NSA_KV_HEADS * NSA_HD
    page_spec = lambda i, c: pl.BlockSpec((None, None, page, None, NSA_KV_HEADS, NSA_HD),
                                          lambda b, s, pt: (layer, pt[b, s * pg + i], 0, c, 0, 0))
    in_specs = [
        pl.BlockSpec((DL, NSA_HEADS * NSA_HD), lambda b, s, pt: (b, 3)),
        pl.BlockSpec((DL, 2 * kvw), lambda b, s, pt: (b, 9)),
        pl.BlockSpec((DL, 2 * kvw), lambda b, s, pt: (b, 10)),
        pl.BlockSpec((DL, LANES), lambda b, s, pt: (b, 44)),
        pl.BlockSpec((None, None, NSA_KV_HEADS, nch, NSA_HD), lambda b, s, pt: (b, 0, 0, 0, 0)),
        pl.BlockSpec((None, None, NSA_KV_HEADS, nch, NSA_HD), lambda b, s, pt: (b, 1, 0, 0, 0)),
        pl.BlockSpec((nch, nbl), lambda b, s, pt: (0, 0)),
        pl.BlockSpec((None, None, w_buf, 2, NSA_KV_HEADS, NSA_HD), lambda b, s, pt: (layer, b, 0, 0, 0, 0)),
    ] + [page_spec(i, 2) for i in range(pg)] + [page_spec(i, 3) for i in range(pg)]
    rows = NSA_HEADS * DL
    grid_spec = pltpu.PrefetchScalarGridSpec(
        num_scalar_prefetch=1,
        grid=(DB, n_steps),
        in_specs=in_specs,
        out_specs=pl.BlockSpec((DL, NSA_HEADS * NSA_HD), lambda b, s, pt: (b, 0)),
        scratch_shapes=[
            pltpu.VMEM((rows, NSA_HD), BF16),
            pltpu.VMEM((rows, nbl), F32),
            pltpu.VMEM((rows, 1), F32),
            pltpu.VMEM((rows, 1), F32),
            pltpu.VMEM((rows, NSA_HD), F32),
            pltpu.VMEM((rows, NSA_HD), F32),
        ],
    )
    body = functools.partial(_nsa_sample_body, pg=pg, n_steps=n_steps, dl=DL, q_off=past_len, w_buf=w_buf,
                             n_cmp=n_cmp, n_slc=n_slc, n_sel=n_sel, nbl=nbl, page=page)
    return pl.pallas_call(
        body, grid_spec=grid_spec,
        out_shape=jax.ShapeDtypeStruct((DB * DL, NSA_HEADS * NSA_HD), F32),
        compiler_params=_cparams(("parallel", "arbitrary")),
        name="nsa_sample",
    )(page_table, P, P, P, P, CMP, CMP, cov, cache_win, *([cache_kv] * (2 * pg)))


def _out_ln_body(h_ref, ro_ref, no_ref, wr_ref, wn_ref, g_ref, b_ref, o_ref, *, alpha):
    m = _dot(ro_ref[...].astype(BF16), wr_ref[...]) + _dot(no_ref[...].astype(BF16), wn_ref[...])
    o_ref[...] = _layer_norm(alpha * h_ref[...] + m, g_ref[...], b_ref[...])


def _out_ln(h, ro, no, w_out, g, b, alpha):
    T, D = h.shape
    kr = ro.shape[1]
    kn = no.shape[1]
    tm = _pick_tile(T, (512, 256, 128, 64, 32, 16, 8))
    return pl.pallas_call(
        functools.partial(_out_ln_body, alpha=alpha),
        grid=(T // tm,),
        in_specs=[
            pl.BlockSpec((tm, D), lambda i: (i, 0)),
            pl.BlockSpec((tm, kr), lambda i: (i, 0)),
            pl.BlockSpec((tm, kn), lambda i: (i, 0)),
            pl.BlockSpec((kr, D), lambda i: (0, 0)),
            pl.BlockSpec((kn, D), lambda i: (1, 0)),
            pl.BlockSpec((1, D), lambda i: (0, 0)),
            pl.BlockSpec((1, D), lambda i: (0, 0)),
        ],
        out_specs=pl.BlockSpec((tm, D), lambda i: (i, 0)),
        out_shape=jax.ShapeDtypeStruct((T, D), F32),
        compiler_params=_cparams(("parallel",)),
        name="out_ln",
    )(h, ro, no, w_out, w_out, g.reshape(1, D), b.reshape(1, D))


def _rope_tables(pos):
    half = NSA_HD // 2
    inv = ROPE_THETA ** (-jnp.arange(half, dtype=F32) / half)
    ang = pos.astype(F32)[:, None] * inv[None, :]
    cos = jnp.cos(ang)
    sin = jnp.sin(ang)
    return jnp.concatenate([cos, cos], -1), jnp.concatenate([-sin, sin], -1)


def _cmp_weights(w1, pos, w2):
    r = CMP_BLOCK // CMP_STRIDE
    w1r = w1.reshape(r, CMP_STRIDE * NSA_HD, CMP_HIDDEN)
    w1c = jnp.concatenate([w1r[i] for i in range(r)], axis=1).astype(BF16)
    posr = jnp.pad(pos.reshape(r, CMP_STRIDE * NSA_HD), ((0, SUBLANES - r), (0, 0))).astype(BF16)
    return w1c, posr, w2.astype(BF16)


def _layer_view(arr, l, shape):
    return arr.reshape(shape) if arr.shape[0] == 1 else arr[l].reshape(shape)


def _decoder_layer(x, B, L, q_off, s0, p, sample_ctx):
    alpha = p['alpha']
    h1, h1_bf = _ffn_ln(x, p['ffn1_w_up'], p['ffn1_w_down'], p['ln1_g'], p['ln1_b'], alpha, True)
    cos, sin = _rope_tables(q_off + jnp.arange(L, dtype=jnp.int32))
    P = _proj(h1_bf, p['w_in'], p['rope_cols'], p['scale_cols'], cos, sin, L)
    ro, ret_s = _retention(P, s0, p['ret_gn_g'], p['ret_gn_b'], B, L)
    if sample_ctx is None:
        assert L % CMP_STRIDE == 0
        n_cmp = L // CMP_STRIDE - CMP_BLOCK // CMP_STRIDE + 1
        specs = [pl.BlockSpec((L, NSA_HD), (lambda hd: (lambda b, t, g: (b, 32 + NSA_KV_HEADS * t + hd)))(hd))
                 for hd in range(NSA_KV_HEADS)]
        CMP = _compress([P] * NSA_KV_HEADS, specs, 1, L, 1, n_cmp, B, p['cmp_w1'], p['cmp_pos'], p['cmp_w2'])
        no = _nsa_prompt(P, CMP, B, L)
    else:
        cache_kv, cache_win, layer, page_table, past_len, page = sample_ctx
        n_pages = page_table.shape[1]
        lk = past_len + L
        assert (lk // CMP_STRIDE) * CMP_STRIDE <= past_len, "compression blocks must lie in the paged past"
        n_cmp = lk // CMP_STRIDE - CMP_BLOCK // CMP_STRIDE + 1
        n_in = _pick_tile(n_pages, (32, 16, 8, 4, 2, 1))
        n_grp = n_pages // n_in
        specs = [pl.BlockSpec((None, None, page, None, NSA_KV_HEADS, NSA_HD),
                              (lambda i: (lambda b, t, g, pt: (layer, pt[b, (n_grp - 1 - g) * n_in + i], 0, t, 0,
                                                               0)))(i))
                 for i in range(n_in)]
        CMP = _compress([cache_kv] * n_in, specs, n_in, page, n_grp, n_cmp, B,
                        p['cmp_w1'], p['cmp_pos'], p['cmp_w2'], page_table=page_table)
        no = _nsa_sample(P, CMP, cache_kv, cache_win, layer, page_table, B, L, past_len, page)
    x2 = _out_ln(h1, ro, no, p['w_out'], p['ln2_g'], p['ln2_b'], alpha)
    y, _ = _ffn_ln(x2, p['ffn2_w_up'], p['ffn2_w_down'], p['ln3_g'], p['ln3_b'], alpha, False)
    return y, ret_s, P


def kernel(x_prompt, x_sample, state_ret, cache_nsa_kv, cache_win, page_table, ffn1_w_up, ffn1_w_down, ln1_g, ln1_b, w_in, w_out, ret_gn_g, ret_gn_b, cmp_pos_k, cmp_w1_k, cmp_w2_k, cmp_pos_v, cmp_w1_v, cmp_w2_v, ln2_g, ln2_b, ffn2_w_up, ffn2_w_down, ln3_g, ln3_b):
    B, L, D = x_prompt.shape
    DB, DL, _ = x_sample.shape
    depth = w_in.shape[0]
    n_pool, page = cache_nsa_kv.shape[1], cache_nsa_kv.shape[2]
    n_pages = page_table.shape[1]
    past_len = n_pages * page
    w_buf = cache_win.shape[2]
    alpha = (2.0 * depth) ** 0.25
    kv_cols = 4 * NSA_KV_HEADS * NSA_HD
    win_cols = 2 * NSA_KV_HEADS * NSA_HD
    rope_np = np.zeros((N_IN_PAD // LANES, LANES), np.float32)
    rope_np[list(ROPE_CHUNKS)] = 1.0
    scale_np = np.ones((N_IN_PAD // LANES, LANES), np.float32)
    scale_np[list(KSCALE_CHUNKS)] = RET_DK ** -0.5
    rope_cols = jnp.asarray(rope_np.reshape(1, N_IN_PAD))
    scale_cols = jnp.asarray(scale_np.reshape(1, N_IN_PAD))

    yp = x_prompt.reshape(B * L, D)
    ys = x_sample.reshape(DB * DL, D)
    outs = [[] for _ in range(6)]
    for l in range(depth):
        k1, p1, k2 = _cmp_weights(cmp_w1_k[l], cmp_pos_k[l], cmp_w2_k[l])
        v1, q1, v2 = _cmp_weights(cmp_w1_v[l], cmp_pos_v[l], cmp_w2_v[l])
        p = {
            'alpha': alpha, 'rope_cols': rope_cols, 'scale_cols': scale_cols,
            'ffn1_w_up': ffn1_w_up[l].astype(BF16), 'ffn1_w_down': ffn1_w_down[l].astype(BF16),
            'ln1_g': ln1_g[l], 'ln1_b': ln1_b[l],
            'w_in': jnp.pad(w_in[l], ((0, 0), (0, N_IN_PAD - N_IN))).astype(BF16),
            'w_out': w_out[l].astype(BF16),
            'ret_gn_g': ret_gn_g[l], 'ret_gn_b': ret_gn_b[l],
            'cmp_w1': jnp.stack([k1, v1]), 'cmp_pos': jnp.stack([p1, q1]), 'cmp_w2': jnp.stack([k2, v2]),
            'ln2_g': ln2_g[l], 'ln2_b': ln2_b[l],
            'ffn2_w_up': ffn2_w_up[l].astype(BF16), 'ffn2_w_down': ffn2_w_down[l].astype(BF16),
            'ln3_g': ln3_g[l], 'ln3_b': ln3_b[l],
        }
        s0 = jnp.zeros((B, RET_HEADS, RET_DK, RET_DV), F32)
        yp, rs_p, P_p = _decoder_layer(yp, B, L, 0, s0, p, None)
        ctx = (cache_nsa_kv, cache_win, l, page_table, past_len, page)
        ys, rs_s, P_s = _decoder_layer(ys, DB, DL, past_len, _layer_view(state_ret, l, state_ret.shape[1:]), p,
                                       ctx)
        kv0 = 32 * LANES
        w0 = 40 * LANES
        P_p3 = P_p.reshape(B, L, N_IN_PAD)
        P_s3 = P_s.reshape(DB, DL, N_IN_PAD)
        wl = min(WINDOW, L)
        outs[0].append(rs_p)
        outs[1].append(rs_s)
        outs[2].append(P_p3[:, :, kv0:kv0 + kv_cols].reshape(B, L, 4, NSA_KV_HEADS, NSA_HD))
        outs[3].append(P_s3[:, :, kv0:kv0 + kv_cols].reshape(DB, DL, 4, NSA_KV_HEADS, NSA_HD))
        outs[4].append(P_p3[:, L - wl:, w0:w0 + win_cols].reshape(B, wl, 2, NSA_KV_HEADS, NSA_HD))
        win_s = P_s3[:, :, w0:w0 + win_cols].reshape(DB, DL, 2, NSA_KV_HEADS, NSA_HD)
        outs[5].append(jnp.concatenate([cache_win[l], win_s], axis=1)[:, -w_buf:])
    return (yp.reshape(B, L, D), ys.reshape(DB, DL, D), jnp.stack(outs[0]), jnp.stack(outs[1]),
            jnp.stack(outs[2]), jnp.stack(outs[3]), jnp.stack(outs[4]), jnp.stack(outs[5]))
```

```python
import functools

import numpy as np
import jax
import jax.numpy as jnp
from jax import lax
from jax.experimental import pallas as pl
from jax.experimental.pallas import tpu as pltpu

F32 = jnp.float32
BF16 = jnp.bfloat16

LANES = 128
SUBLANES = 8
VMEM_LIMIT_BYTES = 56 * 1024 * 1024

RET_HEADS = 4
RET_DK = 128
RET_DV = 256
RET_CHUNK = 128
NSA_HEADS = 8
NSA_KV_HEADS = 2
NSA_HD = 128
NSA_GROUP = NSA_HEADS // NSA_KV_HEADS
CMP_BLOCK = 32
CMP_STRIDE = 16
CMP_HIDDEN = 2 * NSA_HD
SLC_BLOCK = 64
SLC_SHIFT = 6
N_SELECT = 16
WINDOW = 512
N_BRANCH = 3
ROPE_THETA = 10000.0
LN_EPS = 1e-5
NEG_BIG = -1e30
LOG2_E = 1.4426950408889634

N_IN = 5656
N_IN_PAD = 5760
KV_CHUNK0 = 32
WIN_CHUNK0 = 40
GATE_CHUNK = 44
ROPE_CHUNKS = tuple(range(0, 8)) + tuple(range(24, 32)) + (32, 33, 36, 37, 40, 41)
KSCALE_CHUNKS = tuple(range(4, 8))


def _cparams(sem):
    return pltpu.CompilerParams(dimension_semantics=sem, vmem_limit_bytes=VMEM_LIMIT_BYTES)


def _pick_tile(n, candidates):
    for c in candidates:
        if n % c == 0:
            return c
    return n


def _round_up(n, m):
    return (n + m - 1) // m * m


def _layer_norm(z, g, b):
    mu = jnp.mean(z, axis=-1, keepdims=True)
    zc = z - mu
    var = jnp.mean(zc * zc, axis=-1, keepdims=True)
    return zc * lax.rsqrt(var + LN_EPS) * g + b


def _dot(a, b):
    return jnp.dot(a, b, preferred_element_type=F32)


def _dot_nt(a, b):
    return lax.dot_general(a, b, (((1,), (1,)), ((), ())), preferred_element_type=F32)


def _ffn_ln_body(x_ref, wa_ref, wb_ref, wd_ref, g_ref, b_ref, *rest, nj, alpha, emit_bf16):
    if emit_bf16:
        o_ref, obf_ref, xbf_sc, acc_sc = rest
    else:
        o_ref, xbf_sc, acc_sc = rest
    j = pl.program_id(1)

    @pl.when(j == 0)
    def _init():
        xbf_sc[...] = x_ref[...].astype(BF16)
        acc_sc[...] = jnp.zeros_like(acc_sc)

    xb = xbf_sc[...]
    a = _dot(xb, wa_ref[...])
    b = _dot(xb, wb_ref[...])
    h = (a * jax.nn.sigmoid(a)) * b
    acc_sc[...] += _dot(h.astype(BF16), wd_ref[...])

    @pl.when(j == nj - 1)
    def _finish():
        z = alpha * x_ref[...] + 0.5 * acc_sc[...]
        y = _layer_norm(z, g_ref[...], b_ref[...])
        o_ref[...] = y
        if emit_bf16:
            obf_ref[...] = y.astype(BF16)


def _ffn_ln(x, w_up, w_down, g, b, alpha, emit_bf16):
    T, D = x.shape
    F = w_down.shape[0]
    tm = _pick_tile(T, (512, 256, 128, 64, 32, 16, 8))
    tf = _pick_tile(F, (512, 256, 128))
    nj = F // tf
    out_shape = [jax.ShapeDtypeStruct((T, D), F32)]
    out_specs = [pl.BlockSpec((tm, D), lambda i, j: (i, 0))]
    if emit_bf16:
        out_shape.append(jax.ShapeDtypeStruct((T, D), BF16))
        out_specs.append(pl.BlockSpec((tm, D), lambda i, j: (i, 0)))
    res = pl.pallas_call(
        functools.partial(_ffn_ln_body, nj=nj, alpha=alpha, emit_bf16=emit_bf16),
        grid=(T // tm, nj),
        in_specs=[
            pl.BlockSpec((tm, D), lambda i, j: (i, 0)),
            pl.BlockSpec((D, tf), lambda i, j: (0, j)),
            pl.BlockSpec((D, tf), lambda i, j: (0, nj + j)),
            pl.BlockSpec((tf, D), lambda i, j: (j, 0)),
            pl.BlockSpec((1, D), lambda i, j: (0, 0)),
            pl.BlockSpec((1, D), lambda i, j: (0, 0)),
        ],
        out_specs=out_specs,
        out_shape=out_shape,
        scratch_shapes=[pltpu.VMEM((tm, D), BF16), pltpu.VMEM((tm, D), F32)],
        compiler_params=_cparams(("parallel", "arbitrary")),
        name="ffn_ln",
    )(x, w_up, w_up, w_down, g.reshape(1, D), b.reshape(1, D))
    return res if emit_bf16 else (res[0], None)


def _rows_view(ref, j):
    rows = ref.shape[0]
    n = int(np.prod(ref.shape[1:-1]))
    return ref.reshape(n * rows, ref.shape[-1]), pl.ds(j, rows, stride=n)


def _head_rows(ref, j):
    view, idx = _rows_view(ref, j)
    return view[idx, :]


def _store_rows(ref, j, val):
    view, idx = _rows_view(ref, j)
    view[idx, :] = val


def _proj_body(x_ref, w_ref, cos_ref, sin_ref, rope_ref, scale_ref, o_ref, kv_ref, win_ref, *, n_chunk, j_rows):
    y = _dot(x_ref[...], w_ref[...])
    cos = cos_ref[...]
    sin = sin_ref[...]
    for c in range(n_chunk):
        sl = slice(c * LANES, (c + 1) * LANES)
        yc = y[:, sl]
        roped = yc * cos + pltpu.roll(yc, NSA_HD // 2, 1) * sin
        o_ref[:, sl] = jnp.where(rope_ref[:, sl] > 0.5, roped, yc) * scale_ref[:, sl]

    @pl.when(pl.program_id(1) == j_rows)
    def _emit_cache_rows():
        c0 = KV_CHUNK0 - j_rows * n_chunk
        for r in range(WIN_CHUNK0 - KV_CHUNK0):
            _store_rows(kv_ref, r, o_ref[:, (c0 + r) * LANES:(c0 + r + 1) * LANES])
        c0 = WIN_CHUNK0 - j_rows * n_chunk
        for r in range(GATE_CHUNK - WIN_CHUNK0):
            _store_rows(win_ref, r, o_ref[:, (c0 + r) * LANES:(c0 + r + 1) * LANES])


def _proj(x_bf, w_bf, rope_cols, scale_cols, cos, sin, rows_per_seq):
    T, D = x_bf.shape
    N = w_bf.shape[1]
    tm = _pick_tile(T, (512, 256, 128, 64, 32, 16, 8))
    tn = 1920
    assert N % tn == 0
    n_chunk = tn // LANES
    if rows_per_seq >= tm:
        assert rows_per_seq % tm == 0
        n_tab = rows_per_seq // tm
    else:
        assert tm % rows_per_seq == 0
        cos = jnp.tile(cos, (tm // rows_per_seq, 1))
        sin = jnp.tile(sin, (tm // rows_per_seq, 1))
        n_tab = 1
    j_rows = KV_CHUNK0 // n_chunk
    assert (GATE_CHUNK - 1) // n_chunk == j_rows, "cache-row columns must sit in one column tile"
    kv_shape = (T, 4, NSA_KV_HEADS, NSA_HD)
    win_shape = (T, 2, NSA_KV_HEADS, NSA_HD)
    return pl.pallas_call(
        functools.partial(_proj_body, n_chunk=n_chunk, j_rows=j_rows),
        grid=(T // tm, N // tn),
        in_specs=[
            pl.BlockSpec((tm, D), lambda i, j: (i, 0)),
            pl.BlockSpec((D, tn), lambda i, j: (0, j)),
            pl.BlockSpec((tm, LANES), lambda i, j: (i % n_tab, 0)),
            pl.BlockSpec((tm, LANES), lambda i, j: (i % n_tab, 0)),
            pl.BlockSpec((1, tn), lambda i, j: (0, j)),
            pl.BlockSpec((1, tn), lambda i, j: (0, j)),
        ],
        out_specs=[
            pl.BlockSpec((tm, tn), lambda i, j: (i, j)),
            pl.BlockSpec((tm,) + kv_shape[1:], lambda i, j: (i, 0, 0, 0)),
            pl.BlockSpec((tm,) + win_shape[1:], lambda i, j: (i, 0, 0, 0)),
        ],
        out_shape=[
            jax.ShapeDtypeStruct((T, N), F32),
            jax.ShapeDtypeStruct(kv_shape, F32),
            jax.ShapeDtypeStruct(win_shape, F32),
        ],
        compiler_params=_cparams(("parallel", "arbitrary")),
        name="proj_rope",
    )(x_bf, w_bf, cos, sin, rope_cols, scale_cols)


def _ret_body(q_ref, k_ref, v_ref, g_ref, s0_ref, dm_ref, ind_ref, std_ref, cd_ref, gng_ref, gnb_ref,
              o_ref, sout_ref, s_sc, *, nc, rows, rows_pad):
    c = pl.program_id(1)

    @pl.when(c == 0)
    def _load_state():
        s_sc[...] = s0_ref[...]

    pad = rows_pad - rows
    for h in range(RET_HEADS):
        ks = slice(h * RET_DK, (h + 1) * RET_DK)
        vs = slice(h * RET_DV, (h + 1) * RET_DV)
        q = q_ref[:, ks]
        k = k_ref[:, ks]
        v = v_ref[:, vs]
        kd = k * std_ref[h]
        if pad:
            k = jnp.concatenate([k, jnp.zeros((pad, RET_DK), F32)], axis=0)
            kd = jnp.concatenate([kd, jnp.zeros((pad, RET_DK), F32)], axis=0)
            v = jnp.concatenate([v, jnp.zeros((pad, RET_DV), F32)], axis=0)
        s_old = s_sc[h]
        vb = v.astype(BF16)
        a = _dot_nt(q.astype(BF16), k.astype(BF16)) * dm_ref[h]
        o = _dot(a.astype(BF16), vb) + _dot((q * ind_ref[h]).astype(BF16), s_old.astype(BF16))
        s_sc[h] = s_old * cd_ref[h] + _dot(kd.T.astype(BF16), vb)
        mu = jnp.mean(o, axis=-1, keepdims=True)
        oc = o - mu
        var = jnp.mean(oc * oc, axis=-1, keepdims=True)
        on = oc * lax.rsqrt(var + LN_EPS) * gng_ref[:, vs] + gnb_ref[:, vs]
        gate = g_ref[:, vs]
        o_ref[:, vs] = (gate * jax.nn.sigmoid(gate)) * on

    @pl.when(c == nc - 1)
    def _store_state():
        sout_ref[...] = s_sc[...]


def _retention(P, s0, gn_g, gn_b, B, L):
    C = RET_CHUNK if L % RET_CHUNK == 0 else L
    nc = L // C
    CP = max(C, LANES)
    lg = jnp.log1p(-jnp.exp2(-5.0 - jnp.arange(RET_HEADS, dtype=F32)))
    i = jnp.arange(C, dtype=F32)
    diff = i[:, None] - i[None, :]
    dmask = jnp.where(diff >= 0, jnp.exp(lg[:, None, None] * jnp.maximum(diff, 0.0)), 0.0)
    dmask = jnp.pad(dmask, ((0, 0), (0, 0), (0, CP - C)))
    in_decay = jnp.broadcast_to(jnp.exp(lg[:, None] * (i + 1.0))[:, :, None], (RET_HEADS, C, RET_DK))
    st_decay = jnp.broadcast_to(jnp.exp(lg[:, None] * (C - 1.0 - i))[:, :, None], (RET_HEADS, C, RET_DK))
    chunk_decay = jnp.broadcast_to(jnp.exp(lg * C)[:, None, None], (RET_HEADS, 1, RET_DV))
    qw = RET_HEADS * RET_DK
    vw = RET_HEADS * RET_DV
    const3 = lambda b, c: (0, 0, 0)
    return pl.pallas_call(
        functools.partial(_ret_body, nc=nc, rows=C, rows_pad=CP),
        grid=(B, nc),
        in_specs=[
            pl.BlockSpec((C, qw), lambda b, c: (b * nc + c, 0)),
            pl.BlockSpec((C, qw), lambda b, c: (b * nc + c, 1)),
            pl.BlockSpec((C, vw), lambda b, c: (b * nc + c, 1)),
            pl.BlockSpec((C, vw), lambda b, c: (b * nc + c, 2)),
            pl.BlockSpec((None, RET_HEADS, RET_DK, RET_DV), lambda b, c: (b, 0, 0, 0)),
            pl.BlockSpec((RET_HEADS, C, CP), const3),
            pl.BlockSpec((RET_HEADS, C, RET_DK), const3),
            pl.BlockSpec((RET_HEADS, C, RET_DK), const3),
            pl.BlockSpec((RET_HEADS, 1, RET_DV), const3),
            pl.BlockSpec((1, vw), lambda b, c: (0, 0)),
            pl.BlockSpec((1, vw), lambda b, c: (0, 0)),
        ],
        out_specs=[
            pl.BlockSpec((C, vw), lambda b, c: (b * nc + c, 0)),
            pl.BlockSpec((None, RET_HEADS, RET_DK, RET_DV), lambda b, c: (b, 0, 0, 0)),
        ],
        out_shape=[
            jax.ShapeDtypeStruct((B * L, vw), F32),
            jax.ShapeDtypeStruct((B, RET_HEADS, RET_DK, RET_DV), F32),
        ],
        scratch_shapes=[pltpu.VMEM((RET_HEADS, RET_DK, RET_DV), F32)],
        compiler_params=_cparams(("parallel", "arbitrary")),
        name="retention",
    )(P, P, P, P, s0, dmask, in_decay, st_decay, chunk_decay, gn_g.reshape(1, vw), gn_b.reshape(1, vw))


def _cmp_body(*refs, n_in, rows, n_grp, n_cmp, paged):
    if paged:
        refs = refs[1:]
    n_src = n_in if paged else NSA_KV_HEADS * n_in
    x_refs = refs[:n_src]
    w1_ref, pos_ref, w2_ref, o_ref, carry_sc = refs[n_src:]
    g = pl.program_id(2)
    grp = n_grp - 1 - g
    cpi = rows // CMP_STRIDE
    M = n_in * cpi

    @pl.when(g == 0)
    def _init():
        carry_sc[...] = jnp.zeros_like(carry_sc)

    w1 = w1_ref[...]
    gp = _dot(pos_ref[...], w1)
    posterm = gp[0:1, :CMP_HIDDEN] + gp[1:2, CMP_HIDDEN:]
    row = lax.broadcasted_iota(jnp.int32, (M, 1), 0)
    for hd in range(NSA_KV_HEADS):
        blocks = []
        if paged:
            for x_ref in x_refs:
                x3 = _head_rows(x_ref, hd).reshape(cpi, CMP_STRIDE, NSA_HD)
                xt = pltpu.einshape("csd->scd", x3)
                blocks.append(jnp.concatenate([xt[s] for s in range(CMP_STRIDE)], axis=1))
        else:
            for x_ref in x_refs[hd * n_in:(hd + 1) * n_in]:
                blocks.append(jnp.concatenate(
                    [x_ref[pl.ds(s, cpi, stride=CMP_STRIDE), :] for s in range(CMP_STRIDE)], axis=1))
        xc = blocks[0] if n_in == 1 else jnp.concatenate(blocks, axis=0)
        gg = _dot(xc.astype(BF16), w1)
        g0 = gg[:, :CMP_HIDDEN]
        g1 = gg[:, CMP_HIDDEN:]
        nxt = pltpu.roll(g1, M - 1, 0)
        nxt = jnp.where(row == M - 1, carry_sc[hd][0:1, :], nxt)
        carry_sc[hd] = g1[0:SUBLANES, :]
        hid = g0 + nxt + posterm
        out = _dot(jax.nn.gelu(hid).astype(BF16), w2_ref[...])
        o_ref[hd] = jnp.where(grp * M + row < n_cmp, out, 0.0)


def _compress(srcs, src_specs, n_in, rows, n_grp, n_cmp, B, w1, pos, w2, page_table=None):
    M = n_in * rows // CMP_STRIDE
    nch = n_grp * M
    paged = page_table is not None
    extra = (lambda *a: a[:3]) if paged else (lambda *a: a)
    in_specs = list(src_specs) + [
        pl.BlockSpec((None, CMP_STRIDE * NSA_HD, 2 * CMP_HIDDEN), lambda *a: (extra(*a)[1], 0, 0)),
        pl.BlockSpec((None, SUBLANES, CMP_STRIDE * NSA_HD), lambda *a: (extra(*a)[1], 0, 0)),
        pl.BlockSpec((None, CMP_HIDDEN, NSA_HD), lambda *a: (extra(*a)[1], 0, 0)),
    ]
    out_spec = pl.BlockSpec((None, None, NSA_KV_HEADS, M, NSA_HD),
                            lambda *a: (extra(*a)[0], extra(*a)[1], 0, n_grp - 1 - extra(*a)[2], 0))
    body = functools.partial(_cmp_body, n_in=n_in, rows=rows, n_grp=n_grp, n_cmp=n_cmp, paged=paged)
    out_shape = jax.ShapeDtypeStruct((B, 2, NSA_KV_HEADS, nch, NSA_HD), F32)
    scratch = [pltpu.VMEM((NSA_KV_HEADS, SUBLANES, CMP_HIDDEN), F32)]
    sem = ("parallel", "arbitrary", "arbitrary")
    if paged:
        grid_spec = pltpu.PrefetchScalarGridSpec(
            num_scalar_prefetch=1, grid=(B, 2, n_grp), in_specs=in_specs, out_specs=out_spec,
            scratch_shapes=scratch)
        return pl.pallas_call(body, grid_spec=grid_spec, out_shape=out_shape,
                              compiler_params=_cparams(sem), name="nsa_compress_paged")(
            page_table, *srcs, w1, pos, w2)
    return pl.pallas_call(body, grid=(B, 2, n_grp), in_specs=in_specs, out_specs=out_spec,
                          out_shape=out_shape, scratch_shapes=scratch,
                          compiler_params=_cparams(sem), name="nsa_compress")(*srcs, w1, pos, w2)


def _cover_matrix(n_cmp, n_slc, rows, cols):
    c_i = np.arange(n_cmp)[:, None]
    n_i = np.arange(n_slc)[None, :]
    cov = np.clip(np.minimum(c_i * CMP_STRIDE + CMP_BLOCK, (n_i + 1) * SLC_BLOCK)
                  - np.maximum(c_i * CMP_STRIDE, n_i * SLC_BLOCK), 0, None).astype(np.float32) / CMP_BLOCK
    out = np.zeros((rows, cols), np.float32)
    out[:n_cmp, :n_slc] = cov
    return out


def _nsa_prompt_body(q_ref, kc_ref, vc_ref, ks_ref, vs_ref, kw_ref, vw_ref, gate_ref, covt_ref, o_ref,
                     kcb_sc, vct_sc, ksb_sc, vst_sc, kwb_sc, vwt_sc, prio_sc, sel_sc, gt_sc,
                     m_sc, l_sc, acc_sc, out_sc, s_sc, *, tq, L, n_cmp, n_slc, n_sel, nbp):
    kvh = pl.program_id(1)
    qi = pl.program_id(2)
    tk = tq
    nch = kc_ref.shape[0]
    scale = NSA_HD ** -0.5

    @pl.when(qi == 0)
    def _stage_kv():
        kcb_sc[...] = kc_ref[...].astype(BF16)
        vct_sc[...] = vc_ref[...].T.astype(BF16)
        ksb_sc[...] = ks_ref[...].astype(BF16)
        kwb_sc[...] = kw_ref[...].astype(BF16)
        for i in range(L // tk):
            vst_sc[i] = vs_ref[i * tk:(i + 1) * tk, :].T.astype(BF16)
            vwt_sc[i] = vw_ref[i * tk:(i + 1) * tk, :].T.astype(BF16)

    G = NSA_GROUP
    W = G * tq
    t0 = qi * tq
    t_row = t0 + lax.broadcasted_iota(jnp.int32, (1, tq), 1)
    lane = lax.broadcasted_iota(jnp.int32, (1, W), 1)
    t_all = t0 + (lane & (tq - 1))
    qcat = jnp.concatenate([q_ref[:, g * NSA_HD:(g + 1) * NSA_HD] for g in range(G)], axis=0)
    qcat = (qcat * (scale * LOG2_E)).astype(BF16)
    gt_sc[...] = jax.nn.sigmoid(gate_ref[...]).T

    def gate_row(br):
        return jnp.concatenate(
            [gt_sc[pl.ds((kvh * G + g) * N_BRANCH + br, 1), :] for g in range(G)], axis=1)

    c_i = lax.broadcasted_iota(jnp.int32, (nch, 1), 0)
    cmask = (c_i * CMP_STRIDE + (CMP_BLOCK - 1) <= t_all) & (c_i < n_cmp)
    s = jnp.where(cmask, _dot_nt(kcb_sc[...], qcat), -jnp.inf)
    m = jnp.max(s, axis=0, keepdims=True)
    m = jnp.where(m > -jnp.inf, m, 0.0)
    e = jnp.exp2(s - m)
    d = jnp.sum(e, axis=0, keepdims=True)
    p = e / jnp.where(d > 0, d, 1.0)
    out_sc[...] = gate_row(0) * _dot(vct_sc[...], p.astype(BF16))
    psum = p[:, 0:tq]
    for g in range(1, G):
        psum = psum + p[:, g * tq:(g + 1) * tq]

    imp = jnp.dot(covt_ref[...], psum, preferred_element_type=F32, precision=lax.Precision.HIGHEST)
    n_i = lax.broadcasted_iota(jnp.int32, (nbp, 1), 0)
    valid = (n_i * SLC_BLOCK <= t_row) & (n_i < n_slc)
    cur = t_row >> SLC_SHIFT
    forced = (n_i == 0) | (n_i == cur) | (n_i == cur - 1)
    prio = jnp.where(forced, jnp.inf, jnp.where(valid, imp, -jnp.inf))
    prio_sc[...] = prio
    cnt = jnp.zeros((nbp, tq), jnp.int32)
    for mm in range(n_slc):
        pm = prio_sc[mm:mm + 1, :]
        beats = (pm > prio) | ((pm == prio) & (n_i > mm))
        cnt = cnt + beats.astype(jnp.int32)
    selbias = jnp.where((cnt < n_sel) & valid, 0.0, NEG_BIG)
    sel_sc[...] = jnp.concatenate([selbias] * G, axis=1)

    kp_l = lax.broadcasted_iota(jnp.int32, (tk, 1), 0)
    t_l = lane & (tq - 1)
    causal_bias = jnp.where(kp_l <= t_l, 0.0, NEG_BIG)
    far_bias = jnp.where(kp_l > t_l, 0.0, NEG_BIG)

    def reset():
        m_sc[...] = jnp.full((1, W), NEG_BIG, F32)
        l_sc[...] = jnp.zeros((1, W), F32)
        acc_sc[...] = jnp.zeros((NSA_HD, W), F32)

    def score(kb_sc, kt, bias):
        k0 = pl.multiple_of(kt * tk, tk)
        s = _dot_nt(kb_sc[pl.ds(k0, tk), :], qcat)
        return s if bias is None else s + bias

    def update(vt_sc, kt, s):
        m_old = m_sc[...]
        m_new = jnp.maximum(m_old, jnp.max(s, axis=0, keepdims=True))
        p = jnp.exp2(s - m_new)
        alpha = jnp.exp2(m_old - m_new)
        l_sc[...] = alpha * l_sc[...] + jnp.sum(p, axis=0, keepdims=True)
        acc_sc[...] = alpha * acc_sc[...] + _dot(vt_sc[kt], p.astype(BF16))
        m_sc[...] = m_new

    def sel_bias(kt):
        per_tile = tk // SLC_BLOCK
        rows = [jnp.broadcast_to(sel_sc[pl.ds(kt * per_tile + r, 1), :], (SLC_BLOCK, W))
                for r in range(per_tile)]
        return jnp.concatenate(rows, axis=0)

    def flush(br):
        out_sc[...] += (gate_row(br) / l_sc[...]) * acc_sc[...]

    reset()
    s_sc[...] = score(ksb_sc, 0, sel_bias(0))

    def slc_body(kt, carry):
        s_next = score(ksb_sc, kt + 1, sel_bias(kt + 1))
        update(vst_sc, kt, s_sc[...])
        s_sc[...] = s_next
        return carry

    lax.fori_loop(0, qi, slc_body, 0)
    update(vst_sc, qi, s_sc[...] + causal_bias)
    flush(1)

    reset()
    far = WINDOW // tk
    tiles = []
    for back in range(far, -1, -1):
        bias = far_bias if back == far else (causal_bias if back == 0 else None)
        if back > 0:
            off = jnp.where(qi >= back, 0.0, NEG_BIG).astype(F32)
            bias = off if bias is None else bias + off
        kt = jnp.maximum(qi - back, 0)
        tiles.append((kt, score(kwb_sc, kt, bias)))
    for kt, s_w in tiles:
        update(vwt_sc, kt, s_w)
    flush(2)
    for g in range(G):
        o_ref[:, g * NSA_HD:(g + 1) * NSA_HD] = out_sc[:, g * tq:(g + 1) * tq].T


def _nsa_prompt(P, CMP, B, L):
    tq = _pick_tile(L, (256, 128))
    assert L % tq == 0 and tq % SLC_BLOCK == 0 and WINDOW % tq == 0 and tq & (tq - 1) == 0
    nq = L // tq
    nch = CMP.shape[3]
    n_cmp = L // CMP_STRIDE - CMP_BLOCK // CMP_STRIDE + 1
    n_slc = -(-L // SLC_BLOCK)
    n_sel = min(N_SELECT, n_slc)
    nbp = _round_up(n_slc, SUBLANES)
    covt = jnp.asarray(_cover_matrix(n_cmp, n_slc, nch, nbp).T)
    gw = NSA_GROUP * NSA_HD
    col = lambda c: (lambda b, k, i: (b, c + k))
    body = functools.partial(_nsa_prompt_body, tq=tq, L=L, n_cmp=n_cmp, n_slc=n_slc, n_sel=n_sel, nbp=nbp)
    return pl.pallas_call(
        body,
        grid=(B, NSA_KV_HEADS, nq),
        in_specs=[
            pl.BlockSpec((tq, gw), lambda b, k, i: (b * nq + i, 6 + k)),
            pl.BlockSpec((None, None, None, nch, NSA_HD), lambda b, k, i: (b, 0, k, 0, 0)),
            pl.BlockSpec((None, None, None, nch, NSA_HD), lambda b, k, i: (b, 1, k, 0, 0)),
            pl.BlockSpec((L, NSA_HD), col(36)),
            pl.BlockSpec((L, NSA_HD), col(38)),
            pl.BlockSpec((L, NSA_HD), col(40)),
            pl.BlockSpec((L, NSA_HD), col(42)),
            pl.BlockSpec((tq, LANES), lambda b, k, i: (b * nq + i, 44)),
            pl.BlockSpec((nbp, nch), lambda b, k, i: (0, 0)),
        ],
        out_specs=pl.BlockSpec((tq, gw), lambda b, k, i: (b * nq + i, k)),
        out_shape=jax.ShapeDtypeStruct((B * L, NSA_HEADS * NSA_HD), F32),
        scratch_shapes=[
            pltpu.VMEM((nch, NSA_HD), BF16),
            pltpu.VMEM((NSA_HD, nch), BF16),
            pltpu.VMEM((L, NSA_HD), BF16),
            pltpu.VMEM((L // tq, NSA_HD, tq), BF16),
            pltpu.VMEM((L, NSA_HD), BF16),
            pltpu.VMEM((L // tq, NSA_HD, tq), BF16),
            pltpu.VMEM((nbp, tq), F32),
            pltpu.VMEM((nbp, NSA_GROUP * tq), F32),
            pltpu.VMEM((LANES, tq), F32),
            pltpu.VMEM((1, NSA_GROUP * tq), F32),
            pltpu.VMEM((1, NSA_GROUP * tq), F32),
            pltpu.VMEM((NSA_HD, NSA_GROUP * tq), F32),
            pltpu.VMEM((NSA_HD, NSA_GROUP * tq), F32),
            pltpu.VMEM((tq, NSA_GROUP * tq), F32),
        ],
        compiler_params=_cparams(("parallel", "arbitrary", "arbitrary")),
        name="nsa_prompt",
    )(P, CMP, CMP, P, P, P, P, P, covt)


def _nsa_sample_body(*refs, pg, n_steps, dl, q_off, w_buf, n_cmp, n_slc, n_sel, nbl, page):
    refs = refs[1:]
    q_ref, kvn_ref, wn_ref, gate_ref, kc_ref, vc_ref, cov_ref, cw_ref = refs[:8]
    kp_refs = refs[8:8 + pg]
    vp_refs = refs[8 + pg:8 + 2 * pg]
    o_ref, qb_sc, sel_sc, m_sc, l_sc, acc_sc, ocmp_sc = refs[8 + 2 * pg:]
    step = pl.program_id(1)
    rows_h = NSA_GROUP * dl
    rows = NSA_KV_HEADS * rows_h
    scale = NSA_HD ** -0.5
    nch = kc_ref.shape[1]

    def tok_pos(n):
        r = lax.broadcasted_iota(jnp.int32, (n, 1), 0)
        return q_off + r % dl

    @pl.when(step == 0)
    def _select():
        for k in range(NSA_KV_HEADS):
            for g in range(NSA_GROUP):
                h = k * NSA_GROUP + g
                qb_sc[pl.ds(h * dl, dl), :] = (q_ref[:, h * NSA_HD:(h + 1) * NSA_HD] * scale).astype(BF16)
        t_h = tok_pos(rows_h)
        c_i = lax.broadcasted_iota(jnp.int32, (1, nch), 1)
        cmask = (c_i * CMP_STRIDE + (CMP_BLOCK - 1) <= t_h) & (c_i < n_cmp)
        imps = []
        for k in range(NSA_KV_HEADS):
            qk = qb_sc[pl.ds(k * rows_h, rows_h), :]
            s = jnp.where(cmask, _dot_nt(qk, kc_ref[k].astype(BF16)), -jnp.inf)
            m = jnp.max(s, axis=1, keepdims=True)
            m = jnp.where(m > -jnp.inf, m, 0.0)
            e = jnp.exp(s - m)
            d = jnp.sum(e, axis=1, keepdims=True)
            p = e / jnp.where(d > 0, d, 1.0)
            ocmp_sc[pl.ds(k * rows_h, rows_h), :] = _dot(p.astype(BF16), vc_ref[k].astype(BF16))
            psum = p[0:dl]
            for g in range(1, NSA_GROUP):
                psum = psum + p[g * dl:(g + 1) * dl]
            imps.append(jnp.dot(psum, cov_ref[...], preferred_element_type=F32,
                                precision=lax.Precision.HIGHEST))
        imp = jnp.concatenate(imps, axis=0)
        t_s = tok_pos(NSA_KV_HEADS * dl)
        n_i = lax.broadcasted_iota(jnp.int32, (1, nbl), 1)
        valid = (n_i * SLC_BLOCK <= t_s) & (n_i < n_slc)
        cur = t_s >> SLC_SHIFT
        forced = (n_i == 0) | (n_i == cur) | (n_i == cur - 1)
        prio = jnp.where(forced, jnp.inf, jnp.where(valid, imp, -jnp.inf))
        n_f = n_i.astype(F32)
        alive = jnp.broadcast_to(jnp.where(n_i < n_slc, 1.0, 0.0), prio.shape)
        sel = jnp.zeros(prio.shape, F32)
        for _ in range(n_sel):
            mx = jnp.max(jnp.where(alive > 0.5, prio, -jnp.inf), axis=1, keepdims=True)
            cand = (alive > 0.5) & (prio == mx)
            first = jnp.min(jnp.where(cand, n_f, float(nbl)), axis=1, keepdims=True)
            pick = n_f == first
            sel = jnp.where(pick, 1.0, sel)
            alive = jnp.where(pick, 0.0, alive)
        sel = jnp.where(valid, sel, 0.0)
        for k in range(NSA_KV_HEADS):
            for g in range(NSA_GROUP):
                sel_sc[pl.ds((k * NSA_GROUP + g) * dl, dl), :] = sel[k * dl:(k + 1) * dl]
        m_sc[...] = jnp.full(m_sc.shape, NEG_BIG, F32)
        l_sc[...] = jnp.zeros_like(l_sc)
        acc_sc[...] = jnp.zeros_like(acc_sc)

    selb = sel_sc[...].astype(BF16)
    n_col = lax.broadcasted_iota(jnp.int32, (nbl, 1), 0)

    def online_update(k, s, mask, v_tiles):
        rs = pl.ds(k * rows_h, rows_h)
        m_old = m_sc[rs, :]
        m_new = jnp.maximum(m_old, jnp.max(s, axis=1, keepdims=True))
        p = jnp.where(mask, jnp.exp(s - m_new), 0.0)
        alpha = jnp.exp(m_old - m_new)
        l_sc[rs, :] = alpha * l_sc[rs, :] + jnp.sum(p, axis=1, keepdims=True)
        pv = jnp.zeros((rows_h, NSA_HD), F32)
        for i, vt in enumerate(v_tiles):
            pv = pv + _dot(p[:, i * LANES:(i + 1) * LANES].astype(BF16), vt)
        acc_sc[rs, :] = alpha * acc_sc[rs, :] + pv
        m_sc[rs, :] = m_new

    t_h = tok_pos(rows_h)

    def selected(kpos):
        blk = jnp.where(n_col == (kpos >> SLC_SHIFT), 1.0, 0.0).astype(BF16)
        return _dot(selb, blk)

    kpos = step * (pg * page) + lax.broadcasted_iota(jnp.int32, (1, pg * page), 1)
    sel_all = selected(kpos)
    for k in range(NSA_KV_HEADS):
        qk = qb_sc[pl.ds(k * rows_h, rows_h), :]
        mask = (sel_all[k * rows_h:(k + 1) * rows_h] > 0.5) & (kpos <= t_h)
        s = jnp.concatenate([_dot_nt(qk, _head_rows(r, k).astype(BF16)) for r in kp_refs], axis=1)
        s = jnp.where(mask, s, NEG_BIG)
        online_update(k, s, mask, [_head_rows(r, k).astype(BF16) for r in vp_refs])

    @pl.when(step == n_steps - 1)
    def _finish():
        padn = LANES - dl
        j_new = lax.broadcasted_iota(jnp.int32, (1, LANES), 1)
        kpos_n = q_off + j_new
        sel_n = selected(kpos_n)
        for k in range(NSA_KV_HEADS):
            kn = jnp.concatenate([kvn_ref[:, k * NSA_HD:(k + 1) * NSA_HD], jnp.zeros((padn, NSA_HD), F32)], axis=0)
            vn = jnp.concatenate([kvn_ref[:, (2 + k) * NSA_HD:(3 + k) * NSA_HD], jnp.zeros((padn, NSA_HD), F32)],
                                 axis=0)
            qk = qb_sc[pl.ds(k * rows_h, rows_h), :]
            mask = (sel_n[k * rows_h:(k + 1) * rows_h] > 0.5) & (kpos_n <= t_h) & (j_new < dl)
            s = jnp.where(mask, _dot_nt(qk, kn.astype(BF16)), NEG_BIG)
            online_update(k, s, mask, [vn.astype(BF16)])
        j_w = lax.broadcasted_iota(jnp.int32, (1, w_buf + LANES), 1)
        pos_w = q_off - w_buf + j_w
        dlt = t_h - pos_w
        wmask = (j_w < w_buf + dl) & (pos_w >= 0) & (dlt >= 0) & (dlt < WINDOW)
        gates = jax.nn.sigmoid(gate_ref[...])
        for k in range(NSA_KV_HEADS):
            kw = jnp.concatenate([_head_rows(cw_ref, k),
                                  wn_ref[:, k * NSA_HD:(k + 1) * NSA_HD], jnp.zeros((padn, NSA_HD), F32)], axis=0)
            vw = jnp.concatenate([_head_rows(cw_ref, NSA_KV_HEADS + k),
                                  wn_ref[:, (2 + k) * NSA_HD:(3 + k) * NSA_HD], jnp.zeros((padn, NSA_HD), F32)],
                                 axis=0)
            qk = qb_sc[pl.ds(k * rows_h, rows_h), :]
            s = jnp.where(wmask, _dot_nt(qk, kw.astype(BF16)), -jnp.inf)
            m = jnp.max(s, axis=1, keepdims=True)
            m = jnp.where(m > -jnp.inf, m, 0.0)
            e = jnp.exp(s - m)
            d = jnp.sum(e, axis=1, keepdims=True)
            o_win = _dot((e / jnp.where(d > 0, d, 1.0)).astype(BF16), vw.astype(BF16))
            rs = pl.ds(k * rows_h, rows_h)
            o_slc = acc_sc[rs, :] / l_sc[rs, :]
            o_cmp = ocmp_sc[rs, :]
            for g in range(NSA_GROUP):
                h = k * NSA_GROUP + g
                r = slice(g * dl, (g + 1) * dl)
                gc = gates[:, h * N_BRANCH:h * N_BRANCH + 1]
                gs = gates[:, h * N_BRANCH + 1:h * N_BRANCH + 2]
                gw = gates[:, h * N_BRANCH + 2:h * N_BRANCH + 3]
                o_ref[:, h * NSA_HD:(h + 1) * NSA_HD] = gc * o_cmp[r] + gs * o_slc[r] + gw * o_win[r]


def _nsa_sample(P, CMP, cache_kv, cache_win, layer, page_table, DB, DL, past_len, page):
    n_pages = page_table.shape[1]
    w_buf = cache_win.shape[2]
    lk = past_len + DL
    nch = CMP.shape[3]
    n_cmp = lk // CMP_STRIDE - CMP_BLOCK // CMP_STRIDE + 1
    n_slc = -(-lk // SLC_BLOCK)
    n_sel = min(N_SELECT, n_slc)
    nbl = _round_up(n_slc, LANES)
    pg = _pick_tile(n_pages, (16, 8, 4, 2, 1))
    n_steps = n_pages // pg
    assert DL % SUBLANES == 0 and DL <= LANES and page == LANES and past_len == n_pages * page
    cov = jnp.asarray(_cover_matrix(n_cmp, n_slc, nch, nbl))
    kvw = NSA_KV_HEADS * NSA_HD
    page_spec = lambda i, c: pl.BlockSpec((None, None, page, None, NSA_KV_HEADS, NSA_HD),
                                          lambda b, s, pt: (layer, pt[b, s * pg + i], 0, c, 0, 0))
    in_specs = [
        pl.BlockSpec((DL, NSA_HEADS * NSA_HD), lambda b, s, pt: (b, 3)),
        pl.BlockSpec((DL, 2 * kvw), lambda b, s, pt: (b, 9)),
        pl.BlockSpec((DL, 2 * kvw), lambda b, s, pt: (b, 10)),
        pl.BlockSpec((DL, LANES), lambda b, s, pt: (b, 44)),
        pl.BlockSpec((None, None, NSA_KV_HEADS, nch, NSA_HD), lambda b, s, pt: (b, 0, 0, 0, 0)),
        pl.BlockSpec((None, None, NSA_KV_HEADS, nch, NSA_HD), lambda b, s, pt: (b, 1, 0, 0, 0)),
        pl.BlockSpec((nch, nbl), lambda b, s, pt: (0, 0)),
        pl.BlockSpec((None, None, w_buf, 2, NSA_KV_HEADS, NSA_HD), lambda b, s, pt: (layer, b, 0, 0, 0, 0)),
    ] + [page_spec(i, 2) for i in range(pg)] + [page_spec(i, 3) for i in range(pg)]
    rows = NSA_HEADS * DL
    grid_spec = pltpu.PrefetchScalarGridSpec(
        num_scalar_prefetch=1,
        grid=(DB, n_steps),
        in_specs=in_specs,
        out_specs=pl.BlockSpec((DL, NSA_HEADS * NSA_HD), lambda b, s, pt: (b, 0)),
        scratch_shapes=[
            pltpu.VMEM((rows, NSA_HD), BF16),
            pltpu.VMEM((rows, nbl), F32),
            pltpu.VMEM((rows, 1), F32),
            pltpu.VMEM((rows, 1), F32),
            pltpu.VMEM((rows, NSA_HD), F32),
            pltpu.VMEM((rows, NSA_HD), F32),
        ],
    )
    body = functools.partial(_nsa_sample_body, pg=pg, n_steps=n_steps, dl=DL, q_off=past_len, w_buf=w_buf,
                             n_cmp=n_cmp, n_slc=n_slc, n_sel=n_sel, nbl=nbl, page=page)
    return pl.pallas_call(
        body, grid_spec=grid_spec,
        out_shape=jax.ShapeDtypeStruct((DB * DL, NSA_HEADS * NSA_HD), F32),
        compiler_params=_cparams(("parallel", "arbitrary")),
        name="nsa_sample",
    )(page_table, P, P, P, P, CMP, CMP, cov, cache_win, *([cache_kv] * (2 * pg)))


def _out_ln_body(h_ref, ro_ref, no_ref, wr_ref, wn_ref, g_ref, b_ref, o_ref, *, alpha):
    m = _dot(ro_ref[...].astype(BF16), wr_ref[...]) + _dot(no_ref[...].astype(BF16), wn_ref[...])
    o_ref[...] = _layer_norm(alpha * h_ref[...] + m, g_ref[...], b_ref[...])


def _out_ln(h, ro, no, w_out, g, b, alpha):
    T, D = h.shape
    kr = ro.shape[1]
    kn = no.shape[1]
    tm = _pick_tile(T, (512, 256, 128, 64, 32, 16, 8))
    return pl.pallas_call(
        functools.partial(_out_ln_body, alpha=alpha),
        grid=(T // tm,),
        in_specs=[
            pl.BlockSpec((tm, D), lambda i: (i, 0)),
            pl.BlockSpec((tm, kr), lambda i: (i, 0)),
            pl.BlockSpec((tm, kn), lambda i: (i, 0)),
            pl.BlockSpec((kr, D), lambda i: (0, 0)),
            pl.BlockSpec((kn, D), lambda i: (1, 0)),
            pl.BlockSpec((1, D), lambda i: (0, 0)),
            pl.BlockSpec((1, D), lambda i: (0, 0)),
        ],
        out_specs=pl.BlockSpec((tm, D), lambda i: (i, 0)),
        out_shape=jax.ShapeDtypeStruct((T, D), F32),
        compiler_params=_cparams(("parallel",)),
        name="out_ln",
    )(h, ro, no, w_out, w_out, g.reshape(1, D), b.reshape(1, D))


def _rope_tables(pos):
    half = NSA_HD // 2
    inv = ROPE_THETA ** (-jnp.arange(half, dtype=F32) / half)
    ang = pos.astype(F32)[:, None] * inv[None, :]
    cos = jnp.cos(ang)
    sin = jnp.sin(ang)
    return jnp.concatenate([cos, cos], -1), jnp.concatenate([-sin, sin], -1)


def _cmp_weights(w1, pos, w2):
    r = CMP_BLOCK // CMP_STRIDE
    w1r = w1.reshape(r, CMP_STRIDE * NSA_HD, CMP_HIDDEN)
    w1c = jnp.concatenate([w1r[i] for i in range(r)], axis=1).astype(BF16)
    posr = jnp.pad(pos.reshape(r, CMP_STRIDE * NSA_HD), ((0, SUBLANES - r), (0, 0))).astype(BF16)
    return w1c, posr, w2.astype(BF16)


def _layer_view(arr, l, shape):
    return arr.reshape(shape) if arr.shape[0] == 1 else arr[l].reshape(shape)


def _decoder_layer(x, B, L, q_off, s0, p, sample_ctx):
    alpha = p['alpha']
    h1, h1_bf = _ffn_ln(x, p['ffn1_w_up'], p['ffn1_w_down'], p['ln1_g'], p['ln1_b'], alpha, True)
    cos, sin = _rope_tables(q_off + jnp.arange(L, dtype=jnp.int32))
    P, kv_rows, win_rows = _proj(h1_bf, p['w_in'], p['rope_cols'], p['scale_cols'], cos, sin, L)
    ro, ret_s = _retention(P, s0, p['ret_gn_g'], p['ret_gn_b'], B, L)
    if sample_ctx is None:
        assert L % CMP_STRIDE == 0
        n_cmp = L // CMP_STRIDE - CMP_BLOCK // CMP_STRIDE + 1
        specs = [pl.BlockSpec((L, NSA_HD), (lambda hd: (lambda b, t, g: (b, 32 + NSA_KV_HEADS * t + hd)))(hd))
                 for hd in range(NSA_KV_HEADS)]
        CMP = _compress([P] * NSA_KV_HEADS, specs, 1, L, 1, n_cmp, B, p['cmp_w1'], p['cmp_pos'], p['cmp_w2'])
        no = _nsa_prompt(P, CMP, B, L)
    else:
        cache_kv, cache_win, layer, page_table, past_len, page = sample_ctx
        n_pages = page_table.shape[1]
        lk = past_len + L
        assert (lk // CMP_STRIDE) * CMP_STRIDE <= past_len, "compression blocks must lie in the paged past"
        n_cmp = lk // CMP_STRIDE - CMP_BLOCK // CMP_STRIDE + 1
        n_in = _pick_tile(n_pages, (32, 16, 8, 4, 2, 1))
        n_grp = n_pages // n_in
        specs = [pl.BlockSpec((None, None, page, None, NSA_KV_HEADS, NSA_HD),
                              (lambda i: (lambda b, t, g, pt: (layer, pt[b, (n_grp - 1 - g) * n_in + i], 0, t, 0,
                                                               0)))(i))
                 for i in range(n_in)]
        CMP = _compress([cache_kv] * n_in, specs, n_in, page, n_grp, n_cmp, B,
                        p['cmp_w1'], p['cmp_pos'], p['cmp_w2'], page_table=page_table)
        no = _nsa_sample(P, CMP, cache_kv, cache_win, layer, page_table, B, L, past_len, page)
    x2 = _out_ln(h1, ro, no, p['w_out'], p['ln2_g'], p['ln2_b'], alpha)
    y, _ = _ffn_ln(x2, p['ffn2_w_up'], p['ffn2_w_down'], p['ln3_g'], p['ln3_b'], alpha, False)
    return y, ret_s, kv_rows, win_rows


def kernel(x_prompt, x_sample, state_ret, cache_nsa_kv, cache_win, page_table, ffn1_w_up, ffn1_w_down, ln1_g, ln1_b, w_in, w_out, ret_gn_g, ret_gn_b, cmp_pos_k, cmp_w1_k, cmp_w2_k, cmp_pos_v, cmp_w1_v, cmp_w2_v, ln2_g, ln2_b, ffn2_w_up, ffn2_w_down, ln3_g, ln3_b):
    B, L, D = x_prompt.shape
    DB, DL, _ = x_sample.shape
    depth = w_in.shape[0]
    n_pool, page = cache_nsa_kv.shape[1], cache_nsa_kv.shape[2]
    n_pages = page_table.shape[1]
    past_len = n_pages * page
    w_buf = cache_win.shape[2]
    alpha = (2.0 * depth) ** 0.25
    rope_np = np.zeros((N_IN_PAD // LANES, LANES), np.float32)
    rope_np[list(ROPE_CHUNKS)] = 1.0
    scale_np = np.ones((N_IN_PAD // LANES, LANES), np.float32)
    scale_np[list(KSCALE_CHUNKS)] = RET_DK ** -0.5
    rope_cols = jnp.asarray(rope_np.reshape(1, N_IN_PAD))
    scale_cols = jnp.asarray(scale_np.reshape(1, N_IN_PAD))

    yp = x_prompt.reshape(B * L, D)
    ys = x_sample.reshape(DB * DL, D)
    outs = [[] for _ in range(6)]
    for l in range(depth):
        k1, p1, k2 = _cmp_weights(cmp_w1_k[l], cmp_pos_k[l], cmp_w2_k[l])
        v1, q1, v2 = _cmp_weights(cmp_w1_v[l], cmp_pos_v[l], cmp_w2_v[l])
        p = {
            'alpha': alpha, 'rope_cols': rope_cols, 'scale_cols': scale_cols,
            'ffn1_w_up': ffn1_w_up[l].astype(BF16), 'ffn1_w_down': ffn1_w_down[l].astype(BF16),
            'ln1_g': ln1_g[l], 'ln1_b': ln1_b[l],
            'w_in': jnp.pad(w_in[l], ((0, 0), (0, N_IN_PAD - N_IN))).astype(BF16),
            'w_out': w_out[l].astype(BF16),
            'ret_gn_g': ret_gn_g[l], 'ret_gn_b': ret_gn_b[l],
            'cmp_w1': jnp.stack([k1, v1]), 'cmp_pos': jnp.stack([p1, q1]), 'cmp_w2': jnp.stack([k2, v2]),
            'ln2_g': ln2_g[l], 'ln2_b': ln2_b[l],
            'ffn2_w_up': ffn2_w_up[l].astype(BF16), 'ffn2_w_down': ffn2_w_down[l].astype(BF16),
            'ln3_g': ln3_g[l], 'ln3_b': ln3_b[l],
        }
        s0 = jnp.zeros((B, RET_HEADS, RET_DK, RET_DV), F32)
        yp, rs_p, kv_p, win_p = _decoder_layer(yp, B, L, 0, s0, p, None)
        ctx = (cache_nsa_kv, cache_win, l, page_table, past_len, page)
        ys, rs_s, kv_s, win_s = _decoder_layer(ys, DB, DL, past_len,
                                               _layer_view(state_ret, l, state_ret.shape[1:]), p, ctx)
        wl = min(WINDOW, L)
        outs[0].append(rs_p)
        outs[1].append(rs_s)
        outs[2].append(kv_p.reshape(B, L, 4, NSA_KV_HEADS, NSA_HD))
        outs[3].append(kv_s.reshape(DB, DL, 4, NSA_KV_HEADS, NSA_HD))
        outs[4].append(win_p.reshape(B, L, 2, NSA_KV_HEADS, NSA_HD)[:, L - wl:])
        win_s = win_s.reshape(DB, DL, 2, NSA_KV_HEADS, NSA_HD)
        outs[5].append(jnp.concatenate([cache_win[l], win_s], axis=1)[:, -w_buf:])
    return (yp.reshape(B, L, D), ys.reshape(DB, DL, D), jnp.stack(outs[0]), jnp.stack(outs[1]),
            jnp.stack(outs[2]), jnp.stack(outs[3]), jnp.stack(outs[4]), jnp.stack(outs[5]))
```

```python
import functools

import numpy as np
import jax
import jax.numpy as jnp
from jax import lax
from jax.experimental import pallas as pl
from jax.experimental.pallas import tpu as pltpu

F32 = jnp.float32
BF16 = jnp.bfloat16

LANES = 128
SUBLANES = 8
VMEM_LIMIT_BYTES = 56 * 1024 * 1024

RET_HEADS = 4
RET_DK = 128
RET_DV = 256
RET_CHUNK = 128
NSA_HEADS = 8
NSA_KV_HEADS = 2
NSA_HD = 128
NSA_GROUP = NSA_HEADS // NSA_KV_HEADS
CMP_BLOCK = 32
CMP_STRIDE = 16
CMP_HIDDEN = 2 * NSA_HD
SLC_BLOCK = 64
SLC_SHIFT = 6
N_SELECT = 16
WINDOW = 512
N_BRANCH = 3
ROPE_THETA = 10000.0
LN_EPS = 1e-5
NEG_BIG = -1e30
LOG2_E = 1.4426950408889634

N_IN = 5656
N_IN_PAD = 5760
KV_CHUNK0 = 32
WIN_CHUNK0 = 40
GATE_CHUNK = 44
ROPE_CHUNKS = tuple(range(0, 8)) + tuple(range(24, 32)) + (32, 33, 36, 37, 40, 41)
KSCALE_CHUNKS = tuple(range(4, 8))


def _cparams(sem):
    return pltpu.CompilerParams(dimension_semantics=sem, vmem_limit_bytes=VMEM_LIMIT_BYTES)


def _pick_tile(n, candidates):
    for c in candidates:
        if n % c == 0:
            return c
    return n


def _round_up(n, m):
    return (n + m - 1) // m * m


def _layer_norm(z, g, b):
    mu = jnp.mean(z, axis=-1, keepdims=True)
    zc = z - mu
    var = jnp.mean(zc * zc, axis=-1, keepdims=True)
    return zc * lax.rsqrt(var + LN_EPS) * g + b


def _dot(a, b):
    return jnp.dot(a, b, preferred_element_type=F32)


def _dot_nt(a, b):
    return lax.dot_general(a, b, (((1,), (1,)), ((), ())), preferred_element_type=F32)


def _ffn_ln_body(x_ref, wa_ref, wb_ref, wd_ref, g_ref, b_ref, *rest, nj, alpha, emit_bf16):
    if emit_bf16:
        o_ref, obf_ref, xbf_sc, acc_sc = rest
    else:
        o_ref, xbf_sc, acc_sc = rest
    j = pl.program_id(1)

    @pl.when(j == 0)
    def _init():
        xbf_sc[...] = x_ref[...].astype(BF16)
        acc_sc[...] = jnp.zeros_like(acc_sc)

    xb = xbf_sc[...]
    a = _dot(xb, wa_ref[...])
    b = _dot(xb, wb_ref[...])
    h = (a * jax.nn.sigmoid(a)) * b
    acc_sc[...] += _dot(h.astype(BF16), wd_ref[...])

    @pl.when(j == nj - 1)
    def _finish():
        z = alpha * x_ref[...] + 0.5 * acc_sc[...]
        y = _layer_norm(z, g_ref[...], b_ref[...])
        o_ref[...] = y
        if emit_bf16:
            obf_ref[...] = y.astype(BF16)


def _ffn_ln(x, w_up, w_down, g, b, alpha, emit_bf16):
    T, D = x.shape
    F = w_down.shape[0]
    tm = _pick_tile(T, (512, 256, 128, 64, 32, 16, 8))
    tf = _pick_tile(F, (512, 256, 128))
    nj = F // tf
    out_shape = [jax.ShapeDtypeStruct((T, D), F32)]
    out_specs = [pl.BlockSpec((tm, D), lambda i, j: (i, 0))]
    if emit_bf16:
        out_shape.append(jax.ShapeDtypeStruct((T, D), BF16))
        out_specs.append(pl.BlockSpec((tm, D), lambda i, j: (i, 0)))
    res = pl.pallas_call(
        functools.partial(_ffn_ln_body, nj=nj, alpha=alpha, emit_bf16=emit_bf16),
        grid=(T // tm, nj),
        in_specs=[
            pl.BlockSpec((tm, D), lambda i, j: (i, 0)),
            pl.BlockSpec((D, tf), lambda i, j: (0, j)),
            pl.BlockSpec((D, tf), lambda i, j: (0, nj + j)),
            pl.BlockSpec((tf, D), lambda i, j: (j, 0)),
            pl.BlockSpec((1, D), lambda i, j: (0, 0)),
            pl.BlockSpec((1, D), lambda i, j: (0, 0)),
        ],
        out_specs=out_specs,
        out_shape=out_shape,
        scratch_shapes=[pltpu.VMEM((tm, D), BF16), pltpu.VMEM((tm, D), F32)],
        compiler_params=_cparams(("parallel", "arbitrary")),
        name="ffn_ln",
    )(x, w_up, w_up, w_down, g.reshape(1, D), b.reshape(1, D))
    return res if emit_bf16 else (res[0], None)


def _rows_view(ref, j):
    rows = ref.shape[0]
    n = int(np.prod(ref.shape[1:-1]))
    return ref.reshape(n * rows, ref.shape[-1]), pl.ds(j, rows, stride=n)


def _head_rows(ref, j):
    view, idx = _rows_view(ref, j)
    return view[idx, :]


def _store_rows(ref, j, val):
    view, idx = _rows_view(ref, j)
    view[idx, :] = val


def _proj_body(x_ref, w_ref, cos_ref, sin_ref, rope_ref, scale_ref, o_ref, kv_ref, win_ref, *, n_chunk, j_rows):
    y = _dot(x_ref[...], w_ref[...])
    cos = cos_ref[...]
    sin = sin_ref[...]
    for c in range(n_chunk):
        sl = slice(c * LANES, (c + 1) * LANES)
        yc = y[:, sl]
        roped = yc * cos + pltpu.roll(yc, NSA_HD // 2, 1) * sin
        o_ref[:, sl] = jnp.where(rope_ref[:, sl] > 0.5, roped, yc) * scale_ref[:, sl]

    @pl.when(pl.program_id(1) == j_rows)
    def _emit_cache_rows():
        c0 = KV_CHUNK0 - j_rows * n_chunk
        for r in range(WIN_CHUNK0 - KV_CHUNK0):
            _store_rows(kv_ref, r, o_ref[:, (c0 + r) * LANES:(c0 + r + 1) * LANES])
        c0 = WIN_CHUNK0 - j_rows * n_chunk
        for r in range(GATE_CHUNK - WIN_CHUNK0):
            _store_rows(win_ref, r, o_ref[:, (c0 + r) * LANES:(c0 + r + 1) * LANES])


def _proj(x_bf, w_bf, rope_cols, scale_cols, cos, sin, rows_per_seq):
    T, D = x_bf.shape
    N = w_bf.shape[1]
    tm = _pick_tile(T, (512, 256, 128, 64, 32, 16, 8))
    tn = 1920
    assert N % tn == 0
    n_chunk = tn // LANES
    if rows_per_seq >= tm:
        assert rows_per_seq % tm == 0
        n_tab = rows_per_seq // tm
    else:
        assert tm % rows_per_seq == 0
        cos = jnp.tile(cos, (tm // rows_per_seq, 1))
        sin = jnp.tile(sin, (tm // rows_per_seq, 1))
        n_tab = 1
    j_rows = KV_CHUNK0 // n_chunk
    assert (GATE_CHUNK - 1) // n_chunk == j_rows, "cache-row columns must sit in one column tile"
    kv_shape = (T, 4, NSA_KV_HEADS, NSA_HD)
    win_shape = (T, 2, NSA_KV_HEADS, NSA_HD)
    return pl.pallas_call(
        functools.partial(_proj_body, n_chunk=n_chunk, j_rows=j_rows),
        grid=(T // tm, N // tn),
        in_specs=[
            pl.BlockSpec((tm, D), lambda i, j: (i, 0)),
            pl.BlockSpec((D, tn), lambda i, j: (0, j)),
            pl.BlockSpec((tm, LANES), lambda i, j: (i % n_tab, 0)),
            pl.BlockSpec((tm, LANES), lambda i, j: (i % n_tab, 0)),
            pl.BlockSpec((1, tn), lambda i, j: (0, j)),
            pl.BlockSpec((1, tn), lambda i, j: (0, j)),
        ],
        out_specs=[
            pl.BlockSpec((tm, tn), lambda i, j: (i, j)),
            pl.BlockSpec((tm,) + kv_shape[1:], lambda i, j: (i, 0, 0, 0)),
            pl.BlockSpec((tm,) + win_shape[1:], lambda i, j: (i, 0, 0, 0)),
        ],
        out_shape=[
            jax.ShapeDtypeStruct((T, N), F32),
            jax.ShapeDtypeStruct(kv_shape, F32),
            jax.ShapeDtypeStruct(win_shape, F32),
        ],
        compiler_params=_cparams(("parallel", "arbitrary")),
        name="proj_rope",
    )(x_bf, w_bf, cos, sin, rope_cols, scale_cols)


def _ret_body(q_ref, k_ref, v_ref, g_ref, s0_ref, dm_ref, ind_ref, std_ref, cd_ref, gng_ref, gnb_ref,
              o_ref, sout_ref, s_sc, *, nc, rows, rows_pad):
    c = pl.program_id(1)

    @pl.when(c == 0)
    def _load_state():
        s_sc[...] = s0_ref[...]

    pad = rows_pad - rows
    for h in range(RET_HEADS):
        ks = slice(h * RET_DK, (h + 1) * RET_DK)
        vs = slice(h * RET_DV, (h + 1) * RET_DV)
        q = q_ref[:, ks]
        k = k_ref[:, ks]
        v = v_ref[:, vs]
        kd = k * std_ref[h]
        if pad:
            k = jnp.concatenate([k, jnp.zeros((pad, RET_DK), F32)], axis=0)
            kd = jnp.concatenate([kd, jnp.zeros((pad, RET_DK), F32)], axis=0)
            v = jnp.concatenate([v, jnp.zeros((pad, RET_DV), F32)], axis=0)
        s_old = s_sc[h]
        vb = v.astype(BF16)
        a = _dot_nt(q.astype(BF16), k.astype(BF16)) * dm_ref[h]
        o = _dot(a.astype(BF16), vb) + _dot((q * ind_ref[h]).astype(BF16), s_old.astype(BF16))
        s_sc[h] = s_old * cd_ref[h] + _dot(kd.T.astype(BF16), vb)
        mu = jnp.mean(o, axis=-1, keepdims=True)
        oc = o - mu
        var = jnp.mean(oc * oc, axis=-1, keepdims=True)
        on = oc * lax.rsqrt(var + LN_EPS) * gng_ref[:, vs] + gnb_ref[:, vs]
        gate = g_ref[:, vs]
        o_ref[:, vs] = (gate * jax.nn.sigmoid(gate)) * on

    @pl.when(c == nc - 1)
    def _store_state():
        sout_ref[...] = s_sc[...]


def _retention(P, s0, gn_g, gn_b, B, L):
    C = RET_CHUNK if L % RET_CHUNK == 0 else L
    nc = L // C
    CP = max(C, LANES)
    lg = jnp.log1p(-jnp.exp2(-5.0 - jnp.arange(RET_HEADS, dtype=F32)))
    i = jnp.arange(C, dtype=F32)
    diff = i[:, None] - i[None, :]
    dmask = jnp.where(diff >= 0, jnp.exp(lg[:, None, None] * jnp.maximum(diff, 0.0)), 0.0)
    dmask = jnp.pad(dmask, ((0, 0), (0, 0), (0, CP - C)))
    in_decay = jnp.broadcast_to(jnp.exp(lg[:, None] * (i + 1.0))[:, :, None], (RET_HEADS, C, RET_DK))
    st_decay = jnp.broadcast_to(jnp.exp(lg[:, None] * (C - 1.0 - i))[:, :, None], (RET_HEADS, C, RET_DK))
    chunk_decay = jnp.broadcast_to(jnp.exp(lg * C)[:, None, None], (RET_HEADS, 1, RET_DV))
    qw = RET_HEADS * RET_DK
    vw = RET_HEADS * RET_DV
    const3 = lambda b, c: (0, 0, 0)
    return pl.pallas_call(
        functools.partial(_ret_body, nc=nc, rows=C, rows_pad=CP),
        grid=(B, nc),
        in_specs=[
            pl.BlockSpec((C, qw), lambda b, c: (b * nc + c, 0)),
            pl.BlockSpec((C, qw), lambda b, c: (b * nc + c, 1)),
            pl.BlockSpec((C, vw), lambda b, c: (b * nc + c, 1)),
            pl.BlockSpec((C, vw), lambda b, c: (b * nc + c, 2)),
            pl.BlockSpec((None, RET_HEADS, RET_DK, RET_DV), lambda b, c: (b, 0, 0, 0)),
            pl.BlockSpec((RET_HEADS, C, CP), const3),
            pl.BlockSpec((RET_HEADS, C, RET_DK), const3),
            pl.BlockSpec((RET_HEADS, C, RET_DK), const3),
            pl.BlockSpec((RET_HEADS, 1, RET_DV), const3),
            pl.BlockSpec((1, vw), lambda b, c: (0, 0)),
            pl.BlockSpec((1, vw), lambda b, c: (0, 0)),
        ],
        out_specs=[
            pl.BlockSpec((C, vw), lambda b, c: (b * nc + c, 0)),
            pl.BlockSpec((None, RET_HEADS, RET_DK, RET_DV), lambda b, c: (b, 0, 0, 0)),
        ],
        out_shape=[
            jax.ShapeDtypeStruct((B * L, vw), F32),
            jax.ShapeDtypeStruct((B, RET_HEADS, RET_DK, RET_DV), F32),
        ],
        scratch_shapes=[pltpu.VMEM((RET_HEADS, RET_DK, RET_DV), F32)],
        compiler_params=_cparams(("parallel", "arbitrary")),
        name="retention",
    )(P, P, P, P, s0, dmask, in_decay, st_decay, chunk_decay, gn_g.reshape(1, vw), gn_b.reshape(1, vw))


def _cmp_body(*refs, n_in, rows, n_grp, n_cmp, paged):
    if paged:
        refs = refs[1:]
    n_src = n_in if paged else NSA_KV_HEADS * n_in
    x_refs = refs[:n_src]
    w1_ref, pos_ref, w2_ref, o_ref, carry_sc = refs[n_src:]
    g = pl.program_id(2)
    grp = n_grp - 1 - g
    cpi = rows // CMP_STRIDE
    M = n_in * cpi

    @pl.when(g == 0)
    def _init():
        carry_sc[...] = jnp.zeros_like(carry_sc)

    w1 = w1_ref[...]
    gp = _dot(pos_ref[...], w1)
    posterm = gp[0:1, :CMP_HIDDEN] + gp[1:2, CMP_HIDDEN:]
    row = lax.broadcasted_iota(jnp.int32, (M, 1), 0)
    for hd in range(NSA_KV_HEADS):
        blocks = []
        if paged:
            for x_ref in x_refs:
                x3 = _head_rows(x_ref, hd).reshape(cpi, CMP_STRIDE, NSA_HD)
                xt = pltpu.einshape("csd->scd", x3)
                blocks.append(jnp.concatenate([xt[s] for s in range(CMP_STRIDE)], axis=1))
        else:
            for x_ref in x_refs[hd * n_in:(hd + 1) * n_in]:
                blocks.append(jnp.concatenate(
                    [x_ref[pl.ds(s, cpi, stride=CMP_STRIDE), :] for s in range(CMP_STRIDE)], axis=1))
        xc = blocks[0] if n_in == 1 else jnp.concatenate(blocks, axis=0)
        gg = _dot(xc.astype(BF16), w1)
        g0 = gg[:, :CMP_HIDDEN]
        g1 = gg[:, CMP_HIDDEN:]
        nxt = pltpu.roll(g1, M - 1, 0)
        nxt = jnp.where(row == M - 1, carry_sc[hd][0:1, :], nxt)
        carry_sc[hd] = g1[0:SUBLANES, :]
        hid = g0 + nxt + posterm
        out = _dot(jax.nn.gelu(hid).astype(BF16), w2_ref[...])
        o_ref[hd] = jnp.where(grp * M + row < n_cmp, out, 0.0)


def _compress(srcs, src_specs, n_in, rows, n_grp, n_cmp, B, w1, pos, w2, page_table=None):
    M = n_in * rows // CMP_STRIDE
    nch = n_grp * M
    paged = page_table is not None
    extra = (lambda *a: a[:3]) if paged else (lambda *a: a)
    in_specs = list(src_specs) + [
        pl.BlockSpec((None, CMP_STRIDE * NSA_HD, 2 * CMP_HIDDEN), lambda *a: (extra(*a)[1], 0, 0)),
        pl.BlockSpec((None, SUBLANES, CMP_STRIDE * NSA_HD), lambda *a: (extra(*a)[1], 0, 0)),
        pl.BlockSpec((None, CMP_HIDDEN, NSA_HD), lambda *a: (extra(*a)[1], 0, 0)),
    ]
    out_spec = pl.BlockSpec((None, None, NSA_KV_HEADS, M, NSA_HD),
                            lambda *a: (extra(*a)[0], extra(*a)[1], 0, n_grp - 1 - extra(*a)[2], 0))
    body = functools.partial(_cmp_body, n_in=n_in, rows=rows, n_grp=n_grp, n_cmp=n_cmp, paged=paged)
    out_shape = jax.ShapeDtypeStruct((B, 2, NSA_KV_HEADS, nch, NSA_HD), F32)
    scratch = [pltpu.VMEM((NSA_KV_HEADS, SUBLANES, CMP_HIDDEN), F32)]
    sem = ("parallel", "arbitrary", "arbitrary")
    if paged:
        grid_spec = pltpu.PrefetchScalarGridSpec(
            num_scalar_prefetch=1, grid=(B, 2, n_grp), in_specs=in_specs, out_specs=out_spec,
            scratch_shapes=scratch)
        return pl.pallas_call(body, grid_spec=grid_spec, out_shape=out_shape,
                              compiler_params=_cparams(sem), name="nsa_compress_paged")(
            page_table, *srcs, w1, pos, w2)
    return pl.pallas_call(body, grid=(B, 2, n_grp), in_specs=in_specs, out_specs=out_spec,
                          out_shape=out_shape, scratch_shapes=scratch,
                          compiler_params=_cparams(sem), name="nsa_compress")(*srcs, w1, pos, w2)


def _cover_matrix(n_cmp, n_slc, rows, cols):
    c_i = np.arange(n_cmp)[:, None]
    n_i = np.arange(n_slc)[None, :]
    cov = np.clip(np.minimum(c_i * CMP_STRIDE + CMP_BLOCK, (n_i + 1) * SLC_BLOCK)
                  - np.maximum(c_i * CMP_STRIDE, n_i * SLC_BLOCK), 0, None).astype(np.float32) / CMP_BLOCK
    out = np.zeros((rows, cols), np.float32)
    out[:n_cmp, :n_slc] = cov
    return out


def _nsa_prompt_body(q_ref, kc_ref, vc_ref, ks_ref, vs_ref, kw_ref, vw_ref, gate_ref, covt_ref, o_ref,
                     kcb_sc, vct_sc, ksb_sc, vst_sc, kwb_sc, vwt_sc, prio_sc, sel_sc, gt_sc,
                     m_sc, l_sc, acc_sc, out_sc, s_sc, *, tq, L, n_cmp, n_slc, n_sel, nbp):
    kvh = pl.program_id(1)
    qi = pl.program_id(2)
    tk = tq
    nch = kc_ref.shape[0]
    scale = NSA_HD ** -0.5

    @pl.when(qi == 0)
    def _stage_kv():
        kcb_sc[...] = kc_ref[...].astype(BF16)
        vct_sc[...] = vc_ref[...].T.astype(BF16)
        ksb_sc[...] = ks_ref[...].astype(BF16)
        kwb_sc[...] = kw_ref[...].astype(BF16)
        for i in range(L // tk):
            vst_sc[i] = vs_ref[i * tk:(i + 1) * tk, :].T.astype(BF16)
            vwt_sc[i] = vw_ref[i * tk:(i + 1) * tk, :].T.astype(BF16)

    G = NSA_GROUP
    W = G * tq
    t0 = qi * tq
    t_row = t0 + lax.broadcasted_iota(jnp.int32, (1, tq), 1)
    lane = lax.broadcasted_iota(jnp.int32, (1, W), 1)
    t_all = t0 + (lane & (tq - 1))
    qcat = jnp.concatenate([q_ref[:, g * NSA_HD:(g + 1) * NSA_HD] for g in range(G)], axis=0)
    qcat = (qcat * (scale * LOG2_E)).astype(BF16)
    gt_sc[...] = jax.nn.sigmoid(gate_ref[...]).T

    def gate_row(br):
        return jnp.concatenate(
            [gt_sc[pl.ds((kvh * G + g) * N_BRANCH + br, 1), :] for g in range(G)], axis=1)

    c_i = lax.broadcasted_iota(jnp.int32, (nch, 1), 0)
    cmask = (c_i * CMP_STRIDE + (CMP_BLOCK - 1) <= t_all) & (c_i < n_cmp)
    s = jnp.where(cmask, _dot_nt(kcb_sc[...], qcat), -jnp.inf)
    m = jnp.max(s, axis=0, keepdims=True)
    m = jnp.where(m > -jnp.inf, m, 0.0)
    e = jnp.exp2(s - m)
    d = jnp.sum(e, axis=0, keepdims=True)
    p = e / jnp.where(d > 0, d, 1.0)
    out_sc[...] = gate_row(0) * _dot(vct_sc[...], p.astype(BF16))
    psum = p[:, 0:tq]
    for g in range(1, G):
        psum = psum + p[:, g * tq:(g + 1) * tq]

    imp = jnp.dot(covt_ref[...], psum, preferred_element_type=F32, precision=lax.Precision.HIGHEST)
    n_i = lax.broadcasted_iota(jnp.int32, (nbp, 1), 0)
    valid = (n_i * SLC_BLOCK <= t_row) & (n_i < n_slc)
    cur = t_row >> SLC_SHIFT
    forced = (n_i == 0) | (n_i == cur) | (n_i == cur - 1)
    prio = jnp.where(forced, jnp.inf, jnp.where(valid, imp, -jnp.inf))
    prio_sc[...] = prio
    cnt = jnp.zeros((nbp, tq), jnp.int32)
    for mm in range(n_slc):
        pm = prio_sc[mm:mm + 1, :]
        beats = (pm > prio) | ((pm == prio) & (n_i > mm))
        cnt = cnt + beats.astype(jnp.int32)
    selbias = jnp.where((cnt < n_sel) & valid, 0.0, NEG_BIG)
    sel_sc[...] = jnp.concatenate([selbias] * G, axis=1)

    kp_l = lax.broadcasted_iota(jnp.int32, (tk, 1), 0)
    t_l = lane & (tq - 1)
    causal_bias = jnp.where(kp_l <= t_l, 0.0, NEG_BIG)
    far_bias = jnp.where(kp_l > t_l, 0.0, NEG_BIG)

    def reset():
        m_sc[...] = jnp.full((1, W), NEG_BIG, F32)
        l_sc[...] = jnp.zeros((1, W), F32)
        acc_sc[...] = jnp.zeros((NSA_HD, W), F32)

    def score(kb_sc, kt, bias):
        k0 = pl.multiple_of(kt * tk, tk)
        s = _dot_nt(kb_sc[pl.ds(k0, tk), :], qcat)
        return s if bias is None else s + bias

    def update(vt_sc, kt, s):
        m_old = m_sc[...]
        m_new = jnp.maximum(m_old, jnp.max(s, axis=0, keepdims=True))
        p = jnp.exp2(s - m_new)
        alpha = jnp.exp2(m_old - m_new)
        l_sc[...] = alpha * l_sc[...] + jnp.sum(p, axis=0, keepdims=True)
        acc_sc[...] = alpha * acc_sc[...] + _dot(vt_sc[kt], p.astype(BF16))
        m_sc[...] = m_new

    def sel_bias(kt):
        per_tile = tk // SLC_BLOCK
        rows = [jnp.broadcast_to(sel_sc[pl.ds(kt * per_tile + r, 1), :], (SLC_BLOCK, W))
                for r in range(per_tile)]
        return jnp.concatenate(rows, axis=0)

    def flush(br):
        out_sc[...] += (gate_row(br) / l_sc[...]) * acc_sc[...]

    reset()
    s_sc[...] = score(ksb_sc, 0, sel_bias(0))

    def slc_body(kt, carry):
        s_next = score(ksb_sc, kt + 1, sel_bias(kt + 1))
        update(vst_sc, kt, s_sc[...])
        s_sc[...] = s_next
        return carry

    lax.fori_loop(0, qi, slc_body, 0)
    update(vst_sc, qi, s_sc[...] + causal_bias)
    flush(1)

    reset()
    far = WINDOW // tk
    tiles = []
    for back in range(far, -1, -1):
        bias = far_bias if back == far else (causal_bias if back == 0 else None)
        if back > 0:
            off = jnp.where(qi >= back, 0.0, NEG_BIG).astype(F32)
            bias = off if bias is None else bias + off
        kt = jnp.maximum(qi - back, 0)
        tiles.append((kt, score(kwb_sc, kt, bias)))
    for kt, s_w in tiles:
        update(vwt_sc, kt, s_w)
    flush(2)
    for g in range(G):
        o_ref[:, g * NSA_HD:(g + 1) * NSA_HD] = out_sc[:, g * tq:(g + 1) * tq].T


def _nsa_prompt(P, CMP, B, L):
    tq = _pick_tile(L, (256, 128))
    assert L % tq == 0 and tq % SLC_BLOCK == 0 and WINDOW % tq == 0 and tq & (tq - 1) == 0
    nq = L // tq
    nch = CMP.shape[3]
    n_cmp = L // CMP_STRIDE - CMP_BLOCK // CMP_STRIDE + 1
    n_slc = -(-L // SLC_BLOCK)
    n_sel = min(N_SELECT, n_slc)
    nbp = _round_up(n_slc, SUBLANES)
    covt = jnp.asarray(_cover_matrix(n_cmp, n_slc, nch, nbp).T)
    gw = NSA_GROUP * NSA_HD
    col = lambda c: (lambda b, k, i: (b, c + k))
    body = functools.partial(_nsa_prompt_body, tq=tq, L=L, n_cmp=n_cmp, n_slc=n_slc, n_sel=n_sel, nbp=nbp)
    return pl.pallas_call(
        body,
        grid=(B, NSA_KV_HEADS, nq),
        in_specs=[
            pl.BlockSpec((tq, gw), lambda b, k, i: (b * nq + i, 6 + k)),
            pl.BlockSpec((None, None, None, nch, NSA_HD), lambda b, k, i: (b, 0, k, 0, 0)),
            pl.BlockSpec((None, None, None, nch, NSA_HD), lambda b, k, i: (b, 1, k, 0, 0)),
            pl.BlockSpec((L, NSA_HD), col(36)),
            pl.BlockSpec((L, NSA_HD), col(38)),
            pl.BlockSpec((L, NSA_HD), col(40)),
            pl.BlockSpec((L, NSA_HD), col(42)),
            pl.BlockSpec((tq, LANES), lambda b, k, i: (b * nq + i, 44)),
            pl.BlockSpec((nbp, nch), lambda b, k, i: (0, 0)),
        ],
        out_specs=pl.BlockSpec((tq, gw), lambda b, k, i: (b * nq + i, k)),
        out_shape=jax.ShapeDtypeStruct((B * L, NSA_HEADS * NSA_HD), F32),
        scratch_shapes=[
            pltpu.VMEM((nch, NSA_HD), BF16),
            pltpu.VMEM((NSA_HD, nch), BF16),
            pltpu.VMEM((L, NSA_HD), BF16),
            pltpu.VMEM((L // tq, NSA_HD, tq), BF16),
            pltpu.VMEM((L, NSA_HD), BF16),
            pltpu.VMEM((L // tq, NSA_HD, tq), BF16),
            pltpu.VMEM((nbp, tq), F32),
            pltpu.VMEM((nbp, NSA_GROUP * tq), F32),
            pltpu.VMEM((LANES, tq), F32),
            pltpu.VMEM((1, NSA_GROUP * tq), F32),
            pltpu.VMEM((1, NSA_GROUP * tq), F32),
            pltpu.VMEM((NSA_HD, NSA_GROUP * tq), F32),
            pltpu.VMEM((NSA_HD, NSA_GROUP * tq), F32),
            pltpu.VMEM((tq, NSA_GROUP * tq), F32),
        ],
        compiler_params=_cparams(("parallel", "arbitrary", "arbitrary")),
        name="nsa_prompt",
    )(P, CMP, CMP, P, P, P, P, P, covt)


def _nsa_sample_body(*refs, pg, n_steps, dl, q_off, w_buf, n_cmp, n_slc, n_sel, nbl, page):
    refs = refs[1:]
    q_ref, kvn_ref, wn_ref, gate_ref, kc_ref, vc_ref, cov_ref, cw_ref = refs[:8]
    kp_refs = refs[8:8 + pg]
    vp_refs = refs[8 + pg:8 + 2 * pg]
    o_ref, qb_sc, sel_sc, m_sc, l_sc, acc_sc, ocmp_sc = refs[8 + 2 * pg:]
    step = pl.program_id(1)
    rows_h = NSA_GROUP * dl
    rows = NSA_KV_HEADS * rows_h
    scale = NSA_HD ** -0.5
    nch = kc_ref.shape[1]

    def tok_pos(n):
        r = lax.broadcasted_iota(jnp.int32, (n, 1), 0)
        return q_off + r % dl

    @pl.when(step == 0)
    def _select():
        for k in range(NSA_KV_HEADS):
            for g in range(NSA_GROUP):
                h = k * NSA_GROUP + g
                qb_sc[pl.ds(h * dl, dl), :] = (q_ref[:, h * NSA_HD:(h + 1) * NSA_HD] * scale).astype(BF16)
        t_h = tok_pos(rows_h)
        c_i = lax.broadcasted_iota(jnp.int32, (1, nch), 1)
        cmask = (c_i * CMP_STRIDE + (CMP_BLOCK - 1) <= t_h) & (c_i < n_cmp)
        imps = []
        for k in range(NSA_KV_HEADS):
            qk = qb_sc[pl.ds(k * rows_h, rows_h), :]
            s = jnp.where(cmask, _dot_nt(qk, kc_ref[k].astype(BF16)), -jnp.inf)
            m = jnp.max(s, axis=1, keepdims=True)
            m = jnp.where(m > -jnp.inf, m, 0.0)
            e = jnp.exp(s - m)
            d = jnp.sum(e, axis=1, keepdims=True)
            p = e / jnp.where(d > 0, d, 1.0)
            ocmp_sc[pl.ds(k * rows_h, rows_h), :] = _dot(p.astype(BF16), vc_ref[k].astype(BF16))
            psum = p[0:dl]
            for g in range(1, NSA_GROUP):
                psum = psum + p[g * dl:(g + 1) * dl]
            imps.append(jnp.dot(psum, cov_ref[...], preferred_element_type=F32,
                                precision=lax.Precision.HIGHEST))
        imp = jnp.concatenate(imps, axis=0)
        t_s = tok_pos(NSA_KV_HEADS * dl)
        n_i = lax.broadcasted_iota(jnp.int32, (1, nbl), 1)
        valid = (n_i * SLC_BLOCK <= t_s) & (n_i < n_slc)
        cur = t_s >> SLC_SHIFT
        forced = (n_i == 0) | (n_i == cur) | (n_i == cur - 1)
        prio = jnp.where(forced, jnp.inf, jnp.where(valid, imp, -jnp.inf))
        n_f = n_i.astype(F32)
        alive = jnp.broadcast_to(jnp.where(n_i < n_slc, 1.0, 0.0), prio.shape)
        sel = jnp.zeros(prio.shape, F32)
        for _ in range(n_sel):
            mx = jnp.max(jnp.where(alive > 0.5, prio, -jnp.inf), axis=1, keepdims=True)
            cand = (alive > 0.5) & (prio == mx)
            first = jnp.min(jnp.where(cand, n_f, float(nbl)), axis=1, keepdims=True)
            pick = n_f == first
            sel = jnp.where(pick, 1.0, sel)
            alive = jnp.where(pick, 0.0, alive)
        sel = jnp.where(valid, sel, 0.0)
        for k in range(NSA_KV_HEADS):
            for g in range(NSA_GROUP):
                sel_sc[pl.ds((k * NSA_GROUP + g) * dl, dl), :] = sel[k * dl:(k + 1) * dl]
        m_sc[...] = jnp.full(m_sc.shape, NEG_BIG, F32)
        l_sc[...] = jnp.zeros_like(l_sc)
        acc_sc[...] = jnp.zeros_like(acc_sc)

    selb = sel_sc[...].astype(BF16)
    n_col = lax.broadcasted_iota(jnp.int32, (nbl, 1), 0)

    def online_update(k, s, mask, v_rows):
        rs = pl.ds(k * rows_h, rows_h)
        m_old = m_sc[rs, :]
        m_new = jnp.maximum(m_old, jnp.max(s, axis=1, keepdims=True))
        p = jnp.where(mask, jnp.exp(s - m_new), 0.0)
        alpha = jnp.exp(m_old - m_new)
        l_sc[rs, :] = alpha * l_sc[rs, :] + jnp.sum(p, axis=1, keepdims=True)
        acc_sc[rs, :] = alpha * acc_sc[rs, :] + _dot(p.astype(BF16), v_rows)
        m_sc[rs, :] = m_new

    t_h = tok_pos(rows_h)

    def selected(kpos):
        blk = jnp.where(n_col == (kpos >> SLC_SHIFT), 1.0, 0.0).astype(BF16)
        return _dot(selb, blk)

    kpos = step * (pg * page) + lax.broadcasted_iota(jnp.int32, (1, pg * page), 1)
    sel_all = selected(kpos)
    for k in range(NSA_KV_HEADS):
        qk = qb_sc[pl.ds(k * rows_h, rows_h), :]
        mask = (sel_all[k * rows_h:(k + 1) * rows_h] > 0.5) & (kpos <= t_h)
        k_rows = jnp.concatenate([_head_rows(r, k).astype(BF16) for r in kp_refs], axis=0)
        v_rows = jnp.concatenate([_head_rows(r, k).astype(BF16) for r in vp_refs], axis=0)
        s = jnp.where(mask, _dot_nt(qk, k_rows), NEG_BIG)
        online_update(k, s, mask, v_rows)

    @pl.when(step == n_steps - 1)
    def _finish():
        padn = LANES - dl
        j_new = lax.broadcasted_iota(jnp.int32, (1, LANES), 1)
        kpos_n = q_off + j_new
        sel_n = selected(kpos_n)
        for k in range(NSA_KV_HEADS):
            kn = jnp.concatenate([kvn_ref[:, k * NSA_HD:(k + 1) * NSA_HD], jnp.zeros((padn, NSA_HD), F32)], axis=0)
            vn = jnp.concatenate([kvn_ref[:, (2 + k) * NSA_HD:(3 + k) * NSA_HD], jnp.zeros((padn, NSA_HD), F32)],
                                 axis=0)
            qk = qb_sc[pl.ds(k * rows_h, rows_h), :]
            mask = (sel_n[k * rows_h:(k + 1) * rows_h] > 0.5) & (kpos_n <= t_h) & (j_new < dl)
            s = jnp.where(mask, _dot_nt(qk, kn.astype(BF16)), NEG_BIG)
            online_update(k, s, mask, vn.astype(BF16))
        j_w = lax.broadcasted_iota(jnp.int32, (1, w_buf + LANES), 1)
        pos_w = q_off - w_buf + j_w
        dlt = t_h - pos_w
        wmask = (j_w < w_buf + dl) & (pos_w >= 0) & (dlt >= 0) & (dlt < WINDOW)
        gates = jax.nn.sigmoid(gate_ref[...])
        for k in range(NSA_KV_HEADS):
            kw = jnp.concatenate([_head_rows(cw_ref, k),
                                  wn_ref[:, k * NSA_HD:(k + 1) * NSA_HD], jnp.zeros((padn, NSA_HD), F32)], axis=0)
            vw = jnp.concatenate([_head_rows(cw_ref, NSA_KV_HEADS + k),
                                  wn_ref[:, (2 + k) * NSA_HD:(3 + k) * NSA_HD], jnp.zeros((padn, NSA_HD), F32)],
                                 axis=0)
            qk = qb_sc[pl.ds(k * rows_h, rows_h), :]
            s = jnp.where(wmask, _dot_nt(qk, kw.astype(BF16)), -jnp.inf)
            m = jnp.max(s, axis=1, keepdims=True)
            m = jnp.where(m > -jnp.inf, m, 0.0)
            e = jnp.exp(s - m)
            d = jnp.sum(e, axis=1, keepdims=True)
            o_win = _dot((e / jnp.where(d > 0, d, 1.0)).astype(BF16), vw.astype(BF16))
            rs = pl.ds(k * rows_h, rows_h)
            o_slc = acc_sc[rs, :] / l_sc[rs, :]
            o_cmp = ocmp_sc[rs, :]
            for g in range(NSA_GROUP):
                h = k * NSA_GROUP + g
                r = slice(g * dl, (g + 1) * dl)
                gc = gates[:, h * N_BRANCH:h * N_BRANCH + 1]
                gs = gates[:, h * N_BRANCH + 1:h * N_BRANCH + 2]
                gw = gates[:, h * N_BRANCH + 2:h * N_BRANCH + 3]
                o_ref[:, h * NSA_HD:(h + 1) * NSA_HD] = gc * o_cmp[r] + gs * o_slc[r] + gw * o_win[r]


def _nsa_sample(P, CMP, cache_kv, cache_win, layer, page_table, DB, DL, past_len, page):
    n_pages = page_table.shape[1]
    w_buf = cache_win.shape[2]
    lk = past_len + DL
    nch = CMP.shape[3]
    n_cmp = lk // CMP_STRIDE - CMP_BLOCK // CMP_STRIDE + 1
    n_slc = -(-lk // SLC_BLOCK)
    n_sel = min(N_SELECT, n_slc)
    nbl = _round_up(n_slc, LANES)
    pg = _pick_tile(n_pages, (32, 16, 8, 4, 2, 1))
    n_steps = n_pages // pg
    assert DL % SUBLANES == 0 and DL <= LANES and page == LANES and past_len == n_pages * page
    cov = jnp.asarray(_cover_matrix(n_cmp, n_slc, nch, nbl))
    kvw = NSA_KV_HEADS * NSA_HD
    page_spec = lambda i, c: pl.BlockSpec((None, None, page, None, NSA_KV_HEADS, NSA_HD),
                                          lambda b, s, pt: (layer, pt[b, s * pg + i], 0, c, 0, 0))
    in_specs = [
        pl.BlockSpec((DL, NSA_HEADS * NSA_HD), lambda b, s, pt: (b, 3)),
        pl.BlockSpec((DL, 2 * kvw), lambda b, s, pt: (b, 9)),
        pl.BlockSpec((DL, 2 * kvw), lambda b, s, pt: (b, 10)),
        pl.BlockSpec((DL, LANES), lambda b, s, pt: (b, 44)),
        pl.BlockSpec((None, None, NSA_KV_HEADS, nch, NSA_HD), lambda b, s, pt: (b, 0, 0, 0, 0)),
        pl.BlockSpec((None, None, NSA_KV_HEADS, nch, NSA_HD), lambda b, s, pt: (b, 1, 0, 0, 0)),
        pl.BlockSpec((nch, nbl), lambda b, s, pt: (0, 0)),
        pl.BlockSpec((None, None, w_buf, 2, NSA_KV_HEADS, NSA_HD), lambda b, s, pt: (layer, b, 0, 0, 0, 0)),
    ] + [page_spec(i, 2) for i in range(pg)] + [page_spec(i, 3) for i in range(pg)]
    rows = NSA_HEADS * DL
    grid_spec = pltpu.PrefetchScalarGridSpec(
        num_scalar_prefetch=1,
        grid=(DB, n_steps),
        in_specs=in_specs,
        out_specs=pl.BlockSpec((DL, NSA_HEADS * NSA_HD), lambda b, s, pt: (b, 0)),
        scratch_shapes=[
            pltpu.VMEM((rows, NSA_HD), BF16),
            pltpu.VMEM((rows, nbl), F32),
            pltpu.VMEM((rows, 1), F32),
            pltpu.VMEM((rows, 1), F32),
            pltpu.VMEM((rows, NSA_HD), F32),
            pltpu.VMEM((rows, NSA_HD), F32),
        ],
    )
    body = functools.partial(_nsa_sample_body, pg=pg, n_steps=n_steps, dl=DL, q_off=past_len, w_buf=w_buf,
                             n_cmp=n_cmp, n_slc=n_slc, n_sel=n_sel, nbl=nbl, page=page)
    return pl.pallas_call(
        body, grid_spec=grid_spec,
        out_shape=jax.ShapeDtypeStruct((DB * DL, NSA_HEADS * NSA_HD), F32),
        compiler_params=_cparams(("parallel", "arbitrary")),
        name="nsa_sample",
    )(page_table, P, P, P, P, CMP, CMP, cov, cache_win, *([cache_kv] * (2 * pg)))


def _out_ln_body(h_ref, ro_ref, no_ref, wr_ref, wn_ref, g_ref, b_ref, o_ref, *, alpha):
    m = _dot(ro_ref[...].astype(BF16), wr_ref[...]) + _dot(no_ref[...].astype(BF16), wn_ref[...])
    o_ref[...] = _layer_norm(alpha * h_ref[...] + m, g_ref[...], b_ref[...])


def _out_ln(h, ro, no, w_out, g, b, alpha):
    T, D = h.shape
    kr = ro.shape[1]
    kn = no.shape[1]
    tm = _pick_tile(T, (512, 256, 128, 64, 32, 16, 8))
    return pl.pallas_call(
        functools.partial(_out_ln_body, alpha=alpha),
        grid=(T // tm,),
        in_specs=[
            pl.BlockSpec((tm, D), lambda i: (i, 0)),
            pl.BlockSpec((tm, kr), lambda i: (i, 0)),
            pl.BlockSpec((tm, kn), lambda i: (i, 0)),
            pl.BlockSpec((kr, D), lambda i: (0, 0)),
            pl.BlockSpec((kn, D), lambda i: (1, 0)),
            pl.BlockSpec((1, D), lambda i: (0, 0)),
            pl.BlockSpec((1, D), lambda i: (0, 0)),
        ],
        out_specs=pl.BlockSpec((tm, D), lambda i: (i, 0)),
        out_shape=jax.ShapeDtypeStruct((T, D), F32),
        compiler_params=_cparams(("parallel",)),
        name="out_ln",
    )(h, ro, no, w_out, w_out, g.reshape(1, D), b.reshape(1, D))


def _rope_tables(pos):
    half = NSA_HD // 2
    inv = ROPE_THETA ** (-jnp.arange(half, dtype=F32) / half)
    ang = pos.astype(F32)[:, None] * inv[None, :]
    cos = jnp.cos(ang)
    sin = jnp.sin(ang)
    return jnp.concatenate([cos, cos], -1), jnp.concatenate([-sin, sin], -1)


def _cmp_weights(w1, pos, w2):
    r = CMP_BLOCK // CMP_STRIDE
    w1r = w1.reshape(r, CMP_STRIDE * NSA_HD, CMP_HIDDEN)
    w1c = jnp.concatenate([w1r[i] for i in range(r)], axis=1).astype(BF16)
    posr = jnp.pad(pos.reshape(r, CMP_STRIDE * NSA_HD), ((0, SUBLANES - r), (0, 0))).astype(BF16)
    return w1c, posr, w2.astype(BF16)


def _layer_view(arr, l, shape):
    return arr.reshape(shape) if arr.shape[0] == 1 else arr[l].reshape(shape)


def _decoder_layer(x, B, L, q_off, s0, p, sample_ctx):
    alpha = p['alpha']
    h1, h1_bf = _ffn_ln(x, p['ffn1_w_up'], p['ffn1_w_down'], p['ln1_g'], p['ln1_b'], alpha, True)
    cos, sin = _rope_tables(q_off + jnp.arange(L, dtype=jnp.int32))
    P, kv_rows, win_rows = _proj(h1_bf, p['w_in'], p['rope_cols'], p['scale_cols'], cos, sin, L)
    ro, ret_s = _retention(P, s0, p['ret_gn_g'], p['ret_gn_b'], B, L)
    if sample_ctx is None:
        assert L % CMP_STRIDE == 0
        n_cmp = L // CMP_STRIDE - CMP_BLOCK // CMP_STRIDE + 1
        specs = [pl.BlockSpec((L, NSA_HD), (lambda hd: (lambda b, t, g: (b, 32 + NSA_KV_HEADS * t + hd)))(hd))
                 for hd in range(NSA_KV_HEADS)]
        CMP = _compress([P] * NSA_KV_HEADS, specs, 1, L, 1, n_cmp, B, p['cmp_w1'], p['cmp_pos'], p['cmp_w2'])
        no = _nsa_prompt(P, CMP, B, L)
    else:
        cache_kv, cache_win, layer, page_table, past_len, page = sample_ctx
        n_pages = page_table.shape[1]
        lk = past_len + L
        assert (lk // CMP_STRIDE) * CMP_STRIDE <= past_len, "compression blocks must lie in the paged past"
        n_cmp = lk // CMP_STRIDE - CMP_BLOCK // CMP_STRIDE + 1
        n_in = _pick_tile(n_pages, (64, 32, 16, 8, 4, 2, 1))
        n_grp = n_pages // n_in
        specs = [pl.BlockSpec((None, None, page, None, NSA_KV_HEADS, NSA_HD),
                              (lambda i: (lambda b, t, g, pt: (layer, pt[b, (n_grp - 1 - g) * n_in + i], 0, t, 0,
                                                               0)))(i))
                 for i in range(n_in)]
        CMP = _compress([cache_kv] * n_in, specs, n_in, page, n_grp, n_cmp, B,
                        p['cmp_w1'], p['cmp_pos'], p['cmp_w2'], page_table=page_table)
        no = _nsa_sample(P, CMP, cache_kv, cache_win, layer, page_table, B, L, past_len, page)
    x2 = _out_ln(h1, ro, no, p['w_out'], p['ln2_g'], p['ln2_b'], alpha)
    y, _ = _ffn_ln(x2, p['ffn2_w_up'], p['ffn2_w_down'], p['ln3_g'], p['ln3_b'], alpha, False)
    return y, ret_s, kv_rows, win_rows


def kernel(x_prompt, x_sample, state_ret, cache_nsa_kv, cache_win, page_table, ffn1_w_up, ffn1_w_down, ln1_g, ln1_b, w_in, w_out, ret_gn_g, ret_gn_b, cmp_pos_k, cmp_w1_k, cmp_w2_k, cmp_pos_v, cmp_w1_v, cmp_w2_v, ln2_g, ln2_b, ffn2_w_up, ffn2_w_down, ln3_g, ln3_b):
    B, L, D = x_prompt.shape
    DB, DL, _ = x_sample.shape
    depth = w_in.shape[0]
    n_pool, page = cache_nsa_kv.shape[1], cache_nsa_kv.shape[2]
    n_pages = page_table.shape[1]
    past_len = n_pages * page
    w_buf = cache_win.shape[2]
    alpha = (2.0 * depth) ** 0.25
    rope_np = np.zeros((N_IN_PAD // LANES, LANES), np.float32)
    rope_np[list(ROPE_CHUNKS)] = 1.0
    scale_np = np.ones((N_IN_PAD // LANES, LANES), np.float32)
    scale_np[list(KSCALE_CHUNKS)] = RET_DK ** -0.5
    rope_cols = jnp.asarray(rope_np.reshape(1, N_IN_PAD))
    scale_cols = jnp.asarray(scale_np.reshape(1, N_IN_PAD))

    yp = x_prompt.reshape(B * L, D)
    ys = x_sample.reshape(DB * DL, D)
    outs = [[] for _ in range(6)]
    for l in range(depth):
        k1, p1, k2 = _cmp_weights(cmp_w1_k[l], cmp_pos_k[l], cmp_w2_k[l])
        v1, q1, v2 = _cmp_weights(cmp_w1_v[l], cmp_pos_v[l], cmp_w2_v[l])
        p = {
            'alpha': alpha, 'rope_cols': rope_cols, 'scale_cols': scale_cols,
            'ffn1_w_up': ffn1_w_up[l].astype(BF16), 'ffn1_w_down': ffn1_w_down[l].astype(BF16),
            'ln1_g': ln1_g[l], 'ln1_b': ln1_b[l],
            'w_in': jnp.pad(w_in[l], ((0, 0), (0, N_IN_PAD - N_IN))).astype(BF16),
            'w_out': w_out[l].astype(BF16),
            'ret_gn_g': ret_gn_g[l], 'ret_gn_b': ret_gn_b[l],
            'cmp_w1': jnp.stack([k1, v1]), 'cmp_pos': jnp.stack([p1, q1]), 'cmp_w2': jnp.stack([k2, v2]),
            'ln2_g': ln2_g[l], 'ln2_b': ln2_b[l],
            'ffn2_w_up': ffn2_w_up[l].astype(BF16), 'ffn2_w_down': ffn2_w_down[l].astype(BF16),
            'ln3_g': ln3_g[l], 'ln3_b': ln3_b[l],
        }
        s0 = jnp.zeros((B, RET_HEADS, RET_DK, RET_DV), F32)
        yp, rs_p, kv_p, win_p = _decoder_layer(yp, B, L, 0, s0, p, None)
        ctx = (cache_nsa_kv, cache_win, l, page_table, past_len, page)
        ys, rs_s, kv_s, win_s = _decoder_layer(ys, DB, DL, past_len,
                                               _layer_view(state_ret, l, state_ret.shape[1:]), p, ctx)
        wl = min(WINDOW, L)
        outs[0].append(rs_p)
        outs[1].append(rs_s)
        outs[2].append(kv_p.reshape(B, L, 4, NSA_KV_HEADS, NSA_HD))
        outs[3].append(kv_s.reshape(DB, DL, 4, NSA_KV_HEADS, NSA_HD))
        outs[4].append(win_p.reshape(B, L, 2, NSA_KV_HEADS, NSA_HD)[:, L - wl:])
        win_s = win_s.reshape(DB, DL, 2, NSA_KV_HEADS, NSA_HD)
        outs[5].append(jnp.concatenate([cache_win[l], win_s], axis=1)[:, -w_buf:])
    return (yp.reshape(B, L, D), ys.reshape(DB, DL, D), jnp.stack(outs[0]), jnp.stack(outs[1]),
            jnp.stack(outs[2]), jnp.stack(outs[3]), jnp.stack(outs[4]), jnp.stack(outs[5]))
```

```python
import functools

import numpy as np
import jax
import jax.numpy as jnp
from jax import lax
from jax.experimental import pallas as pl
from jax.experimental.pallas import tpu as pltpu

F32 = jnp.float32
BF16 = jnp.bfloat16

LANES = 128
SUBLANES = 8
BF16_ROWS = 16
VMEM_LIMIT_BYTES = 56 * 1024 * 1024

RET_HEADS = 4
RET_DK = 128
RET_DV = 256
RET_CHUNK = 128
NSA_HEADS = 8
NSA_KV_HEADS = 2
NSA_HD = 128
NSA_GROUP = NSA_HEADS // NSA_KV_HEADS
CMP_BLOCK = 32
CMP_STRIDE = 16
CMP_HIDDEN = 2 * NSA_HD
SLC_BLOCK = 64
SLC_SHIFT = 6
N_SELECT = 16
WINDOW = 512
N_BRANCH = 3
ROPE_THETA = 10000.0
LN_EPS = 1e-5
NEG_BIG = -1e30
LOG2_E = 1.4426950408889634

N_IN = 5656
N_IN_PAD = 5760
KV_CHUNK0 = 32
WIN_CHUNK0 = 40
GATE_CHUNK = 44
ROPE_CHUNKS = tuple(range(0, 8)) + tuple(range(24, 32)) + (32, 33, 36, 37, 40, 41)
KSCALE_CHUNKS = tuple(range(4, 8))


def _cparams(sem):
    return pltpu.CompilerParams(dimension_semantics=sem, vmem_limit_bytes=VMEM_LIMIT_BYTES)


def _pick_tile(n, candidates):
    for c in candidates:
        if n % c == 0:
            return c
    return n


def _round_up(n, m):
    return (n + m - 1) // m * m


def _layer_norm(z, g, b):
    mu = jnp.mean(z, axis=-1, keepdims=True)
    zc = z - mu
    var = jnp.mean(zc * zc, axis=-1, keepdims=True)
    return zc * lax.rsqrt(var + LN_EPS) * g + b


def _dot(a, b):
    return jnp.dot(a, b, preferred_element_type=F32)


def _dot_nt(a, b):
    return lax.dot_general(a, b, (((1,), (1,)), ((), ())), preferred_element_type=F32)


def _ffn_ln_body(x_ref, wa_ref, wb_ref, wd_ref, g_ref, b_ref, *rest, nj, alpha, emit_bf16):
    if emit_bf16:
        o_ref, obf_ref, xbf_sc, acc_sc = rest
    else:
        o_ref, xbf_sc, acc_sc = rest
    j = pl.program_id(1)

    @pl.when(j == 0)
    def _init():
        xbf_sc[...] = x_ref[...].astype(BF16)
        acc_sc[...] = jnp.zeros_like(acc_sc)

    xb = xbf_sc[...]
    a = _dot(xb, wa_ref[...])
    b = _dot(xb, wb_ref[...])
    h = (a * jax.nn.sigmoid(a)) * b
    acc_sc[...] += _dot(h.astype(BF16), wd_ref[...])

    @pl.when(j == nj - 1)
    def _finish():
        z = alpha * x_ref[...] + 0.5 * acc_sc[...]
        y = _layer_norm(z, g_ref[...], b_ref[...])
        o_ref[...] = y
        if emit_bf16:
            obf_ref[...] = y.astype(BF16)


def _ffn_ln(x, w_up, w_down, g, b, alpha, emit_bf16):
    T, D = x.shape
    F = w_down.shape[0]
    tm = _pick_tile(T, (512, 256, 128, 64, 32, 16, 8))
    tf = _pick_tile(F, (512, 256, 128))
    nj = F // tf
    out_shape = [jax.ShapeDtypeStruct((T, D), F32)]
    out_specs = [pl.BlockSpec((tm, D), lambda i, j: (i, 0))]
    if emit_bf16:
        out_shape.append(jax.ShapeDtypeStruct((T, D), BF16))
        out_specs.append(pl.BlockSpec((tm, D), lambda i, j: (i, 0)))
    res = pl.pallas_call(
        functools.partial(_ffn_ln_body, nj=nj, alpha=alpha, emit_bf16=emit_bf16),
        grid=(T // tm, nj),
        in_specs=[
            pl.BlockSpec((tm, D), lambda i, j: (i, 0)),
            pl.BlockSpec((D, tf), lambda i, j: (0, j)),
            pl.BlockSpec((D, tf), lambda i, j: (0, nj + j)),
            pl.BlockSpec((tf, D), lambda i, j: (j, 0)),
            pl.BlockSpec((1, D), lambda i, j: (0, 0)),
            pl.BlockSpec((1, D), lambda i, j: (0, 0)),
        ],
        out_specs=out_specs,
        out_shape=out_shape,
        scratch_shapes=[pltpu.VMEM((tm, D), BF16), pltpu.VMEM((tm, D), F32)],
        compiler_params=_cparams(("parallel", "arbitrary")),
        name="ffn_ln",
    )(x, w_up, w_up, w_down, g.reshape(1, D), b.reshape(1, D))
    return res if emit_bf16 else (res[0], None)


def _rows_view(ref, j):
    rows = ref.shape[0]
    n = int(np.prod(ref.shape[1:-1]))
    return ref.reshape(n * rows, ref.shape[-1]), pl.ds(j, rows, stride=n)


def _head_rows(ref, j):
    view, idx = _rows_view(ref, j)
    return view[idx, :]


def _store_rows(ref, j, val):
    view, idx = _rows_view(ref, j)
    view[idx, :] = val


def _proj_body(x_ref, w_ref, cos_ref, sin_ref, rope_ref, scale_ref, o_ref, kv_ref, win_ref, *, n_chunk, j_rows):
    y = _dot(x_ref[...], w_ref[...])
    cos = cos_ref[...]
    sin = sin_ref[...]
    for c in range(n_chunk):
        sl = slice(c * LANES, (c + 1) * LANES)
        yc = y[:, sl]
        roped = yc * cos + pltpu.roll(yc, NSA_HD // 2, 1) * sin
        o_ref[:, sl] = jnp.where(rope_ref[:, sl] > 0.5, roped, yc) * scale_ref[:, sl]

    @pl.when(pl.program_id(1) == j_rows)
    def _emit_cache_rows():
        c0 = KV_CHUNK0 - j_rows * n_chunk
        for r in range(WIN_CHUNK0 - KV_CHUNK0):
            _store_rows(kv_ref, r, o_ref[:, (c0 + r) * LANES:(c0 + r + 1) * LANES])
        c0 = WIN_CHUNK0 - j_rows * n_chunk
        for r in range(GATE_CHUNK - WIN_CHUNK0):
            _store_rows(win_ref, r, o_ref[:, (c0 + r) * LANES:(c0 + r + 1) * LANES])


def _proj(x_bf, w_bf, rope_cols, scale_cols, cos, sin, rows_per_seq):
    T, D = x_bf.shape
    N = w_bf.shape[1]
    tm = _pick_tile(T, (512, 256, 128, 64, 32, 16, 8))
    tn = 1920
    assert N % tn == 0
    n_chunk = tn // LANES
    if rows_per_seq >= tm:
        assert rows_per_seq % tm == 0
        n_tab = rows_per_seq // tm
    else:
        assert tm % rows_per_seq == 0
        cos = jnp.tile(cos, (tm // rows_per_seq, 1))
        sin = jnp.tile(sin, (tm // rows_per_seq, 1))
        n_tab = 1
    j_rows = KV_CHUNK0 // n_chunk
    assert (GATE_CHUNK - 1) // n_chunk == j_rows, "cache-row columns must sit in one column tile"
    kv_shape = (T, 4, NSA_KV_HEADS, NSA_HD)
    win_shape = (T, 2, NSA_KV_HEADS, NSA_HD)
    return pl.pallas_call(
        functools.partial(_proj_body, n_chunk=n_chunk, j_rows=j_rows),
        grid=(T // tm, N // tn),
        in_specs=[
            pl.BlockSpec((tm, D), lambda i, j: (i, 0)),
            pl.BlockSpec((D, tn), lambda i, j: (0, j)),
            pl.BlockSpec((tm, LANES), lambda i, j: (i % n_tab, 0)),
            pl.BlockSpec((tm, LANES), lambda i, j: (i % n_tab, 0)),
            pl.BlockSpec((1, tn), lambda i, j: (0, j)),
            pl.BlockSpec((1, tn), lambda i, j: (0, j)),
        ],
        out_specs=[
            pl.BlockSpec((tm, tn), lambda i, j: (i, j)),
            pl.BlockSpec((tm,) + kv_shape[1:], lambda i, j: (i, 0, 0, 0)),
            pl.BlockSpec((tm,) + win_shape[1:], lambda i, j: (i, 0, 0, 0)),
        ],
        out_shape=[
            jax.ShapeDtypeStruct((T, N), F32),
            jax.ShapeDtypeStruct(kv_shape, F32),
            jax.ShapeDtypeStruct(win_shape, F32),
        ],
        compiler_params=_cparams(("parallel", "arbitrary")),
        name="proj_rope",
    )(x_bf, w_bf, cos, sin, rope_cols, scale_cols)


def _ret_body(q_ref, k_ref, v_ref, g_ref, s0_ref, dm_ref, ind_ref, std_ref, cd_ref, gng_ref, gnb_ref,
              o_ref, sout_ref, s_sc, *, nc, rows, rows_pad):
    c = pl.program_id(1)

    @pl.when(c == 0)
    def _load_state():
        s_sc[...] = s0_ref[...]

    pad = rows_pad - rows
    for h in range(RET_HEADS):
        ks = slice(h * RET_DK, (h + 1) * RET_DK)
        vs = slice(h * RET_DV, (h + 1) * RET_DV)
        q = q_ref[:, ks]
        k = k_ref[:, ks]
        v = v_ref[:, vs]
        kd = k * std_ref[h]
        if pad:
            k = jnp.concatenate([k, jnp.zeros((pad, RET_DK), F32)], axis=0)
            kd = jnp.concatenate([kd, jnp.zeros((pad, RET_DK), F32)], axis=0)
            v = jnp.concatenate([v, jnp.zeros((pad, RET_DV), F32)], axis=0)
        s_old = s_sc[h]
        vb = v.astype(BF16)
        a = _dot_nt(q.astype(BF16), k.astype(BF16)) * dm_ref[h]
        o = _dot(a.astype(BF16), vb) + _dot((q * ind_ref[h]).astype(BF16), s_old.astype(BF16))
        s_sc[h] = s_old * cd_ref[h] + _dot(kd.T.astype(BF16), vb)
        mu = jnp.mean(o, axis=-1, keepdims=True)
        oc = o - mu
        var = jnp.mean(oc * oc, axis=-1, keepdims=True)
        on = oc * lax.rsqrt(var + LN_EPS) * gng_ref[:, vs] + gnb_ref[:, vs]
        gate = g_ref[:, vs]
        o_ref[:, vs] = (gate * jax.nn.sigmoid(gate)) * on

    @pl.when(c == nc - 1)
    def _store_state():
        sout_ref[...] = s_sc[...]


def _retention(P, s0, gn_g, gn_b, B, L):
    C = RET_CHUNK if L % RET_CHUNK == 0 else L
    nc = L // C
    CP = max(C, LANES)
    lg = jnp.log1p(-jnp.exp2(-5.0 - jnp.arange(RET_HEADS, dtype=F32)))
    i = jnp.arange(C, dtype=F32)
    diff = i[:, None] - i[None, :]
    dmask = jnp.where(diff >= 0, jnp.exp(lg[:, None, None] * jnp.maximum(diff, 0.0)), 0.0)
    dmask = jnp.pad(dmask, ((0, 0), (0, 0), (0, CP - C)))
    in_decay = jnp.broadcast_to(jnp.exp(lg[:, None] * (i + 1.0))[:, :, None], (RET_HEADS, C, RET_DK))
    st_decay = jnp.broadcast_to(jnp.exp(lg[:, None] * (C - 1.0 - i))[:, :, None], (RET_HEADS, C, RET_DK))
    chunk_decay = jnp.broadcast_to(jnp.exp(lg * C)[:, None, None], (RET_HEADS, 1, RET_DV))
    qw = RET_HEADS * RET_DK
    vw = RET_HEADS * RET_DV
    const3 = lambda b, c: (0, 0, 0)
    return pl.pallas_call(
        functools.partial(_ret_body, nc=nc, rows=C, rows_pad=CP),
        grid=(B, nc),
        in_specs=[
            pl.BlockSpec((C, qw), lambda b, c: (b * nc + c, 0)),
            pl.BlockSpec((C, qw), lambda b, c: (b * nc + c, 1)),
            pl.BlockSpec((C, vw), lambda b, c: (b * nc + c, 1)),
            pl.BlockSpec((C, vw), lambda b, c: (b * nc + c, 2)),
            pl.BlockSpec((None, RET_HEADS, RET_DK, RET_DV), lambda b, c: (b, 0, 0, 0)),
            pl.BlockSpec((RET_HEADS, C, CP), const3),
            pl.BlockSpec((RET_HEADS, C, RET_DK), const3),
            pl.BlockSpec((RET_HEADS, C, RET_DK), const3),
            pl.BlockSpec((RET_HEADS, 1, RET_DV), const3),
            pl.BlockSpec((1, vw), lambda b, c: (0, 0)),
            pl.BlockSpec((1, vw), lambda b, c: (0, 0)),
        ],
        out_specs=[
            pl.BlockSpec((C, vw), lambda b, c: (b * nc + c, 0)),
            pl.BlockSpec((None, RET_HEADS, RET_DK, RET_DV), lambda b, c: (b, 0, 0, 0)),
        ],
        out_shape=[
            jax.ShapeDtypeStruct((B * L, vw), F32),
            jax.ShapeDtypeStruct((B, RET_HEADS, RET_DK, RET_DV), F32),
        ],
        scratch_shapes=[pltpu.VMEM((RET_HEADS, RET_DK, RET_DV), F32)],
        compiler_params=_cparams(("parallel", "arbitrary")),
        name="retention",
    )(P, P, P, P, s0, dmask, in_decay, st_decay, chunk_decay, gn_g.reshape(1, vw), gn_b.reshape(1, vw))


def _cmp_body(*refs, n_in, rows, n_grp, n_cmp, paged):
    n_vec = 2 * NSA_KV_HEADS
    if paged:
        refs = refs[1:]
    n_src = n_in if paged else n_vec * n_in
    x_refs = refs[:n_src]
    w1_ref, pos_ref, w2_ref, o_ref, carry_sc, xc_sc = refs[n_src:]
    g = pl.program_id(1)
    grp = n_grp - 1 - g
    cpi = rows // CMP_STRIDE
    M = n_in * cpi

    @pl.when(g == 0)
    def _init():
        carry_sc[...] = jnp.zeros_like(carry_sc)

    row = lax.broadcasted_iota(jnp.int32, (M, 1), 0)
    grp_in = 1 if cpi % BF16_ROWS == 0 else BF16_ROWS // cpi
    assert n_in % grp_in == 0 and (grp_in * cpi) % BF16_ROWS == 0
    for t in range(2):
        w1 = w1_ref[t]
        gp = _dot(pos_ref[t], w1)
        posterm = gp[0:1, :CMP_HIDDEN] + gp[1:2, CMP_HIDDEN:]
        for hd in range(NSA_KV_HEADS):
            v = t * NSA_KV_HEADS + hd
            for i0 in range(0, n_in, grp_in):
                if paged:
                    parts = [pltpu.einshape("csd->scd",
                                            _head_rows(x_refs[i], v).reshape(cpi, CMP_STRIDE, NSA_HD))
                             for i in range(i0, i0 + grp_in)]
                    piece = lambda s: jnp.concatenate([xt[s] for xt in parts], axis=0)
                else:
                    refs_i = [x_refs[v * n_in + i] for i in range(i0, i0 + grp_in)]
                    piece = lambda s: jnp.concatenate(
                        [r[pl.ds(s, cpi, stride=CMP_STRIDE), :] for r in refs_i], axis=0)
                r0 = v * M + i0 * cpi
                for s in range(CMP_STRIDE):
                    xc_sc[r0:r0 + grp_in * cpi, s * NSA_HD:(s + 1) * NSA_HD] = piece(s).astype(BF16)
        t0 = t * NSA_KV_HEADS * M
        gg_all = _dot(xc_sc[t0:t0 + NSA_KV_HEADS * M, :], w1)
        for hd in range(NSA_KV_HEADS):
            v = t * NSA_KV_HEADS + hd
            gg = gg_all[hd * M:(hd + 1) * M]
            g0 = gg[:, :CMP_HIDDEN]
            g1 = gg[:, CMP_HIDDEN:]
            nxt = pltpu.roll(g1, M - 1, 0)
            nxt = jnp.where(row == M - 1, carry_sc[v][0:1, :], nxt)
            carry_sc[v] = g1[0:SUBLANES, :]
            hid = g0 + nxt + posterm
            out = _dot(jax.nn.gelu(hid).astype(BF16), w2_ref[t])
            o_ref[t, hd] = jnp.where(grp * M + row < n_cmp, out, 0.0)


def _compress(srcs, src_specs, n_in, rows, n_grp, n_cmp, B, w1, pos, w2, page_table=None):
    n_vec = 2 * NSA_KV_HEADS
    M = n_in * rows // CMP_STRIDE
    nch = n_grp * M
    paged = page_table is not None
    const3 = lambda *a: (0, 0, 0)
    in_specs = list(src_specs) + [
        pl.BlockSpec((2, CMP_STRIDE * NSA_HD, 2 * CMP_HIDDEN), const3),
        pl.BlockSpec((2, SUBLANES, CMP_STRIDE * NSA_HD), const3),
        pl.BlockSpec((2, CMP_HIDDEN, NSA_HD), const3),
    ]
    out_spec = pl.BlockSpec((None, 2, NSA_KV_HEADS, M, NSA_HD), lambda *a: (a[0], 0, 0, n_grp - 1 - a[1], 0))
    body = functools.partial(_cmp_body, n_in=n_in, rows=rows, n_grp=n_grp, n_cmp=n_cmp, paged=paged)
    out_shape = jax.ShapeDtypeStruct((B, 2, NSA_KV_HEADS, nch, NSA_HD), F32)
    scratch = [pltpu.VMEM((n_vec, SUBLANES, CMP_HIDDEN), F32),
               pltpu.VMEM((n_vec * M, CMP_STRIDE * NSA_HD), BF16)]
    sem = ("parallel", "arbitrary")
    if paged:
        grid_spec = pltpu.PrefetchScalarGridSpec(
            num_scalar_prefetch=1, grid=(B, n_grp), in_specs=in_specs, out_specs=out_spec,
            scratch_shapes=scratch)
        return pl.pallas_call(body, grid_spec=grid_spec, out_shape=out_shape,
                              compiler_params=_cparams(sem), name="nsa_compress_paged")(
            page_table, *srcs, w1, pos, w2)
    return pl.pallas_call(body, grid=(B, n_grp), in_specs=in_specs, out_specs=out_spec,
                          out_shape=out_shape, scratch_shapes=scratch,
                          compiler_params=_cparams(sem), name="nsa_compress")(*srcs, w1, pos, w2)


def _cover_matrix(n_cmp, n_slc, rows, cols):
    c_i = np.arange(n_cmp)[:, None]
    n_i = np.arange(n_slc)[None, :]
    cov = np.clip(np.minimum(c_i * CMP_STRIDE + CMP_BLOCK, (n_i + 1) * SLC_BLOCK)
                  - np.maximum(c_i * CMP_STRIDE, n_i * SLC_BLOCK), 0, None).astype(np.float32) / CMP_BLOCK
    out = np.zeros((rows, cols), np.float32)
    out[:n_cmp, :n_slc] = cov
    return out


def _nsa_prompt_body(q_ref, kc_ref, vc_ref, ks_ref, vs_ref, kw_ref, vw_ref, gate_ref, covt_ref, o_ref,
                     kcb_sc, vct_sc, ksb_sc, vst_sc, kwb_sc, vwt_sc, prio_sc, sel_sc, gt_sc,
                     m_sc, l_sc, acc_sc, out_sc, s_sc, *, tq, L, n_cmp, n_slc, n_sel, nbp):
    kvh = pl.program_id(1)
    qi = pl.program_id(2)
    tk = tq
    nch = kc_ref.shape[0]
    scale = NSA_HD ** -0.5

    @pl.when(qi == 0)
    def _stage_kv():
        kcb_sc[...] = kc_ref[...].astype(BF16)
        vct_sc[...] = vc_ref[...].T.astype(BF16)
        ksb_sc[...] = ks_ref[...].astype(BF16)
        kwb_sc[...] = kw_ref[...].astype(BF16)
        for i in range(L // tk):
            vst_sc[i] = vs_ref[i * tk:(i + 1) * tk, :].T.astype(BF16)
            vwt_sc[i] = vw_ref[i * tk:(i + 1) * tk, :].T.astype(BF16)

    G = NSA_GROUP
    W = G * tq
    t0 = qi * tq
    t_row = t0 + lax.broadcasted_iota(jnp.int32, (1, tq), 1)
    lane = lax.broadcasted_iota(jnp.int32, (1, W), 1)
    t_all = t0 + (lane & (tq - 1))
    qcat = jnp.concatenate([q_ref[:, g * NSA_HD:(g + 1) * NSA_HD] for g in range(G)], axis=0)
    qcat = (qcat * (scale * LOG2_E)).astype(BF16)
    gt_sc[...] = jax.nn.sigmoid(gate_ref[...]).T

    def gate_row(br):
        return jnp.concatenate(
            [gt_sc[pl.ds((kvh * G + g) * N_BRANCH + br, 1), :] for g in range(G)], axis=1)

    c_i = lax.broadcasted_iota(jnp.int32, (nch, 1), 0)
    c_end = jnp.where(c_i < n_cmp, c_i * CMP_STRIDE + (CMP_BLOCK - 1), L)
    s = jnp.where(c_end <= t_all, _dot_nt(kcb_sc[...], qcat), -jnp.inf)
    m = jnp.max(s, axis=0, keepdims=True)
    m = jnp.where(m > -jnp.inf, m, 0.0)
    e = jnp.exp2(s - m)
    d = jnp.sum(e, axis=0, keepdims=True)
    p = e / jnp.where(d > 0, d, 1.0)
    out_sc[...] = gate_row(0) * _dot(vct_sc[...], p.astype(BF16))
    psum = p[:, 0:tq]
    for g in range(1, G):
        psum = psum + p[:, g * tq:(g + 1) * tq]

    imp = jnp.dot(covt_ref[...], psum, preferred_element_type=F32, precision=lax.Precision.HIGHEST)
    n_i = lax.broadcasted_iota(jnp.int32, (nbp, 1), 0)
    valid = (n_i * SLC_BLOCK <= t_row) & (n_i < n_slc)
    cur = t_row >> SLC_SHIFT
    forced = (n_i == 0) | (n_i == cur) | (n_i == cur - 1)
    prio = jnp.where(forced, jnp.inf, jnp.where(valid, imp, -jnp.inf))
    prio_sc[...] = prio
    cnt = jnp.zeros((nbp, tq), jnp.int32)
    for mm in range(n_slc):
        pm = prio_sc[mm:mm + 1, :]
        tie = jnp.where(n_i > mm, 1, 0)
        cnt = cnt + jnp.where(pm > prio, 1, jnp.where(pm == prio, tie, 0))
    selbias = jnp.where((cnt < n_sel) & valid, 0.0, NEG_BIG)
    sel_sc[...] = jnp.concatenate([selbias] * G, axis=1)

    kp_l = lax.broadcasted_iota(jnp.int32, (tk, 1), 0)
    t_l = lane & (tq - 1)
    causal_bias = jnp.where(kp_l <= t_l, 0.0, NEG_BIG)
    far_bias = jnp.where(kp_l > t_l, 0.0, NEG_BIG)

    def reset():
        m_sc[...] = jnp.full((1, W), NEG_BIG, F32)
        l_sc[...] = jnp.zeros((1, W), F32)
        acc_sc[...] = jnp.zeros((NSA_HD, W), F32)

    def score(kb_sc, kt, bias):
        k0 = pl.multiple_of(kt * tk, tk)
        s = _dot_nt(kb_sc[pl.ds(k0, tk), :], qcat)
        return s if bias is None else s + bias

    def update(vt_sc, kt, s):
        m_old = m_sc[...]
        m_new = jnp.maximum(m_old, jnp.max(s, axis=0, keepdims=True))
        p = jnp.exp2(s - m_new)
        alpha = jnp.exp2(m_old - m_new)
        l_sc[...] = alpha * l_sc[...] + jnp.sum(p, axis=0, keepdims=True)
        acc_sc[...] = alpha * acc_sc[...] + _dot(vt_sc[kt], p.astype(BF16))
        m_sc[...] = m_new

    def sel_bias(kt):
        per_tile = tk // SLC_BLOCK
        rows = [jnp.broadcast_to(sel_sc[pl.ds(kt * per_tile + r, 1), :], (SLC_BLOCK, W))
                for r in range(per_tile)]
        return jnp.concatenate(rows, axis=0)

    def flush(br):
        out_sc[...] += (gate_row(br) / l_sc[...]) * acc_sc[...]

    reset()
    s_sc[...] = score(ksb_sc, 0, sel_bias(0))

    def slc_body(kt, carry):
        s_next = score(ksb_sc, kt + 1, sel_bias(kt + 1))
        update(vst_sc, kt, s_sc[...])
        s_sc[...] = s_next
        return carry

    lax.fori_loop(0, qi, slc_body, 0)
    update(vst_sc, qi, s_sc[...] + causal_bias)
    flush(1)

    reset()
    far = WINDOW // tk
    tiles = []
    for back in range(far, -1, -1):
        bias = far_bias if back == far else (causal_bias if back == 0 else None)
        if back > 0:
            off = jnp.where(qi >= back, 0.0, NEG_BIG).astype(F32)
            bias = off if bias is None else bias + off
        kt = jnp.maximum(qi - back, 0)
        tiles.append((kt, score(kwb_sc, kt, bias)))
    for kt, s_w in tiles:
        update(vwt_sc, kt, s_w)
    flush(2)
    for g in range(G):
        o_ref[:, g * NSA_HD:(g + 1) * NSA_HD] = out_sc[:, g * tq:(g + 1) * tq].T


def _nsa_prompt(P, CMP, B, L):
    tq = _pick_tile(L, (256, 128))
    assert L % tq == 0 and tq % SLC_BLOCK == 0 and WINDOW % tq == 0 and tq & (tq - 1) == 0
    nq = L // tq
    nch = CMP.shape[3]
    n_cmp = L // CMP_STRIDE - CMP_BLOCK // CMP_STRIDE + 1
    n_slc = -(-L // SLC_BLOCK)
    n_sel = min(N_SELECT, n_slc)
    nbp = _round_up(n_slc, SUBLANES)
    covt = jnp.asarray(_cover_matrix(n_cmp, n_slc, nch, nbp).T)
    gw = NSA_GROUP * NSA_HD
    col = lambda c: (lambda b, k, i: (b, c + k))
    body = functools.partial(_nsa_prompt_body, tq=tq, L=L, n_cmp=n_cmp, n_slc=n_slc, n_sel=n_sel, nbp=nbp)
    return pl.pallas_call(
        body,
        grid=(B, NSA_KV_HEADS, nq),
        in_specs=[
            pl.BlockSpec((tq, gw), lambda b, k, i: (b * nq + i, 6 + k)),
            pl.BlockSpec((None, None, None, nch, NSA_HD), lambda b, k, i: (b, 0, k, 0, 0)),
            pl.BlockSpec((None, None, None, nch, NSA_HD), lambda b, k, i: (b, 1, k, 0, 0)),
            pl.BlockSpec((L, NSA_HD), col(36)),
            pl.BlockSpec((L, NSA_HD), col(38)),
            pl.BlockSpec((L, NSA_HD), col(40)),
            pl.BlockSpec((L, NSA_HD), col(42)),
            pl.BlockSpec((tq, LANES), lambda b, k, i: (b * nq + i, 44)),
            pl.BlockSpec((nbp, nch), lambda b, k, i: (0, 0)),
        ],
        out_specs=pl.BlockSpec((tq, gw), lambda b, k, i: (b * nq + i, k)),
        out_shape=jax.ShapeDtypeStruct((B * L, NSA_HEADS * NSA_HD), F32),
        scratch_shapes=[
            pltpu.VMEM((nch, NSA_HD), BF16),
            pltpu.VMEM((NSA_HD, nch), BF16),
            pltpu.VMEM((L, NSA_HD), BF16),
            pltpu.VMEM((L // tq, NSA_HD, tq), BF16),
            pltpu.VMEM((L, NSA_HD), BF16),
            pltpu.VMEM((L // tq, NSA_HD, tq), BF16),
            pltpu.VMEM((nbp, tq), F32),
            pltpu.VMEM((nbp, NSA_GROUP * tq), F32),
            pltpu.VMEM((LANES, tq), F32),
            pltpu.VMEM((1, NSA_GROUP * tq), F32),
            pltpu.VMEM((1, NSA_GROUP * tq), F32),
            pltpu.VMEM((NSA_HD, NSA_GROUP * tq), F32),
            pltpu.VMEM((NSA_HD, NSA_GROUP * tq), F32),
            pltpu.VMEM((tq, NSA_GROUP * tq), F32),
        ],
        compiler_params=_cparams(("parallel", "arbitrary", "arbitrary")),
        name="nsa_prompt",
    )(P, CMP, CMP, P, P, P, P, P, covt)


def _nsa_sample_body(*refs, pg, n_steps, dl, q_off, w_buf, n_cmp, n_slc, n_sel, nbl, page):
    refs = refs[1:]
    q_ref, kvn_ref, wn_ref, gate_ref, kc_ref, vc_ref, cov_ref, cw_ref = refs[:8]
    pg_refs = refs[8:8 + pg]
    o_ref, qb_sc, sel_sc, m_sc, l_sc, acc_sc, ocmp_sc = refs[8 + pg:]
    step = pl.program_id(1)
    rows_h = NSA_GROUP * dl
    rows = NSA_KV_HEADS * rows_h
    scale = NSA_HD ** -0.5
    nch = kc_ref.shape[1]

    def tok_pos(n):
        r = lax.broadcasted_iota(jnp.int32, (n, 1), 0)
        return q_off + r % dl

    @pl.when(step == 0)
    def _select():
        for k in range(NSA_KV_HEADS):
            for g in range(NSA_GROUP):
                h = k * NSA_GROUP + g
                qb_sc[pl.ds(h * dl, dl), :] = (q_ref[:, h * NSA_HD:(h + 1) * NSA_HD] * scale).astype(BF16)
        t_h = tok_pos(rows_h)
        c_i = lax.broadcasted_iota(jnp.int32, (1, nch), 1)
        cmask = (c_i * CMP_STRIDE + (CMP_BLOCK - 1) <= t_h) & (c_i < n_cmp)
        psums = []
        for k in range(NSA_KV_HEADS):
            qk = qb_sc[pl.ds(k * rows_h, rows_h), :]
            s = jnp.where(cmask, _dot_nt(qk, kc_ref[k].astype(BF16)), -jnp.inf)
            m = jnp.max(s, axis=1, keepdims=True)
            m = jnp.where(m > -jnp.inf, m, 0.0)
            e = jnp.exp(s - m)
            d = jnp.sum(e, axis=1, keepdims=True)
            p = e / jnp.where(d > 0, d, 1.0)
            ocmp_sc[pl.ds(k * rows_h, rows_h), :] = _dot(p.astype(BF16), vc_ref[k].astype(BF16))
            psum = p[0:dl]
            for g in range(1, NSA_GROUP):
                psum = psum + p[g * dl:(g + 1) * dl]
            psums.append(psum)
        imp = jnp.dot(jnp.concatenate(psums, axis=0), cov_ref[...], preferred_element_type=F32,
                      precision=lax.Precision.HIGHEST)
        t_s = tok_pos(NSA_KV_HEADS * dl)
        n_i = lax.broadcasted_iota(jnp.int32, (1, nbl), 1)
        valid = (n_i * SLC_BLOCK <= t_s) & (n_i < n_slc)
        cur = t_s >> SLC_SHIFT
        forced = (n_i == 0) | (n_i == cur) | (n_i == cur - 1)
        prio = jnp.where(forced, jnp.inf, jnp.where(valid, imp, -jnp.inf))
        n_f = n_i.astype(F32)
        alive = jnp.broadcast_to(jnp.where(n_i < n_slc, 1.0, 0.0), prio.shape)
        sel = jnp.zeros(prio.shape, F32)
        for _ in range(n_sel):
            mx = jnp.max(jnp.where(alive > 0.5, prio, -jnp.inf), axis=1, keepdims=True)
            cand = (alive > 0.5) & (prio == mx)
            first = jnp.min(jnp.where(cand, n_f, float(nbl)), axis=1, keepdims=True)
            pick = n_f == first
            sel = jnp.where(pick, 1.0, sel)
            alive = jnp.where(pick, 0.0, alive)
        sel = jnp.where(valid, sel, 0.0)
        for k in range(NSA_KV_HEADS):
            for g in range(NSA_GROUP):
                sel_sc[pl.ds((k * NSA_GROUP + g) * dl, dl), :] = sel[k * dl:(k + 1) * dl]
        m_sc[...] = jnp.full(m_sc.shape, NEG_BIG, F32)
        l_sc[...] = jnp.zeros_like(l_sc)
        acc_sc[...] = jnp.zeros_like(acc_sc)

    selb = sel_sc[...].astype(BF16)
    n_col = lax.broadcasted_iota(jnp.int32, (nbl, 1), 0)

    def online_update(k, s, mask, v_rows):
        rs = pl.ds(k * rows_h, rows_h)
        m_old = m_sc[rs, :]
        m_new = jnp.maximum(m_old, jnp.max(s, axis=1, keepdims=True))
        p = jnp.where(mask, jnp.exp(s - m_new), 0.0)
        alpha = jnp.exp(m_old - m_new)
        l_sc[rs, :] = alpha * l_sc[rs, :] + jnp.sum(p, axis=1, keepdims=True)
        acc_sc[rs, :] = alpha * acc_sc[rs, :] + _dot(p.astype(BF16), v_rows)
        m_sc[rs, :] = m_new

    t_h = tok_pos(rows_h)

    def selected(kpos):
        blk = jnp.where(n_col == (kpos >> SLC_SHIFT), 1.0, 0.0).astype(BF16)
        return _dot(selb, blk)

    kpos = step * (pg * page) + lax.broadcasted_iota(jnp.int32, (1, pg * page), 1)
    sel_all = selected(kpos)
    for k in range(NSA_KV_HEADS):
        qk = qb_sc[pl.ds(k * rows_h, rows_h), :]
        mask = (sel_all[k * rows_h:(k + 1) * rows_h] > 0.5) & (kpos <= t_h)
        k_rows = jnp.concatenate([_head_rows(r, k).astype(BF16) for r in pg_refs], axis=0)
        v_rows = jnp.concatenate([_head_rows(r, NSA_KV_HEADS + k).astype(BF16) for r in pg_refs], axis=0)
        s = jnp.where(mask, _dot_nt(qk, k_rows), NEG_BIG)
        online_update(k, s, mask, v_rows)

    @pl.when(step == n_steps - 1)
    def _finish():
        padn = LANES - dl
        j_new = lax.broadcasted_iota(jnp.int32, (1, LANES), 1)
        kpos_n = q_off + j_new
        sel_n = selected(kpos_n)
        for k in range(NSA_KV_HEADS):
            kn = jnp.concatenate([kvn_ref[:, k * NSA_HD:(k + 1) * NSA_HD], jnp.zeros((padn, NSA_HD), F32)], axis=0)
            vn = jnp.concatenate([kvn_ref[:, (2 + k) * NSA_HD:(3 + k) * NSA_HD], jnp.zeros((padn, NSA_HD), F32)],
                                 axis=0)
            qk = qb_sc[pl.ds(k * rows_h, rows_h), :]
            mask = (sel_n[k * rows_h:(k + 1) * rows_h] > 0.5) & (kpos_n <= t_h) & (j_new < dl)
            s = jnp.where(mask, _dot_nt(qk, kn.astype(BF16)), NEG_BIG)
            online_update(k, s, mask, vn.astype(BF16))
        j_w = lax.broadcasted_iota(jnp.int32, (1, w_buf + LANES), 1)
        pos_w = q_off - w_buf + j_w
        dlt = t_h - pos_w
        wmask = (j_w < w_buf + dl) & (pos_w >= 0) & (dlt >= 0) & (dlt < WINDOW)
        gates = jax.nn.sigmoid(gate_ref[...])
        for k in range(NSA_KV_HEADS):
            kw = jnp.concatenate([_head_rows(cw_ref, k),
                                  wn_ref[:, k * NSA_HD:(k + 1) * NSA_HD], jnp.zeros((padn, NSA_HD), F32)], axis=0)
            vw = jnp.concatenate([_head_rows(cw_ref, NSA_KV_HEADS + k),
                                  wn_ref[:, (2 + k) * NSA_HD:(3 + k) * NSA_HD], jnp.zeros((padn, NSA_HD), F32)],
                                 axis=0)
            qk = qb_sc[pl.ds(k * rows_h, rows_h), :]
            s = jnp.where(wmask, _dot_nt(qk, kw.astype(BF16)), -jnp.inf)
            m = jnp.max(s, axis=1, keepdims=True)
            m = jnp.where(m > -jnp.inf, m, 0.0)
            e = jnp.exp(s - m)
            d = jnp.sum(e, axis=1, keepdims=True)
            o_win = _dot((e / jnp.where(d > 0, d, 1.0)).astype(BF16), vw.astype(BF16))
            rs = pl.ds(k * rows_h, rows_h)
            o_slc = acc_sc[rs, :] / l_sc[rs, :]
            o_cmp = ocmp_sc[rs, :]
            for g in range(NSA_GROUP):
                h = k * NSA_GROUP + g
                r = slice(g * dl, (g + 1) * dl)
                gc = gates[:, h * N_BRANCH:h * N_BRANCH + 1]
                gs = gates[:, h * N_BRANCH + 1:h * N_BRANCH + 2]
                gw = gates[:, h * N_BRANCH + 2:h * N_BRANCH + 3]
                o_ref[:, h * NSA_HD:(h + 1) * NSA_HD] = gc * o_cmp[r] + gs * o_slc[r] + gw * o_win[r]


def _nsa_sample(P, CMP, cache_kv, cache_win, layer, page_table, DB, DL, past_len, page):
    n_pages = page_table.shape[1]
    w_buf = cache_win.shape[2]
    lk = past_len + DL
    nch = CMP.shape[3]
    n_cmp = lk // CMP_STRIDE - CMP_BLOCK // CMP_STRIDE + 1
    n_slc = -(-lk // SLC_BLOCK)
    n_sel = min(N_SELECT, n_slc)
    nbl = _round_up(n_slc, LANES)
    pg = _pick_tile(n_pages, (32, 16, 8, 4, 2, 1))
    n_steps = n_pages // pg
    assert DL % SUBLANES == 0 and DL <= LANES and page == LANES and past_len == n_pages * page
    cov = jnp.asarray(_cover_matrix(n_cmp, n_slc, nch, nbl))
    kvw = NSA_KV_HEADS * NSA_HD
    page_spec = lambda i: pl.BlockSpec((None, None, page, 2, NSA_KV_HEADS, NSA_HD),
                                       lambda b, s, pt: (layer, pt[b, s * pg + i], 0, 1, 0, 0))
    in_specs = [
        pl.BlockSpec((DL, NSA_HEADS * NSA_HD), lambda b, s, pt: (b, 3)),
        pl.BlockSpec((DL, 2 * kvw), lambda b, s, pt: (b, 9)),
        pl.BlockSpec((DL, 2 * kvw), lambda b, s, pt: (b, 10)),
        pl.BlockSpec((DL, LANES), lambda b, s, pt: (b, 44)),
        pl.BlockSpec((None, None, NSA_KV_HEADS, nch, NSA_HD), lambda b, s, pt: (b, 0, 0, 0, 0)),
        pl.BlockSpec((None, None, NSA_KV_HEADS, nch, NSA_HD), lambda b, s, pt: (b, 1, 0, 0, 0)),
        pl.BlockSpec((nch, nbl), lambda b, s, pt: (0, 0)),
        pl.BlockSpec((None, None, w_buf, 2, NSA_KV_HEADS, NSA_HD), lambda b, s, pt: (layer, b, 0, 0, 0, 0)),
    ] + [page_spec(i) for i in range(pg)]
    rows = NSA_HEADS * DL
    grid_spec = pltpu.PrefetchScalarGridSpec(
        num_scalar_prefetch=1,
        grid=(DB, n_steps),
        in_specs=in_specs,
        out_specs=pl.BlockSpec((DL, NSA_HEADS * NSA_HD), lambda b, s, pt: (b, 0)),
        scratch_shapes=[
            pltpu.VMEM((rows, NSA_HD), BF16),
            pltpu.VMEM((rows, nbl), F32),
            pltpu.VMEM((rows, 1), F32),
            pltpu.VMEM((rows, 1), F32),
            pltpu.VMEM((rows, NSA_HD), F32),
            pltpu.VMEM((rows, NSA_HD), F32),
        ],
    )
    body = functools.partial(_nsa_sample_body, pg=pg, n_steps=n_steps, dl=DL, q_off=past_len, w_buf=w_buf,
                             n_cmp=n_cmp, n_slc=n_slc, n_sel=n_sel, nbl=nbl, page=page)
    return pl.pallas_call(
        body, grid_spec=grid_spec,
        out_shape=jax.ShapeDtypeStruct((DB * DL, NSA_HEADS * NSA_HD), F32),
        compiler_params=_cparams(("parallel", "arbitrary")),
        name="nsa_sample",
    )(page_table, P, P, P, P, CMP, CMP, cov, cache_win, *([cache_kv] * pg))


def _out_ln_body(h_ref, ro_ref, no_ref, wr_ref, wn_ref, g_ref, b_ref, o_ref, *, alpha):
    tm = h_ref.shape[0]
    half = tm // 2 if tm % (2 * BF16_ROWS) == 0 else tm
    for r0 in range(0, tm, half):
        rows = slice(r0, r0 + half)
        m = _dot(ro_ref[rows, :].astype(BF16), wr_ref[...]) + _dot(no_ref[rows, :].astype(BF16), wn_ref[...])
        o_ref[rows, :] = _layer_norm(alpha * h_ref[rows, :] + m, g_ref[...], b_ref[...])


def _out_ln(h, ro, no, w_out, g, b, alpha):
    T, D = h.shape
    kr = ro.shape[1]
    kn = no.shape[1]
    tm = _pick_tile(T, (512, 256, 128, 64, 32, 16, 8))
    return pl.pallas_call(
        functools.partial(_out_ln_body, alpha=alpha),
        grid=(T // tm,),
        in_specs=[
            pl.BlockSpec((tm, D), lambda i: (i, 0)),
            pl.BlockSpec((tm, kr), lambda i: (i, 0)),
            pl.BlockSpec((tm, kn), lambda i: (i, 0)),
            pl.BlockSpec((kr, D), lambda i: (0, 0)),
            pl.BlockSpec((kn, D), lambda i: (1, 0)),
            pl.BlockSpec((1, D), lambda i: (0, 0)),
            pl.BlockSpec((1, D), lambda i: (0, 0)),
        ],
        out_specs=pl.BlockSpec((tm, D), lambda i: (i, 0)),
        out_shape=jax.ShapeDtypeStruct((T, D), F32),
        compiler_params=_cparams(("parallel",)),
        name="out_ln",
    )(h, ro, no, w_out, w_out, g.reshape(1, D), b.reshape(1, D))


def _rope_tables(pos):
    half = NSA_HD // 2
    inv = ROPE_THETA ** (-jnp.arange(half, dtype=F32) / half)
    ang = pos.astype(F32)[:, None] * inv[None, :]
    cos = jnp.cos(ang)
    sin = jnp.sin(ang)
    return jnp.concatenate([cos, cos], -1), jnp.concatenate([-sin, sin], -1)


def _cmp_weights(w1, pos, w2):
    r = CMP_BLOCK // CMP_STRIDE
    w1r = w1.reshape(r, CMP_STRIDE * NSA_HD, CMP_HIDDEN)
    w1c = jnp.concatenate([w1r[i] for i in range(r)], axis=1).astype(BF16)
    posr = jnp.pad(pos.reshape(r, CMP_STRIDE * NSA_HD), ((0, SUBLANES - r), (0, 0))).astype(BF16)
    return w1c, posr, w2.astype(BF16)


def _layer_view(arr, l, shape):
    return arr.reshape(shape) if arr.shape[0] == 1 else arr[l].reshape(shape)


def _decoder_layer(x, B, L, q_off, s0, p, sample_ctx):
    alpha = p['alpha']
    h1, h1_bf = _ffn_ln(x, p['ffn1_w_up'], p['ffn1_w_down'], p['ln1_g'], p['ln1_b'], alpha, True)
    cos, sin = _rope_tables(q_off + jnp.arange(L, dtype=jnp.int32))
    P, kv_rows, win_rows = _proj(h1_bf, p['w_in'], p['rope_cols'], p['scale_cols'], cos, sin, L)
    ro, ret_s = _retention(P, s0, p['ret_gn_g'], p['ret_gn_b'], B, L)
    if sample_ctx is None:
        assert L % CMP_STRIDE == 0
        n_cmp = L // CMP_STRIDE - CMP_BLOCK // CMP_STRIDE + 1
        n_vec = 2 * NSA_KV_HEADS
        specs = [pl.BlockSpec((L, NSA_HD), (lambda v: (lambda b, g: (b, KV_CHUNK0 + v)))(v)) for v in range(n_vec)]
        CMP = _compress([P] * n_vec, specs, 1, L, 1, n_cmp, B, p['cmp_w1'], p['cmp_pos'], p['cmp_w2'])
        no = _nsa_prompt(P, CMP, B, L)
    else:
        cache_kv, cache_win, layer, page_table, past_len, page = sample_ctx
        n_pages = page_table.shape[1]
        lk = past_len + L
        assert (lk // CMP_STRIDE) * CMP_STRIDE <= past_len, "compression blocks must lie in the paged past"
        n_cmp = lk // CMP_STRIDE - CMP_BLOCK // CMP_STRIDE + 1
        n_in = _pick_tile(n_pages, (32, 16, 8, 4, 2, 1))
        n_grp = n_pages // n_in
        specs = [pl.BlockSpec((None, None, page, 2, NSA_KV_HEADS, NSA_HD),
                              (lambda i: (lambda b, g, pt: (layer, pt[b, (n_grp - 1 - g) * n_in + i], 0, 0, 0, 0)))(i))
                 for i in range(n_in)]
        CMP = _compress([cache_kv] * n_in, specs, n_in, page, n_grp, n_cmp, B,
                        p['cmp_w1'], p['cmp_pos'], p['cmp_w2'], page_table=page_table)
        no = _nsa_sample(P, CMP, cache_kv, cache_win, layer, page_table, B, L, past_len, page)
    x2 = _out_ln(h1, ro, no, p['w_out'], p['ln2_g'], p['ln2_b'], alpha)
    y, _ = _ffn_ln(x2, p['ffn2_w_up'], p['ffn2_w_down'], p['ln3_g'], p['ln3_b'], alpha, False)
    return y, ret_s, kv_rows, win_rows


def kernel(x_prompt, x_sample, state_ret, cache_nsa_kv, cache_win, page_table, ffn1_w_up, ffn1_w_down, ln1_g, ln1_b, w_in, w_out, ret_gn_g, ret_gn_b, cmp_pos_k, cmp_w1_k, cmp_w2_k, cmp_pos_v, cmp_w1_v, cmp_w2_v, ln2_g, ln2_b, ffn2_w_up, ffn2_w_down, ln3_g, ln3_b):
    B, L, D = x_prompt.shape
    DB, DL, _ = x_sample.shape
    depth = w_in.shape[0]
    n_pool, page = cache_nsa_kv.shape[1], cache_nsa_kv.shape[2]
    n_pages = page_table.shape[1]
    past_len = n_pages * page
    w_buf = cache_win.shape[2]
    alpha = (2.0 * depth) ** 0.25
    rope_np = np.zeros((N_IN_PAD // LANES, LANES), np.float32)
    rope_np[list(ROPE_CHUNKS)] = 1.0
    scale_np = np.ones((N_IN_PAD // LANES, LANES), np.float32)
    scale_np[list(KSCALE_CHUNKS)] = RET_DK ** -0.5
    rope_cols = jnp.asarray(rope_np.reshape(1, N_IN_PAD))
    scale_cols = jnp.asarray(scale_np.reshape(1, N_IN_PAD))

    yp = x_prompt.reshape(B * L, D)
    ys = x_sample.reshape(DB * DL, D)
    outs = [[] for _ in range(6)]
    for l in range(depth):
        k1, p1, k2 = _cmp_weights(cmp_w1_k[l], cmp_pos_k[l], cmp_w2_k[l])
        v1, q1, v2 = _cmp_weights(cmp_w1_v[l], cmp_pos_v[l], cmp_w2_v[l])
        p = {
            'alpha': alpha, 'rope_cols': rope_cols, 'scale_cols': scale_cols,
            'ffn1_w_up': ffn1_w_up[l].astype(BF16), 'ffn1_w_down': ffn1_w_down[l].astype(BF16),
            'ln1_g': ln1_g[l], 'ln1_b': ln1_b[l],
            'w_in': jnp.pad(w_in[l], ((0, 0), (0, N_IN_PAD - N_IN))).astype(BF16),
            'w_out': w_out[l].astype(BF16),
            'ret_gn_g': ret_gn_g[l], 'ret_gn_b': ret_gn_b[l],
            'cmp_w1': jnp.stack([k1, v1]), 'cmp_pos': jnp.stack([p1, q1]), 'cmp_w2': jnp.stack([k2, v2]),
            'ln2_g': ln2_g[l], 'ln2_b': ln2_b[l],
            'ffn2_w_up': ffn2_w_up[l].astype(BF16), 'ffn2_w_down': ffn2_w_down[l].astype(BF16),
            'ln3_g': ln3_g[l], 'ln3_b': ln3_b[l],
        }
        s0 = jnp.zeros((B, RET_HEADS, RET_DK, RET_DV), F32)
        yp, rs_p, kv_p, win_p = _decoder_layer(yp, B, L, 0, s0, p, None)
        ctx = (cache_nsa_kv, cache_win, l, page_table, past_len, page)
        ys, rs_s, kv_s, win_s = _decoder_layer(ys, DB, DL, past_len,
                                               _layer_view(state_ret, l, state_ret.shape[1:]), p, ctx)
        wl = min(WINDOW, L)
        outs[0].append(rs_p)
        outs[1].append(rs_s)
        outs[2].append(kv_p.reshape(B, L, 4, NSA_KV_HEADS, NSA_HD))
        outs[3].append(kv_s.reshape(DB, DL, 4, NSA_KV_HEADS, NSA_HD))
        outs[4].append(win_p.reshape(B, L, 2, NSA_KV_HEADS, NSA_HD)[:, L - wl:])
        win_s = win_s.reshape(DB, DL, 2, NSA_KV_HEADS, NSA_HD)
        outs[5].append(jnp.concatenate([cache_win[l], win_s], axis=1)[:, -w_buf:])
    return (yp.reshape(B, L, D), ys.reshape(DB, DL, D), jnp.stack(outs[0]), jnp.stack(outs[1]),
            jnp.stack(outs[2]), jnp.stack(outs[3]), jnp.stack(outs[4]), jnp.stack(outs[5]))
```

```python
import functools

import numpy as np
import jax
import jax.numpy as jnp
from jax import lax
from jax.experimental import pallas as pl
from jax.experimental.pallas import tpu as pltpu

F32 = jnp.float32
BF16 = jnp.bfloat16

LANES = 128
SUBLANES = 8
BF16_ROWS = 16
VMEM_LIMIT_BYTES = 56 * 1024 * 1024

RET_HEADS = 4
RET_DK = 128
RET_DV = 256
RET_CHUNK = 128
NSA_HEADS = 8
NSA_KV_HEADS = 2
NSA_HD = 128
NSA_GROUP = NSA_HEADS // NSA_KV_HEADS
CMP_BLOCK = 32
CMP_STRIDE = 16
CMP_HIDDEN = 2 * NSA_HD
SLC_BLOCK = 64
SLC_SHIFT = 6
N_SELECT = 16
WINDOW = 512
N_BRANCH = 3
ROPE_THETA = 10000.0
LN_EPS = 1e-5
NEG_BIG = -1e30
LOG2_E = 1.4426950408889634

N_IN = 5656
N_IN_PAD = 5760
KV_CHUNK0 = 32
WIN_CHUNK0 = 40
GATE_CHUNK = 44
ROPE_CHUNKS = tuple(range(0, 8)) + tuple(range(24, 32)) + (32, 33, 36, 37, 40, 41)
KSCALE_CHUNKS = tuple(range(4, 8))


def _cparams(sem):
    return pltpu.CompilerParams(dimension_semantics=sem, vmem_limit_bytes=VMEM_LIMIT_BYTES)


def _pick_tile(n, candidates):
    for c in candidates:
        if n % c == 0:
            return c
    return n


def _round_up(n, m):
    return (n + m - 1) // m * m


def _layer_norm(z, g, b):
    mu = jnp.mean(z, axis=-1, keepdims=True)
    zc = z - mu
    var = jnp.mean(zc * zc, axis=-1, keepdims=True)
    return zc * lax.rsqrt(var + LN_EPS) * g + b


def _dot(a, b):
    return jnp.dot(a, b, preferred_element_type=F32)


def _dot_nt(a, b):
    return lax.dot_general(a, b, (((1,), (1,)), ((), ())), preferred_element_type=F32)


def _ffn_ln_body(x_ref, wa_ref, wb_ref, wd_ref, g_ref, b_ref, *rest, nj, alpha, emit_bf16):
    if emit_bf16:
        o_ref, obf_ref, xbf_sc, acc_sc = rest
    else:
        o_ref, xbf_sc, acc_sc = rest
    j = pl.program_id(1)

    @pl.when(j == 0)
    def _init():
        xbf_sc[...] = x_ref[...].astype(BF16)
        acc_sc[...] = jnp.zeros_like(acc_sc)

    xb = xbf_sc[...]
    a = _dot(xb, wa_ref[...])
    b = _dot(xb, wb_ref[...])
    h = (a * jax.nn.sigmoid(a)) * b
    acc_sc[...] += _dot(h.astype(BF16), wd_ref[...])

    @pl.when(j == nj - 1)
    def _finish():
        z = alpha * x_ref[...] + 0.5 * acc_sc[...]
        y = _layer_norm(z, g_ref[...], b_ref[...])
        o_ref[...] = y
        if emit_bf16:
            obf_ref[...] = y.astype(BF16)


def _ffn_ln(x, w_up, w_down, g, b, alpha, emit_bf16):
    T, D = x.shape
    F = w_down.shape[0]
    tm = _pick_tile(T, (512, 256, 128, 64, 32, 16, 8))
    tf = _pick_tile(F, (512, 256, 128))
    nj = F // tf
    out_shape = [jax.ShapeDtypeStruct((T, D), F32)]
    out_specs = [pl.BlockSpec((tm, D), lambda i, j: (i, 0))]
    if emit_bf16:
        out_shape.append(jax.ShapeDtypeStruct((T, D), BF16))
        out_specs.append(pl.BlockSpec((tm, D), lambda i, j: (i, 0)))
    res = pl.pallas_call(
        functools.partial(_ffn_ln_body, nj=nj, alpha=alpha, emit_bf16=emit_bf16),
        grid=(T // tm, nj),
        in_specs=[
            pl.BlockSpec((tm, D), lambda i, j: (i, 0)),
            pl.BlockSpec((D, tf), lambda i, j: (0, j)),
            pl.BlockSpec((D, tf), lambda i, j: (0, nj + j)),
            pl.BlockSpec((tf, D), lambda i, j: (j, 0)),
            pl.BlockSpec((1, D), lambda i, j: (0, 0)),
            pl.BlockSpec((1, D), lambda i, j: (0, 0)),
        ],
        out_specs=out_specs,
        out_shape=out_shape,
        scratch_shapes=[pltpu.VMEM((tm, D), BF16), pltpu.VMEM((tm, D), F32)],
        compiler_params=_cparams(("parallel", "arbitrary")),
        name="ffn_ln",
    )(x, w_up, w_up, w_down, g.reshape(1, D), b.reshape(1, D))
    return res if emit_bf16 else (res[0], None)


def _rows_view(ref, j):
    rows = ref.shape[0]
    n = int(np.prod(ref.shape[1:-1]))
    return ref.reshape(n * rows, ref.shape[-1]), pl.ds(j, rows, stride=n)


def _head_rows(ref, j):
    view, idx = _rows_view(ref, j)
    return view[idx, :]


def _store_rows(ref, j, val):
    view, idx = _rows_view(ref, j)
    view[idx, :] = val


def _proj_body(x_ref, w_ref, cos_ref, sin_ref, rope_ref, scale_ref, o_ref, kv_ref, win_ref, *, n_chunk, j_rows):
    y = _dot(x_ref[...], w_ref[...])
    cos = cos_ref[...]
    sin = sin_ref[...]
    for c in range(n_chunk):
        sl = slice(c * LANES, (c + 1) * LANES)
        yc = y[:, sl]
        roped = yc * cos + pltpu.roll(yc, NSA_HD // 2, 1) * sin
        o_ref[:, sl] = jnp.where(rope_ref[:, sl] > 0.5, roped, yc) * scale_ref[:, sl]

    @pl.when(pl.program_id(1) == j_rows)
    def _emit_cache_rows():
        c0 = KV_CHUNK0 - j_rows * n_chunk
        for r in range(WIN_CHUNK0 - KV_CHUNK0):
            _store_rows(kv_ref, r, o_ref[:, (c0 + r) * LANES:(c0 + r + 1) * LANES])
        c0 = WIN_CHUNK0 - j_rows * n_chunk
        for r in range(GATE_CHUNK - WIN_CHUNK0):
            _store_rows(win_ref, r, o_ref[:, (c0 + r) * LANES:(c0 + r + 1) * LANES])


def _proj(x_bf, w_bf, rope_cols, scale_cols, cos, sin, rows_per_seq):
    T, D = x_bf.shape
    N = w_bf.shape[1]
    tm = _pick_tile(T, (512, 256, 128, 64, 32, 16, 8))
    tn = 1920
    assert N % tn == 0
    n_chunk = tn // LANES
    if rows_per_seq >= tm:
        assert rows_per_seq % tm == 0
        n_tab = rows_per_seq // tm
    else:
        assert tm % rows_per_seq == 0
        cos = jnp.tile(cos, (tm // rows_per_seq, 1))
        sin = jnp.tile(sin, (tm // rows_per_seq, 1))
        n_tab = 1
    j_rows = KV_CHUNK0 // n_chunk
    assert (GATE_CHUNK - 1) // n_chunk == j_rows, "cache-row columns must sit in one column tile"
    kv_shape = (T, 4, NSA_KV_HEADS, NSA_HD)
    win_shape = (T, 2, NSA_KV_HEADS, NSA_HD)
    return pl.pallas_call(
        functools.partial(_proj_body, n_chunk=n_chunk, j_rows=j_rows),
        grid=(T // tm, N // tn),
        in_specs=[
            pl.BlockSpec((tm, D), lambda i, j: (i, 0)),
            pl.BlockSpec((D, tn), lambda i, j: (0, j)),
            pl.BlockSpec((tm, LANES), lambda i, j: (i % n_tab, 0)),
            pl.BlockSpec((tm, LANES), lambda i, j: (i % n_tab, 0)),
            pl.BlockSpec((1, tn), lambda i, j: (0, j)),
            pl.BlockSpec((1, tn), lambda i, j: (0, j)),
        ],
        out_specs=[
            pl.BlockSpec((tm, tn), lambda i, j: (i, j)),
            pl.BlockSpec((tm,) + kv_shape[1:], lambda i, j: (i, 0, 0, 0)),
            pl.BlockSpec((tm,) + win_shape[1:], lambda i, j: (i, 0, 0, 0)),
        ],
        out_shape=[
            jax.ShapeDtypeStruct((T, N), F32),
            jax.ShapeDtypeStruct(kv_shape, F32),
            jax.ShapeDtypeStruct(win_shape, F32),
        ],
        compiler_params=_cparams(("parallel", "arbitrary")),
        name="proj_rope",
    )(x_bf, w_bf, cos, sin, rope_cols, scale_cols)


def _ret_body(q_ref, k_ref, v_ref, g_ref, s0_ref, dm_ref, ind_ref, std_ref, cd_ref, gng_ref, gnb_ref,
              o_ref, sout_ref, s_sc, *, nc, rows, rows_pad, per_step):
    c = pl.program_id(1)

    @pl.when(c == 0)
    def _load_state():
        s_sc[...] = s0_ref[...]

    pad = rows_pad - rows
    for sub in range(per_step):
        rs = slice(sub * rows, (sub + 1) * rows)
        for h in range(RET_HEADS):
            ks = slice(h * RET_DK, (h + 1) * RET_DK)
            vs = slice(h * RET_DV, (h + 1) * RET_DV)
            q = q_ref[rs, ks]
            k = k_ref[rs, ks]
            v = v_ref[rs, vs]
            kd = k * std_ref[h]
            if pad:
                k = jnp.concatenate([k, jnp.zeros((pad, RET_DK), F32)], axis=0)
                kd = jnp.concatenate([kd, jnp.zeros((pad, RET_DK), F32)], axis=0)
                v = jnp.concatenate([v, jnp.zeros((pad, RET_DV), F32)], axis=0)
            s_old = s_sc[h]
            vb = v.astype(BF16)
            a = _dot_nt(q.astype(BF16), k.astype(BF16)) * dm_ref[h]
            o = _dot(a.astype(BF16), vb) + _dot((q * ind_ref[h]).astype(BF16), s_old.astype(BF16))
            s_sc[h] = s_old * cd_ref[h] + _dot(kd.T.astype(BF16), vb)
            mu = jnp.mean(o, axis=-1, keepdims=True)
            oc = o - mu
            var = jnp.mean(oc * oc, axis=-1, keepdims=True)
            on = oc * lax.rsqrt(var + LN_EPS) * gng_ref[:, vs] + gnb_ref[:, vs]
            gate = g_ref[rs, vs]
            o_ref[rs, vs] = (gate * jax.nn.sigmoid(gate)) * on

    @pl.when(c == nc - 1)
    def _store_state():
        sout_ref[...] = s_sc[...]


def _retention(P, s0, gn_g, gn_b, B, L):
    C = RET_CHUNK if L % RET_CHUNK == 0 else L
    nc = L // C
    CP = max(C, LANES)
    lg = jnp.log1p(-jnp.exp2(-5.0 - jnp.arange(RET_HEADS, dtype=F32)))
    i = jnp.arange(C, dtype=F32)
    diff = i[:, None] - i[None, :]
    dmask = jnp.where(diff >= 0, jnp.exp(lg[:, None, None] * jnp.maximum(diff, 0.0)), 0.0)
    dmask = jnp.pad(dmask, ((0, 0), (0, 0), (0, CP - C)))
    in_decay = jnp.broadcast_to(jnp.exp(lg[:, None] * (i + 1.0))[:, :, None], (RET_HEADS, C, RET_DK))
    st_decay = jnp.broadcast_to(jnp.exp(lg[:, None] * (C - 1.0 - i))[:, :, None], (RET_HEADS, C, RET_DK))
    chunk_decay = jnp.broadcast_to(jnp.exp(lg * C)[:, None, None], (RET_HEADS, 1, RET_DV))
    qw = RET_HEADS * RET_DK
    vw = RET_HEADS * RET_DV
    const3 = lambda b, c: (0, 0, 0)
    per_step = _pick_tile(nc, (4, 2, 1))
    nc = nc // per_step
    CS = C * per_step
    return pl.pallas_call(
        functools.partial(_ret_body, nc=nc, rows=C, rows_pad=CP, per_step=per_step),
        grid=(B, nc),
        in_specs=[
            pl.BlockSpec((CS, qw), lambda b, c: (b * nc + c, 0)),
            pl.BlockSpec((CS, qw), lambda b, c: (b * nc + c, 1)),
            pl.BlockSpec((CS, vw), lambda b, c: (b * nc + c, 1)),
            pl.BlockSpec((CS, vw), lambda b, c: (b * nc + c, 2)),
            pl.BlockSpec((None, RET_HEADS, RET_DK, RET_DV), lambda b, c: (b, 0, 0, 0)),
            pl.BlockSpec((RET_HEADS, C, CP), const3),
            pl.BlockSpec((RET_HEADS, C, RET_DK), const3),
            pl.BlockSpec((RET_HEADS, C, RET_DK), const3),
            pl.BlockSpec((RET_HEADS, 1, RET_DV), const3),
            pl.BlockSpec((1, vw), lambda b, c: (0, 0)),
            pl.BlockSpec((1, vw), lambda b, c: (0, 0)),
        ],
        out_specs=[
            pl.BlockSpec((CS, vw), lambda b, c: (b * nc + c, 0)),
            pl.BlockSpec((None, RET_HEADS, RET_DK, RET_DV), lambda b, c: (b, 0, 0, 0)),
        ],
        out_shape=[
            jax.ShapeDtypeStruct((B * L, vw), F32),
            jax.ShapeDtypeStruct((B, RET_HEADS, RET_DK, RET_DV), F32),
        ],
        scratch_shapes=[pltpu.VMEM((RET_HEADS, RET_DK, RET_DV), F32)],
        compiler_params=_cparams(("parallel", "arbitrary")),
        name="retention",
    )(P, P, P, P, s0, dmask, in_decay, st_decay, chunk_decay, gn_g.reshape(1, vw), gn_b.reshape(1, vw))


def _cmp_body(*refs, n_in, rows, n_grp, n_cmp, paged):
    n_vec = 2 * NSA_KV_HEADS
    if paged:
        refs = refs[1:]
    n_src = n_in if paged else n_vec * n_in
    x_refs = refs[:n_src]
    w1_ref, pos_ref, w2_ref, o_ref, carry_sc, xc_sc = refs[n_src:]
    g = pl.program_id(1)
    grp = n_grp - 1 - g
    cpi = rows // CMP_STRIDE
    M = n_in * cpi

    @pl.when(g == 0)
    def _init():
        carry_sc[...] = jnp.zeros_like(carry_sc)

    row = lax.broadcasted_iota(jnp.int32, (M, 1), 0)
    grp_in = 1 if cpi % BF16_ROWS == 0 else BF16_ROWS // cpi
    assert n_in % grp_in == 0 and (grp_in * cpi) % BF16_ROWS == 0
    for t in range(2):
        w1 = w1_ref[t]
        gp = _dot(pos_ref[t], w1)
        posterm = gp[0:1, :CMP_HIDDEN] + gp[1:2, CMP_HIDDEN:]
        for hd in range(NSA_KV_HEADS):
            v = t * NSA_KV_HEADS + hd
            for i0 in range(0, n_in, grp_in):
                if paged:
                    parts = [pltpu.einshape("csd->scd",
                                            _head_rows(x_refs[i], v).reshape(cpi, CMP_STRIDE, NSA_HD))
                             for i in range(i0, i0 + grp_in)]
                    piece = lambda s: jnp.concatenate([xt[s] for xt in parts], axis=0)
                else:
                    refs_i = [x_refs[v * n_in + i] for i in range(i0, i0 + grp_in)]
                    piece = lambda s: jnp.concatenate(
                        [r[pl.ds(s, cpi, stride=CMP_STRIDE), :] for r in refs_i], axis=0)
                r0 = v * M + i0 * cpi
                for s in range(CMP_STRIDE):
                    xc_sc[r0:r0 + grp_in * cpi, s * NSA_HD:(s + 1) * NSA_HD] = piece(s).astype(BF16)
        t0 = t * NSA_KV_HEADS * M
        gg_all = _dot(xc_sc[t0:t0 + NSA_KV_HEADS * M, :], w1)
        for hd in range(NSA_KV_HEADS):
            v = t * NSA_KV_HEADS + hd
            gg = gg_all[hd * M:(hd + 1) * M]
            g0 = gg[:, :CMP_HIDDEN]
            g1 = gg[:, CMP_HIDDEN:]
            nxt = pltpu.roll(g1, M - 1, 0)
            nxt = jnp.where(row == M - 1, carry_sc[v][0:1, :], nxt)
            carry_sc[v] = g1[0:SUBLANES, :]
            hid = g0 + nxt + posterm
            out = _dot(jax.nn.gelu(hid).astype(BF16), w2_ref[t])
            o_ref[t, hd] = jnp.where(grp * M + row < n_cmp, out, 0.0)


def _compress(srcs, src_specs, n_in, rows, n_grp, n_cmp, B, w1, pos, w2, page_table=None):
    n_vec = 2 * NSA_KV_HEADS
    M = n_in * rows // CMP_STRIDE
    nch = n_grp * M
    paged = page_table is not None
    const3 = lambda *a: (0, 0, 0)
    in_specs = list(src_specs) + [
        pl.BlockSpec((2, CMP_STRIDE * NSA_HD, 2 * CMP_HIDDEN), const3),
        pl.BlockSpec((2, SUBLANES, CMP_STRIDE * NSA_HD), const3),
        pl.BlockSpec((2, CMP_HIDDEN, NSA_HD), const3),
    ]
    out_spec = pl.BlockSpec((None, 2, NSA_KV_HEADS, M, NSA_HD), lambda *a: (a[0], 0, 0, n_grp - 1 - a[1], 0))
    body = functools.partial(_cmp_body, n_in=n_in, rows=rows, n_grp=n_grp, n_cmp=n_cmp, paged=paged)
    out_shape = jax.ShapeDtypeStruct((B, 2, NSA_KV_HEADS, nch, NSA_HD), F32)
    scratch = [pltpu.VMEM((n_vec, SUBLANES, CMP_HIDDEN), F32),
               pltpu.VMEM((n_vec * M, CMP_STRIDE * NSA_HD), BF16)]
    sem = ("parallel", "arbitrary")
    if paged:
        grid_spec = pltpu.PrefetchScalarGridSpec(
            num_scalar_prefetch=1, grid=(B, n_grp), in_specs=in_specs, out_specs=out_spec,
            scratch_shapes=scratch)
        return pl.pallas_call(body, grid_spec=grid_spec, out_shape=out_shape,
                              compiler_params=_cparams(sem), name="nsa_compress_paged")(
            page_table, *srcs, w1, pos, w2)
    return pl.pallas_call(body, grid=(B, n_grp), in_specs=in_specs, out_specs=out_spec,
                          out_shape=out_shape, scratch_shapes=scratch,
                          compiler_params=_cparams(sem), name="nsa_compress")(*srcs, w1, pos, w2)


def _cover_matrix(n_cmp, n_slc, rows, cols):
    c_i = np.arange(n_cmp)[:, None]
    n_i = np.arange(n_slc)[None, :]
    cov = np.clip(np.minimum(c_i * CMP_STRIDE + CMP_BLOCK, (n_i + 1) * SLC_BLOCK)
                  - np.maximum(c_i * CMP_STRIDE, n_i * SLC_BLOCK), 0, None).astype(np.float32) / CMP_BLOCK
    out = np.zeros((rows, cols), np.float32)
    out[:n_cmp, :n_slc] = cov
    return out


def _nsa_prompt_body(q_ref, kc_ref, vc_ref, ks_ref, vs_ref, kw_ref, vw_ref, gate_ref, covt_ref, o_ref,
                     kcb_sc, vct_sc, ksb_sc, vst_sc, kwb_sc, vwt_sc, prio_sc, sel_sc, gt_sc,
                     m_sc, l_sc, acc_sc, out_sc, s_sc, *, tq, L, n_cmp, n_slc, n_sel, nbp):
    kvh = pl.program_id(1)
    qi = pl.program_id(2)
    tk = tq
    nch = kc_ref.shape[0]
    scale = NSA_HD ** -0.5

    @pl.when(qi == 0)
    def _stage_kv():
        kcb_sc[...] = kc_ref[...].astype(BF16)
        vct_sc[...] = vc_ref[...].T.astype(BF16)
        ksb_sc[...] = ks_ref[...].astype(BF16)
        kwb_sc[...] = kw_ref[...].astype(BF16)
        for i in range(L // tk):
            vst_sc[i] = vs_ref[i * tk:(i + 1) * tk, :].T.astype(BF16)
            vwt_sc[i] = vw_ref[i * tk:(i + 1) * tk, :].T.astype(BF16)

    G = NSA_GROUP
    W = G * tq
    t0 = qi * tq
    t_row = t0 + lax.broadcasted_iota(jnp.int32, (1, tq), 1)
    lane = lax.broadcasted_iota(jnp.int32, (1, W), 1)
    t_all = t0 + (lane & (tq - 1))
    qcat = jnp.concatenate([q_ref[:, g * NSA_HD:(g + 1) * NSA_HD] for g in range(G)], axis=0)
    qcat = (qcat * (scale * LOG2_E)).astype(BF16)
    gt_sc[...] = jax.nn.sigmoid(gate_ref[...]).T

    def gate_row(br):
        return jnp.concatenate(
            [gt_sc[pl.ds((kvh * G + g) * N_BRANCH + br, 1), :] for g in range(G)], axis=1)

    kp_l = lax.broadcasted_iota(jnp.int32, (tk, 1), 0)
    t_l = lane & (tq - 1)
    causal_bias = jnp.where(kp_l <= t_l, 0.0, NEG_BIG)
    far_bias = jnp.where(kp_l > t_l, 0.0, NEG_BIG)

    def reset():
        m_sc[...] = jnp.full((1, W), NEG_BIG, F32)
        l_sc[...] = jnp.zeros((1, W), F32)
        acc_sc[...] = jnp.zeros((NSA_HD, W), F32)

    def score(kb_sc, kt, bias):
        k0 = pl.multiple_of(kt * tk, tk)
        s = _dot_nt(kb_sc[pl.ds(k0, tk), :], qcat)
        return s if bias is None else s + bias

    def update(vt_sc, kt, s):
        m_old = m_sc[...]
        m_new = jnp.maximum(m_old, jnp.max(s, axis=0, keepdims=True))
        p = jnp.exp2(s - m_new)
        alpha = jnp.exp2(m_old - m_new)
        l_sc[...] = alpha * l_sc[...] + jnp.sum(p, axis=0, keepdims=True)
        acc_sc[...] = alpha * acc_sc[...] + _dot(vt_sc[kt], p.astype(BF16))
        m_sc[...] = m_new

    def branch_out(br):
        return (gate_row(br) / l_sc[...]) * acc_sc[...]

    reset()
    far = WINDOW // tk
    tiles = []
    for back in range(far, -1, -1):
        bias = far_bias if back == far else (causal_bias if back == 0 else None)
        if back > 0:
            off = jnp.where(qi >= back, 0.0, NEG_BIG).astype(F32)
            bias = off if bias is None else bias + off
        kt = jnp.maximum(qi - back, 0)
        tiles.append((kt, score(kwb_sc, kt, bias)))
    for kt, s_w in tiles:
        update(vwt_sc, kt, s_w)
    out_sc[...] = branch_out(2)

    c_i = lax.broadcasted_iota(jnp.int32, (nch, 1), 0)
    c_end = jnp.where(c_i < n_cmp, c_i * CMP_STRIDE + (CMP_BLOCK - 1), L)
    s = jnp.where(c_end <= t_all, _dot_nt(kcb_sc[...], qcat), -jnp.inf)
    m = jnp.max(s, axis=0, keepdims=True)
    m = jnp.where(m > -jnp.inf, m, 0.0)
    e = jnp.exp2(s - m)
    d = jnp.sum(e, axis=0, keepdims=True)
    p = e / jnp.where(d > 0, d, 1.0)
    out_sc[...] += gate_row(0) * _dot(vct_sc[...], p.astype(BF16))
    psum = p[:, 0:tq]
    for g in range(1, G):
        psum = psum + p[:, g * tq:(g + 1) * tq]

    imp = jnp.dot(covt_ref[...], psum, preferred_element_type=F32, precision=lax.Precision.HIGHEST)
    n_i = lax.broadcasted_iota(jnp.int32, (nbp, 1), 0)
    valid = (n_i * SLC_BLOCK <= t_row) & (n_i < n_slc)
    cur = t_row >> SLC_SHIFT
    forced = (n_i == 0) | (n_i == cur) | (n_i == cur - 1)
    prio = jnp.where(forced, jnp.inf, jnp.where(valid, imp, -jnp.inf))
    prio_sc[...] = prio
    cnt = jnp.zeros((nbp, tq), jnp.int32)
    for mm in range(n_slc):
        pm = prio_sc[mm:mm + 1, :]
        tie = jnp.where(n_i > mm, 1, 0)
        cnt = cnt + jnp.where(pm > prio, 1, jnp.where(pm == prio, tie, 0))
    selbias = jnp.where((cnt < n_sel) & valid, 0.0, NEG_BIG)
    sel_sc[...] = jnp.concatenate([selbias] * G, axis=1)

    def sel_bias(kt):
        per_tile = tk // SLC_BLOCK
        rows = [jnp.broadcast_to(sel_sc[pl.ds(kt * per_tile + r, 1), :], (SLC_BLOCK, W))
                for r in range(per_tile)]
        return jnp.concatenate(rows, axis=0)

    reset()
    s_sc[...] = score(ksb_sc, 0, sel_bias(0))

    def slc_body(kt, carry):
        s_next = score(ksb_sc, kt + 1, sel_bias(kt + 1))
        update(vst_sc, kt, s_sc[...])
        s_sc[...] = s_next
        return carry

    lax.fori_loop(0, qi, slc_body, 0)
    update(vst_sc, qi, s_sc[...] + causal_bias)
    out_sc[...] += branch_out(1)
    for g in range(G):
        o_ref[:, g * NSA_HD:(g + 1) * NSA_HD] = out_sc[:, g * tq:(g + 1) * tq].T


def _nsa_prompt(P, CMP, B, L):
    tq = _pick_tile(L, (256, 128))
    assert L % tq == 0 and tq % SLC_BLOCK == 0 and WINDOW % tq == 0 and tq & (tq - 1) == 0
    nq = L // tq
    nch = CMP.shape[3]
    n_cmp = L // CMP_STRIDE - CMP_BLOCK // CMP_STRIDE + 1
    n_slc = -(-L // SLC_BLOCK)
    n_sel = min(N_SELECT, n_slc)
    nbp = _round_up(n_slc, SUBLANES)
    covt = jnp.asarray(_cover_matrix(n_cmp, n_slc, nch, nbp).T)
    gw = NSA_GROUP * NSA_HD
    col = lambda c: (lambda b, k, i: (b, c + k))
    body = functools.partial(_nsa_prompt_body, tq=tq, L=L, n_cmp=n_cmp, n_slc=n_slc, n_sel=n_sel, nbp=nbp)
    return pl.pallas_call(
        body,
        grid=(B, NSA_KV_HEADS, nq),
        in_specs=[
            pl.BlockSpec((tq, gw), lambda b, k, i: (b * nq + i, 6 + k)),
            pl.BlockSpec((None, None, None, nch, NSA_HD), lambda b, k, i: (b, 0, k, 0, 0)),
            pl.BlockSpec((None, None, None, nch, NSA_HD), lambda b, k, i: (b, 1, k, 0, 0)),
            pl.BlockSpec((L, NSA_HD), col(36)),
            pl.BlockSpec((L, NSA_HD), col(38)),
            pl.BlockSpec((L, NSA_HD), col(40)),
            pl.BlockSpec((L, NSA_HD), col(42)),
            pl.BlockSpec((tq, LANES), lambda b, k, i: (b * nq + i, 44)),
            pl.BlockSpec((nbp, nch), lambda b, k, i: (0, 0)),
        ],
        out_specs=pl.BlockSpec((tq, gw), lambda b, k, i: (b * nq + i, k)),
        out_shape=jax.ShapeDtypeStruct((B * L, NSA_HEADS * NSA_HD), F32),
        scratch_shapes=[
            pltpu.VMEM((nch, NSA_HD), BF16),
            pltpu.VMEM((NSA_HD, nch), BF16),
            pltpu.VMEM((L, NSA_HD), BF16),
            pltpu.VMEM((L // tq, NSA_HD, tq), BF16),
            pltpu.VMEM((L, NSA_HD), BF16),
            pltpu.VMEM((L // tq, NSA_HD, tq), BF16),
            pltpu.VMEM((nbp, tq), F32),
            pltpu.VMEM((nbp, NSA_GROUP * tq), F32),
            pltpu.VMEM((LANES, tq), F32),
            pltpu.VMEM((1, NSA_GROUP * tq), F32),
            pltpu.VMEM((1, NSA_GROUP * tq), F32),
            pltpu.VMEM((NSA_HD, NSA_GROUP * tq), F32),
            pltpu.VMEM((NSA_HD, NSA_GROUP * tq), F32),
            pltpu.VMEM((tq, NSA_GROUP * tq), F32),
        ],
        compiler_params=_cparams(("parallel", "arbitrary", "arbitrary")),
        name="nsa_prompt",
    )(P, CMP, CMP, P, P, P, P, P, covt)


def _nsa_sample_body(*refs, pg, n_steps, dl, q_off, w_buf, n_cmp, n_slc, n_sel, nbl, page):
    refs = refs[1:]
    q_ref, kvn_ref, wn_ref, gate_ref, kc_ref, vc_ref, cov_ref, cw_ref = refs[:8]
    pg_refs = refs[8:8 + pg]
    o_ref, qb_sc, sel_sc, m_sc, l_sc, acc_sc, ocmp_sc = refs[8 + pg:]
    step = pl.program_id(1)
    rows_h = NSA_GROUP * dl
    rows = NSA_KV_HEADS * rows_h
    scale = NSA_HD ** -0.5
    nch = kc_ref.shape[1]

    def tok_pos(n):
        r = lax.broadcasted_iota(jnp.int32, (n, 1), 0)
        return q_off + r % dl

    @pl.when(step == 0)
    def _select():
        for k in range(NSA_KV_HEADS):
            for g in range(NSA_GROUP):
                h = k * NSA_GROUP + g
                qb_sc[pl.ds(h * dl, dl), :] = (q_ref[:, h * NSA_HD:(h + 1) * NSA_HD] * scale).astype(BF16)
        t_h = tok_pos(rows_h)
        c_i = lax.broadcasted_iota(jnp.int32, (1, nch), 1)
        cmask = (c_i * CMP_STRIDE + (CMP_BLOCK - 1) <= t_h) & (c_i < n_cmp)
        psums = []
        for k in range(NSA_KV_HEADS):
            qk = qb_sc[pl.ds(k * rows_h, rows_h), :]
            s = jnp.where(cmask, _dot_nt(qk, kc_ref[k].astype(BF16)), -jnp.inf)
            m = jnp.max(s, axis=1, keepdims=True)
            m = jnp.where(m > -jnp.inf, m, 0.0)
            e = jnp.exp(s - m)
            d = jnp.sum(e, axis=1, keepdims=True)
            p = e / jnp.where(d > 0, d, 1.0)
            ocmp_sc[pl.ds(k * rows_h, rows_h), :] = _dot(p.astype(BF16), vc_ref[k].astype(BF16))
            psum = p[0:dl]
            for g in range(1, NSA_GROUP):
                psum = psum + p[g * dl:(g + 1) * dl]
            psums.append(psum)
        imp = jnp.dot(jnp.concatenate(psums, axis=0), cov_ref[...], preferred_element_type=F32,
                      precision=lax.Precision.HIGHEST)
        t_s = tok_pos(NSA_KV_HEADS * dl)
        n_i = lax.broadcasted_iota(jnp.int32, (1, nbl), 1)
        valid = (n_i * SLC_BLOCK <= t_s) & (n_i < n_slc)
        cur = t_s >> SLC_SHIFT
        forced = (n_i == 0) | (n_i == cur) | (n_i == cur - 1)
        prio = jnp.where(forced, jnp.inf, jnp.where(valid, imp, -jnp.inf))
        n_f = n_i.astype(F32)
        alive = jnp.broadcast_to(jnp.where(n_i < n_slc, 1.0, 0.0), prio.shape)
        sel = jnp.zeros(prio.shape, F32)
        for _ in range(n_sel):
            mx = jnp.max(jnp.where(alive > 0.5, prio, -jnp.inf), axis=1, keepdims=True)
            cand = (alive > 0.5) & (prio == mx)
            first = jnp.min(jnp.where(cand, n_f, float(nbl)), axis=1, keepdims=True)
            pick = n_f == first
            sel = jnp.where(pick, 1.0, sel)
            alive = jnp.where(pick, 0.0, alive)
        sel = jnp.where(valid, sel, 0.0)
        for k in range(NSA_KV_HEADS):
            for g in range(NSA_GROUP):
                sel_sc[pl.ds((k * NSA_GROUP + g) * dl, dl), :] = sel[k * dl:(k + 1) * dl]
        m_sc[...] = jnp.full(m_sc.shape, NEG_BIG, F32)
        l_sc[...] = jnp.zeros_like(l_sc)
        acc_sc[...] = jnp.zeros_like(acc_sc)

    selb = sel_sc[...].astype(BF16)
    n_col = lax.broadcasted_iota(jnp.int32, (nbl, 1), 0)

    def online_update(k, s, mask, v_rows):
        rs = pl.ds(k * rows_h, rows_h)
        m_old = m_sc[rs, :]
        m_new = jnp.maximum(m_old, jnp.max(s, axis=1, keepdims=True))
        p = jnp.where(mask, jnp.exp(s - m_new), 0.0)
        alpha = jnp.exp(m_old - m_new)
        l_sc[rs, :] = alpha * l_sc[rs, :] + jnp.sum(p, axis=1, keepdims=True)
        acc_sc[rs, :] = alpha * acc_sc[rs, :] + _dot(p.astype(BF16), v_rows)
        m_sc[rs, :] = m_new

    t_h = tok_pos(rows_h)

    def selected(kpos):
        blk = jnp.where(n_col == (kpos >> SLC_SHIFT), 1.0, 0.0).astype(BF16)
        return _dot(selb, blk)

    kpos = step * (pg * page) + lax.broadcasted_iota(jnp.int32, (1, pg * page), 1)
    sel_all = selected(kpos)
    for k in range(NSA_KV_HEADS):
        qk = qb_sc[pl.ds(k * rows_h, rows_h), :]
        mask = (sel_all[k * rows_h:(k + 1) * rows_h] > 0.5) & (kpos <= t_h)
        k_rows = jnp.concatenate([_head_rows(r, k).astype(BF16) for r in pg_refs], axis=0)
        v_rows = jnp.concatenate([_head_rows(r, NSA_KV_HEADS + k).astype(BF16) for r in pg_refs], axis=0)
        s = jnp.where(mask, _dot_nt(qk, k_rows), NEG_BIG)
        online_update(k, s, mask, v_rows)

    @pl.when(step == n_steps - 1)
    def _finish():
        padn = LANES - dl
        j_new = lax.broadcasted_iota(jnp.int32, (1, LANES), 1)
        kpos_n = q_off + j_new
        sel_n = selected(kpos_n)
        for k in range(NSA_KV_HEADS):
            kn = jnp.concatenate([kvn_ref[:, k * NSA_HD:(k + 1) * NSA_HD], jnp.zeros((padn, NSA_HD), F32)], axis=0)
            vn = jnp.concatenate([kvn_ref[:, (2 + k) * NSA_HD:(3 + k) * NSA_HD], jnp.zeros((padn, NSA_HD), F32)],
                                 axis=0)
            qk = qb_sc[pl.ds(k * rows_h, rows_h), :]
            mask = (sel_n[k * rows_h:(k + 1) * rows_h] > 0.5) & (kpos_n <= t_h) & (j_new < dl)
            s = jnp.where(mask, _dot_nt(qk, kn.astype(BF16)), NEG_BIG)
            online_update(k, s, mask, vn.astype(BF16))
        j_w = lax.broadcasted_iota(jnp.int32, (1, w_buf + LANES), 1)
        pos_w = q_off - w_buf + j_w
        dlt = t_h - pos_w
        wmask = (j_w < w_buf + dl) & (pos_w >= 0) & (dlt >= 0) & (dlt < WINDOW)
        gates = jax.nn.sigmoid(gate_ref[...])
        for k in range(NSA_KV_HEADS):
            kw = jnp.concatenate([_head_rows(cw_ref, k),
                                  wn_ref[:, k * NSA_HD:(k + 1) * NSA_HD], jnp.zeros((padn, NSA_HD), F32)], axis=0)
            vw = jnp.concatenate([_head_rows(cw_ref, NSA_KV_HEADS + k),
                                  wn_ref[:, (2 + k) * NSA_HD:(3 + k) * NSA_HD], jnp.zeros((padn, NSA_HD), F32)],
                                 axis=0)
            qk = qb_sc[pl.ds(k * rows_h, rows_h), :]
            s = jnp.where(wmask, _dot_nt(qk, kw.astype(BF16)), -jnp.inf)
            m = jnp.max(s, axis=1, keepdims=True)
            m = jnp.where(m > -jnp.inf, m, 0.0)
            e = jnp.exp(s - m)
            d = jnp.sum(e, axis=1, keepdims=True)
            o_win = _dot((e / jnp.where(d > 0, d, 1.0)).astype(BF16), vw.astype(BF16))
            rs = pl.ds(k * rows_h, rows_h)
            o_slc = acc_sc[rs, :] / l_sc[rs, :]
            o_cmp = ocmp_sc[rs, :]
            for g in range(NSA_GROUP):
                h = k * NSA_GROUP + g
                r = slice(g * dl, (g + 1) * dl)
                gc = gates[:, h * N_BRANCH:h * N_BRANCH + 1]
                gs = gates[:, h * N_BRANCH + 1:h * N_BRANCH + 2]
                gw = gates[:, h * N_BRANCH + 2:h * N_BRANCH + 3]
                o_ref[:, h * NSA_HD:(h + 1) * NSA_HD] = gc * o_cmp[r] + gs * o_slc[r] + gw * o_win[r]


def _nsa_sample(P, CMP, cache_kv, cache_win, layer, page_table, DB, DL, past_len, page):
    n_pages = page_table.shape[1]
    w_buf = cache_win.shape[2]
    lk = past_len + DL
    nch = CMP.shape[3]
    n_cmp = lk // CMP_STRIDE - CMP_BLOCK // CMP_STRIDE + 1
    n_slc = -(-lk // SLC_BLOCK)
    n_sel = min(N_SELECT, n_slc)
    nbl = _round_up(n_slc, LANES)
    pg = _pick_tile(n_pages, (32, 16, 8, 4, 2, 1))
    n_steps = n_pages // pg
    assert DL % SUBLANES == 0 and DL <= LANES and page == LANES and past_len == n_pages * page
    cov = jnp.asarray(_cover_matrix(n_cmp, n_slc, nch, nbl))
    kvw = NSA_KV_HEADS * NSA_HD
    page_spec = lambda i: pl.BlockSpec((None, None, page, 2, NSA_KV_HEADS, NSA_HD),
                                       lambda b, s, pt: (layer, pt[b, s * pg + i], 0, 1, 0, 0))
    in_specs = [
        pl.BlockSpec((DL, NSA_HEADS * NSA_HD), lambda b, s, pt: (b, 3)),
        pl.BlockSpec((DL, 2 * kvw), lambda b, s, pt: (b, 9)),
        pl.BlockSpec((DL, 2 * kvw), lambda b, s, pt: (b, 10)),
        pl.BlockSpec((DL, LANES), lambda b, s, pt: (b, 44)),
        pl.BlockSpec((None, None, NSA_KV_HEADS, nch, NSA_HD), lambda b, s, pt: (b, 0, 0, 0, 0)),
        pl.BlockSpec((None, None, NSA_KV_HEADS, nch, NSA_HD), lambda b, s, pt: (b, 1, 0, 0, 0)),
        pl.BlockSpec((nch, nbl), lambda b, s, pt: (0, 0)),
        pl.BlockSpec((None, None, w_buf, 2, NSA_KV_HEADS, NSA_HD), lambda b, s, pt: (layer, b, 0, 0, 0, 0)),
    ] + [page_spec(i) for i in range(pg)]
    rows = NSA_HEADS * DL
    grid_spec = pltpu.PrefetchScalarGridSpec(
        num_scalar_prefetch=1,
        grid=(DB, n_steps),
        in_specs=in_specs,
        out_specs=pl.BlockSpec((DL, NSA_HEADS * NSA_HD), lambda b, s, pt: (b, 0)),
        scratch_shapes=[
            pltpu.VMEM((rows, NSA_HD), BF16),
            pltpu.VMEM((rows, nbl), F32),
            pltpu.VMEM((rows, 1), F32),
            pltpu.VMEM((rows, 1), F32),
            pltpu.VMEM((rows, NSA_HD), F32),
            pltpu.VMEM((rows, NSA_HD), F32),
        ],
    )
    body = functools.partial(_nsa_sample_body, pg=pg, n_steps=n_steps, dl=DL, q_off=past_len, w_buf=w_buf,
                             n_cmp=n_cmp, n_slc=n_slc, n_sel=n_sel, nbl=nbl, page=page)
    return pl.pallas_call(
        body, grid_spec=grid_spec,
        out_shape=jax.ShapeDtypeStruct((DB * DL, NSA_HEADS * NSA_HD), F32),
        compiler_params=_cparams(("parallel", "arbitrary")),
        name="nsa_sample",
    )(page_table, P, P, P, P, CMP, CMP, cov, cache_win, *([cache_kv] * pg))


def _out_ln_body(h_ref, ro_ref, no_ref, wr_ref, wn_ref, g_ref, b_ref, o_ref, *, alpha):
    tm = h_ref.shape[0]
    half = tm // 2 if tm % (2 * BF16_ROWS) == 0 else tm
    for r0 in range(0, tm, half):
        rows = slice(r0, r0 + half)
        m = _dot(ro_ref[rows, :].astype(BF16), wr_ref[...]) + _dot(no_ref[rows, :].astype(BF16), wn_ref[...])
        o_ref[rows, :] = _layer_norm(alpha * h_ref[rows, :] + m, g_ref[...], b_ref[...])


def _out_ln(h, ro, no, w_out, g, b, alpha):
    T, D = h.shape
    kr = ro.shape[1]
    kn = no.shape[1]
    tm = _pick_tile(T, (512, 256, 128, 64, 32, 16, 8))
    return pl.pallas_call(
        functools.partial(_out_ln_body, alpha=alpha),
        grid=(T // tm,),
        in_specs=[
            pl.BlockSpec((tm, D), lambda i: (i, 0)),
            pl.BlockSpec((tm, kr), lambda i: (i, 0)),
            pl.BlockSpec((tm, kn), lambda i: (i, 0)),
            pl.BlockSpec((kr, D), lambda i: (0, 0)),
            pl.BlockSpec((kn, D), lambda i: (1, 0)),
            pl.BlockSpec((1, D), lambda i: (0, 0)),
            pl.BlockSpec((1, D), lambda i: (0, 0)),
        ],
        out_specs=pl.BlockSpec((tm, D), lambda i: (i, 0)),
        out_shape=jax.ShapeDtypeStruct((T, D), F32),
        compiler_params=_cparams(("parallel",)),
        name="out_ln",
    )(h, ro, no, w_out, w_out, g.reshape(1, D), b.reshape(1, D))


def _rope_tables(pos):
    half = NSA_HD // 2
    inv = ROPE_THETA ** (-jnp.arange(half, dtype=F32) / half)
    ang = pos.astype(F32)[:, None] * inv[None, :]
    cos = jnp.cos(ang)
    sin = jnp.sin(ang)
    return jnp.concatenate([cos, cos], -1), jnp.concatenate([-sin, sin], -1)


def _cmp_weights(w1, pos, w2):
    r = CMP_BLOCK // CMP_STRIDE
    w1r = w1.reshape(r, CMP_STRIDE * NSA_HD, CMP_HIDDEN)
    w1c = jnp.concatenate([w1r[i] for i in range(r)], axis=1).astype(BF16)
    posr = jnp.pad(pos.reshape(r, CMP_STRIDE * NSA_HD), ((0, SUBLANES - r), (0, 0))).astype(BF16)
    return w1c, posr, w2.astype(BF16)


def _layer_view(arr, l, shape):
    return arr.reshape(shape) if arr.shape[0] == 1 else arr[l].reshape(shape)


def _decoder_layer(x, B, L, q_off, s0, p, sample_ctx):
    alpha = p['alpha']
    h1, h1_bf = _ffn_ln(x, p['ffn1_w_up'], p['ffn1_w_down'], p['ln1_g'], p['ln1_b'], alpha, True)
    cos, sin = _rope_tables(q_off + jnp.arange(L, dtype=jnp.int32))
    P, kv_rows, win_rows = _proj(h1_bf, p['w_in'], p['rope_cols'], p['scale_cols'], cos, sin, L)
    ro, ret_s = _retention(P, s0, p['ret_gn_g'], p['ret_gn_b'], B, L)
    if sample_ctx is None:
        assert L % CMP_STRIDE == 0
        n_cmp = L // CMP_STRIDE - CMP_BLOCK // CMP_STRIDE + 1
        n_vec = 2 * NSA_KV_HEADS
        specs = [pl.BlockSpec((L, NSA_HD), (lambda v: (lambda b, g: (b, KV_CHUNK0 + v)))(v)) for v in range(n_vec)]
        CMP = _compress([P] * n_vec, specs, 1, L, 1, n_cmp, B, p['cmp_w1'], p['cmp_pos'], p['cmp_w2'])
        no = _nsa_prompt(P, CMP, B, L)
    else:
        cache_kv, cache_win, layer, page_table, past_len, page = sample_ctx
        n_pages = page_table.shape[1]
        lk = past_len + L
        assert (lk // CMP_STRIDE) * CMP_STRIDE <= past_len, "compression blocks must lie in the paged past"
        n_cmp = lk // CMP_STRIDE - CMP_BLOCK // CMP_STRIDE + 1
        n_in = _pick_tile(n_pages, (32, 16, 8, 4, 2, 1))
        n_grp = n_pages // n_in
        specs = [pl.BlockSpec((None, None, page, 2, NSA_KV_HEADS, NSA_HD),
                              (lambda i: (lambda b, g, pt: (layer, pt[b, (n_grp - 1 - g) * n_in + i], 0, 0, 0, 0)))(i))
                 for i in range(n_in)]
        CMP = _compress([cache_kv] * n_in, specs, n_in, page, n_grp, n_cmp, B,
                        p['cmp_w1'], p['cmp_pos'], p['cmp_w2'], page_table=page_table)
        no = _nsa_sample(P, CMP, cache_kv, cache_win, layer, page_table, B, L, past_len, page)
    x2 = _out_ln(h1, ro, no, p['w_out'], p['ln2_g'], p['ln2_b'], alpha)
    y, _ = _ffn_ln(x2, p['ffn2_w_up'], p['ffn2_w_down'], p['ln3_g'], p['ln3_b'], alpha, False)
    return y, ret_s, kv_rows, win_rows


def kernel(x_prompt, x_sample, state_ret, cache_nsa_kv, cache_win, page_table, ffn1_w_up, ffn1_w_down, ln1_g, ln1_b, w_in, w_out, ret_gn_g, ret_gn_b, cmp_pos_k, cmp_w1_k, cmp_w2_k, cmp_pos_v, cmp_w1_v, cmp_w2_v, ln2_g, ln2_b, ffn2_w_up, ffn2_w_down, ln3_g, ln3_b):
    B, L, D = x_prompt.shape
    DB, DL, _ = x_sample.shape
    depth = w_in.shape[0]
    n_pool, page = cache_nsa_kv.shape[1], cache_nsa_kv.shape[2]
    n_pages = page_table.shape[1]
    past_len = n_pages * page
    w_buf = cache_win.shape[2]
    alpha = (2.0 * depth) ** 0.25
    rope_np = np.zeros((N_IN_PAD // LANES, LANES), np.float32)
    rope_np[list(ROPE_CHUNKS)] = 1.0
    scale_np = np.ones((N_IN_PAD // LANES, LANES), np.float32)
    scale_np[list(KSCALE_CHUNKS)] = RET_DK ** -0.5
    rope_cols = jnp.asarray(rope_np.reshape(1, N_IN_PAD))
    scale_cols = jnp.asarray(scale_np.reshape(1, N_IN_PAD))

    yp = x_prompt.reshape(B * L, D)
    ys = x_sample.reshape(DB * DL, D)
    outs = [[] for _ in range(6)]
    for l in range(depth):
        k1, p1, k2 = _cmp_weights(cmp_w1_k[l], cmp_pos_k[l], cmp_w2_k[l])
        v1, q1, v2 = _cmp_weights(cmp_w1_v[l], cmp_pos_v[l], cmp_w2_v[l])
        p = {
            'alpha': alpha, 'rope_cols': rope_cols, 'scale_cols': scale_cols,
            'ffn1_w_up': ffn1_w_up[l].astype(BF16), 'ffn1_w_down': ffn1_w_down[l].astype(BF16),
            'ln1_g': ln1_g[l], 'ln1_b': ln1_b[l],
            'w_in': jnp.pad(w_in[l], ((0, 0), (0, N_IN_PAD - N_IN))).astype(BF16),
            'w_out': w_out[l].astype(BF16),
            'ret_gn_g': ret_gn_g[l], 'ret_gn_b': ret_gn_b[l],
            'cmp_w1': jnp.stack([k1, v1]), 'cmp_pos': jnp.stack([p1, q1]), 'cmp_w2': jnp.stack([k2, v2]),
            'ln2_g': ln2_g[l], 'ln2_b': ln2_b[l],
            'ffn2_w_up': ffn2_w_up[l].astype(BF16), 'ffn2_w_down': ffn2_w_down[l].astype(BF16),
            'ln3_g': ln3_g[l], 'ln3_b': ln3_b[l],
        }
        s0 = jnp.zeros((B, RET_HEADS, RET_DK, RET_DV), F32)
        yp, rs_p, kv_p, win_p = _decoder_layer(yp, B, L, 0, s0, p, None)
        ctx = (cache_nsa_kv, cache_win, l, page_table, past_len, page)
        ys, rs_s, kv_s, win_s = _decoder_layer(ys, DB, DL, past_len,
                                               _layer_view(state_ret, l, state_ret.shape[1:]), p, ctx)
        wl = min(WINDOW, L)
        outs[0].append(rs_p)
        outs[1].append(rs_s)
        outs[2].append(kv_p.reshape(B, L, 4, NSA_KV_HEADS, NSA_HD))
        outs[3].append(kv_s.reshape(DB, DL, 4, NSA_KV_HEADS, NSA_HD))
        outs[4].append(win_p.reshape(B, L, 2, NSA_KV_HEADS, NSA_HD)[:, L - wl:])
        win_s = win_s.reshape(DB, DL, 2, NSA_KV_HEADS, NSA_HD)
        outs[5].append(jnp.concatenate([cache_win[l], win_s], axis=1)[:, -w_buf:])
    return (yp.reshape(B, L, D), ys.reshape(DB, DL, D), jnp.stack(outs[0]), jnp.stack(outs[1]),
            jnp.stack(outs[2]), jnp.stack(outs[3]), jnp.stack(outs[4]), jnp.stack(outs[5]))
```

```python
import functools

import numpy as np
import jax
import jax.numpy as jnp
from jax import lax
from jax.experimental import pallas as pl
from jax.experimental.pallas import tpu as pltpu

F32 = jnp.float32
BF16 = jnp.bfloat16

LANES = 128
SUBLANES = 8
BF16_ROWS = 16
VMEM_LIMIT_BYTES = 56 * 1024 * 1024

RET_HEADS = 4
RET_DK = 128
RET_DV = 256
RET_CHUNK = 128
NSA_HEADS = 8
NSA_KV_HEADS = 2
NSA_HD = 128
NSA_GROUP = NSA_HEADS // NSA_KV_HEADS
CMP_BLOCK = 32
CMP_STRIDE = 16
CMP_HIDDEN = 2 * NSA_HD
SLC_BLOCK = 64
SLC_SHIFT = 6
N_SELECT = 16
WINDOW = 512
N_BRANCH = 3
ROPE_THETA = 10000.0
LN_EPS = 1e-5
NEG_BIG = -1e30
LOG2_E = 1.4426950408889634

N_IN = 5656
N_IN_PAD = 5760
KV_CHUNK0 = 32
WIN_CHUNK0 = 40
GATE_CHUNK = 44
ROPE_CHUNKS = tuple(range(0, 8)) + tuple(range(24, 32)) + (32, 33, 36, 37, 40, 41)
KSCALE_CHUNKS = tuple(range(4, 8))


def _cparams(sem):
    return pltpu.CompilerParams(dimension_semantics=sem, vmem_limit_bytes=VMEM_LIMIT_BYTES)


def _pick_tile(n, candidates):
    for c in candidates:
        if n % c == 0:
            return c
    return n


def _round_up(n, m):
    return (n + m - 1) // m * m


def _layer_norm(z, g, b):
    mu = jnp.mean(z, axis=-1, keepdims=True)
    zc = z - mu
    var = jnp.mean(zc * zc, axis=-1, keepdims=True)
    return zc * lax.rsqrt(var + LN_EPS) * g + b


def _dot(a, b):
    return jnp.dot(a, b, preferred_element_type=F32)


def _dot_nt(a, b):
    return lax.dot_general(a, b, (((1,), (1,)), ((), ())), preferred_element_type=F32)


def _ffn_ln_body(x_ref, wa_ref, wb_ref, wd_ref, g_ref, b_ref, *rest, nj, alpha, emit_bf16):
    if emit_bf16:
        o_ref, obf_ref, xbf_sc, acc_sc = rest
    else:
        o_ref, xbf_sc, acc_sc = rest
    j = pl.program_id(1)

    @pl.when(j == 0)
    def _init():
        xbf_sc[...] = x_ref[...].astype(BF16)
        acc_sc[...] = jnp.zeros_like(acc_sc)

    xb = xbf_sc[...]
    a = _dot(xb, wa_ref[...])
    b = _dot(xb, wb_ref[...])
    h = (a * jax.nn.sigmoid(a)) * b
    acc_sc[...] += _dot(h.astype(BF16), wd_ref[...])

    @pl.when(j == nj - 1)
    def _finish():
        z = alpha * x_ref[...] + 0.5 * acc_sc[...]
        y = _layer_norm(z, g_ref[...], b_ref[...])
        o_ref[...] = y
        if emit_bf16:
            obf_ref[...] = y.astype(BF16)


def _ffn_ln(x, w_up, w_down, g, b, alpha, emit_bf16):
    T, D = x.shape
    F = w_down.shape[0]
    tm = _pick_tile(T, (512, 256, 128, 64, 32, 16, 8))
    tf = _pick_tile(F, (512, 256, 128))
    nj = F // tf
    out_shape = [jax.ShapeDtypeStruct((T, D), F32)]
    out_specs = [pl.BlockSpec((tm, D), lambda i, j: (i, 0))]
    if emit_bf16:
        out_shape.append(jax.ShapeDtypeStruct((T, D), BF16))
        out_specs.append(pl.BlockSpec((tm, D), lambda i, j: (i, 0)))
    res = pl.pallas_call(
        functools.partial(_ffn_ln_body, nj=nj, alpha=alpha, emit_bf16=emit_bf16),
        grid=(T // tm, nj),
        in_specs=[
            pl.BlockSpec((tm, D), lambda i, j: (i, 0)),
            pl.BlockSpec((D, tf), lambda i, j: (0, j)),
            pl.BlockSpec((D, tf), lambda i, j: (0, nj + j)),
            pl.BlockSpec((tf, D), lambda i, j: (j, 0)),
            pl.BlockSpec((1, D), lambda i, j: (0, 0)),
            pl.BlockSpec((1, D), lambda i, j: (0, 0)),
        ],
        out_specs=out_specs,
        out_shape=out_shape,
        scratch_shapes=[pltpu.VMEM((tm, D), BF16), pltpu.VMEM((tm, D), F32)],
        compiler_params=_cparams(("parallel", "arbitrary")),
        name="ffn_ln",
    )(x, w_up, w_up, w_down, g.reshape(1, D), b.reshape(1, D))
    return res if emit_bf16 else (res[0], None)


def _rows_view(ref, j):
    rows = ref.shape[0]
    n = int(np.prod(ref.shape[1:-1]))
    return ref.reshape(n * rows, ref.shape[-1]), pl.ds(j, rows, stride=n)


def _head_rows(ref, j):
    view, idx = _rows_view(ref, j)
    return view[idx, :]


def _store_rows(ref, j, val):
    view, idx = _rows_view(ref, j)
    view[idx, :] = val


def _proj_body(x_ref, w_ref, cos_ref, sin_ref, rope_ref, scale_ref, o_ref, kv_ref, win_ref, *, n_chunk, j_rows):
    y = _dot(x_ref[...], w_ref[...])
    cos = cos_ref[...]
    sin = sin_ref[...]
    for c in range(n_chunk):
        sl = slice(c * LANES, (c + 1) * LANES)
        yc = y[:, sl]
        roped = yc * cos + pltpu.roll(yc, NSA_HD // 2, 1) * sin
        o_ref[:, sl] = jnp.where(rope_ref[:, sl] > 0.5, roped, yc) * scale_ref[:, sl]

    @pl.when(pl.program_id(1) == j_rows)
    def _emit_cache_rows():
        c0 = KV_CHUNK0 - j_rows * n_chunk
        for r in range(WIN_CHUNK0 - KV_CHUNK0):
            _store_rows(kv_ref, r, o_ref[:, (c0 + r) * LANES:(c0 + r + 1) * LANES])
        c0 = WIN_CHUNK0 - j_rows * n_chunk
        for r in range(GATE_CHUNK - WIN_CHUNK0):
            _store_rows(win_ref, r, o_ref[:, (c0 + r) * LANES:(c0 + r + 1) * LANES])


def _proj(x_bf, w_bf, rope_cols, scale_cols, cos, sin, rows_per_seq):
    T, D = x_bf.shape
    N = w_bf.shape[1]
    tm = _pick_tile(T, (512, 256, 128, 64, 32, 16, 8))
    tn = 1920
    assert N % tn == 0
    n_chunk = tn // LANES
    if rows_per_seq >= tm:
        assert rows_per_seq % tm == 0
        n_tab = rows_per_seq // tm
    else:
        assert tm % rows_per_seq == 0
        cos = jnp.tile(cos, (tm // rows_per_seq, 1))
        sin = jnp.tile(sin, (tm // rows_per_seq, 1))
        n_tab = 1
    j_rows = KV_CHUNK0 // n_chunk
    assert (GATE_CHUNK - 1) // n_chunk == j_rows, "cache-row columns must sit in one column tile"
    kv_shape = (T, 4, NSA_KV_HEADS, NSA_HD)
    win_shape = (T, 2, NSA_KV_HEADS, NSA_HD)
    return pl.pallas_call(
        functools.partial(_proj_body, n_chunk=n_chunk, j_rows=j_rows),
        grid=(T // tm, N // tn),
        in_specs=[
            pl.BlockSpec((tm, D), lambda i, j: (i, 0)),
            pl.BlockSpec((D, tn), lambda i, j: (0, j)),
            pl.BlockSpec((tm, LANES), lambda i, j: (i % n_tab, 0)),
            pl.BlockSpec((tm, LANES), lambda i, j: (i % n_tab, 0)),
            pl.BlockSpec((1, tn), lambda i, j: (0, j)),
            pl.BlockSpec((1, tn), lambda i, j: (0, j)),
        ],
        out_specs=[
            pl.BlockSpec((tm, tn), lambda i, j: (i, j)),
            pl.BlockSpec((tm,) + kv_shape[1:], lambda i, j: (i, 0, 0, 0)),
            pl.BlockSpec((tm,) + win_shape[1:], lambda i, j: (i, 0, 0, 0)),
        ],
        out_shape=[
            jax.ShapeDtypeStruct((T, N), F32),
            jax.ShapeDtypeStruct(kv_shape, F32),
            jax.ShapeDtypeStruct(win_shape, F32),
        ],
        compiler_params=_cparams(("parallel", "arbitrary")),
        name="proj_rope",
    )(x_bf, w_bf, cos, sin, rope_cols, scale_cols)


def _ret_body(q_ref, k_ref, v_ref, g_ref, s0_ref, dm_ref, ind_ref, std_ref, cd_ref, gng_ref, gnb_ref,
              o_ref, sout_ref, s_sc, *, nc, rows, rows_pad, per_step):
    c = pl.program_id(1)

    @pl.when(c == 0)
    def _load_state():
        s_sc[...] = s0_ref[...]

    pad = rows_pad - rows
    for sub in range(per_step):
        rs = slice(sub * rows, (sub + 1) * rows)
        for h in range(RET_HEADS):
            ks = slice(h * RET_DK, (h + 1) * RET_DK)
            vs = slice(h * RET_DV, (h + 1) * RET_DV)
            q = q_ref[rs, ks]
            k = k_ref[rs, ks]
            v = v_ref[rs, vs]
            kd = k * std_ref[h]
            if pad:
                k = jnp.concatenate([k, jnp.zeros((pad, RET_DK), F32)], axis=0)
                kd = jnp.concatenate([kd, jnp.zeros((pad, RET_DK), F32)], axis=0)
                v = jnp.concatenate([v, jnp.zeros((pad, RET_DV), F32)], axis=0)
            s_old = s_sc[h]
            vb = v.astype(BF16)
            a = _dot_nt(q.astype(BF16), k.astype(BF16)) * dm_ref[h]
            o = _dot(a.astype(BF16), vb) + _dot((q * ind_ref[h]).astype(BF16), s_old.astype(BF16))
            s_sc[h] = s_old * cd_ref[h] + _dot(kd.T.astype(BF16), vb)
            mu = jnp.mean(o, axis=-1, keepdims=True)
            oc = o - mu
            var = jnp.mean(oc * oc, axis=-1, keepdims=True)
            on = oc * lax.rsqrt(var + LN_EPS) * gng_ref[:, vs] + gnb_ref[:, vs]
            gate = g_ref[rs, vs]
            o_ref[rs, vs] = (gate * jax.nn.sigmoid(gate)) * on

    @pl.when(c == nc - 1)
    def _store_state():
        sout_ref[...] = s_sc[...]


def _retention(P, s0, gn_g, gn_b, B, L):
    C = RET_CHUNK if L % RET_CHUNK == 0 else L
    nc = L // C
    CP = max(C, LANES)
    lg = jnp.log1p(-jnp.exp2(-5.0 - jnp.arange(RET_HEADS, dtype=F32)))
    i = jnp.arange(C, dtype=F32)
    diff = i[:, None] - i[None, :]
    dmask = jnp.where(diff >= 0, jnp.exp(lg[:, None, None] * jnp.maximum(diff, 0.0)), 0.0)
    dmask = jnp.pad(dmask, ((0, 0), (0, 0), (0, CP - C)))
    in_decay = jnp.broadcast_to(jnp.exp(lg[:, None] * (i + 1.0))[:, :, None], (RET_HEADS, C, RET_DK))
    st_decay = jnp.broadcast_to(jnp.exp(lg[:, None] * (C - 1.0 - i))[:, :, None], (RET_HEADS, C, RET_DK))
    chunk_decay = jnp.broadcast_to(jnp.exp(lg * C)[:, None, None], (RET_HEADS, 1, RET_DV))
    qw = RET_HEADS * RET_DK
    vw = RET_HEADS * RET_DV
    const3 = lambda b, c: (0, 0, 0)
    per_step = _pick_tile(nc, (4, 2, 1))
    nc = nc // per_step
    CS = C * per_step
    return pl.pallas_call(
        functools.partial(_ret_body, nc=nc, rows=C, rows_pad=CP, per_step=per_step),
        grid=(B, nc),
        in_specs=[
            pl.BlockSpec((CS, qw), lambda b, c: (b * nc + c, 0)),
            pl.BlockSpec((CS, qw), lambda b, c: (b * nc + c, 1)),
            pl.BlockSpec((CS, vw), lambda b, c: (b * nc + c, 1)),
            pl.BlockSpec((CS, vw), lambda b, c: (b * nc + c, 2)),
            pl.BlockSpec((None, RET_HEADS, RET_DK, RET_DV), lambda b, c: (b, 0, 0, 0)),
            pl.BlockSpec((RET_HEADS, C, CP), const3),
            pl.BlockSpec((RET_HEADS, C, RET_DK), const3),
            pl.BlockSpec((RET_HEADS, C, RET_DK), const3),
            pl.BlockSpec((RET_HEADS, 1, RET_DV), const3),
            pl.BlockSpec((1, vw), lambda b, c: (0, 0)),
            pl.BlockSpec((1, vw), lambda b, c: (0, 0)),
        ],
        out_specs=[
            pl.BlockSpec((CS, vw), lambda b, c: (b * nc + c, 0)),
            pl.BlockSpec((None, RET_HEADS, RET_DK, RET_DV), lambda b, c: (b, 0, 0, 0)),
        ],
        out_shape=[
            jax.ShapeDtypeStruct((B * L, vw), F32),
            jax.ShapeDtypeStruct((B, RET_HEADS, RET_DK, RET_DV), F32),
        ],
        scratch_shapes=[pltpu.VMEM((RET_HEADS, RET_DK, RET_DV), F32)],
        compiler_params=_cparams(("parallel", "arbitrary")),
        name="retention",
    )(P, P, P, P, s0, dmask, in_decay, st_decay, chunk_decay, gn_g.reshape(1, vw), gn_b.reshape(1, vw))


def _cmp_body(*refs, n_in, rows, n_grp, n_cmp, paged):
    n_vec = 2 * NSA_KV_HEADS
    if paged:
        refs = refs[1:]
    n_src = n_in if paged else n_vec * n_in
    x_refs = refs[:n_src]
    w1_ref, pos_ref, w2_ref, o_ref, carry_sc, xc_sc = refs[n_src:]
    g = pl.program_id(1)
    grp = n_grp - 1 - g
    cpi = rows // CMP_STRIDE
    M = n_in * cpi

    @pl.when(g == 0)
    def _init():
        carry_sc[...] = jnp.zeros_like(carry_sc)

    row = lax.broadcasted_iota(jnp.int32, (M, 1), 0)
    grp_in = 1 if cpi % BF16_ROWS == 0 else BF16_ROWS // cpi
    assert n_in % grp_in == 0 and (grp_in * cpi) % BF16_ROWS == 0
    for t in range(2):
        w1 = w1_ref[t]
        gp = _dot(pos_ref[t], w1)
        posterm = gp[0:1, :CMP_HIDDEN] + gp[1:2, CMP_HIDDEN:]
        for hd in range(NSA_KV_HEADS):
            v = t * NSA_KV_HEADS + hd
            for i0 in range(0, n_in, grp_in):
                if paged:
                    parts = [pltpu.einshape("csd->scd",
                                            _head_rows(x_refs[i], v).reshape(cpi, CMP_STRIDE, NSA_HD))
                             for i in range(i0, i0 + grp_in)]
                    piece = lambda s: jnp.concatenate([xt[s] for xt in parts], axis=0)
                else:
                    refs_i = [x_refs[v * n_in + i] for i in range(i0, i0 + grp_in)]
                    piece = lambda s: jnp.concatenate(
                        [r[pl.ds(s, cpi, stride=CMP_STRIDE), :] for r in refs_i], axis=0)
                r0 = v * M + i0 * cpi
                for s in range(CMP_STRIDE):
                    xc_sc[r0:r0 + grp_in * cpi, s * NSA_HD:(s + 1) * NSA_HD] = piece(s).astype(BF16)
        t0 = t * NSA_KV_HEADS * M
        gg_all = _dot(xc_sc[t0:t0 + NSA_KV_HEADS * M, :], w1)
        for hd in range(NSA_KV_HEADS):
            v = t * NSA_KV_HEADS + hd
            gg = gg_all[hd * M:(hd + 1) * M]
            g0 = gg[:, :CMP_HIDDEN]
            g1 = gg[:, CMP_HIDDEN:]
            nxt = pltpu.roll(g1, M - 1, 0)
            nxt = jnp.where(row == M - 1, carry_sc[v][0:1, :], nxt)
            carry_sc[v] = g1[0:SUBLANES, :]
            hid = g0 + nxt + posterm
            out = _dot(jax.nn.gelu(hid).astype(BF16), w2_ref[t])
            o_ref[t, hd] = jnp.where(grp * M + row < n_cmp, out, 0.0)


def _compress(srcs, src_specs, n_in, rows, n_grp, n_cmp, B, w1, pos, w2, page_table=None):
    n_vec = 2 * NSA_KV_HEADS
    M = n_in * rows // CMP_STRIDE
    nch = n_grp * M
    paged = page_table is not None
    const3 = lambda *a: (0, 0, 0)
    in_specs = list(src_specs) + [
        pl.BlockSpec((2, CMP_STRIDE * NSA_HD, 2 * CMP_HIDDEN), const3),
        pl.BlockSpec((2, SUBLANES, CMP_STRIDE * NSA_HD), const3),
        pl.BlockSpec((2, CMP_HIDDEN, NSA_HD), const3),
    ]
    out_spec = pl.BlockSpec((None, 2, NSA_KV_HEADS, M, NSA_HD), lambda *a: (a[0], 0, 0, n_grp - 1 - a[1], 0))
    body = functools.partial(_cmp_body, n_in=n_in, rows=rows, n_grp=n_grp, n_cmp=n_cmp, paged=paged)
    out_shape = jax.ShapeDtypeStruct((B, 2, NSA_KV_HEADS, nch, NSA_HD), F32)
    scratch = [pltpu.VMEM((n_vec, SUBLANES, CMP_HIDDEN), F32),
               pltpu.VMEM((n_vec * M, CMP_STRIDE * NSA_HD), BF16)]
    sem = ("parallel", "arbitrary")
    if paged:
        grid_spec = pltpu.PrefetchScalarGridSpec(
            num_scalar_prefetch=1, grid=(B, n_grp), in_specs=in_specs, out_specs=out_spec,
            scratch_shapes=scratch)
        return pl.pallas_call(body, grid_spec=grid_spec, out_shape=out_shape,
                              compiler_params=_cparams(sem), name="nsa_compress_paged")(
            page_table, *srcs, w1, pos, w2)
    return pl.pallas_call(body, grid=(B, n_grp), in_specs=in_specs, out_specs=out_spec,
                          out_shape=out_shape, scratch_shapes=scratch,
                          compiler_params=_cparams(sem), name="nsa_compress")(*srcs, w1, pos, w2)


def _cover_matrix(n_cmp, n_slc, rows, cols):
    c_i = np.arange(n_cmp)[:, None]
    n_i = np.arange(n_slc)[None, :]
    cov = np.clip(np.minimum(c_i * CMP_STRIDE + CMP_BLOCK, (n_i + 1) * SLC_BLOCK)
                  - np.maximum(c_i * CMP_STRIDE, n_i * SLC_BLOCK), 0, None).astype(np.float32) / CMP_BLOCK
    out = np.zeros((rows, cols), np.float32)
    out[:n_cmp, :n_slc] = cov
    return out


def _nsa_prompt_body(q_ref, kc_ref, vc_ref, ks_ref, vs_ref, kw_ref, vw_ref, gate_ref, covt_ref, o_ref,
                     kcb_sc, vct_sc, ksb_sc, vst_sc, kwb_sc, vwt_sc, prio_sc, sel_sc, gt_sc,
                     m_sc, l_sc, acc_sc, out_sc, s_sc, *, tq, L, n_cmp, n_slc, n_sel, nbp):
    KV = NSA_KV_HEADS
    qi = pl.program_id(1)
    tk = tq
    nch = kc_ref.shape[1]
    scale = NSA_HD ** -0.5

    @pl.when(qi == 0)
    def _stage_kv():
        for kv in range(KV):
            cs = slice(kv * NSA_HD, (kv + 1) * NSA_HD)
            kcb_sc[kv] = kc_ref[kv].astype(BF16)
            vct_sc[kv] = vc_ref[kv].T.astype(BF16)
            ksb_sc[kv] = ks_ref[:, cs].astype(BF16)
            kwb_sc[kv] = kw_ref[:, cs].astype(BF16)
            for i in range(L // tk):
                vst_sc[kv, i] = vs_ref[i * tk:(i + 1) * tk, cs].T.astype(BF16)
                vwt_sc[kv, i] = vw_ref[i * tk:(i + 1) * tk, cs].T.astype(BF16)

    G = NSA_GROUP
    W = G * tq
    t0 = qi * tq
    t_row = t0 + lax.broadcasted_iota(jnp.int32, (1, tq), 1)
    lane = lax.broadcasted_iota(jnp.int32, (1, W), 1)
    t_all = t0 + (lane & (tq - 1))
    qcats = []
    for kv in range(KV):
        qc = jnp.concatenate([q_ref[:, (kv * G + g) * NSA_HD:(kv * G + g + 1) * NSA_HD] for g in range(G)], axis=0)
        qcats.append((qc * (scale * LOG2_E)).astype(BF16))
    gt_sc[...] = jax.nn.sigmoid(gate_ref[...]).T

    def gate_row(kv, br):
        return jnp.concatenate(
            [gt_sc[(kv * G + g) * N_BRANCH + br:(kv * G + g) * N_BRANCH + br + 1, :] for g in range(G)], axis=1)

    kp_l = lax.broadcasted_iota(jnp.int32, (tk, 1), 0)
    t_l = lane & (tq - 1)
    causal_bias = jnp.where(kp_l <= t_l, 0.0, NEG_BIG)
    far_bias = jnp.where(kp_l > t_l, 0.0, NEG_BIG)

    def reset(kv):
        m_sc[kv] = jnp.full((1, W), NEG_BIG, F32)
        l_sc[kv] = jnp.zeros((1, W), F32)
        acc_sc[kv] = jnp.zeros((NSA_HD, W), F32)

    def score(kb_sc, kv, kt, bias):
        k0 = pl.multiple_of(kt * tk, tk)
        s = _dot_nt(kb_sc[kv, pl.ds(k0, tk), :], qcats[kv])
        return s if bias is None else s + bias

    def update(vt_sc, kv, kt, s):
        m_old = m_sc[kv]
        m_new = jnp.maximum(m_old, jnp.max(s, axis=0, keepdims=True))
        p = jnp.exp2(s - m_new)
        alpha = jnp.exp2(m_old - m_new)
        l_sc[kv] = alpha * l_sc[kv] + jnp.sum(p, axis=0, keepdims=True)
        acc_sc[kv] = alpha * acc_sc[kv] + _dot(vt_sc[kv, kt], p.astype(BF16))
        m_sc[kv] = m_new

    def branch_out(kv, br):
        return (gate_row(kv, br) / l_sc[kv]) * acc_sc[kv]

    far = WINDOW // tk
    for kv in range(KV):
        reset(kv)
        tiles = []
        for back in range(far, -1, -1):
            bias = far_bias if back == far else (causal_bias if back == 0 else None)
            if back > 0:
                off = jnp.where(qi >= back, 0.0, NEG_BIG).astype(F32)
                bias = off if bias is None else bias + off
            kt = jnp.maximum(qi - back, 0)
            tiles.append((kt, score(kwb_sc, kv, kt, bias)))
        for kt, s_w in tiles:
            update(vwt_sc, kv, kt, s_w)
        out_sc[kv] = branch_out(kv, 2)

    c_i = lax.broadcasted_iota(jnp.int32, (nch, 1), 0)
    c_end = jnp.where(c_i < n_cmp, c_i * CMP_STRIDE + (CMP_BLOCK - 1), L)
    n_i = lax.broadcasted_iota(jnp.int32, (nbp, 1), 0)
    valid = (n_i * SLC_BLOCK <= t_row) & (n_i < n_slc)
    cur = t_row >> SLC_SHIFT
    forced = (n_i == 0) | (n_i == cur) | (n_i == cur - 1)
    for kv in range(KV):
        s = jnp.where(c_end <= t_all, _dot_nt(kcb_sc[kv], qcats[kv]), -jnp.inf)
        m = jnp.max(s, axis=0, keepdims=True)
        m = jnp.where(m > -jnp.inf, m, 0.0)
        e = jnp.exp2(s - m)
        d = jnp.sum(e, axis=0, keepdims=True)
        p = e / jnp.where(d > 0, d, 1.0)
        out_sc[kv] += gate_row(kv, 0) * _dot(vct_sc[kv], p.astype(BF16))
        psum = p[:, 0:tq]
        for g in range(1, G):
            psum = psum + p[:, g * tq:(g + 1) * tq]
        imp = jnp.dot(covt_ref[...], psum, preferred_element_type=F32, precision=lax.Precision.HIGHEST)
        prio = jnp.where(forced, jnp.inf, jnp.where(valid, imp, -jnp.inf))
        prio_sc[kv] = prio
        cnt = jnp.zeros((nbp, tq), jnp.int32)
        for mm in range(n_slc):
            pm = prio_sc[kv, mm:mm + 1, :]
            tie = jnp.where(n_i > mm, 1, 0)
            cnt = cnt + jnp.where(pm > prio, 1, jnp.where(pm == prio, tie, 0))
        selbias = jnp.where((cnt < n_sel) & valid, 0.0, NEG_BIG)
        sel_sc[kv] = jnp.concatenate([selbias] * G, axis=1)

    def sel_bias(kv, kt):
        per_tile = tk // SLC_BLOCK
        rows = [jnp.broadcast_to(sel_sc[kv, pl.ds(kt * per_tile + r, 1), :], (SLC_BLOCK, W))
                for r in range(per_tile)]
        return jnp.concatenate(rows, axis=0)

    for kv in range(KV):
        reset(kv)
        s_sc[kv] = score(ksb_sc, kv, 0, sel_bias(kv, 0))

    def slc_body(kt, carry):
        nxt = [score(ksb_sc, kv, kt + 1, sel_bias(kv, kt + 1)) for kv in range(KV)]
        for kv in range(KV):
            update(vst_sc, kv, kt, s_sc[kv])
        for kv in range(KV):
            s_sc[kv] = nxt[kv]
        return carry

    lax.fori_loop(0, qi, slc_body, 0)
    for kv in range(KV):
        update(vst_sc, kv, qi, s_sc[kv] + causal_bias)
        out_sc[kv] += branch_out(kv, 1)
        for g in range(G):
            h = kv * G + g
            o_ref[:, h * NSA_HD:(h + 1) * NSA_HD] = out_sc[kv, :, g * tq:(g + 1) * tq].T


def _nsa_prompt(P, CMP, B, L):
    tq = _pick_tile(L, (256, 128))
    assert L % tq == 0 and tq % SLC_BLOCK == 0 and WINDOW % tq == 0 and tq & (tq - 1) == 0
    nq = L // tq
    nch = CMP.shape[3]
    n_cmp = L // CMP_STRIDE - CMP_BLOCK // CMP_STRIDE + 1
    n_slc = -(-L // SLC_BLOCK)
    n_sel = min(N_SELECT, n_slc)
    nbp = _round_up(n_slc, SUBLANES)
    covt = jnp.asarray(_cover_matrix(n_cmp, n_slc, nch, nbp).T)
    qw = NSA_HEADS * NSA_HD
    kvw = NSA_KV_HEADS * NSA_HD
    KV = NSA_KV_HEADS
    W = NSA_GROUP * tq
    kv_col = lambda c: (lambda b, i: (b, (KV_CHUNK0 + c) // NSA_KV_HEADS))
    body = functools.partial(_nsa_prompt_body, tq=tq, L=L, n_cmp=n_cmp, n_slc=n_slc, n_sel=n_sel, nbp=nbp)
    return pl.pallas_call(
        body,
        grid=(B, nq),
        in_specs=[
            pl.BlockSpec((tq, qw), lambda b, i: (b * nq + i, 3)),
            pl.BlockSpec((None, None, KV, nch, NSA_HD), lambda b, i: (b, 0, 0, 0, 0)),
            pl.BlockSpec((None, None, KV, nch, NSA_HD), lambda b, i: (b, 1, 0, 0, 0)),
            pl.BlockSpec((L, kvw), kv_col(4)),
            pl.BlockSpec((L, kvw), kv_col(6)),
            pl.BlockSpec((L, kvw), kv_col(8)),
            pl.BlockSpec((L, kvw), kv_col(10)),
            pl.BlockSpec((tq, LANES), lambda b, i: (b * nq + i, GATE_CHUNK)),
            pl.BlockSpec((nbp, nch), lambda b, i: (0, 0)),
        ],
        out_specs=pl.BlockSpec((tq, qw), lambda b, i: (b * nq + i, 0)),
        out_shape=jax.ShapeDtypeStruct((B * L, qw), F32),
        scratch_shapes=[
            pltpu.VMEM((KV, nch, NSA_HD), BF16),
            pltpu.VMEM((KV, NSA_HD, nch), BF16),
            pltpu.VMEM((KV, L, NSA_HD), BF16),
            pltpu.VMEM((KV, L // tq, NSA_HD, tq), BF16),
            pltpu.VMEM((KV, L, NSA_HD), BF16),
            pltpu.VMEM((KV, L // tq, NSA_HD, tq), BF16),
            pltpu.VMEM((KV, nbp, tq), F32),
            pltpu.VMEM((KV, nbp, W), F32),
            pltpu.VMEM((LANES, tq), F32),
            pltpu.VMEM((KV, 1, W), F32),
            pltpu.VMEM((KV, 1, W), F32),
            pltpu.VMEM((KV, NSA_HD, W), F32),
            pltpu.VMEM((KV, NSA_HD, W), F32),
            pltpu.VMEM((KV, tq, W), F32),
        ],
        compiler_params=_cparams(("parallel", "arbitrary")),
        name="nsa_prompt",
    )(P, CMP, CMP, P, P, P, P, P, covt)


def _nsa_sample_body(*refs, pg, n_steps, dl, q_off, w_buf, n_cmp, n_slc, n_sel, nbl, page):
    refs = refs[1:]
    q_ref, kvn_ref, wn_ref, gate_ref, kc_ref, vc_ref, cov_ref, cw_ref = refs[:8]
    pg_refs = refs[8:8 + pg]
    o_ref, qb_sc, sel_sc, m_sc, l_sc, acc_sc, ocmp_sc = refs[8 + pg:]
    step = pl.program_id(1)
    rows_h = NSA_GROUP * dl
    rows = NSA_KV_HEADS * rows_h
    scale = NSA_HD ** -0.5
    nch = kc_ref.shape[1]

    def tok_pos(n):
        r = lax.broadcasted_iota(jnp.int32, (n, 1), 0)
        return q_off + r % dl

    @pl.when(step == 0)
    def _select():
        for k in range(NSA_KV_HEADS):
            for g in range(NSA_GROUP):
                h = k * NSA_GROUP + g
                qb_sc[pl.ds(h * dl, dl), :] = (q_ref[:, h * NSA_HD:(h + 1) * NSA_HD] * scale).astype(BF16)
        t_h = tok_pos(rows_h)
        c_i = lax.broadcasted_iota(jnp.int32, (1, nch), 1)
        cmask = (c_i * CMP_STRIDE + (CMP_BLOCK - 1) <= t_h) & (c_i < n_cmp)
        psums = []
        for k in range(NSA_KV_HEADS):
            qk = qb_sc[pl.ds(k * rows_h, rows_h), :]
            s = jnp.where(cmask, _dot_nt(qk, kc_ref[k].astype(BF16)), -jnp.inf)
            m = jnp.max(s, axis=1, keepdims=True)
            m = jnp.where(m > -jnp.inf, m, 0.0)
            e = jnp.exp(s - m)
            d = jnp.sum(e, axis=1, keepdims=True)
            p = e / jnp.where(d > 0, d, 1.0)
            ocmp_sc[pl.ds(k * rows_h, rows_h), :] = _dot(p.astype(BF16), vc_ref[k].astype(BF16))
            psum = p[0:dl]
            for g in range(1, NSA_GROUP):
                psum = psum + p[g * dl:(g + 1) * dl]
            psums.append(psum)
        imp = jnp.dot(jnp.concatenate(psums, axis=0), cov_ref[...], preferred_element_type=F32,
                      precision=lax.Precision.HIGHEST)
        t_s = tok_pos(NSA_KV_HEADS * dl)
        n_i = lax.broadcasted_iota(jnp.int32, (1, nbl), 1)
        valid = (n_i * SLC_BLOCK <= t_s) & (n_i < n_slc)
        cur = t_s >> SLC_SHIFT
        forced = (n_i == 0) | (n_i == cur) | (n_i == cur - 1)
        prio = jnp.where(forced, jnp.inf, jnp.where(valid, imp, -jnp.inf))
        n_f = n_i.astype(F32)
        alive = jnp.broadcast_to(jnp.where(n_i < n_slc, 1.0, 0.0), prio.shape)
        sel = jnp.zeros(prio.shape, F32)
        for _ in range(n_sel):
            mx = jnp.max(jnp.where(alive > 0.5, prio, -jnp.inf), axis=1, keepdims=True)
            cand = (alive > 0.5) & (prio == mx)
            first = jnp.min(jnp.where(cand, n_f, float(nbl)), axis=1, keepdims=True)
            pick = n_f == first
            sel = jnp.where(pick, 1.0, sel)
            alive = jnp.where(pick, 0.0, alive)
        sel = jnp.where(valid, sel, 0.0)
        for k in range(NSA_KV_HEADS):
            for g in range(NSA_GROUP):
                sel_sc[pl.ds((k * NSA_GROUP + g) * dl, dl), :] = sel[k * dl:(k + 1) * dl]
        m_sc[...] = jnp.full(m_sc.shape, NEG_BIG, F32)
        l_sc[...] = jnp.zeros_like(l_sc)
        acc_sc[...] = jnp.zeros_like(acc_sc)

    selb = sel_sc[...].astype(BF16)
    n_col = lax.broadcasted_iota(jnp.int32, (nbl, 1), 0)

    def online_update(k, s, mask, v_rows):
        rs = pl.ds(k * rows_h, rows_h)
        m_old = m_sc[rs, :]
        m_new = jnp.maximum(m_old, jnp.max(s, axis=1, keepdims=True))
        p = jnp.where(mask, jnp.exp(s - m_new), 0.0)
        alpha = jnp.exp(m_old - m_new)
        l_sc[rs, :] = alpha * l_sc[rs, :] + jnp.sum(p, axis=1, keepdims=True)
        acc_sc[rs, :] = alpha * acc_sc[rs, :] + _dot(p.astype(BF16), v_rows)
        m_sc[rs, :] = m_new

    t_h = tok_pos(rows_h)

    def selected(kpos):
        blk = jnp.where(n_col == (kpos >> SLC_SHIFT), 1.0, 0.0).astype(BF16)
        return _dot(selb, blk)

    kpos = step * (pg * page) + lax.broadcasted_iota(jnp.int32, (1, pg * page), 1)
    sel_all = selected(kpos)
    for k in range(NSA_KV_HEADS):
        qk = qb_sc[pl.ds(k * rows_h, rows_h), :]
        mask = (sel_all[k * rows_h:(k + 1) * rows_h] > 0.5) & (kpos <= t_h)
        k_rows = jnp.concatenate([_head_rows(r, k).astype(BF16) for r in pg_refs], axis=0)
        v_rows = jnp.concatenate([_head_rows(r, NSA_KV_HEADS + k).astype(BF16) for r in pg_refs], axis=0)
        s = jnp.where(mask, _dot_nt(qk, k_rows), NEG_BIG)
        online_update(k, s, mask, v_rows)

    @pl.when(step == n_steps - 1)
    def _finish():
        padn = LANES - dl
        j_new = lax.broadcasted_iota(jnp.int32, (1, LANES), 1)
        kpos_n = q_off + j_new
        sel_n = selected(kpos_n)
        for k in range(NSA_KV_HEADS):
            kn = jnp.concatenate([kvn_ref[:, k * NSA_HD:(k + 1) * NSA_HD], jnp.zeros((padn, NSA_HD), F32)], axis=0)
            vn = jnp.concatenate([kvn_ref[:, (2 + k) * NSA_HD:(3 + k) * NSA_HD], jnp.zeros((padn, NSA_HD), F32)],
                                 axis=0)
            qk = qb_sc[pl.ds(k * rows_h, rows_h), :]
            mask = (sel_n[k * rows_h:(k + 1) * rows_h] > 0.5) & (kpos_n <= t_h) & (j_new < dl)
            s = jnp.where(mask, _dot_nt(qk, kn.astype(BF16)), NEG_BIG)
            online_update(k, s, mask, vn.astype(BF16))
        j_w = lax.broadcasted_iota(jnp.int32, (1, w_buf + LANES), 1)
        pos_w = q_off - w_buf + j_w
        dlt = t_h - pos_w
        wmask = (j_w < w_buf + dl) & (pos_w >= 0) & (dlt >= 0) & (dlt < WINDOW)
        gates = jax.nn.sigmoid(gate_ref[...])
        for k in range(NSA_KV_HEADS):
            kw = jnp.concatenate([_head_rows(cw_ref, k),
                                  wn_ref[:, k * NSA_HD:(k + 1) * NSA_HD], jnp.zeros((padn, NSA_HD), F32)], axis=0)
            vw = jnp.concatenate([_head_rows(cw_ref, NSA_KV_HEADS + k),
                                  wn_ref[:, (2 + k) * NSA_HD:(3 + k) * NSA_HD], jnp.zeros((padn, NSA_HD), F32)],
                                 axis=0)
            qk = qb_sc[pl.ds(k * rows_h, rows_h), :]
            s = jnp.where(wmask, _dot_nt(qk, kw.astype(BF16)), -jnp.inf)
            m = jnp.max(s, axis=1, keepdims=True)
            m = jnp.where(m > -jnp.inf, m, 0.0)
            e = jnp.exp(s - m)
            d = jnp.sum(e, axis=1, keepdims=True)
            o_win = _dot((e / jnp.where(d > 0, d, 1.0)).astype(BF16), vw.astype(BF16))
            rs = pl.ds(k * rows_h, rows_h)
            o_slc = acc_sc[rs, :] / l_sc[rs, :]
            o_cmp = ocmp_sc[rs, :]
            for g in range(NSA_GROUP):
                h = k * NSA_GROUP + g
                r = slice(g * dl, (g + 1) * dl)
                gc = gates[:, h * N_BRANCH:h * N_BRANCH + 1]
                gs = gates[:, h * N_BRANCH + 1:h * N_BRANCH + 2]
                gw = gates[:, h * N_BRANCH + 2:h * N_BRANCH + 3]
                o_ref[:, h * NSA_HD:(h + 1) * NSA_HD] = gc * o_cmp[r] + gs * o_slc[r] + gw * o_win[r]


def _nsa_sample(P, CMP, cache_kv, cache_win, layer, page_table, DB, DL, past_len, page):
    n_pages = page_table.shape[1]
    w_buf = cache_win.shape[2]
    lk = past_len + DL
    nch = CMP.shape[3]
    n_cmp = lk // CMP_STRIDE - CMP_BLOCK // CMP_STRIDE + 1
    n_slc = -(-lk // SLC_BLOCK)
    n_sel = min(N_SELECT, n_slc)
    nbl = _round_up(n_slc, LANES)
    pg = _pick_tile(n_pages, (32, 16, 8, 4, 2, 1))
    n_steps = n_pages // pg
    assert DL % SUBLANES == 0 and DL <= LANES and page == LANES and past_len == n_pages * page
    cov = jnp.asarray(_cover_matrix(n_cmp, n_slc, nch, nbl))
    kvw = NSA_KV_HEADS * NSA_HD
    page_spec = lambda i: pl.BlockSpec((None, None, page, 2, NSA_KV_HEADS, NSA_HD),
                                       lambda b, s, pt: (layer, pt[b, s * pg + i], 0, 1, 0, 0))
    in_specs = [
        pl.BlockSpec((DL, NSA_HEADS * NSA_HD), lambda b, s, pt: (b, 3)),
        pl.BlockSpec((DL, 2 * kvw), lambda b, s, pt: (b, 9)),
        pl.BlockSpec((DL, 2 * kvw), lambda b, s, pt: (b, 10)),
        pl.BlockSpec((DL, LANES), lambda b, s, pt: (b, 44)),
        pl.BlockSpec((None, None, NSA_KV_HEADS, nch, NSA_HD), lambda b, s, pt: (b, 0, 0, 0, 0)),
        pl.BlockSpec((None, None, NSA_KV_HEADS, nch, NSA_HD), lambda b, s, pt: (b, 1, 0, 0, 0)),
        pl.BlockSpec((nch, nbl), lambda b, s, pt: (0, 0)),
        pl.BlockSpec((None, None, w_buf, 2, NSA_KV_HEADS, NSA_HD), lambda b, s, pt: (layer, b, 0, 0, 0, 0)),
    ] + [page_spec(i) for i in range(pg)]
    rows = NSA_HEADS * DL
    grid_spec = pltpu.PrefetchScalarGridSpec(
        num_scalar_prefetch=1,
        grid=(DB, n_steps),
        in_specs=in_specs,
        out_specs=pl.BlockSpec((DL, NSA_HEADS * NSA_HD), lambda b, s, pt: (b, 0)),
        scratch_shapes=[
            pltpu.VMEM((rows, NSA_HD), BF16),
            pltpu.VMEM((rows, nbl), F32),
            pltpu.VMEM((rows, 1), F32),
            pltpu.VMEM((rows, 1), F32),
            pltpu.VMEM((rows, NSA_HD), F32),
            pltpu.VMEM((rows, NSA_HD), F32),
        ],
    )
    body = functools.partial(_nsa_sample_body, pg=pg, n_steps=n_steps, dl=DL, q_off=past_len, w_buf=w_buf,
                             n_cmp=n_cmp, n_slc=n_slc, n_sel=n_sel, nbl=nbl, page=page)
    return pl.pallas_call(
        body, grid_spec=grid_spec,
        out_shape=jax.ShapeDtypeStruct((DB * DL, NSA_HEADS * NSA_HD), F32),
        compiler_params=_cparams(("parallel", "arbitrary")),
        name="nsa_sample",
    )(page_table, P, P, P, P, CMP, CMP, cov, cache_win, *([cache_kv] * pg))


def _out_ln_body(h_ref, ro_ref, no_ref, wr_ref, wn_ref, g_ref, b_ref, o_ref, *, alpha):
    tm = h_ref.shape[0]
    half = tm // 2 if tm % (2 * BF16_ROWS) == 0 else tm
    for r0 in range(0, tm, half):
        rows = slice(r0, r0 + half)
        m = _dot(ro_ref[rows, :].astype(BF16), wr_ref[...]) + _dot(no_ref[rows, :].astype(BF16), wn_ref[...])
        o_ref[rows, :] = _layer_norm(alpha * h_ref[rows, :] + m, g_ref[...], b_ref[...])


def _out_ln(h, ro, no, w_out, g, b, alpha):
    T, D = h.shape
    kr = ro.shape[1]
    kn = no.shape[1]
    tm = _pick_tile(T, (512, 256, 128, 64, 32, 16, 8))
    return pl.pallas_call(
        functools.partial(_out_ln_body, alpha=alpha),
        grid=(T // tm,),
        in_specs=[
            pl.BlockSpec((tm, D), lambda i: (i, 0)),
            pl.BlockSpec((tm, kr), lambda i: (i, 0)),
            pl.BlockSpec((tm, kn), lambda i: (i, 0)),
            pl.BlockSpec((kr, D), lambda i: (0, 0)),
            pl.BlockSpec((kn, D), lambda i: (1, 0)),
            pl.BlockSpec((1, D), lambda i: (0, 0)),
            pl.BlockSpec((1, D), lambda i: (0, 0)),
        ],
        out_specs=pl.BlockSpec((tm, D), lambda i: (i, 0)),
        out_shape=jax.ShapeDtypeStruct((T, D), F32),
        compiler_params=_cparams(("parallel",)),
        name="out_ln",
    )(h, ro, no, w_out, w_out, g.reshape(1, D), b.reshape(1, D))


def _rope_tables(pos):
    half = NSA_HD // 2
    inv = ROPE_THETA ** (-jnp.arange(half, dtype=F32) / half)
    ang = pos.astype(F32)[:, None] * inv[None, :]
    cos = jnp.cos(ang)
    sin = jnp.sin(ang)
    return jnp.concatenate([cos, cos], -1), jnp.concatenate([-sin, sin], -1)


def _cmp_weights(w1, pos, w2):
    r = CMP_BLOCK // CMP_STRIDE
    w1r = w1.reshape(r, CMP_STRIDE * NSA_HD, CMP_HIDDEN)
    w1c = jnp.concatenate([w1r[i] for i in range(r)], axis=1).astype(BF16)
    posr = jnp.pad(pos.reshape(r, CMP_STRIDE * NSA_HD), ((0, SUBLANES - r), (0, 0))).astype(BF16)
    return w1c, posr, w2.astype(BF16)


def _layer_view(arr, l, shape):
    return arr.reshape(shape) if arr.shape[0] == 1 else arr[l].reshape(shape)


def _decoder_layer(x, B, L, q_off, s0, p, sample_ctx):
    alpha = p['alpha']
    h1, h1_bf = _ffn_ln(x, p['ffn1_w_up'], p['ffn1_w_down'], p['ln1_g'], p['ln1_b'], alpha, True)
    cos, sin = _rope_tables(q_off + jnp.arange(L, dtype=jnp.int32))
    P, kv_rows, win_rows = _proj(h1_bf, p['w_in'], p['rope_cols'], p['scale_cols'], cos, sin, L)
    ro, ret_s = _retention(P, s0, p['ret_gn_g'], p['ret_gn_b'], B, L)
    if sample_ctx is None:
        assert L % CMP_STRIDE == 0
        n_cmp = L // CMP_STRIDE - CMP_BLOCK // CMP_STRIDE + 1
        n_vec = 2 * NSA_KV_HEADS
        specs = [pl.BlockSpec((L, NSA_HD), (lambda v: (lambda b, g: (b, KV_CHUNK0 + v)))(v)) for v in range(n_vec)]
        CMP = _compress([P] * n_vec, specs, 1, L, 1, n_cmp, B, p['cmp_w1'], p['cmp_pos'], p['cmp_w2'])
        no = _nsa_prompt(P, CMP, B, L)
    else:
        cache_kv, cache_win, layer, page_table, past_len, page = sample_ctx
        n_pages = page_table.shape[1]
        lk = past_len + L
        assert (lk // CMP_STRIDE) * CMP_STRIDE <= past_len, "compression blocks must lie in the paged past"
        n_cmp = lk // CMP_STRIDE - CMP_BLOCK // CMP_STRIDE + 1
        n_in = _pick_tile(n_pages, (32, 16, 8, 4, 2, 1))
        n_grp = n_pages // n_in
        specs = [pl.BlockSpec((None, None, page, 2, NSA_KV_HEADS, NSA_HD),
                              (lambda i: (lambda b, g, pt: (layer, pt[b, (n_grp - 1 - g) * n_in + i], 0, 0, 0, 0)))(i))
                 for i in range(n_in)]
        CMP = _compress([cache_kv] * n_in, specs, n_in, page, n_grp, n_cmp, B,
                        p['cmp_w1'], p['cmp_pos'], p['cmp_w2'], page_table=page_table)
        no = _nsa_sample(P, CMP, cache_kv, cache_win, layer, page_table, B, L, past_len, page)
    x2 = _out_ln(h1, ro, no, p['w_out'], p['ln2_g'], p['ln2_b'], alpha)
    y, _ = _ffn_ln(x2, p['ffn2_w_up'], p['ffn2_w_down'], p['ln3_g'], p['ln3_b'], alpha, False)
    return y, ret_s, kv_rows, win_rows


def kernel(x_prompt, x_sample, state_ret, cache_nsa_kv, cache_win, page_table, ffn1_w_up, ffn1_w_down, ln1_g, ln1_b, w_in, w_out, ret_gn_g, ret_gn_b, cmp_pos_k, cmp_w1_k, cmp_w2_k, cmp_pos_v, cmp_w1_v, cmp_w2_v, ln2_g, ln2_b, ffn2_w_up, ffn2_w_down, ln3_g, ln3_b):
    B, L, D = x_prompt.shape
    DB, DL, _ = x_sample.shape
    depth = w_in.shape[0]
    n_pool, page = cache_nsa_kv.shape[1], cache_nsa_kv.shape[2]
    n_pages = page_table.shape[1]
    past_len = n_pages * page
    w_buf = cache_win.shape[2]
    alpha = (2.0 * depth) ** 0.25
    rope_np = np.zeros((N_IN_PAD // LANES, LANES), np.float32)
    rope_np[list(ROPE_CHUNKS)] = 1.0
    scale_np = np.ones((N_IN_PAD // LANES, LANES), np.float32)
    scale_np[list(KSCALE_CHUNKS)] = RET_DK ** -0.5
    rope_cols = jnp.asarray(rope_np.reshape(1, N_IN_PAD))
    scale_cols = jnp.asarray(scale_np.reshape(1, N_IN_PAD))

    yp = x_prompt.reshape(B * L, D)
    ys = x_sample.reshape(DB * DL, D)
    outs = [[] for _ in range(6)]
    for l in range(depth):
        k1, p1, k2 = _cmp_weights(cmp_w1_k[l], cmp_pos_k[l], cmp_w2_k[l])
        v1, q1, v2 = _cmp_weights(cmp_w1_v[l], cmp_pos_v[l], cmp_w2_v[l])
        p = {
            'alpha': alpha, 'rope_cols': rope_cols, 'scale_cols': scale_cols,
            'ffn1_w_up': ffn1_w_up[l].astype(BF16), 'ffn1_w_down': ffn1_w_down[l].astype(BF16),
            'ln1_g': ln1_g[l], 'ln1_b': ln1_b[l],
            'w_in': jnp.pad(w_in[l], ((0, 0), (0, N_IN_PAD - N_IN))).astype(BF16),
            'w_out': w_out[l].astype(BF16),
            'ret_gn_g': ret_gn_g[l], 'ret_gn_b': ret_gn_b[l],
            'cmp_w1': jnp.stack([k1, v1]), 'cmp_pos': jnp.stack([p1, q1]), 'cmp_w2': jnp.stack([k2, v2]),
            'ln2_g': ln2_g[l], 'ln2_b': ln2_b[l],
            'ffn2_w_up': ffn2_w_up[l].astype(BF16), 'ffn2_w_down': ffn2_w_down[l].astype(BF16),
            'ln3_g': ln3_g[l], 'ln3_b': ln3_b[l],
        }
        s0 = jnp.zeros((B, RET_HEADS, RET_DK, RET_DV), F32)
        yp, rs_p, kv_p, win_p = _decoder_layer(yp, B, L, 0, s0, p, None)
        ctx = (cache_nsa_kv, cache_win, l, page_table, past_len, page)
        ys, rs_s, kv_s, win_s = _decoder_layer(ys, DB, DL, past_len,
                                               _layer_view(state_ret, l, state_ret.shape[1:]), p, ctx)
        wl = min(WINDOW, L)
        outs[0].append(rs_p)
        outs[1].append(rs_s)
        outs[2].append(kv_p.reshape(B, L, 4, NSA_KV_HEADS, NSA_HD))
        outs[3].append(kv_s.reshape(DB, DL, 4, NSA_KV_HEADS, NSA_HD))
        outs[4].append(win_p.reshape(B, L, 2, NSA_KV_HEADS, NSA_HD)[:, L - wl:])
        win_s = win_s.reshape(DB, DL, 2, NSA_KV_HEADS, NSA_HD)
        outs[5].append(jnp.concatenate([cache_win[l], win_s], axis=1)[:, -w_buf:])
    return (yp.reshape(B, L, D), ys.reshape(DB, DL, D), jnp.stack(outs[0]), jnp.stack(outs[1]),
            jnp.stack(outs[2]), jnp.stack(outs[3]), jnp.stack(outs[4]), jnp.stack(outs[5]))
```

```python
import functools

import numpy as np
import jax
import jax.numpy as jnp
from jax import lax
from jax.experimental import pallas as pl
from jax.experimental.pallas import tpu as pltpu

F32 = jnp.float32
BF16 = jnp.bfloat16

LANES = 128
SUBLANES = 8
BF16_ROWS = 16
VMEM_LIMIT_BYTES = 56 * 1024 * 1024

RET_HEADS = 4
RET_DK = 128
RET_DV = 256
RET_CHUNK = 128
NSA_HEADS = 8
NSA_KV_HEADS = 2
NSA_HD = 128
NSA_GROUP = NSA_HEADS // NSA_KV_HEADS
CMP_BLOCK = 32
CMP_STRIDE = 16
CMP_HIDDEN = 2 * NSA_HD
SLC_BLOCK = 64
SLC_SHIFT = 6
N_SELECT = 16
WINDOW = 512
N_BRANCH = 3
ROPE_THETA = 10000.0
LN_EPS = 1e-5
NEG_BIG = -1e30
LOG2_E = 1.4426950408889634

N_IN = 5656
N_IN_PAD = 5760
KV_CHUNK0 = 32
WIN_CHUNK0 = 40
GATE_CHUNK = 44
ROPE_CHUNKS = tuple(range(0, 8)) + tuple(range(24, 32)) + (32, 33, 36, 37, 40, 41)
KSCALE_CHUNKS = tuple(range(4, 8))


def _cparams(sem):
    return pltpu.CompilerParams(dimension_semantics=sem, vmem_limit_bytes=VMEM_LIMIT_BYTES)


def _pick_tile(n, candidates):
    for c in candidates:
        if n % c == 0:
            return c
    return n


def _round_up(n, m):
    return (n + m - 1) // m * m


def _layer_norm(z, g, b):
    mu = jnp.mean(z, axis=-1, keepdims=True)
    zc = z - mu
    var = jnp.mean(zc * zc, axis=-1, keepdims=True)
    return zc * lax.rsqrt(var + LN_EPS) * g + b


def _dot(a, b):
    return jnp.dot(a, b, preferred_element_type=F32)


def _dot_nt(a, b):
    return lax.dot_general(a, b, (((1,), (1,)), ((), ())), preferred_element_type=F32)


def _ffn_ln_body(x_ref, wa_ref, wb_ref, wd_ref, g_ref, b_ref, *rest, nj, alpha, emit_bf16):
    if emit_bf16:
        o_ref, obf_ref, xbf_sc, acc_sc = rest
    else:
        o_ref, xbf_sc, acc_sc = rest
    j = pl.program_id(1)

    @pl.when(j == 0)
    def _init():
        xbf_sc[...] = x_ref[...].astype(BF16)
        acc_sc[...] = jnp.zeros_like(acc_sc)

    xb = xbf_sc[...]
    a = _dot(xb, wa_ref[...])
    b = _dot(xb, wb_ref[...])
    h = (a * jax.nn.sigmoid(a)) * b
    acc_sc[...] += _dot(h.astype(BF16), wd_ref[...])

    @pl.when(j == nj - 1)
    def _finish():
        z = alpha * x_ref[...] + 0.5 * acc_sc[...]
        y = _layer_norm(z, g_ref[...], b_ref[...])
        o_ref[...] = y
        if emit_bf16:
            obf_ref[...] = y.astype(BF16)


def _ffn_ln(x, w_up, w_down, g, b, alpha, emit_bf16):
    T, D = x.shape
    F = w_down.shape[0]
    tm = _pick_tile(T, (512, 256, 128, 64, 32, 16, 8))
    tf = _pick_tile(F, (512, 256, 128))
    nj = F // tf
    out_shape = [jax.ShapeDtypeStruct((T, D), F32)]
    out_specs = [pl.BlockSpec((tm, D), lambda i, j: (i, 0))]
    if emit_bf16:
        out_shape.append(jax.ShapeDtypeStruct((T, D), BF16))
        out_specs.append(pl.BlockSpec((tm, D), lambda i, j: (i, 0)))
    res = pl.pallas_call(
        functools.partial(_ffn_ln_body, nj=nj, alpha=alpha, emit_bf16=emit_bf16),
        grid=(T // tm, nj),
        in_specs=[
            pl.BlockSpec((tm, D), lambda i, j: (i, 0)),
            pl.BlockSpec((D, tf), lambda i, j: (0, j)),
            pl.BlockSpec((D, tf), lambda i, j: (0, nj + j)),
            pl.BlockSpec((tf, D), lambda i, j: (j, 0)),
            pl.BlockSpec((1, D), lambda i, j: (0, 0)),
            pl.BlockSpec((1, D), lambda i, j: (0, 0)),
        ],
        out_specs=out_specs,
        out_shape=out_shape,
        scratch_shapes=[pltpu.VMEM((tm, D), BF16), pltpu.VMEM((tm, D), F32)],
        compiler_params=_cparams(("parallel", "arbitrary")),
        name="ffn_ln",
    )(x, w_up, w_up, w_down, g.reshape(1, D), b.reshape(1, D))
    return res if emit_bf16 else (res[0], None)


def _rows_view(ref, j):
    rows = ref.shape[0]
    n = int(np.prod(ref.shape[1:-1]))
    return ref.reshape(n * rows, ref.shape[-1]), pl.ds(j, rows, stride=n)


def _head_rows(ref, j):
    view, idx = _rows_view(ref, j)
    return view[idx, :]


def _store_rows(ref, j, val):
    view, idx = _rows_view(ref, j)
    view[idx, :] = val


def _proj_body(x_ref, w_ref, cos_ref, sin_ref, rope_ref, scale_ref, o_ref, kv_ref, win_ref, *, n_chunk, j_rows):
    y = _dot(x_ref[...], w_ref[...])
    cos = cos_ref[...]
    sin = sin_ref[...]
    for c in range(n_chunk):
        sl = slice(c * LANES, (c + 1) * LANES)
        yc = y[:, sl]
        roped = yc * cos + pltpu.roll(yc, NSA_HD // 2, 1) * sin
        o_ref[:, sl] = jnp.where(rope_ref[:, sl] > 0.5, roped, yc) * scale_ref[:, sl]

    @pl.when(pl.program_id(1) == j_rows)
    def _emit_cache_rows():
        c0 = KV_CHUNK0 - j_rows * n_chunk
        for r in range(WIN_CHUNK0 - KV_CHUNK0):
            _store_rows(kv_ref, r, o_ref[:, (c0 + r) * LANES:(c0 + r + 1) * LANES])
        c0 = WIN_CHUNK0 - j_rows * n_chunk
        for r in range(GATE_CHUNK - WIN_CHUNK0):
            _store_rows(win_ref, r, o_ref[:, (c0 + r) * LANES:(c0 + r + 1) * LANES])


def _proj(x_bf, w_bf, rope_cols, scale_cols, cos, sin, rows_per_seq):
    T, D = x_bf.shape
    N = w_bf.shape[1]
    tm = _pick_tile(T, (512, 256, 128, 64, 32, 16, 8))
    tn = 1920
    assert N % tn == 0
    n_chunk = tn // LANES
    if rows_per_seq >= tm:
        assert rows_per_seq % tm == 0
        n_tab = rows_per_seq // tm
    else:
        assert tm % rows_per_seq == 0
        cos = jnp.tile(cos, (tm // rows_per_seq, 1))
        sin = jnp.tile(sin, (tm // rows_per_seq, 1))
        n_tab = 1
    j_rows = KV_CHUNK0 // n_chunk
    assert (GATE_CHUNK - 1) // n_chunk == j_rows, "cache-row columns must sit in one column tile"
    kv_shape = (T, 4, NSA_KV_HEADS, NSA_HD)
    win_shape = (T, 2, NSA_KV_HEADS, NSA_HD)
    return pl.pallas_call(
        functools.partial(_proj_body, n_chunk=n_chunk, j_rows=j_rows),
        grid=(T // tm, N // tn),
        in_specs=[
            pl.BlockSpec((tm, D), lambda i, j: (i, 0)),
            pl.BlockSpec((D, tn), lambda i, j: (0, j)),
            pl.BlockSpec((tm, LANES), lambda i, j: (i % n_tab, 0)),
            pl.BlockSpec((tm, LANES), lambda i, j: (i % n_tab, 0)),
            pl.BlockSpec((1, tn), lambda i, j: (0, j)),
            pl.BlockSpec((1, tn), lambda i, j: (0, j)),
        ],
        out_specs=[
            pl.BlockSpec((tm, tn), lambda i, j: (i, j)),
            pl.BlockSpec((tm,) + kv_shape[1:], lambda i, j: (i, 0, 0, 0)),
            pl.BlockSpec((tm,) + win_shape[1:], lambda i, j: (i, 0, 0, 0)),
        ],
        out_shape=[
            jax.ShapeDtypeStruct((T, N), F32),
            jax.ShapeDtypeStruct(kv_shape, F32),
            jax.ShapeDtypeStruct(win_shape, F32),
        ],
        compiler_params=_cparams(("parallel", "arbitrary")),
        name="proj_rope",
    )(x_bf, w_bf, cos, sin, rope_cols, scale_cols)


def _ret_body(q_ref, k_ref, v_ref, g_ref, s0_ref, dm_ref, ind_ref, std_ref, cd_ref, gng_ref, gnb_ref,
              o_ref, sout_ref, s_sc, *, nc, rows, rows_pad, per_step):
    c = pl.program_id(1)

    @pl.when(c == 0)
    def _load_state():
        s_sc[...] = s0_ref[...]

    pad = rows_pad - rows
    for sub in range(per_step):
        rs = slice(sub * rows, (sub + 1) * rows)
        for h in range(RET_HEADS):
            ks = slice(h * RET_DK, (h + 1) * RET_DK)
            vs = slice(h * RET_DV, (h + 1) * RET_DV)
            q = q_ref[rs, ks]
            k = k_ref[rs, ks]
            v = v_ref[rs, vs]
            kd = k * std_ref[h]
            if pad:
                k = jnp.concatenate([k, jnp.zeros((pad, RET_DK), F32)], axis=0)
                kd = jnp.concatenate([kd, jnp.zeros((pad, RET_DK), F32)], axis=0)
                v = jnp.concatenate([v, jnp.zeros((pad, RET_DV), F32)], axis=0)
            s_old = s_sc[h]
            vb = v.astype(BF16)
            a = _dot_nt(q.astype(BF16), k.astype(BF16)) * dm_ref[h]
            o = _dot(a.astype(BF16), vb) + _dot((q * ind_ref[h]).astype(BF16), s_old.astype(BF16))
            s_sc[h] = s_old * cd_ref[h] + _dot(kd.T.astype(BF16), vb)
            mu = jnp.mean(o, axis=-1, keepdims=True)
            oc = o - mu
            var = jnp.mean(oc * oc, axis=-1, keepdims=True)
            on = oc * lax.rsqrt(var + LN_EPS) * gng_ref[:, vs] + gnb_ref[:, vs]
            gate = g_ref[rs, vs]
            o_ref[rs, vs] = (gate * jax.nn.sigmoid(gate)) * on

    @pl.when(c == nc - 1)
    def _store_state():
        sout_ref[...] = s_sc[...]


def _retention(P, s0, gn_g, gn_b, B, L):
    C = RET_CHUNK if L % RET_CHUNK == 0 else L
    nc = L // C
    CP = max(C, LANES)
    lg = jnp.log1p(-jnp.exp2(-5.0 - jnp.arange(RET_HEADS, dtype=F32)))
    i = jnp.arange(C, dtype=F32)
    diff = i[:, None] - i[None, :]
    dmask = jnp.where(diff >= 0, jnp.exp(lg[:, None, None] * jnp.maximum(diff, 0.0)), 0.0)
    dmask = jnp.pad(dmask, ((0, 0), (0, 0), (0, CP - C)))
    in_decay = jnp.broadcast_to(jnp.exp(lg[:, None] * (i + 1.0))[:, :, None], (RET_HEADS, C, RET_DK))
    st_decay = jnp.broadcast_to(jnp.exp(lg[:, None] * (C - 1.0 - i))[:, :, None], (RET_HEADS, C, RET_DK))
    chunk_decay = jnp.broadcast_to(jnp.exp(lg * C)[:, None, None], (RET_HEADS, 1, RET_DV))
    qw = RET_HEADS * RET_DK
    vw = RET_HEADS * RET_DV
    const3 = lambda b, c: (0, 0, 0)
    per_step = _pick_tile(nc, (4, 2, 1))
    nc = nc // per_step
    CS = C * per_step
    return pl.pallas_call(
        functools.partial(_ret_body, nc=nc, rows=C, rows_pad=CP, per_step=per_step),
        grid=(B, nc),
        in_specs=[
            pl.BlockSpec((CS, qw), lambda b, c: (b * nc + c, 0)),
            pl.BlockSpec((CS, qw), lambda b, c: (b * nc + c, 1)),
            pl.BlockSpec((CS, vw), lambda b, c: (b * nc + c, 1)),
            pl.BlockSpec((CS, vw), lambda b, c: (b * nc + c, 2)),
            pl.BlockSpec((None, RET_HEADS, RET_DK, RET_DV), lambda b, c: (b, 0, 0, 0)),
            pl.BlockSpec((RET_HEADS, C, CP), const3),
            pl.BlockSpec((RET_HEADS, C, RET_DK), const3),
            pl.BlockSpec((RET_HEADS, C, RET_DK), const3),
            pl.BlockSpec((RET_HEADS, 1, RET_DV), const3),
            pl.BlockSpec((1, vw), lambda b, c: (0, 0)),
            pl.BlockSpec((1, vw), lambda b, c: (0, 0)),
        ],
        out_specs=[
            pl.BlockSpec((CS, vw), lambda b, c: (b * nc + c, 0)),
            pl.BlockSpec((None, RET_HEADS, RET_DK, RET_DV), lambda b, c: (b, 0, 0, 0)),
        ],
        out_shape=[
            jax.ShapeDtypeStruct((B * L, vw), F32),
            jax.ShapeDtypeStruct((B, RET_HEADS, RET_DK, RET_DV), F32),
        ],
        scratch_shapes=[pltpu.VMEM((RET_HEADS, RET_DK, RET_DV), F32)],
        compiler_params=_cparams(("parallel", "arbitrary")),
        name="retention",
    )(P, P, P, P, s0, dmask, in_decay, st_decay, chunk_decay, gn_g.reshape(1, vw), gn_b.reshape(1, vw))


def _cmp_body(*refs, n_in, rows, n_grp, n_cmp, paged):
    n_vec = 2 * NSA_KV_HEADS
    if paged:
        refs = refs[1:]
    n_src = n_in if paged else n_vec * n_in
    x_refs = refs[:n_src]
    w1_ref, pos_ref, w2_ref, o_ref, carry_sc, xc_sc = refs[n_src:]
    g = pl.program_id(1)
    grp = n_grp - 1 - g
    cpi = rows // CMP_STRIDE
    M = n_in * cpi

    @pl.when(g == 0)
    def _init():
        carry_sc[...] = jnp.zeros_like(carry_sc)

    row = lax.broadcasted_iota(jnp.int32, (M, 1), 0)
    grp_in = 1 if cpi % BF16_ROWS == 0 else BF16_ROWS // cpi
    assert n_in % grp_in == 0 and (grp_in * cpi) % BF16_ROWS == 0
    for t in range(2):
        w1 = w1_ref[t]
        gp = _dot(pos_ref[t], w1)
        posterm = gp[0:1, :CMP_HIDDEN] + gp[1:2, CMP_HIDDEN:]
        for hd in range(NSA_KV_HEADS):
            v = t * NSA_KV_HEADS + hd
            for i0 in range(0, n_in, grp_in):
                if paged:
                    parts = [pltpu.einshape("csd->scd",
                                            _head_rows(x_refs[i], v).reshape(cpi, CMP_STRIDE, NSA_HD))
                             for i in range(i0, i0 + grp_in)]
                    piece = lambda s: jnp.concatenate([xt[s] for xt in parts], axis=0)
                else:
                    refs_i = [x_refs[v * n_in + i] for i in range(i0, i0 + grp_in)]
                    piece = lambda s: jnp.concatenate(
                        [r[pl.ds(s, cpi, stride=CMP_STRIDE), :] for r in refs_i], axis=0)
                r0 = v * M + i0 * cpi
                for s in range(CMP_STRIDE):
                    xc_sc[r0:r0 + grp_in * cpi, s * NSA_HD:(s + 1) * NSA_HD] = piece(s).astype(BF16)
        t0 = t * NSA_KV_HEADS * M
        gg_all = _dot(xc_sc[t0:t0 + NSA_KV_HEADS * M, :], w1)
        for hd in range(NSA_KV_HEADS):
            v = t * NSA_KV_HEADS + hd
            gg = gg_all[hd * M:(hd + 1) * M]
            g0 = gg[:, :CMP_HIDDEN]
            g1 = gg[:, CMP_HIDDEN:]
            nxt = pltpu.roll(g1, M - 1, 0)
            nxt = jnp.where(row == M - 1, carry_sc[v][0:1, :], nxt)
            carry_sc[v] = g1[0:SUBLANES, :]
            hid = g0 + nxt + posterm
            out = _dot(jax.nn.gelu(hid).astype(BF16), w2_ref[t])
            o_ref[t, hd] = jnp.where(grp * M + row < n_cmp, out, 0.0)


def _compress(srcs, src_specs, n_in, rows, n_grp, n_cmp, B, w1, pos, w2, page_table=None):
    n_vec = 2 * NSA_KV_HEADS
    M = n_in * rows // CMP_STRIDE
    nch = n_grp * M
    paged = page_table is not None
    const3 = lambda *a: (0, 0, 0)
    in_specs = list(src_specs) + [
        pl.BlockSpec((2, CMP_STRIDE * NSA_HD, 2 * CMP_HIDDEN), const3),
        pl.BlockSpec((2, SUBLANES, CMP_STRIDE * NSA_HD), const3),
        pl.BlockSpec((2, CMP_HIDDEN, NSA_HD), const3),
    ]
    out_spec = pl.BlockSpec((None, 2, NSA_KV_HEADS, M, NSA_HD), lambda *a: (a[0], 0, 0, n_grp - 1 - a[1], 0))
    body = functools.partial(_cmp_body, n_in=n_in, rows=rows, n_grp=n_grp, n_cmp=n_cmp, paged=paged)
    out_shape = jax.ShapeDtypeStruct((B, 2, NSA_KV_HEADS, nch, NSA_HD), F32)
    scratch = [pltpu.VMEM((n_vec, SUBLANES, CMP_HIDDEN), F32),
               pltpu.VMEM((n_vec * M, CMP_STRIDE * NSA_HD), BF16)]
    sem = ("parallel", "arbitrary")
    if paged:
        grid_spec = pltpu.PrefetchScalarGridSpec(
            num_scalar_prefetch=1, grid=(B, n_grp), in_specs=in_specs, out_specs=out_spec,
            scratch_shapes=scratch)
        return pl.pallas_call(body, grid_spec=grid_spec, out_shape=out_shape,
                              compiler_params=_cparams(sem), name="nsa_compress_paged")(
            page_table, *srcs, w1, pos, w2)
    return pl.pallas_call(body, grid=(B, n_grp), in_specs=in_specs, out_specs=out_spec,
                          out_shape=out_shape, scratch_shapes=scratch,
                          compiler_params=_cparams(sem), name="nsa_compress")(*srcs, w1, pos, w2)


def _cover_matrix(n_cmp, n_slc, rows, cols):
    c_i = np.arange(n_cmp)[:, None]
    n_i = np.arange(n_slc)[None, :]
    cov = np.clip(np.minimum(c_i * CMP_STRIDE + CMP_BLOCK, (n_i + 1) * SLC_BLOCK)
                  - np.maximum(c_i * CMP_STRIDE, n_i * SLC_BLOCK), 0, None).astype(np.float32) / CMP_BLOCK
    out = np.zeros((rows, cols), np.float32)
    out[:n_cmp, :n_slc] = cov
    return out


def _nsa_prompt_body(q_ref, kc_ref, vc_ref, ks_ref, vs_ref, kw_ref, vw_ref, gate_ref, covt_ref, o_ref,
                     kcb_sc, vct_sc, ksb_sc, vst_sc, kwb_sc, vwt_sc, prio_sc, sel_sc, gt_sc,
                     m_sc, l_sc, acc_sc, out_sc, s_sc, *, tq, L, n_cmp, n_slc, n_sel, nbp):
    KV = NSA_KV_HEADS
    qi = pl.program_id(1)
    tk = tq
    nch = kc_ref.shape[1]
    scale = NSA_HD ** -0.5

    @pl.when(qi == 0)
    def _stage_kv():
        for kv in range(KV):
            cs = slice(kv * NSA_HD, (kv + 1) * NSA_HD)
            kcb_sc[kv] = kc_ref[kv].astype(BF16)
            vct_sc[kv] = vc_ref[kv].T.astype(BF16)
            ksb_sc[kv] = ks_ref[:, cs].astype(BF16)
            kwb_sc[kv] = kw_ref[:, cs].astype(BF16)
            for i in range(L // tk):
                vst_sc[kv, i] = vs_ref[i * tk:(i + 1) * tk, cs].T.astype(BF16)
                vwt_sc[kv, i] = vw_ref[i * tk:(i + 1) * tk, cs].T.astype(BF16)

    G = NSA_GROUP
    W = G * tq
    t0 = qi * tq
    t_row = t0 + lax.broadcasted_iota(jnp.int32, (1, tq), 1)
    lane = lax.broadcasted_iota(jnp.int32, (1, W), 1)
    t_all = t0 + (lane & (tq - 1))
    qcats = []
    for kv in range(KV):
        qc = jnp.concatenate([q_ref[:, (kv * G + g) * NSA_HD:(kv * G + g + 1) * NSA_HD] for g in range(G)], axis=0)
        qcats.append((qc * (scale * LOG2_E)).astype(BF16))
    gt_sc[...] = jax.nn.sigmoid(gate_ref[...]).T

    def gate_row(kv, br):
        return jnp.concatenate(
            [gt_sc[(kv * G + g) * N_BRANCH + br:(kv * G + g) * N_BRANCH + br + 1, :] for g in range(G)], axis=1)

    kp_l = lax.broadcasted_iota(jnp.int32, (tk, 1), 0)
    t_l = lane & (tq - 1)
    causal_bias = jnp.where(kp_l <= t_l, 0.0, NEG_BIG)
    far_bias = jnp.where(kp_l > t_l, 0.0, NEG_BIG)

    def reset(kv):
        m_sc[kv] = jnp.full((1, W), NEG_BIG, F32)
        l_sc[kv] = jnp.zeros((1, W), F32)
        acc_sc[kv] = jnp.zeros((NSA_HD, W), F32)

    def score(kb_sc, kv, kt, bias):
        k0 = pl.multiple_of(kt * tk, tk)
        s = _dot_nt(kb_sc[kv, pl.ds(k0, tk), :], qcats[kv])
        return s if bias is None else s + bias

    def update(vt_sc, kv, kt, s):
        m_old = m_sc[kv]
        m_new = jnp.maximum(m_old, jnp.max(s, axis=0, keepdims=True))
        p = jnp.exp2(s - m_new)
        alpha = jnp.exp2(m_old - m_new)
        l_sc[kv] = alpha * l_sc[kv] + jnp.sum(p, axis=0, keepdims=True)
        acc_sc[kv] = alpha * acc_sc[kv] + _dot(vt_sc[kv, kt], p.astype(BF16))
        m_sc[kv] = m_new

    def branch_out(kv, br):
        return (gate_row(kv, br) / l_sc[kv]) * acc_sc[kv]

    far = WINDOW // tk
    for kv in range(KV):
        reset(kv)
        tiles = []
        for back in range(far, -1, -1):
            bias = far_bias if back == far else (causal_bias if back == 0 else None)
            if back > 0:
                off = jnp.where(qi >= back, 0.0, NEG_BIG).astype(F32)
                bias = off if bias is None else bias + off
            kt = jnp.maximum(qi - back, 0)
            tiles.append((kt, score(kwb_sc, kv, kt, bias)))
        for kt, s_w in tiles:
            update(vwt_sc, kv, kt, s_w)
        out_sc[kv] = branch_out(kv, 2)

    c_i = lax.broadcasted_iota(jnp.int32, (nch, 1), 0)
    c_end = jnp.where(c_i < n_cmp, c_i * CMP_STRIDE + (CMP_BLOCK - 1), L)
    n_i = lax.broadcasted_iota(jnp.int32, (nbp, 1), 0)
    valid = (n_i * SLC_BLOCK <= t_row) & (n_i < n_slc)
    cur = t_row >> SLC_SHIFT
    forced = (n_i == 0) | (n_i == cur) | (n_i == cur - 1)
    for kv in range(KV):
        s = jnp.where(c_end <= t_all, _dot_nt(kcb_sc[kv], qcats[kv]), -jnp.inf)
        m = jnp.max(s, axis=0, keepdims=True)
        m = jnp.where(m > -jnp.inf, m, 0.0)
        e = jnp.exp2(s - m)
        d = jnp.sum(e, axis=0, keepdims=True)
        p = e / jnp.where(d > 0, d, 1.0)
        out_sc[kv] += gate_row(kv, 0) * _dot(vct_sc[kv], p.astype(BF16))
        psum = p[:, 0:tq]
        for g in range(1, G):
            psum = psum + p[:, g * tq:(g + 1) * tq]
        imp = jnp.dot(covt_ref[...], psum, preferred_element_type=F32, precision=lax.Precision.HIGHEST)
        prio = jnp.where(forced, jnp.inf, jnp.where(valid, imp, -jnp.inf))
        prio_sc[kv] = prio
        cnt = jnp.zeros((nbp, tq), jnp.int32)
        for mm in range(n_slc):
            pm = prio_sc[kv, mm:mm + 1, :]
            tie = jnp.where(n_i > mm, 1, 0)
            cnt = cnt + jnp.where(pm > prio, 1, jnp.where(pm == prio, tie, 0))
        selbias = jnp.where((cnt < n_sel) & valid, 0.0, NEG_BIG)
        sel_sc[kv] = jnp.concatenate([selbias] * G, axis=1)

    def sel_bias(kv, kt):
        per_tile = tk // SLC_BLOCK
        rows = [jnp.broadcast_to(sel_sc[kv, pl.ds(kt * per_tile + r, 1), :], (SLC_BLOCK, W))
                for r in range(per_tile)]
        return jnp.concatenate(rows, axis=0)

    for kv in range(KV):
        reset(kv)
        s_sc[kv] = score(ksb_sc, kv, 0, sel_bias(kv, 0))

    def slc_body(kt, carry):
        nxt = [score(ksb_sc, kv, kt + 1, sel_bias(kv, kt + 1)) for kv in range(KV)]
        for kv in range(KV):
            update(vst_sc, kv, kt, s_sc[kv])
        for kv in range(KV):
            s_sc[kv] = nxt[kv]
        return carry

    lax.fori_loop(0, qi, slc_body, 0)
    for kv in range(KV):
        update(vst_sc, kv, qi, s_sc[kv] + causal_bias)
        out_sc[kv] += branch_out(kv, 1)
        for g in range(G):
            h = kv * G + g
            o_ref[:, h * NSA_HD:(h + 1) * NSA_HD] = out_sc[kv, :, g * tq:(g + 1) * tq].T


def _nsa_prompt(P, CMP, B, L):
    tq = _pick_tile(L, (256, 128))
    assert L % tq == 0 and tq % SLC_BLOCK == 0 and WINDOW % tq == 0 and tq & (tq - 1) == 0
    nq = L // tq
    nch = CMP.shape[3]
    n_cmp = L // CMP_STRIDE - CMP_BLOCK // CMP_STRIDE + 1
    n_slc = -(-L // SLC_BLOCK)
    n_sel = min(N_SELECT, n_slc)
    nbp = _round_up(n_slc, SUBLANES)
    covt = jnp.asarray(_cover_matrix(n_cmp, n_slc, nch, nbp).T)
    qw = NSA_HEADS * NSA_HD
    kvw = NSA_KV_HEADS * NSA_HD
    KV = NSA_KV_HEADS
    W = NSA_GROUP * tq
    kv_col = lambda c: (lambda b, i: (b, (KV_CHUNK0 + c) // NSA_KV_HEADS))
    body = functools.partial(_nsa_prompt_body, tq=tq, L=L, n_cmp=n_cmp, n_slc=n_slc, n_sel=n_sel, nbp=nbp)
    return pl.pallas_call(
        body,
        grid=(B, nq),
        in_specs=[
            pl.BlockSpec((tq, qw), lambda b, i: (b * nq + i, 3)),
            pl.BlockSpec((None, None, KV, nch, NSA_HD), lambda b, i: (b, 0, 0, 0, 0)),
            pl.BlockSpec((None, None, KV, nch, NSA_HD), lambda b, i: (b, 1, 0, 0, 0)),
            pl.BlockSpec((L, kvw), kv_col(4)),
            pl.BlockSpec((L, kvw), kv_col(6)),
            pl.BlockSpec((L, kvw), kv_col(8)),
            pl.BlockSpec((L, kvw), kv_col(10)),
            pl.BlockSpec((tq, LANES), lambda b, i: (b * nq + i, GATE_CHUNK)),
            pl.BlockSpec((nbp, nch), lambda b, i: (0, 0)),
        ],
        out_specs=pl.BlockSpec((tq, qw), lambda b, i: (b * nq + i, 0)),
        out_shape=jax.ShapeDtypeStruct((B * L, qw), F32),
        scratch_shapes=[
            pltpu.VMEM((KV, nch, NSA_HD), BF16),
            pltpu.VMEM((KV, NSA_HD, nch), BF16),
            pltpu.VMEM((KV, L, NSA_HD), BF16),
            pltpu.VMEM((KV, L // tq, NSA_HD, tq), BF16),
            pltpu.VMEM((KV, L, NSA_HD), BF16),
            pltpu.VMEM((KV, L // tq, NSA_HD, tq), BF16),
            pltpu.VMEM((KV, nbp, tq), F32),
            pltpu.VMEM((KV, nbp, W), F32),
            pltpu.VMEM((LANES, tq), F32),
            pltpu.VMEM((KV, 1, W), F32),
            pltpu.VMEM((KV, 1, W), F32),
            pltpu.VMEM((KV, NSA_HD, W), F32),
            pltpu.VMEM((KV, NSA_HD, W), F32),
            pltpu.VMEM((KV, tq, W), F32),
        ],
        compiler_params=_cparams(("parallel", "arbitrary")),
        name="nsa_prompt",
    )(P, CMP, CMP, P, P, P, P, P, covt)


def _nsa_sample_body(*refs, nb, pg, n_steps, dl, q_off, w_buf, n_cmp, n_slc, n_sel, nbl, page):
    refs = refs[1:]
    q_ref, kvn_ref, wn_ref, gate_ref, kc_ref, vc_ref, cov_ref = refs[:7]
    cw_refs = refs[7:7 + nb]
    pg_refs = refs[7 + nb:7 + nb + nb * pg]
    o_ref, qb_sc, sel_sc, m_sc, l_sc, acc_sc, ocmp_sc = refs[7 + nb + nb * pg:]
    step = pl.program_id(1)
    KVH = NSA_KV_HEADS
    n_str = nb * KVH
    rows_h = NSA_GROUP * dl
    scale = NSA_HD ** -0.5
    nch = kc_ref.shape[2]

    def tok_pos(n):
        r = lax.broadcasted_iota(jnp.int32, (n, 1), 0)
        return q_off + r % dl

    @pl.when(step == 0)
    def _select():
        for jk in range(n_str):
            j, k = divmod(jk, KVH)
            for g in range(NSA_GROUP):
                h = k * NSA_GROUP + g
                qb_sc[pl.ds((jk * NSA_GROUP + g) * dl, dl), :] = (
                    q_ref[j * dl:(j + 1) * dl, h * NSA_HD:(h + 1) * NSA_HD] * scale).astype(BF16)
        t_h = tok_pos(rows_h)
        c_i = lax.broadcasted_iota(jnp.int32, (1, nch), 1)
        cmask = (c_i * CMP_STRIDE + (CMP_BLOCK - 1) <= t_h) & (c_i < n_cmp)
        psums = []
        for jk in range(n_str):
            j, k = divmod(jk, KVH)
            qk = qb_sc[pl.ds(jk * rows_h, rows_h), :]
            s = jnp.where(cmask, _dot_nt(qk, kc_ref[j, k].astype(BF16)), -jnp.inf)
            m = jnp.max(s, axis=1, keepdims=True)
            m = jnp.where(m > -jnp.inf, m, 0.0)
            e = jnp.exp(s - m)
            d = jnp.sum(e, axis=1, keepdims=True)
            p = e / jnp.where(d > 0, d, 1.0)
            ocmp_sc[pl.ds(jk * rows_h, rows_h), :] = _dot(p.astype(BF16), vc_ref[j, k].astype(BF16))
            psum = p[0:dl]
            for g in range(1, NSA_GROUP):
                psum = psum + p[g * dl:(g + 1) * dl]
            psums.append(psum)
        imp = jnp.dot(jnp.concatenate(psums, axis=0), cov_ref[...], preferred_element_type=F32,
                      precision=lax.Precision.HIGHEST)
        t_s = tok_pos(n_str * dl)
        n_i = lax.broadcasted_iota(jnp.int32, (1, nbl), 1)
        valid = (n_i * SLC_BLOCK <= t_s) & (n_i < n_slc)
        cur = t_s >> SLC_SHIFT
        forced = (n_i == 0) | (n_i == cur) | (n_i == cur - 1)
        prio = jnp.where(forced, jnp.inf, jnp.where(valid, imp, -jnp.inf))
        n_f = n_i.astype(F32)
        alive = jnp.broadcast_to(jnp.where(n_i < n_slc, 1.0, 0.0), prio.shape)
        sel = jnp.zeros(prio.shape, F32)
        for _ in range(n_sel):
            mx = jnp.max(jnp.where(alive > 0.5, prio, -jnp.inf), axis=1, keepdims=True)
            cand = (alive > 0.5) & (prio == mx)
            first = jnp.min(jnp.where(cand, n_f, float(nbl)), axis=1, keepdims=True)
            pick = n_f == first
            sel = jnp.where(pick, 1.0, sel)
            alive = jnp.where(pick, 0.0, alive)
        sel = jnp.where(valid, sel, 0.0)
        for jk in range(n_str):
            for g in range(NSA_GROUP):
                sel_sc[pl.ds((jk * NSA_GROUP + g) * dl, dl), :] = sel[jk * dl:(jk + 1) * dl]
        m_sc[...] = jnp.full(m_sc.shape, NEG_BIG, F32)
        l_sc[...] = jnp.zeros_like(l_sc)
        acc_sc[...] = jnp.zeros_like(acc_sc)

    selb = sel_sc[...].astype(BF16)
    n_col = lax.broadcasted_iota(jnp.int32, (nbl, 1), 0)

    def online_update(jk, s, mask, v_rows):
        rs = pl.ds(jk * rows_h, rows_h)
        m_old = m_sc[rs, :]
        m_new = jnp.maximum(m_old, jnp.max(s, axis=1, keepdims=True))
        p = jnp.where(mask, jnp.exp(s - m_new), 0.0)
        alpha = jnp.exp(m_old - m_new)
        l_sc[rs, :] = alpha * l_sc[rs, :] + jnp.sum(p, axis=1, keepdims=True)
        acc_sc[rs, :] = alpha * acc_sc[rs, :] + _dot(p.astype(BF16), v_rows)
        m_sc[rs, :] = m_new

    t_h = tok_pos(rows_h)

    def selected(kpos):
        blk = jnp.where(n_col == (kpos >> SLC_SHIFT), 1.0, 0.0).astype(BF16)
        return _dot(selb, blk)

    kpos = step * (pg * page) + lax.broadcasted_iota(jnp.int32, (1, pg * page), 1)
    sel_all = selected(kpos)
    for jk in range(n_str):
        j, k = divmod(jk, KVH)
        pages = pg_refs[j * pg:(j + 1) * pg]
        qk = qb_sc[pl.ds(jk * rows_h, rows_h), :]
        mask = (sel_all[jk * rows_h:(jk + 1) * rows_h] > 0.5) & (kpos <= t_h)
        k_rows = jnp.concatenate([_head_rows(r, k).astype(BF16) for r in pages], axis=0)
        v_rows = jnp.concatenate([_head_rows(r, KVH + k).astype(BF16) for r in pages], axis=0)
        s = jnp.where(mask, _dot_nt(qk, k_rows), NEG_BIG)
        online_update(jk, s, mask, v_rows)

    @pl.when(step == n_steps - 1)
    def _finish():
        padn = LANES - dl
        j_new = lax.broadcasted_iota(jnp.int32, (1, LANES), 1)
        kpos_n = q_off + j_new
        sel_n = selected(kpos_n)
        for jk in range(n_str):
            j, k = divmod(jk, KVH)
            tok = slice(j * dl, (j + 1) * dl)
            kn = jnp.concatenate([kvn_ref[tok, k * NSA_HD:(k + 1) * NSA_HD], jnp.zeros((padn, NSA_HD), F32)],
                                 axis=0)
            vn = jnp.concatenate([kvn_ref[tok, (KVH + k) * NSA_HD:(KVH + k + 1) * NSA_HD],
                                  jnp.zeros((padn, NSA_HD), F32)], axis=0)
            qk = qb_sc[pl.ds(jk * rows_h, rows_h), :]
            mask = (sel_n[jk * rows_h:(jk + 1) * rows_h] > 0.5) & (kpos_n <= t_h) & (j_new < dl)
            s = jnp.where(mask, _dot_nt(qk, kn.astype(BF16)), NEG_BIG)
            online_update(jk, s, mask, vn.astype(BF16))
        j_w = lax.broadcasted_iota(jnp.int32, (1, w_buf + LANES), 1)
        pos_w = q_off - w_buf + j_w
        dlt = t_h - pos_w
        wmask = (j_w < w_buf + dl) & (pos_w >= 0) & (dlt >= 0) & (dlt < WINDOW)
        gates = jax.nn.sigmoid(gate_ref[...])
        for jk in range(n_str):
            j, k = divmod(jk, KVH)
            tok = slice(j * dl, (j + 1) * dl)
            kw = jnp.concatenate([_head_rows(cw_refs[j], k),
                                  wn_ref[tok, k * NSA_HD:(k + 1) * NSA_HD], jnp.zeros((padn, NSA_HD), F32)], axis=0)
            vw = jnp.concatenate([_head_rows(cw_refs[j], KVH + k),
                                  wn_ref[tok, (KVH + k) * NSA_HD:(KVH + k + 1) * NSA_HD],
                                  jnp.zeros((padn, NSA_HD), F32)], axis=0)
            qk = qb_sc[pl.ds(jk * rows_h, rows_h), :]
            s = jnp.where(wmask, _dot_nt(qk, kw.astype(BF16)), -jnp.inf)
            m = jnp.max(s, axis=1, keepdims=True)
            m = jnp.where(m > -jnp.inf, m, 0.0)
            e = jnp.exp(s - m)
            d = jnp.sum(e, axis=1, keepdims=True)
            o_win = _dot((e / jnp.where(d > 0, d, 1.0)).astype(BF16), vw.astype(BF16))
            rs = pl.ds(jk * rows_h, rows_h)
            o_slc = acc_sc[rs, :] / l_sc[rs, :]
            o_cmp = ocmp_sc[rs, :]
            for g in range(NSA_GROUP):
                h = k * NSA_GROUP + g
                r = slice(g * dl, (g + 1) * dl)
                gc = gates[tok, h * N_BRANCH:h * N_BRANCH + 1]
                gs = gates[tok, h * N_BRANCH + 1:h * N_BRANCH + 2]
                gw = gates[tok, h * N_BRANCH + 2:h * N_BRANCH + 3]
                o_ref[tok, h * NSA_HD:(h + 1) * NSA_HD] = gc * o_cmp[r] + gs * o_slc[r] + gw * o_win[r]


def _nsa_sample(P, CMP, cache_kv, cache_win, layer, page_table, DB, DL, past_len, page):
    n_pages = page_table.shape[1]
    w_buf = cache_win.shape[2]
    lk = past_len + DL
    nch = CMP.shape[3]
    n_cmp = lk // CMP_STRIDE - CMP_BLOCK // CMP_STRIDE + 1
    n_slc = -(-lk // SLC_BLOCK)
    n_sel = min(N_SELECT, n_slc)
    nbl = _round_up(n_slc, LANES)
    pg = _pick_tile(n_pages, (32, 16, 8, 4, 2, 1))
    n_steps = n_pages // pg
    nb = _pick_tile(DB, (2, 1))
    assert DL % SUBLANES == 0 and DL <= LANES and page == LANES and past_len == n_pages * page
    cov = jnp.asarray(_cover_matrix(n_cmp, n_slc, nch, nbl))
    kvw = NSA_KV_HEADS * NSA_HD
    page_spec = lambda j, i: pl.BlockSpec((None, None, page, 2, NSA_KV_HEADS, NSA_HD),
                                          lambda b, s, pt: (layer, pt[b * nb + j, s * pg + i], 0, 1, 0, 0))
    win_spec = lambda j: pl.BlockSpec((None, None, w_buf, 2, NSA_KV_HEADS, NSA_HD),
                                      lambda b, s, pt: (layer, b * nb + j, 0, 0, 0, 0))
    tok = nb * DL
    in_specs = [
        pl.BlockSpec((tok, NSA_HEADS * NSA_HD), lambda b, s, pt: (b, 3)),
        pl.BlockSpec((tok, 2 * kvw), lambda b, s, pt: (b, 9)),
        pl.BlockSpec((tok, 2 * kvw), lambda b, s, pt: (b, 10)),
        pl.BlockSpec((tok, LANES), lambda b, s, pt: (b, GATE_CHUNK)),
        pl.BlockSpec((nb, None, NSA_KV_HEADS, nch, NSA_HD), lambda b, s, pt: (b, 0, 0, 0, 0)),
        pl.BlockSpec((nb, None, NSA_KV_HEADS, nch, NSA_HD), lambda b, s, pt: (b, 1, 0, 0, 0)),
        pl.BlockSpec((nch, nbl), lambda b, s, pt: (0, 0)),
    ] + [win_spec(j) for j in range(nb)] + [page_spec(j, i) for j in range(nb) for i in range(pg)]
    rows = nb * NSA_HEADS * DL
    grid_spec = pltpu.PrefetchScalarGridSpec(
        num_scalar_prefetch=1,
        grid=(DB // nb, n_steps),
        in_specs=in_specs,
        out_specs=pl.BlockSpec((tok, NSA_HEADS * NSA_HD), lambda b, s, pt: (b, 0)),
        scratch_shapes=[
            pltpu.VMEM((rows, NSA_HD), BF16),
            pltpu.VMEM((rows, nbl), F32),
            pltpu.VMEM((rows, 1), F32),
            pltpu.VMEM((rows, 1), F32),
            pltpu.VMEM((rows, NSA_HD), F32),
            pltpu.VMEM((rows, NSA_HD), F32),
        ],
    )
    body = functools.partial(_nsa_sample_body, nb=nb, pg=pg, n_steps=n_steps, dl=DL, q_off=past_len, w_buf=w_buf,
                             n_cmp=n_cmp, n_slc=n_slc, n_sel=n_sel, nbl=nbl, page=page)
    return pl.pallas_call(
        body, grid_spec=grid_spec,
        out_shape=jax.ShapeDtypeStruct((DB * DL, NSA_HEADS * NSA_HD), F32),
        compiler_params=_cparams(("parallel", "arbitrary")),
        name="nsa_sample",
    )(page_table, P, P, P, P, CMP, CMP, cov, *([cache_win] * nb), *([cache_kv] * (nb * pg)))


def _out_ln_body(h_ref, ro_ref, no_ref, wr_ref, wn_ref, g_ref, b_ref, o_ref, *, alpha):
    tm = h_ref.shape[0]
    half = tm // 2 if tm % (2 * BF16_ROWS) == 0 else tm
    for r0 in range(0, tm, half):
        rows = slice(r0, r0 + half)
        m = _dot(ro_ref[rows, :].astype(BF16), wr_ref[...]) + _dot(no_ref[rows, :].astype(BF16), wn_ref[...])
        o_ref[rows, :] = _layer_norm(alpha * h_ref[rows, :] + m, g_ref[...], b_ref[...])


def _out_ln(h, ro, no, w_out, g, b, alpha):
    T, D = h.shape
    kr = ro.shape[1]
    kn = no.shape[1]
    tm = _pick_tile(T, (512, 256, 128, 64, 32, 16, 8))
    return pl.pallas_call(
        functools.partial(_out_ln_body, alpha=alpha),
        grid=(T // tm,),
        in_specs=[
            pl.BlockSpec((tm, D), lambda i: (i, 0)),
            pl.BlockSpec((tm, kr), lambda i: (i, 0)),
            pl.BlockSpec((tm, kn), lambda i: (i, 0)),
            pl.BlockSpec((kr, D), lambda i: (0, 0)),
            pl.BlockSpec((kn, D), lambda i: (1, 0)),
            pl.BlockSpec((1, D), lambda i: (0, 0)),
            pl.BlockSpec((1, D), lambda i: (0, 0)),
        ],
        out_specs=pl.BlockSpec((tm, D), lambda i: (i, 0)),
        out_shape=jax.ShapeDtypeStruct((T, D), F32),
        compiler_params=_cparams(("parallel",)),
        name="out_ln",
    )(h, ro, no, w_out, w_out, g.reshape(1, D), b.reshape(1, D))


def _rope_tables(pos):
    half = NSA_HD // 2
    inv = ROPE_THETA ** (-jnp.arange(half, dtype=F32) / half)
    ang = pos.astype(F32)[:, None] * inv[None, :]
    cos = jnp.cos(ang)
    sin = jnp.sin(ang)
    return jnp.concatenate([cos, cos], -1), jnp.concatenate([-sin, sin], -1)


def _cmp_weights(w1, pos, w2):
    r = CMP_BLOCK // CMP_STRIDE
    w1r = w1.reshape(r, CMP_STRIDE * NSA_HD, CMP_HIDDEN)
    w1c = jnp.concatenate([w1r[i] for i in range(r)], axis=1).astype(BF16)
    posr = jnp.pad(pos.reshape(r, CMP_STRIDE * NSA_HD), ((0, SUBLANES - r), (0, 0))).astype(BF16)
    return w1c, posr, w2.astype(BF16)


def _layer_view(arr, l, shape):
    return arr.reshape(shape) if arr.shape[0] == 1 else arr[l].reshape(shape)


def _decoder_layer(x, B, L, q_off, s0, p, sample_ctx):
    alpha = p['alpha']
    h1, h1_bf = _ffn_ln(x, p['ffn1_w_up'], p['ffn1_w_down'], p['ln1_g'], p['ln1_b'], alpha, True)
    cos, sin = _rope_tables(q_off + jnp.arange(L, dtype=jnp.int32))
    P, kv_rows, win_rows = _proj(h1_bf, p['w_in'], p['rope_cols'], p['scale_cols'], cos, sin, L)
    ro, ret_s = _retention(P, s0, p['ret_gn_g'], p['ret_gn_b'], B, L)
    if sample_ctx is None:
        assert L % CMP_STRIDE == 0
        n_cmp = L // CMP_STRIDE - CMP_BLOCK // CMP_STRIDE + 1
        n_vec = 2 * NSA_KV_HEADS
        specs = [pl.BlockSpec((L, NSA_HD), (lambda v: (lambda b, g: (b, KV_CHUNK0 + v)))(v)) for v in range(n_vec)]
        CMP = _compress([P] * n_vec, specs, 1, L, 1, n_cmp, B, p['cmp_w1'], p['cmp_pos'], p['cmp_w2'])
        no = _nsa_prompt(P, CMP, B, L)
    else:
        cache_kv, cache_win, layer, page_table, past_len, page = sample_ctx
        n_pages = page_table.shape[1]
        lk = past_len + L
        assert (lk // CMP_STRIDE) * CMP_STRIDE <= past_len, "compression blocks must lie in the paged past"
        n_cmp = lk // CMP_STRIDE - CMP_BLOCK // CMP_STRIDE + 1
        n_in = _pick_tile(n_pages, (32, 16, 8, 4, 2, 1))
        n_grp = n_pages // n_in
        specs = [pl.BlockSpec((None, None, page, 2, NSA_KV_HEADS, NSA_HD),
                              (lambda i: (lambda b, g, pt: (layer, pt[b, (n_grp - 1 - g) * n_in + i], 0, 0, 0, 0)))(i))
                 for i in range(n_in)]
        CMP = _compress([cache_kv] * n_in, specs, n_in, page, n_grp, n_cmp, B,
                        p['cmp_w1'], p['cmp_pos'], p['cmp_w2'], page_table=page_table)
        no = _nsa_sample(P, CMP, cache_kv, cache_win, layer, page_table, B, L, past_len, page)
    x2 = _out_ln(h1, ro, no, p['w_out'], p['ln2_g'], p['ln2_b'], alpha)
    y, _ = _ffn_ln(x2, p['ffn2_w_up'], p['ffn2_w_down'], p['ln3_g'], p['ln3_b'], alpha, False)
    return y, ret_s, kv_rows, win_rows


def kernel(x_prompt, x_sample, state_ret, cache_nsa_kv, cache_win, page_table, ffn1_w_up, ffn1_w_down, ln1_g, ln1_b, w_in, w_out, ret_gn_g, ret_gn_b, cmp_pos_k, cmp_w1_k, cmp_w2_k, cmp_pos_v, cmp_w1_v, cmp_w2_v, ln2_g, ln2_b, ffn2_w_up, ffn2_w_down, ln3_g, ln3_b):
    B, L, D = x_prompt.shape
    DB, DL, _ = x_sample.shape
    depth = w_in.shape[0]
    n_pool, page = cache_nsa_kv.shape[1], cache_nsa_kv.shape[2]
    n_pages = page_table.shape[1]
    past_len = n_pages * page
    w_buf = cache_win.shape[2]
    alpha = (2.0 * depth) ** 0.25
    rope_np = np.zeros((N_IN_PAD // LANES, LANES), np.float32)
    rope_np[list(ROPE_CHUNKS)] = 1.0
    scale_np = np.ones((N_IN_PAD // LANES, LANES), np.float32)
    scale_np[list(KSCALE_CHUNKS)] = RET_DK ** -0.5
    rope_cols = jnp.asarray(rope_np.reshape(1, N_IN_PAD))
    scale_cols = jnp.asarray(scale_np.reshape(1, N_IN_PAD))

    yp = x_prompt.reshape(B * L, D)
    ys = x_sample.reshape(DB * DL, D)
    outs = [[] for _ in range(6)]
    for l in range(depth):
        k1, p1, k2 = _cmp_weights(cmp_w1_k[l], cmp_pos_k[l], cmp_w2_k[l])
        v1, q1, v2 = _cmp_weights(cmp_w1_v[l], cmp_pos_v[l], cmp_w2_v[l])
        p = {
            'alpha': alpha, 'rope_cols': rope_cols, 'scale_cols': scale_cols,
            'ffn1_w_up': ffn1_w_up[l].astype(BF16), 'ffn1_w_down': ffn1_w_down[l].astype(BF16),
            'ln1_g': ln1_g[l], 'ln1_b': ln1_b[l],
            'w_in': jnp.pad(w_in[l], ((0, 0), (0, N_IN_PAD - N_IN))).astype(BF16),
            'w_out': w_out[l].astype(BF16),
            'ret_gn_g': ret_gn_g[l], 'ret_gn_b': ret_gn_b[l],
            'cmp_w1': jnp.stack([k1, v1]), 'cmp_pos': jnp.stack([p1, q1]), 'cmp_w2': jnp.stack([k2, v2]),
            'ln2_g': ln2_g[l], 'ln2_b': ln2_b[l],
            'ffn2_w_up': ffn2_w_up[l].astype(BF16), 'ffn2_w_down': ffn2_w_down[l].astype(BF16),
            'ln3_g': ln3_g[l], 'ln3_b': ln3_b[l],
        }
        s0 = jnp.zeros((B, RET_HEADS, RET_DK, RET_DV), F32)
        yp, rs_p, kv_p, win_p = _decoder_layer(yp, B, L, 0, s0, p, None)
        ctx = (cache_nsa_kv, cache_win, l, page_table, past_len, page)
        ys, rs_s, kv_s, win_s = _decoder_layer(ys, DB, DL, past_len,
                                               _layer_view(state_ret, l, state_ret.shape[1:]), p, ctx)
        wl = min(WINDOW, L)
        outs[0].append(rs_p)
        outs[1].append(rs_s)
        outs[2].append(kv_p.reshape(B, L, 4, NSA_KV_HEADS, NSA_HD))
        outs[3].append(kv_s.reshape(DB, DL, 4, NSA_KV_HEADS, NSA_HD))
        outs[4].append(win_p.reshape(B, L, 2, NSA_KV_HEADS, NSA_HD)[:, L - wl:])
        win_s = win_s.reshape(DB, DL, 2, NSA_KV_HEADS, NSA_HD)
        outs[5].append(jnp.concatenate([cache_win[l], win_s], axis=1)[:, -w_buf:])
    return (yp.reshape(B, L, D), ys.reshape(DB, DL, D), jnp.stack(outs[0]), jnp.stack(outs[1]),
            jnp.stack(outs[2]), jnp.stack(outs[3]), jnp.stack(outs[4]), jnp.stack(outs[5]))
```

```python
import functools

import numpy as np
import jax
import jax.numpy as jnp
from jax import lax
from jax.experimental import pallas as pl
from jax.experimental.pallas import tpu as pltpu

F32 = jnp.float32
BF16 = jnp.bfloat16

LANES = 128
SUBLANES = 8
BF16_ROWS = 16
VMEM_LIMIT_BYTES = 56 * 1024 * 1024

RET_HEADS = 4
RET_DK = 128
RET_DV = 256
RET_CHUNK = 128
NSA_HEADS = 8
NSA_KV_HEADS = 2
NSA_HD = 128
NSA_GROUP = NSA_HEADS // NSA_KV_HEADS
CMP_BLOCK = 32
CMP_STRIDE = 16
CMP_HIDDEN = 2 * NSA_HD
SLC_BLOCK = 64
SLC_SHIFT = 6
N_SELECT = 16
WINDOW = 512
N_BRANCH = 3
ROPE_THETA = 10000.0
LN_EPS = 1e-5
NEG_BIG = -1e30
LOG2_E = 1.4426950408889634

N_IN = 5656
N_IN_PAD = 5760
KV_CHUNK0 = 32
WIN_CHUNK0 = 40
GATE_CHUNK = 44
ROPE_CHUNKS = tuple(range(0, 8)) + tuple(range(24, 32)) + (32, 33, 36, 37, 40, 41)
KSCALE_CHUNKS = tuple(range(4, 8))


def _cparams(sem):
    return pltpu.CompilerParams(dimension_semantics=sem, vmem_limit_bytes=VMEM_LIMIT_BYTES)


def _pick_tile(n, candidates):
    for c in candidates:
        if n % c == 0:
            return c
    return n


def _round_up(n, m):
    return (n + m - 1) // m * m


def _layer_norm(z, g, b):
    mu = jnp.mean(z, axis=-1, keepdims=True)
    zc = z - mu
    var = jnp.mean(zc * zc, axis=-1, keepdims=True)
    return zc * lax.rsqrt(var + LN_EPS) * g + b


def _dot(a, b):
    return jnp.dot(a, b, preferred_element_type=F32)


def _dot_nt(a, b):
    return lax.dot_general(a, b, (((1,), (1,)), ((), ())), preferred_element_type=F32)


def _ffn_ln_body(x_ref, wa_ref, wb_ref, wd_ref, g_ref, b_ref, *rest, nj, alpha, emit_bf16):
    if emit_bf16:
        o_ref, obf_ref, xbf_sc, acc_sc = rest
    else:
        o_ref, xbf_sc, acc_sc = rest
    j = pl.program_id(1)

    @pl.when(j == 0)
    def _init():
        xbf_sc[...] = x_ref[...].astype(BF16)
        acc_sc[...] = jnp.zeros_like(acc_sc)

    xb = xbf_sc[...]
    a = _dot(xb, wa_ref[...])
    b = _dot(xb, wb_ref[...])
    h = (a * jax.nn.sigmoid(a)) * b
    acc_sc[...] += _dot(h.astype(BF16), wd_ref[...])

    @pl.when(j == nj - 1)
    def _finish():
        z = alpha * x_ref[...] + 0.5 * acc_sc[...]
        y = _layer_norm(z, g_ref[...], b_ref[...])
        o_ref[...] = y
        if emit_bf16:
            obf_ref[...] = y.astype(BF16)


def _ffn_ln(x, w_up, w_down, g, b, alpha, emit_bf16):
    T, D = x.shape
    F = w_down.shape[0]
    tm = _pick_tile(T, (512, 256, 128, 64, 32, 16, 8))
    tf = _pick_tile(F, (512, 256, 128))
    nj = F // tf
    out_shape = [jax.ShapeDtypeStruct((T, D), F32)]
    out_specs = [pl.BlockSpec((tm, D), lambda i, j: (i, 0))]
    if emit_bf16:
        out_shape.append(jax.ShapeDtypeStruct((T, D), BF16))
        out_specs.append(pl.BlockSpec((tm, D), lambda i, j: (i, 0)))
    res = pl.pallas_call(
        functools.partial(_ffn_ln_body, nj=nj, alpha=alpha, emit_bf16=emit_bf16),
        grid=(T // tm, nj),
        in_specs=[
            pl.BlockSpec((tm, D), lambda i, j: (i, 0)),
            pl.BlockSpec((D, tf), lambda i, j: (0, j)),
            pl.BlockSpec((D, tf), lambda i, j: (0, nj + j)),
            pl.BlockSpec((tf, D), lambda i, j: (j, 0)),
            pl.BlockSpec((1, D), lambda i, j: (0, 0)),
            pl.BlockSpec((1, D), lambda i, j: (0, 0)),
        ],
        out_specs=out_specs,
        out_shape=out_shape,
        scratch_shapes=[pltpu.VMEM((tm, D), BF16), pltpu.VMEM((tm, D), F32)],
        compiler_params=_cparams(("parallel", "arbitrary")),
        name="ffn_ln",
    )(x, w_up, w_up, w_down, g.reshape(1, D), b.reshape(1, D))
    return res if emit_bf16 else (res[0], None)


def _rows_view(ref, j):
    rows = ref.shape[0]
    n = int(np.prod(ref.shape[1:-1]))
    return ref.reshape(n * rows, ref.shape[-1]), pl.ds(j, rows, stride=n)


def _head_rows(ref, j):
    view, idx = _rows_view(ref, j)
    return view[idx, :]


def _store_rows(ref, j, val):
    view, idx = _rows_view(ref, j)
    view[idx, :] = val


def _proj_body(x_ref, w_ref, cos_ref, sin_ref, rope_ref, scale_ref, o_ref, kv_ref, win_ref, *, n_chunk, j_rows):
    y = _dot(x_ref[...], w_ref[...])
    cos = cos_ref[...]
    sin = sin_ref[...]
    for c in range(n_chunk):
        sl = slice(c * LANES, (c + 1) * LANES)
        yc = y[:, sl]
        roped = yc * cos + pltpu.roll(yc, NSA_HD // 2, 1) * sin
        o_ref[:, sl] = jnp.where(rope_ref[:, sl] > 0.5, roped, yc) * scale_ref[:, sl]

    c0 = KV_CHUNK0 - j_rows * n_chunk
    for r in range(WIN_CHUNK0 - KV_CHUNK0):
        _store_rows(kv_ref, r, o_ref[:, (c0 + r) * LANES:(c0 + r + 1) * LANES])
    c0 = WIN_CHUNK0 - j_rows * n_chunk
    for r in range(GATE_CHUNK - WIN_CHUNK0):
        _store_rows(win_ref, r, o_ref[:, (c0 + r) * LANES:(c0 + r + 1) * LANES])


def _proj(x_bf, w_bf, rope_cols, scale_cols, cos, sin, rows_per_seq):
    T, D = x_bf.shape
    N = w_bf.shape[1]
    tm = _pick_tile(T, (512, 256, 128, 64, 32, 16, 8))
    tn = 1920
    assert N % tn == 0
    n_chunk = tn // LANES
    if rows_per_seq >= tm:
        assert rows_per_seq % tm == 0
        n_tab = rows_per_seq // tm
    else:
        assert tm % rows_per_seq == 0
        cos = jnp.tile(cos, (tm // rows_per_seq, 1))
        sin = jnp.tile(sin, (tm // rows_per_seq, 1))
        n_tab = 1
    j_rows = KV_CHUNK0 // n_chunk
    assert (GATE_CHUNK - 1) // n_chunk == j_rows == N // tn - 1, "cache-row columns must sit in the last column tile"
    kv_shape = (T, 4, NSA_KV_HEADS, NSA_HD)
    win_shape = (T, 2, NSA_KV_HEADS, NSA_HD)
    return pl.pallas_call(
        functools.partial(_proj_body, n_chunk=n_chunk, j_rows=j_rows),
        grid=(T // tm, N // tn),
        in_specs=[
            pl.BlockSpec((tm, D), lambda i, j: (i, 0)),
            pl.BlockSpec((D, tn), lambda i, j: (0, j)),
            pl.BlockSpec((tm, LANES), lambda i, j: (i % n_tab, 0)),
            pl.BlockSpec((tm, LANES), lambda i, j: (i % n_tab, 0)),
            pl.BlockSpec((1, tn), lambda i, j: (0, j)),
            pl.BlockSpec((1, tn), lambda i, j: (0, j)),
        ],
        out_specs=[
            pl.BlockSpec((tm, tn), lambda i, j: (i, j)),
            pl.BlockSpec((tm,) + kv_shape[1:], lambda i, j: (i, 0, 0, 0)),
            pl.BlockSpec((tm,) + win_shape[1:], lambda i, j: (i, 0, 0, 0)),
        ],
        out_shape=[
            jax.ShapeDtypeStruct((T, N), F32),
            jax.ShapeDtypeStruct(kv_shape, F32),
            jax.ShapeDtypeStruct(win_shape, F32),
        ],
        compiler_params=_cparams(("parallel", "arbitrary")),
        name="proj_rope",
    )(x_bf, w_bf, cos, sin, rope_cols, scale_cols)


def _ret_body(q_ref, k_ref, v_ref, g_ref, s0_ref, dm_ref, ind_ref, std_ref, cd_ref, gng_ref, gnb_ref,
              o_ref, sout_ref, s_sc, *, nc, rows, rows_pad, per_step):
    c = pl.program_id(1)

    @pl.when(c == 0)
    def _load_state():
        s_sc[...] = s0_ref[...]

    pad = rows_pad - rows
    for sub in range(per_step):
        rs = slice(sub * rows, (sub + 1) * rows)
        for h in range(RET_HEADS):
            ks = slice(h * RET_DK, (h + 1) * RET_DK)
            vs = slice(h * RET_DV, (h + 1) * RET_DV)
            q = q_ref[rs, ks]
            k = k_ref[rs, ks]
            v = v_ref[rs, vs]
            kd = k * std_ref[h]
            if pad:
                k = jnp.concatenate([k, jnp.zeros((pad, RET_DK), F32)], axis=0)
                kd = jnp.concatenate([kd, jnp.zeros((pad, RET_DK), F32)], axis=0)
                v = jnp.concatenate([v, jnp.zeros((pad, RET_DV), F32)], axis=0)
            s_old = s_sc[h]
            vb = v.astype(BF16)
            a = _dot_nt(q.astype(BF16), k.astype(BF16)) * dm_ref[h]
            o = _dot(a.astype(BF16), vb) + _dot((q * ind_ref[h]).astype(BF16), s_old.astype(BF16))
            s_sc[h] = s_old * cd_ref[h] + _dot(kd.T.astype(BF16), vb)
            mu = jnp.mean(o, axis=-1, keepdims=True)
            oc = o - mu
            var = jnp.mean(oc * oc, axis=-1, keepdims=True)
            on = oc * lax.rsqrt(var + LN_EPS) * gng_ref[:, vs] + gnb_ref[:, vs]
            gate = g_ref[rs, vs]
            o_ref[rs, vs] = (gate * jax.nn.sigmoid(gate)) * on

    @pl.when(c == nc - 1)
    def _store_state():
        sout_ref[...] = s_sc[...]


def _retention(P, s0, gn_g, gn_b, B, L):
    C = RET_CHUNK if L % RET_CHUNK == 0 else L
    nc = L // C
    CP = max(C, LANES)
    lg = jnp.log1p(-jnp.exp2(-5.0 - jnp.arange(RET_HEADS, dtype=F32)))
    i = jnp.arange(C, dtype=F32)
    diff = i[:, None] - i[None, :]
    dmask = jnp.where(diff >= 0, jnp.exp(lg[:, None, None] * jnp.maximum(diff, 0.0)), 0.0)
    dmask = jnp.pad(dmask, ((0, 0), (0, 0), (0, CP - C)))
    in_decay = jnp.broadcast_to(jnp.exp(lg[:, None] * (i + 1.0))[:, :, None], (RET_HEADS, C, RET_DK))
    st_decay = jnp.broadcast_to(jnp.exp(lg[:, None] * (C - 1.0 - i))[:, :, None], (RET_HEADS, C, RET_DK))
    chunk_decay = jnp.broadcast_to(jnp.exp(lg * C)[:, None, None], (RET_HEADS, 1, RET_DV))
    qw = RET_HEADS * RET_DK
    vw = RET_HEADS * RET_DV
    const3 = lambda b, c: (0, 0, 0)
    per_step = _pick_tile(nc, (8, 4, 2, 1))
    nc = nc // per_step
    CS = C * per_step
    return pl.pallas_call(
        functools.partial(_ret_body, nc=nc, rows=C, rows_pad=CP, per_step=per_step),
        grid=(B, nc),
        in_specs=[
            pl.BlockSpec((CS, qw), lambda b, c: (b * nc + c, 0)),
            pl.BlockSpec((CS, qw), lambda b, c: (b * nc + c, 1)),
            pl.BlockSpec((CS, vw), lambda b, c: (b * nc + c, 1)),
            pl.BlockSpec((CS, vw), lambda b, c: (b * nc + c, 2)),
            pl.BlockSpec((None, RET_HEADS, RET_DK, RET_DV), lambda b, c: (b, 0, 0, 0)),
            pl.BlockSpec((RET_HEADS, C, CP), const3),
            pl.BlockSpec((RET_HEADS, C, RET_DK), const3),
            pl.BlockSpec((RET_HEADS, C, RET_DK), const3),
            pl.BlockSpec((RET_HEADS, 1, RET_DV), const3),
            pl.BlockSpec((1, vw), lambda b, c: (0, 0)),
            pl.BlockSpec((1, vw), lambda b, c: (0, 0)),
        ],
        out_specs=[
            pl.BlockSpec((CS, vw), lambda b, c: (b * nc + c, 0)),
            pl.BlockSpec((None, RET_HEADS, RET_DK, RET_DV), lambda b, c: (b, 0, 0, 0)),
        ],
        out_shape=[
            jax.ShapeDtypeStruct((B * L, vw), F32),
            jax.ShapeDtypeStruct((B, RET_HEADS, RET_DK, RET_DV), F32),
        ],
        scratch_shapes=[pltpu.VMEM((RET_HEADS, RET_DK, RET_DV), F32)],
        compiler_params=_cparams(("parallel", "arbitrary")),
        name="retention",
    )(P, P, P, P, s0, dmask, in_decay, st_decay, chunk_decay, gn_g.reshape(1, vw), gn_b.reshape(1, vw))


def _cmp_body(*refs, n_in, rows, n_grp, n_cmp, paged):
    n_vec = 2 * NSA_KV_HEADS
    if paged:
        refs = refs[1:]
    n_src = n_in if paged else n_vec * n_in
    x_refs = refs[:n_src]
    w1_ref, pos_ref, w2_ref, o_ref, carry_sc, xc_sc = refs[n_src:]
    g = pl.program_id(1)
    grp = n_grp - 1 - g
    cpi = rows // CMP_STRIDE
    M = n_in * cpi

    @pl.when(g == 0)
    def _init():
        carry_sc[...] = jnp.zeros_like(carry_sc)

    row = lax.broadcasted_iota(jnp.int32, (M, 1), 0)
    grp_in = 1 if cpi % BF16_ROWS == 0 else BF16_ROWS // cpi
    assert n_in % grp_in == 0 and (grp_in * cpi) % BF16_ROWS == 0
    for t in range(2):
        w1 = w1_ref[t]
        gp = _dot(pos_ref[t], w1)
        posterm = gp[0:1, :CMP_HIDDEN] + gp[1:2, CMP_HIDDEN:]
        for hd in range(NSA_KV_HEADS):
            v = t * NSA_KV_HEADS + hd
            for i0 in range(0, n_in, grp_in):
                if paged:
                    parts = [pltpu.einshape("csd->scd",
                                            _head_rows(x_refs[i], v).reshape(cpi, CMP_STRIDE, NSA_HD))
                             for i in range(i0, i0 + grp_in)]
                    piece = lambda s: jnp.concatenate([xt[s] for xt in parts], axis=0)
                else:
                    refs_i = [x_refs[v * n_in + i] for i in range(i0, i0 + grp_in)]
                    piece = lambda s: jnp.concatenate(
                        [r[pl.ds(s, cpi, stride=CMP_STRIDE), :] for r in refs_i], axis=0)
                r0 = v * M + i0 * cpi
                for s in range(CMP_STRIDE):
                    xc_sc[r0:r0 + grp_in * cpi, s * NSA_HD:(s + 1) * NSA_HD] = piece(s).astype(BF16)
        t0 = t * NSA_KV_HEADS * M
        gg_all = _dot(xc_sc[t0:t0 + NSA_KV_HEADS * M, :], w1)
        for hd in range(NSA_KV_HEADS):
            v = t * NSA_KV_HEADS + hd
            gg = gg_all[hd * M:(hd + 1) * M]
            g0 = gg[:, :CMP_HIDDEN]
            g1 = gg[:, CMP_HIDDEN:]
            nxt = pltpu.roll(g1, M - 1, 0)
            nxt = jnp.where(row == M - 1, carry_sc[v][0:1, :], nxt)
            carry_sc[v] = g1[0:SUBLANES, :]
            hid = g0 + nxt + posterm
            out = _dot(jax.nn.gelu(hid).astype(BF16), w2_ref[t])
            o_ref[t, hd] = jnp.where(grp * M + row < n_cmp, out, 0.0)


def _compress(srcs, src_specs, n_in, rows, n_grp, n_cmp, B, w1, pos, w2, page_table=None):
    n_vec = 2 * NSA_KV_HEADS
    M = n_in * rows // CMP_STRIDE
    nch = n_grp * M
    paged = page_table is not None
    const3 = lambda *a: (0, 0, 0)
    in_specs = list(src_specs) + [
        pl.BlockSpec((2, CMP_STRIDE * NSA_HD, 2 * CMP_HIDDEN), const3),
        pl.BlockSpec((2, SUBLANES, CMP_STRIDE * NSA_HD), const3),
        pl.BlockSpec((2, CMP_HIDDEN, NSA_HD), const3),
    ]
    out_spec = pl.BlockSpec((None, 2, NSA_KV_HEADS, M, NSA_HD), lambda *a: (a[0], 0, 0, n_grp - 1 - a[1], 0))
    body = functools.partial(_cmp_body, n_in=n_in, rows=rows, n_grp=n_grp, n_cmp=n_cmp, paged=paged)
    out_shape = jax.ShapeDtypeStruct((B, 2, NSA_KV_HEADS, nch, NSA_HD), F32)
    scratch = [pltpu.VMEM((n_vec, SUBLANES, CMP_HIDDEN), F32),
               pltpu.VMEM((n_vec * M, CMP_STRIDE * NSA_HD), BF16)]
    sem = ("parallel", "arbitrary")
    if paged:
        grid_spec = pltpu.PrefetchScalarGridSpec(
            num_scalar_prefetch=1, grid=(B, n_grp), in_specs=in_specs, out_specs=out_spec,
            scratch_shapes=scratch)
        return pl.pallas_call(body, grid_spec=grid_spec, out_shape=out_shape,
                              compiler_params=_cparams(sem), name="nsa_compress_paged")(
            page_table, *srcs, w1, pos, w2)
    return pl.pallas_call(body, grid=(B, n_grp), in_specs=in_specs, out_specs=out_spec,
                          out_shape=out_shape, scratch_shapes=scratch,
                          compiler_params=_cparams(sem), name="nsa_compress")(*srcs, w1, pos, w2)


def _cover_matrix(n_cmp, n_slc, rows, cols):
    c_i = np.arange(n_cmp)[:, None]
    n_i = np.arange(n_slc)[None, :]
    cov = np.clip(np.minimum(c_i * CMP_STRIDE + CMP_BLOCK, (n_i + 1) * SLC_BLOCK)
                  - np.maximum(c_i * CMP_STRIDE, n_i * SLC_BLOCK), 0, None).astype(np.float32) / CMP_BLOCK
    out = np.zeros((rows, cols), np.float32)
    out[:n_cmp, :n_slc] = cov
    return out


def _nsa_prompt_body(q_ref, kc_ref, vc_ref, ks_ref, vs_ref, kw_ref, vw_ref, gate_ref, covt_ref, o_ref,
                     kcb_sc, vct_sc, ksb_sc, vst_sc, kwb_sc, vwt_sc, prio_sc, sel_sc, gt_sc,
                     m_sc, l_sc, acc_sc, out_sc, s_sc, *, tq, L, n_cmp, n_slc, n_sel, nbp):
    KV = NSA_KV_HEADS
    qi = pl.program_id(1)
    tk = tq
    nch = kc_ref.shape[1]
    scale = NSA_HD ** -0.5

    @pl.when(qi == 0)
    def _stage_kv():
        for kv in range(KV):
            cs = slice(kv * NSA_HD, (kv + 1) * NSA_HD)
            kcb_sc[kv] = kc_ref[kv].astype(BF16)
            vct_sc[kv] = vc_ref[kv].T.astype(BF16)
            ksb_sc[kv] = ks_ref[:, cs].astype(BF16)
            kwb_sc[kv] = kw_ref[:, cs].astype(BF16)
            for i in range(L // tk):
                vst_sc[kv, i] = vs_ref[i * tk:(i + 1) * tk, cs].T.astype(BF16)
                vwt_sc[kv, i] = vw_ref[i * tk:(i + 1) * tk, cs].T.astype(BF16)

    G = NSA_GROUP
    W = G * tq
    t0 = qi * tq
    t_row = t0 + lax.broadcasted_iota(jnp.int32, (1, tq), 1)
    lane = lax.broadcasted_iota(jnp.int32, (1, W), 1)
    t_all = t0 + (lane & (tq - 1))
    qcats = []
    for kv in range(KV):
        qc = jnp.concatenate([q_ref[:, (kv * G + g) * NSA_HD:(kv * G + g + 1) * NSA_HD] for g in range(G)], axis=0)
        qcats.append((qc * (scale * LOG2_E)).astype(BF16))
    gt_sc[...] = jax.nn.sigmoid(gate_ref[...]).T

    def gate_row(kv, br):
        return jnp.concatenate(
            [gt_sc[(kv * G + g) * N_BRANCH + br:(kv * G + g) * N_BRANCH + br + 1, :] for g in range(G)], axis=1)

    kp_l = lax.broadcasted_iota(jnp.int32, (tk, 1), 0)
    t_l = lane & (tq - 1)
    causal_bias = jnp.where(kp_l <= t_l, 0.0, NEG_BIG)
    far_bias = jnp.where(kp_l > t_l, 0.0, NEG_BIG)

    def reset(kv):
        m_sc[kv] = jnp.full((1, W), NEG_BIG, F32)
        l_sc[kv] = jnp.zeros((1, W), F32)
        acc_sc[kv] = jnp.zeros((NSA_HD, W), F32)

    def score(kb_sc, kv, kt, bias):
        k0 = pl.multiple_of(kt * tk, tk)
        s = _dot_nt(kb_sc[kv, pl.ds(k0, tk), :], qcats[kv])
        return s if bias is None else s + bias

    def update(vt_sc, kv, kt, s):
        m_old = m_sc[kv]
        m_new = jnp.maximum(m_old, jnp.max(s, axis=0, keepdims=True))
        p = jnp.exp2(s - m_new)
        alpha = jnp.exp2(m_old - m_new)
        l_sc[kv] = alpha * l_sc[kv] + jnp.sum(p, axis=0, keepdims=True)
        acc_sc[kv] = alpha * acc_sc[kv] + _dot(vt_sc[kv, kt], p.astype(BF16))
        m_sc[kv] = m_new

    def branch_out(kv, br):
        return (gate_row(kv, br) / l_sc[kv]) * acc_sc[kv]

    far = WINDOW // tk
    for kv in range(KV):
        reset(kv)
        tiles = []
        for back in range(far, -1, -1):
            bias = far_bias if back == far else (causal_bias if back == 0 else None)
            if back > 0:
                off = jnp.where(qi >= back, 0.0, NEG_BIG).astype(F32)
                bias = off if bias is None else bias + off
            kt = jnp.maximum(qi - back, 0)
            tiles.append((kt, score(kwb_sc, kv, kt, bias)))
        for kt, s_w in tiles:
            update(vwt_sc, kv, kt, s_w)
        out_sc[kv] = branch_out(kv, 2)

    c_i = lax.broadcasted_iota(jnp.int32, (nch, 1), 0)
    c_end = jnp.where(c_i < n_cmp, c_i * CMP_STRIDE + (CMP_BLOCK - 1), L)
    n_i = lax.broadcasted_iota(jnp.int32, (nbp, 1), 0)
    valid = (n_i * SLC_BLOCK <= t_row) & (n_i < n_slc)
    cur = t_row >> SLC_SHIFT
    forced = (n_i == 0) | (n_i == cur) | (n_i == cur - 1)
    for kv in range(KV):
        s = jnp.where(c_end <= t_all, _dot_nt(kcb_sc[kv], qcats[kv]), -jnp.inf)
        m = jnp.max(s, axis=0, keepdims=True)
        m = jnp.where(m > -jnp.inf, m, 0.0)
        e = jnp.exp2(s - m)
        d = jnp.sum(e, axis=0, keepdims=True)
        p = e / jnp.where(d > 0, d, 1.0)
        out_sc[kv] += gate_row(kv, 0) * _dot(vct_sc[kv], p.astype(BF16))
        psum = p[:, 0:tq]
        for g in range(1, G):
            psum = psum + p[:, g * tq:(g + 1) * tq]
        imp = jnp.dot(covt_ref[...], psum, preferred_element_type=F32, precision=lax.Precision.HIGHEST)
        prio = jnp.where(forced, jnp.inf, jnp.where(valid, imp, -jnp.inf))
        prio_sc[kv] = prio
        cnt = jnp.zeros((nbp, tq), jnp.int32)
        for mm in range(n_slc):
            pm = prio_sc[kv, mm:mm + 1, :]
            tie = jnp.where(n_i > mm, 1, 0)
            cnt = cnt + jnp.where(pm > prio, 1, jnp.where(pm == prio, tie, 0))
        selbias = jnp.where((cnt < n_sel) & valid, 0.0, NEG_BIG)
        sel_sc[kv] = jnp.concatenate([selbias] * G, axis=1)

    def sel_bias(kv, kt):
        per_tile = tk // SLC_BLOCK
        rows = [jnp.broadcast_to(sel_sc[kv, pl.ds(kt * per_tile + r, 1), :], (SLC_BLOCK, W))
                for r in range(per_tile)]
        return jnp.concatenate(rows, axis=0)

    for kv in range(KV):
        reset(kv)
        s_sc[kv] = score(ksb_sc, kv, 0, sel_bias(kv, 0))

    def slc_body(kt, carry):
        nxt = [score(ksb_sc, kv, kt + 1, sel_bias(kv, kt + 1)) for kv in range(KV)]
        for kv in range(KV):
            update(vst_sc, kv, kt, s_sc[kv])
        for kv in range(KV):
            s_sc[kv] = nxt[kv]
        return carry

    lax.fori_loop(0, qi, slc_body, 0)
    for kv in range(KV):
        update(vst_sc, kv, qi, s_sc[kv] + causal_bias)
        out_sc[kv] += branch_out(kv, 1)
        for g in range(G):
            h = kv * G + g
            o_ref[:, h * NSA_HD:(h + 1) * NSA_HD] = out_sc[kv, :, g * tq:(g + 1) * tq].T


def _nsa_prompt(P, CMP, B, L):
    tq = _pick_tile(L, (256, 128))
    assert L % tq == 0 and tq % SLC_BLOCK == 0 and WINDOW % tq == 0 and tq & (tq - 1) == 0
    nq = L // tq
    nch = CMP.shape[3]
    n_cmp = L // CMP_STRIDE - CMP_BLOCK // CMP_STRIDE + 1
    n_slc = -(-L // SLC_BLOCK)
    n_sel = min(N_SELECT, n_slc)
    nbp = _round_up(n_slc, SUBLANES)
    covt = jnp.asarray(_cover_matrix(n_cmp, n_slc, nch, nbp).T)
    qw = NSA_HEADS * NSA_HD
    kvw = NSA_KV_HEADS * NSA_HD
    KV = NSA_KV_HEADS
    W = NSA_GROUP * tq
    kv_col = lambda c: (lambda b, i: (b, (KV_CHUNK0 + c) // NSA_KV_HEADS))
    body = functools.partial(_nsa_prompt_body, tq=tq, L=L, n_cmp=n_cmp, n_slc=n_slc, n_sel=n_sel, nbp=nbp)
    return pl.pallas_call(
        body,
        grid=(B, nq),
        in_specs=[
            pl.BlockSpec((tq, qw), lambda b, i: (b * nq + i, 3)),
            pl.BlockSpec((None, None, KV, nch, NSA_HD), lambda b, i: (b, 0, 0, 0, 0)),
            pl.BlockSpec((None, None, KV, nch, NSA_HD), lambda b, i: (b, 1, 0, 0, 0)),
            pl.BlockSpec((L, kvw), kv_col(4)),
            pl.BlockSpec((L, kvw), kv_col(6)),
            pl.BlockSpec((L, kvw), kv_col(8)),
            pl.BlockSpec((L, kvw), kv_col(10)),
            pl.BlockSpec((tq, LANES), lambda b, i: (b * nq + i, GATE_CHUNK)),
            pl.BlockSpec((nbp, nch), lambda b, i: (0, 0)),
        ],
        out_specs=pl.BlockSpec((tq, qw), lambda b, i: (b * nq + i, 0)),
        out_shape=jax.ShapeDtypeStruct((B * L, qw), F32),
        scratch_shapes=[
            pltpu.VMEM((KV, nch, NSA_HD), BF16),
            pltpu.VMEM((KV, NSA_HD, nch), BF16),
            pltpu.VMEM((KV, L, NSA_HD), BF16),
            pltpu.VMEM((KV, L // tq, NSA_HD, tq), BF16),
            pltpu.VMEM((KV, L, NSA_HD), BF16),
            pltpu.VMEM((KV, L // tq, NSA_HD, tq), BF16),
            pltpu.VMEM((KV, nbp, tq), F32),
            pltpu.VMEM((KV, nbp, W), F32),
            pltpu.VMEM((LANES, tq), F32),
            pltpu.VMEM((KV, 1, W), F32),
            pltpu.VMEM((KV, 1, W), F32),
            pltpu.VMEM((KV, NSA_HD, W), F32),
            pltpu.VMEM((KV, NSA_HD, W), F32),
            pltpu.VMEM((KV, tq, W), F32),
        ],
        compiler_params=_cparams(("parallel", "arbitrary")),
        name="nsa_prompt",
    )(P, CMP, CMP, P, P, P, P, P, covt)


def _nsa_sample_body(*refs, nb, pg, n_steps, dl, q_off, w_buf, n_cmp, n_slc, n_sel, nbl, page):
    refs = refs[1:]
    q_ref, kvn_ref, wn_ref, gate_ref, kc_ref, vc_ref, cov_ref = refs[:7]
    cw_refs = refs[7:7 + nb]
    pg_refs = refs[7 + nb:7 + nb + nb * pg]
    o_ref, qb_sc, sel_sc, m_sc, l_sc, acc_sc, ocmp_sc = refs[7 + nb + nb * pg:]
    step = pl.program_id(1)
    KVH = NSA_KV_HEADS
    n_str = nb * KVH
    rows_h = NSA_GROUP * dl
    scale = NSA_HD ** -0.5
    nch = kc_ref.shape[2]

    def tok_pos(n):
        r = lax.broadcasted_iota(jnp.int32, (n, 1), 0)
        return q_off + r % dl

    @pl.when(step == 0)
    def _select():
        for jk in range(n_str):
            j, k = divmod(jk, KVH)
            for g in range(NSA_GROUP):
                h = k * NSA_GROUP + g
                qb_sc[pl.ds((jk * NSA_GROUP + g) * dl, dl), :] = (
                    q_ref[j * dl:(j + 1) * dl, h * NSA_HD:(h + 1) * NSA_HD] * scale).astype(BF16)
        t_h = tok_pos(rows_h)
        c_i = lax.broadcasted_iota(jnp.int32, (1, nch), 1)
        cmask = (c_i * CMP_STRIDE + (CMP_BLOCK - 1) <= t_h) & (c_i < n_cmp)
        psums = []
        for jk in range(n_str):
            j, k = divmod(jk, KVH)
            qk = qb_sc[pl.ds(jk * rows_h, rows_h), :]
            s = jnp.where(cmask, _dot_nt(qk, kc_ref[j, k].astype(BF16)), -jnp.inf)
            m = jnp.max(s, axis=1, keepdims=True)
            m = jnp.where(m > -jnp.inf, m, 0.0)
            e = jnp.exp(s - m)
            d = jnp.sum(e, axis=1, keepdims=True)
            p = e / jnp.where(d > 0, d, 1.0)
            ocmp_sc[pl.ds(jk * rows_h, rows_h), :] = _dot(p.astype(BF16), vc_ref[j, k].astype(BF16))
            psum = p[0:dl]
            for g in range(1, NSA_GROUP):
                psum = psum + p[g * dl:(g + 1) * dl]
            psums.append(psum)
        imp = jnp.dot(jnp.concatenate(psums, axis=0), cov_ref[...], preferred_element_type=F32,
                      precision=lax.Precision.HIGHEST)
        t_s = tok_pos(n_str * dl)
        n_i = lax.broadcasted_iota(jnp.int32, (1, nbl), 1)
        valid = (n_i * SLC_BLOCK <= t_s) & (n_i < n_slc)
        cur = t_s >> SLC_SHIFT
        forced = (n_i == 0) | (n_i == cur) | (n_i == cur - 1)
        prio = jnp.where(forced, jnp.inf, jnp.where(valid, imp, -jnp.inf))
        n_f = n_i.astype(F32)
        alive = jnp.broadcast_to(jnp.where(n_i < n_slc, 1.0, 0.0), prio.shape)
        sel = jnp.zeros(prio.shape, F32)
        for _ in range(n_sel):
            mx = jnp.max(jnp.where(alive > 0.5, prio, -jnp.inf), axis=1, keepdims=True)
            cand = (alive > 0.5) & (prio == mx)
            first = jnp.min(jnp.where(cand, n_f, float(nbl)), axis=1, keepdims=True)
            pick = n_f == first
            sel = jnp.where(pick, 1.0, sel)
            alive = jnp.where(pick, 0.0, alive)
        sel = jnp.where(valid, sel, 0.0)
        for jk in range(n_str):
            for g in range(NSA_GROUP):
                sel_sc[pl.ds((jk * NSA_GROUP + g) * dl, dl), :] = sel[jk * dl:(jk + 1) * dl]
        m_sc[...] = jnp.full(m_sc.shape, NEG_BIG, F32)
        l_sc[...] = jnp.zeros_like(l_sc)
        acc_sc[...] = jnp.zeros_like(acc_sc)

    selb = sel_sc[...].astype(BF16)
    n_col = lax.broadcasted_iota(jnp.int32, (nbl, 1), 0)

    def online_update(jk, s, mask, v_rows):
        rs = pl.ds(jk * rows_h, rows_h)
        m_old = m_sc[rs, :]
        m_new = jnp.maximum(m_old, jnp.max(s, axis=1, keepdims=True))
        p = jnp.where(mask, jnp.exp(s - m_new), 0.0)
        alpha = jnp.exp(m_old - m_new)
        l_sc[rs, :] = alpha * l_sc[rs, :] + jnp.sum(p, axis=1, keepdims=True)
        acc_sc[rs, :] = alpha * acc_sc[rs, :] + _dot(p.astype(BF16), v_rows)
        m_sc[rs, :] = m_new

    t_h = tok_pos(rows_h)

    def selected(kpos):
        blk = jnp.where(n_col == (kpos >> SLC_SHIFT), 1.0, 0.0).astype(BF16)
        return _dot(selb, blk)

    kpos = step * (pg * page) + lax.broadcasted_iota(jnp.int32, (1, pg * page), 1)
    sel_all = selected(kpos)
    for jk in range(n_str):
        j, k = divmod(jk, KVH)
        pages = pg_refs[j * pg:(j + 1) * pg]
        qk = qb_sc[pl.ds(jk * rows_h, rows_h), :]
        mask = (sel_all[jk * rows_h:(jk + 1) * rows_h] > 0.5) & (kpos <= t_h)
        k_rows = jnp.concatenate([_head_rows(r, k).astype(BF16) for r in pages], axis=0)
        v_rows = jnp.concatenate([_head_rows(r, KVH + k).astype(BF16) for r in pages], axis=0)
        s = jnp.where(mask, _dot_nt(qk, k_rows), NEG_BIG)
        online_update(jk, s, mask, v_rows)

    @pl.when(step == n_steps - 1)
    def _finish():
        padn = LANES - dl
        j_new = lax.broadcasted_iota(jnp.int32, (1, LANES), 1)
        kpos_n = q_off + j_new
        sel_n = selected(kpos_n)
        for jk in range(n_str):
            j, k = divmod(jk, KVH)
            tok = slice(j * dl, (j + 1) * dl)
            kn = jnp.concatenate([kvn_ref[tok, k * NSA_HD:(k + 1) * NSA_HD], jnp.zeros((padn, NSA_HD), F32)],
                                 axis=0)
            vn = jnp.concatenate([kvn_ref[tok, (KVH + k) * NSA_HD:(KVH + k + 1) * NSA_HD],
                                  jnp.zeros((padn, NSA_HD), F32)], axis=0)
            qk = qb_sc[pl.ds(jk * rows_h, rows_h), :]
            mask = (sel_n[jk * rows_h:(jk + 1) * rows_h] > 0.5) & (kpos_n <= t_h) & (j_new < dl)
            s = jnp.where(mask, _dot_nt(qk, kn.astype(BF16)), NEG_BIG)
            online_update(jk, s, mask, vn.astype(BF16))
        j_w = lax.broadcasted_iota(jnp.int32, (1, w_buf + LANES), 1)
        pos_w = q_off - w_buf + j_w
        dlt = t_h - pos_w
        wmask = (j_w < w_buf + dl) & (pos_w >= 0) & (dlt >= 0) & (dlt < WINDOW)
        gates = jax.nn.sigmoid(gate_ref[...])
        for jk in range(n_str):
            j, k = divmod(jk, KVH)
            tok = slice(j * dl, (j + 1) * dl)
            kw = jnp.concatenate([_head_rows(cw_refs[j], k),
                                  wn_ref[tok, k * NSA_HD:(k + 1) * NSA_HD], jnp.zeros((padn, NSA_HD), F32)], axis=0)
            vw = jnp.concatenate([_head_rows(cw_refs[j], KVH + k),
                                  wn_ref[tok, (KVH + k) * NSA_HD:(KVH + k + 1) * NSA_HD],
                                  jnp.zeros((padn, NSA_HD), F32)], axis=0)
            qk = qb_sc[pl.ds(jk * rows_h, rows_h), :]
            s = jnp.where(wmask, _dot_nt(qk, kw.astype(BF16)), -jnp.inf)
            m = jnp.max(s, axis=1, keepdims=True)
            m = jnp.where(m > -jnp.inf, m, 0.0)
            e = jnp.exp(s - m)
            d = jnp.sum(e, axis=1, keepdims=True)
            o_win = _dot((e / jnp.where(d > 0, d, 1.0)).astype(BF16), vw.astype(BF16))
            rs = pl.ds(jk * rows_h, rows_h)
            o_slc = acc_sc[rs, :] / l_sc[rs, :]
            o_cmp = ocmp_sc[rs, :]
            for g in range(NSA_GROUP):
                h = k * NSA_GROUP + g
                r = slice(g * dl, (g + 1) * dl)
                gc = gates[tok, h * N_BRANCH:h * N_BRANCH + 1]
                gs = gates[tok, h * N_BRANCH + 1:h * N_BRANCH + 2]
                gw = gates[tok, h * N_BRANCH + 2:h * N_BRANCH + 3]
                o_ref[tok, h * NSA_HD:(h + 1) * NSA_HD] = gc * o_cmp[r] + gs * o_slc[r] + gw * o_win[r]


def _nsa_sample(P, CMP, cache_kv, cache_win, layer, page_table, DB, DL, past_len, page):
    n_pages = page_table.shape[1]
    w_buf = cache_win.shape[2]
    lk = past_len + DL
    nch = CMP.shape[3]
    n_cmp = lk // CMP_STRIDE - CMP_BLOCK // CMP_STRIDE + 1
    n_slc = -(-lk // SLC_BLOCK)
    n_sel = min(N_SELECT, n_slc)
    nbl = _round_up(n_slc, LANES)
    pg = _pick_tile(n_pages, (32, 16, 8, 4, 2, 1))
    n_steps = n_pages // pg
    nb = _pick_tile(DB, (2, 1))
    assert DL % SUBLANES == 0 and DL <= LANES and page == LANES and past_len == n_pages * page
    cov = jnp.asarray(_cover_matrix(n_cmp, n_slc, nch, nbl))
    kvw = NSA_KV_HEADS * NSA_HD
    page_spec = lambda j, i: pl.BlockSpec((None, None, page, 2, NSA_KV_HEADS, NSA_HD),
                                          lambda b, s, pt: (layer, pt[b * nb + j, s * pg + i], 0, 1, 0, 0))
    win_spec = lambda j: pl.BlockSpec((None, None, w_buf, 2, NSA_KV_HEADS, NSA_HD),
                                      lambda b, s, pt: (layer, b * nb + j, 0, 0, 0, 0))
    tok = nb * DL
    in_specs = [
        pl.BlockSpec((tok, NSA_HEADS * NSA_HD), lambda b, s, pt: (b, 3)),
        pl.BlockSpec((tok, 2 * kvw), lambda b, s, pt: (b, 9)),
        pl.BlockSpec((tok, 2 * kvw), lambda b, s, pt: (b, 10)),
        pl.BlockSpec((tok, LANES), lambda b, s, pt: (b, GATE_CHUNK)),
        pl.BlockSpec((nb, None, NSA_KV_HEADS, nch, NSA_HD), lambda b, s, pt: (b, 0, 0, 0, 0)),
        pl.BlockSpec((nb, None, NSA_KV_HEADS, nch, NSA_HD), lambda b, s, pt: (b, 1, 0, 0, 0)),
        pl.BlockSpec((nch, nbl), lambda b, s, pt: (0, 0)),
    ] + [win_spec(j) for j in range(nb)] + [page_spec(j, i) for j in range(nb) for i in range(pg)]
    rows = nb * NSA_HEADS * DL
    grid_spec = pltpu.PrefetchScalarGridSpec(
        num_scalar_prefetch=1,
        grid=(DB // nb, n_steps),
        in_specs=in_specs,
        out_specs=pl.BlockSpec((tok, NSA_HEADS * NSA_HD), lambda b, s, pt: (b, 0)),
        scratch_shapes=[
            pltpu.VMEM((rows, NSA_HD), BF16),
            pltpu.VMEM((rows, nbl), F32),
            pltpu.VMEM((rows, 1), F32),
            pltpu.VMEM((rows, 1), F32),
            pltpu.VMEM((rows, NSA_HD), F32),
            pltpu.VMEM((rows, NSA_HD), F32),
        ],
    )
    body = functools.partial(_nsa_sample_body, nb=nb, pg=pg, n_steps=n_steps, dl=DL, q_off=past_len, w_buf=w_buf,
                             n_cmp=n_cmp, n_slc=n_slc, n_sel=n_sel, nbl=nbl, page=page)
    return pl.pallas_call(
        body, grid_spec=grid_spec,
        out_shape=jax.ShapeDtypeStruct((DB * DL, NSA_HEADS * NSA_HD), F32),
        compiler_params=_cparams(("parallel", "arbitrary")),
        name="nsa_sample",
    )(page_table, P, P, P, P, CMP, CMP, cov, *([cache_win] * nb), *([cache_kv] * (nb * pg)))


def _out_ln_body(h_ref, ro_ref, no_ref, wr_ref, wn_ref, g_ref, b_ref, o_ref, *, alpha):
    tm = h_ref.shape[0]
    half = tm // 4 if tm % (4 * BF16_ROWS) == 0 else tm
    for r0 in range(0, tm, half):
        rows = slice(r0, r0 + half)
        m = _dot(ro_ref[rows, :].astype(BF16), wr_ref[...]) + _dot(no_ref[rows, :].astype(BF16), wn_ref[...])
        o_ref[rows, :] = _layer_norm(alpha * h_ref[rows, :] + m, g_ref[...], b_ref[...])


def _out_ln(h, ro, no, w_out, g, b, alpha):
    T, D = h.shape
    kr = ro.shape[1]
    kn = no.shape[1]
    tm = _pick_tile(T, (512, 256, 128, 64, 32, 16, 8))
    return pl.pallas_call(
        functools.partial(_out_ln_body, alpha=alpha),
        grid=(T // tm,),
        in_specs=[
            pl.BlockSpec((tm, D), lambda i: (i, 0)),
            pl.BlockSpec((tm, kr), lambda i: (i, 0)),
            pl.BlockSpec((tm, kn), lambda i: (i, 0)),
            pl.BlockSpec((kr, D), lambda i: (0, 0)),
            pl.BlockSpec((kn, D), lambda i: (1, 0)),
            pl.BlockSpec((1, D), lambda i: (0, 0)),
            pl.BlockSpec((1, D), lambda i: (0, 0)),
        ],
        out_specs=pl.BlockSpec((tm, D), lambda i: (i, 0)),
        out_shape=jax.ShapeDtypeStruct((T, D), F32),
        compiler_params=_cparams(("parallel",)),
        name="out_ln",
    )(h, ro, no, w_out, w_out, g.reshape(1, D), b.reshape(1, D))


def _rope_tables(pos):
    half = NSA_HD // 2
    inv = ROPE_THETA ** (-jnp.arange(half, dtype=F32) / half)
    ang = pos.astype(F32)[:, None] * inv[None, :]
    cos = jnp.cos(ang)
    sin = jnp.sin(ang)
    return jnp.concatenate([cos, cos], -1), jnp.concatenate([-sin, sin], -1)


def _cmp_weights(w1, pos, w2):
    r = CMP_BLOCK // CMP_STRIDE
    w1r = w1.reshape(r, CMP_STRIDE * NSA_HD, CMP_HIDDEN)
    w1c = jnp.concatenate([w1r[i] for i in range(r)], axis=1).astype(BF16)
    posr = jnp.pad(pos.reshape(r, CMP_STRIDE * NSA_HD), ((0, SUBLANES - r), (0, 0))).astype(BF16)
    return w1c, posr, w2.astype(BF16)


def _layer_view(arr, l, shape):
    return arr.reshape(shape) if arr.shape[0] == 1 else arr[l].reshape(shape)


def _decoder_layer(x, B, L, q_off, s0, p, sample_ctx):
    alpha = p['alpha']
    h1, h1_bf = _ffn_ln(x, p['ffn1_w_up'], p['ffn1_w_down'], p['ln1_g'], p['ln1_b'], alpha, True)
    cos, sin = _rope_tables(q_off + jnp.arange(L, dtype=jnp.int32))
    P, kv_rows, win_rows = _proj(h1_bf, p['w_in'], p['rope_cols'], p['scale_cols'], cos, sin, L)
    ro, ret_s = _retention(P, s0, p['ret_gn_g'], p['ret_gn_b'], B, L)
    if sample_ctx is None:
        assert L % CMP_STRIDE == 0
        n_cmp = L // CMP_STRIDE - CMP_BLOCK // CMP_STRIDE + 1
        n_vec = 2 * NSA_KV_HEADS
        specs = [pl.BlockSpec((L, NSA_HD), (lambda v: (lambda b, g: (b, KV_CHUNK0 + v)))(v)) for v in range(n_vec)]
        CMP = _compress([P] * n_vec, specs, 1, L, 1, n_cmp, B, p['cmp_w1'], p['cmp_pos'], p['cmp_w2'])
        no = _nsa_prompt(P, CMP, B, L)
    else:
        cache_kv, cache_win, layer, page_table, past_len, page = sample_ctx
        n_pages = page_table.shape[1]
        lk = past_len + L
        assert (lk // CMP_STRIDE) * CMP_STRIDE <= past_len, "compression blocks must lie in the paged past"
        n_cmp = lk // CMP_STRIDE - CMP_BLOCK // CMP_STRIDE + 1
        n_in = _pick_tile(n_pages, (32, 16, 8, 4, 2, 1))
        n_grp = n_pages // n_in
        specs = [pl.BlockSpec((None, None, page, 2, NSA_KV_HEADS, NSA_HD),
                              (lambda i: (lambda b, g, pt: (layer, pt[b, (n_grp - 1 - g) * n_in + i], 0, 0, 0, 0)))(i))
                 for i in range(n_in)]
        CMP = _compress([cache_kv] * n_in, specs, n_in, page, n_grp, n_cmp, B,
                        p['cmp_w1'], p['cmp_pos'], p['cmp_w2'], page_table=page_table)
        no = _nsa_sample(P, CMP, cache_kv, cache_win, layer, page_table, B, L, past_len, page)
    x2 = _out_ln(h1, ro, no, p['w_out'], p['ln2_g'], p['ln2_b'], alpha)
    y, _ = _ffn_ln(x2, p['ffn2_w_up'], p['ffn2_w_down'], p['ln3_g'], p['ln3_b'], alpha, False)
    return y, ret_s, kv_rows, win_rows


def kernel(x_prompt, x_sample, state_ret, cache_nsa_kv, cache_win, page_table, ffn1_w_up, ffn1_w_down, ln1_g, ln1_b, w_in, w_out, ret_gn_g, ret_gn_b, cmp_pos_k, cmp_w1_k, cmp_w2_k, cmp_pos_v, cmp_w1_v, cmp_w2_v, ln2_g, ln2_b, ffn2_w_up, ffn2_w_down, ln3_g, ln3_b):
    B, L, D = x_prompt.shape
    DB, DL, _ = x_sample.shape
    depth = w_in.shape[0]
    n_pool, page = cache_nsa_kv.shape[1], cache_nsa_kv.shape[2]
    n_pages = page_table.shape[1]
    past_len = n_pages * page
    w_buf = cache_win.shape[2]
    alpha = (2.0 * depth) ** 0.25
    rope_np = np.zeros((N_IN_PAD // LANES, LANES), np.float32)
    rope_np[list(ROPE_CHUNKS)] = 1.0
    scale_np = np.ones((N_IN_PAD // LANES, LANES), np.float32)
    scale_np[list(KSCALE_CHUNKS)] = RET_DK ** -0.5
    rope_cols = jnp.asarray(rope_np.reshape(1, N_IN_PAD))
    scale_cols = jnp.asarray(scale_np.reshape(1, N_IN_PAD))

    yp = x_prompt.reshape(B * L, D)
    ys = x_sample.reshape(DB * DL, D)
    outs = [[] for _ in range(6)]
    for l in range(depth):
        k1, p1, k2 = _cmp_weights(cmp_w1_k[l], cmp_pos_k[l], cmp_w2_k[l])
        v1, q1, v2 = _cmp_weights(cmp_w1_v[l], cmp_pos_v[l], cmp_w2_v[l])
        p = {
            'alpha': alpha, 'rope_cols': rope_cols, 'scale_cols': scale_cols,
            'ffn1_w_up': ffn1_w_up[l].astype(BF16), 'ffn1_w_down': ffn1_w_down[l].astype(BF16),
            'ln1_g': ln1_g[l], 'ln1_b': ln1_b[l],
            'w_in': jnp.pad(w_in[l], ((0, 0), (0, N_IN_PAD - N_IN))).astype(BF16),
            'w_out': w_out[l].astype(BF16),
            'ret_gn_g': ret_gn_g[l], 'ret_gn_b': ret_gn_b[l],
            'cmp_w1': jnp.stack([k1, v1]), 'cmp_pos': jnp.stack([p1, q1]), 'cmp_w2': jnp.stack([k2, v2]),
            'ln2_g': ln2_g[l], 'ln2_b': ln2_b[l],
            'ffn2_w_up': ffn2_w_up[l].astype(BF16), 'ffn2_w_down': ffn2_w_down[l].astype(BF16),
            'ln3_g': ln3_g[l], 'ln3_b': ln3_b[l],
        }
        s0 = jnp.zeros((B, RET_HEADS, RET_DK, RET_DV), F32)
        yp, rs_p, kv_p, win_p = _decoder_layer(yp, B, L, 0, s0, p, None)
        ctx = (cache_nsa_kv, cache_win, l, page_table, past_len, page)
        ys, rs_s, kv_s, win_s = _decoder_layer(ys, DB, DL, past_len,
                                               _layer_view(state_ret, l, state_ret.shape[1:]), p, ctx)
        wl = min(WINDOW, L)
        outs[0].append(rs_p)
        outs[1].append(rs_s)
        outs[2].append(kv_p.reshape(B, L, 4, NSA_KV_HEADS, NSA_HD))
        outs[3].append(kv_s.reshape(DB, DL, 4, NSA_KV_HEADS, NSA_HD))
        outs[4].append(win_p.reshape(B, L, 2, NSA_KV_HEADS, NSA_HD)[:, L - wl:])
        win_s = win_s.reshape(DB, DL, 2, NSA_KV_HEADS, NSA_HD)
        outs[5].append(jnp.concatenate([cache_win[l], win_s], axis=1)[:, -w_buf:])
    return (yp.reshape(B, L, D), ys.reshape(DB, DL, D), jnp.stack(outs[0]), jnp.stack(outs[1]),
            jnp.stack(outs[2]), jnp.stack(outs[3]), jnp.stack(outs[4]), jnp.stack(outs[5]))
```

```python
import functools

import numpy as np
import jax
import jax.numpy as jnp
from jax import lax
from jax.experimental import pallas as pl
from jax.experimental.pallas import tpu as pltpu

F32 = jnp.float32
BF16 = jnp.bfloat16

LANES = 128
SUBLANES = 8
BF16_ROWS = 16
VMEM_LIMIT_BYTES = 56 * 1024 * 1024

RET_HEADS = 4
RET_DK = 128
RET_DV = 256
RET_CHUNK = 128
NSA_HEADS = 8
NSA_KV_HEADS = 2
NSA_HD = 128
NSA_GROUP = NSA_HEADS // NSA_KV_HEADS
CMP_BLOCK = 32
CMP_STRIDE = 16
CMP_HIDDEN = 2 * NSA_HD
SLC_BLOCK = 64
SLC_SHIFT = 6
N_SELECT = 16
WINDOW = 512
N_BRANCH = 3
ROPE_THETA = 10000.0
LN_EPS = 1e-5
NEG_BIG = -1e30
LOG2_E = 1.4426950408889634

N_IN = 5656
N_IN_PAD = 5760
KV_CHUNK0 = 32
WIN_CHUNK0 = 40
GATE_CHUNK = 44
ROPE_CHUNKS = tuple(range(0, 8)) + tuple(range(24, 32)) + (32, 33, 36, 37, 40, 41)
KSCALE_CHUNKS = tuple(range(4, 8))


def _cparams(sem):
    return pltpu.CompilerParams(dimension_semantics=sem, vmem_limit_bytes=VMEM_LIMIT_BYTES)


def _pick_tile(n, candidates):
    for c in candidates:
        if n % c == 0:
            return c
    return n


def _round_up(n, m):
    return (n + m - 1) // m * m


def _layer_norm(z, g, b):
    mu = jnp.mean(z, axis=-1, keepdims=True)
    zc = z - mu
    var = jnp.mean(zc * zc, axis=-1, keepdims=True)
    return zc * lax.rsqrt(var + LN_EPS) * g + b


def _dot(a, b):
    return jnp.dot(a, b, preferred_element_type=F32)


def _dot_nt(a, b):
    return lax.dot_general(a, b, (((1,), (1,)), ((), ())), preferred_element_type=F32)


def _ffn_ln_body(x_ref, wa_ref, wb_ref, wd_ref, g_ref, b_ref, *rest, nj, alpha, emit_bf16):
    if emit_bf16:
        o_ref, obf_ref, xbf_sc, acc_sc = rest
    else:
        o_ref, xbf_sc, acc_sc = rest
    j = pl.program_id(1)

    @pl.when(j == 0)
    def _init():
        xbf_sc[...] = x_ref[...].astype(BF16)
        acc_sc[...] = jnp.zeros_like(acc_sc)

    xb = xbf_sc[...]
    a = _dot(xb, wa_ref[...])
    b = _dot(xb, wb_ref[...])
    h = (a * jax.nn.sigmoid(a)) * b
    acc_sc[...] += _dot(h.astype(BF16), wd_ref[...])

    @pl.when(j == nj - 1)
    def _finish():
        z = alpha * x_ref[...] + acc_sc[...]
        y = _layer_norm(z, g_ref[...], b_ref[...])
        o_ref[...] = y
        if emit_bf16:
            obf_ref[...] = y.astype(BF16)


def _ffn_ln(x, w_up, w_down, g, b, alpha, emit_bf16):
    T, D = x.shape
    F = w_down.shape[0]
    tm = _pick_tile(T, (512, 256, 128, 64, 32, 16, 8))
    tf = _pick_tile(F, (512, 256, 128))
    nj = F // tf
    out_shape = [jax.ShapeDtypeStruct((T, D), F32)]
    out_specs = [pl.BlockSpec((tm, D), lambda i, j: (i, 0))]
    if emit_bf16:
        out_shape.append(jax.ShapeDtypeStruct((T, D), BF16))
        out_specs.append(pl.BlockSpec((tm, D), lambda i, j: (i, 0)))
    res = pl.pallas_call(
        functools.partial(_ffn_ln_body, nj=nj, alpha=alpha, emit_bf16=emit_bf16),
        grid=(T // tm, nj),
        in_specs=[
            pl.BlockSpec((tm, D), lambda i, j: (i, 0)),
            pl.BlockSpec((D, tf), lambda i, j: (0, j)),
            pl.BlockSpec((D, tf), lambda i, j: (0, nj + j)),
            pl.BlockSpec((tf, D), lambda i, j: (j, 0)),
            pl.BlockSpec((1, D), lambda i, j: (0, 0)),
            pl.BlockSpec((1, D), lambda i, j: (0, 0)),
        ],
        out_specs=out_specs,
        out_shape=out_shape,
        scratch_shapes=[pltpu.VMEM((tm, D), BF16), pltpu.VMEM((tm, D), F32)],
        compiler_params=_cparams(("parallel", "arbitrary")),
        name="ffn_ln",
    )(x, w_up, w_up, w_down, g.reshape(1, D), b.reshape(1, D))
    return res if emit_bf16 else (res[0], None)


def _rows_view(ref, j):
    rows = ref.shape[0]
    n = int(np.prod(ref.shape[1:-1]))
    return ref.reshape(n * rows, ref.shape[-1]), pl.ds(j, rows, stride=n)


def _head_rows(ref, j):
    view, idx = _rows_view(ref, j)
    return view[idx, :]


def _store_rows(ref, j, val):
    view, idx = _rows_view(ref, j)
    view[idx, :] = val


def _proj_body(x_ref, w_ref, cos_ref, sin_ref, rope_ref, scale_ref, o_ref, kv_ref, win_ref, *, n_chunk, j_rows):
    y = _dot(x_ref[...], w_ref[...])
    cos = cos_ref[...]
    sin = sin_ref[...]
    for c in range(n_chunk):
        sl = slice(c * LANES, (c + 1) * LANES)
        yc = y[:, sl]
        roped = yc * cos + pltpu.roll(yc, NSA_HD // 2, 1) * sin
        o_ref[:, sl] = jnp.where(rope_ref[:, sl] > 0.5, roped, yc) * scale_ref[:, sl]

    c0 = KV_CHUNK0 - j_rows * n_chunk
    for r in range(WIN_CHUNK0 - KV_CHUNK0):
        _store_rows(kv_ref, r, o_ref[:, (c0 + r) * LANES:(c0 + r + 1) * LANES])
    c0 = WIN_CHUNK0 - j_rows * n_chunk
    for r in range(GATE_CHUNK - WIN_CHUNK0):
        _store_rows(win_ref, r, o_ref[:, (c0 + r) * LANES:(c0 + r + 1) * LANES])


def _proj(x_bf, w_bf, rope_cols, scale_cols, cos, sin, rows_per_seq):
    T, D = x_bf.shape
    N = w_bf.shape[1]
    tm = _pick_tile(T, (512, 256, 128, 64, 32, 16, 8))
    tn = 1920
    assert N % tn == 0
    n_chunk = tn // LANES
    if rows_per_seq >= tm:
        assert rows_per_seq % tm == 0
        n_tab = rows_per_seq // tm
    else:
        assert tm % rows_per_seq == 0
        cos = jnp.tile(cos, (tm // rows_per_seq, 1))
        sin = jnp.tile(sin, (tm // rows_per_seq, 1))
        n_tab = 1
    j_rows = KV_CHUNK0 // n_chunk
    assert (GATE_CHUNK - 1) // n_chunk == j_rows == N // tn - 1, "cache-row columns must sit in the last column tile"
    kv_shape = (T, 4, NSA_KV_HEADS, NSA_HD)
    win_shape = (T, 2, NSA_KV_HEADS, NSA_HD)
    return pl.pallas_call(
        functools.partial(_proj_body, n_chunk=n_chunk, j_rows=j_rows),
        grid=(T // tm, N // tn),
        in_specs=[
            pl.BlockSpec((tm, D), lambda i, j: (i, 0)),
            pl.BlockSpec((D, tn), lambda i, j: (0, j)),
            pl.BlockSpec((tm, LANES), lambda i, j: (i % n_tab, 0)),
            pl.BlockSpec((tm, LANES), lambda i, j: (i % n_tab, 0)),
            pl.BlockSpec((1, tn), lambda i, j: (0, j)),
            pl.BlockSpec((1, tn), lambda i, j: (0, j)),
        ],
        out_specs=[
            pl.BlockSpec((tm, tn), lambda i, j: (i, j)),
            pl.BlockSpec((tm,) + kv_shape[1:], lambda i, j: (i, 0, 0, 0)),
            pl.BlockSpec((tm,) + win_shape[1:], lambda i, j: (i, 0, 0, 0)),
        ],
        out_shape=[
            jax.ShapeDtypeStruct((T, N), F32),
            jax.ShapeDtypeStruct(kv_shape, F32),
            jax.ShapeDtypeStruct(win_shape, F32),
        ],
        compiler_params=_cparams(("parallel", "arbitrary")),
        name="proj_rope",
    )(x_bf, w_bf, cos, sin, rope_cols, scale_cols)


def _ret_body(q_ref, k_ref, v_ref, g_ref, s0_ref, dm_ref, ind_ref, std_ref, cd_ref, gng_ref, gnb_ref,
              o_ref, sout_ref, s_sc, *, nc, rows, rows_pad, per_step):
    c = pl.program_id(1)

    @pl.when(c == 0)
    def _load_state():
        s_sc[...] = s0_ref[...]

    pad = rows_pad - rows
    for sub in range(per_step):
        rs = slice(sub * rows, (sub + 1) * rows)
        for h in range(RET_HEADS):
            ks = slice(h * RET_DK, (h + 1) * RET_DK)
            vs = slice(h * RET_DV, (h + 1) * RET_DV)
            q = q_ref[rs, ks]
            k = k_ref[rs, ks]
            v = v_ref[rs, vs]
            kd = k * std_ref[h]
            if pad:
                k = jnp.concatenate([k, jnp.zeros((pad, RET_DK), F32)], axis=0)
                kd = jnp.concatenate([kd, jnp.zeros((pad, RET_DK), F32)], axis=0)
                v = jnp.concatenate([v, jnp.zeros((pad, RET_DV), F32)], axis=0)
            s_old = s_sc[h]
            vb = v.astype(BF16)
            a = _dot_nt(q.astype(BF16), k.astype(BF16)) * dm_ref[h]
            o = _dot(a.astype(BF16), vb) + _dot((q * ind_ref[h]).astype(BF16), s_old.astype(BF16))
            s_sc[h] = s_old * cd_ref[h] + _dot(kd.T.astype(BF16), vb)
            mu = jnp.mean(o, axis=-1, keepdims=True)
            oc = o - mu
            var = jnp.mean(oc * oc, axis=-1, keepdims=True)
            on = oc * lax.rsqrt(var + LN_EPS) * gng_ref[:, vs] + gnb_ref[:, vs]
            gate = g_ref[rs, vs]
            o_ref[rs, vs] = (gate * jax.nn.sigmoid(gate)) * on

    @pl.when(c == nc - 1)
    def _store_state():
        sout_ref[...] = s_sc[...]


def _retention(P, s0, gn_g, gn_b, B, L):
    C = RET_CHUNK if L % RET_CHUNK == 0 else L
    nc = L // C
    CP = max(C, LANES)
    lg = jnp.log1p(-jnp.exp2(-5.0 - jnp.arange(RET_HEADS, dtype=F32)))
    i = jnp.arange(C, dtype=F32)
    diff = i[:, None] - i[None, :]
    dmask = jnp.where(diff >= 0, jnp.exp(lg[:, None, None] * jnp.maximum(diff, 0.0)), 0.0)
    dmask = jnp.pad(dmask, ((0, 0), (0, 0), (0, CP - C)))
    in_decay = jnp.broadcast_to(jnp.exp(lg[:, None] * (i + 1.0))[:, :, None], (RET_HEADS, C, RET_DK))
    st_decay = jnp.broadcast_to(jnp.exp(lg[:, None] * (C - 1.0 - i))[:, :, None], (RET_HEADS, C, RET_DK))
    chunk_decay = jnp.broadcast_to(jnp.exp(lg * C)[:, None, None], (RET_HEADS, 1, RET_DV))
    qw = RET_HEADS * RET_DK
    vw = RET_HEADS * RET_DV
    const3 = lambda b, c: (0, 0, 0)
    per_step = _pick_tile(nc, (8, 4, 2, 1))
    nc = nc // per_step
    CS = C * per_step
    return pl.pallas_call(
        functools.partial(_ret_body, nc=nc, rows=C, rows_pad=CP, per_step=per_step),
        grid=(B, nc),
        in_specs=[
            pl.BlockSpec((CS, qw), lambda b, c: (b * nc + c, 0)),
            pl.BlockSpec((CS, qw), lambda b, c: (b * nc + c, 1)),
            pl.BlockSpec((CS, vw), lambda b, c: (b * nc + c, 1)),
            pl.BlockSpec((CS, vw), lambda b, c: (b * nc + c, 2)),
            pl.BlockSpec((None, RET_HEADS, RET_DK, RET_DV), lambda b, c: (b, 0, 0, 0)),
            pl.BlockSpec((RET_HEADS, C, CP), const3),
            pl.BlockSpec((RET_HEADS, C, RET_DK), const3),
            pl.BlockSpec((RET_HEADS, C, RET_DK), const3),
            pl.BlockSpec((RET_HEADS, 1, RET_DV), const3),
            pl.BlockSpec((1, vw), lambda b, c: (0, 0)),
            pl.BlockSpec((1, vw), lambda b, c: (0, 0)),
        ],
        out_specs=[
            pl.BlockSpec((CS, vw), lambda b, c: (b * nc + c, 0)),
            pl.BlockSpec((None, RET_HEADS, RET_DK, RET_DV), lambda b, c: (b, 0, 0, 0)),
        ],
        out_shape=[
            jax.ShapeDtypeStruct((B * L, vw), F32),
            jax.ShapeDtypeStruct((B, RET_HEADS, RET_DK, RET_DV), F32),
        ],
        scratch_shapes=[pltpu.VMEM((RET_HEADS, RET_DK, RET_DV), F32)],
        compiler_params=_cparams(("parallel", "arbitrary")),
        name="retention",
    )(P, P, P, P, s0, dmask, in_decay, st_decay, chunk_decay, gn_g.reshape(1, vw), gn_b.reshape(1, vw))


def _cmp_body(*refs, n_in, rows, n_grp, n_cmp, paged):
    n_vec = 2 * NSA_KV_HEADS
    if paged:
        refs = refs[1:]
    n_src = n_in if paged else n_vec * n_in
    x_refs = refs[:n_src]
    w1_ref, pos_ref, w2_ref, o_ref, carry_sc, xc_sc = refs[n_src:]
    g = pl.program_id(1)
    grp = n_grp - 1 - g
    cpi = rows // CMP_STRIDE
    M = n_in * cpi

    @pl.when(g == 0)
    def _init():
        carry_sc[...] = jnp.zeros_like(carry_sc)

    row = lax.broadcasted_iota(jnp.int32, (M, 1), 0)
    grp_in = 1 if cpi % BF16_ROWS == 0 else BF16_ROWS // cpi
    assert n_in % grp_in == 0 and (grp_in * cpi) % BF16_ROWS == 0
    for t in range(2):
        w1 = w1_ref[t]
        gp = _dot(pos_ref[t], w1)
        posterm = gp[0:1, :CMP_HIDDEN] + gp[1:2, CMP_HIDDEN:]
        for hd in range(NSA_KV_HEADS):
            v = t * NSA_KV_HEADS + hd
            for i0 in range(0, n_in, grp_in):
                if paged:
                    parts = [pltpu.einshape("csd->scd",
                                            _head_rows(x_refs[i], v).reshape(cpi, CMP_STRIDE, NSA_HD))
                             for i in range(i0, i0 + grp_in)]
                    piece = lambda s: jnp.concatenate([xt[s] for xt in parts], axis=0)
                else:
                    refs_i = [x_refs[v * n_in + i] for i in range(i0, i0 + grp_in)]
                    piece = lambda s: jnp.concatenate(
                        [r[pl.ds(s, cpi, stride=CMP_STRIDE), :] for r in refs_i], axis=0)
                r0 = v * M + i0 * cpi
                for s in range(CMP_STRIDE):
                    xc_sc[r0:r0 + grp_in * cpi, s * NSA_HD:(s + 1) * NSA_HD] = piece(s).astype(BF16)
        t0 = t * NSA_KV_HEADS * M
        gg_all = _dot(xc_sc[t0:t0 + NSA_KV_HEADS * M, :], w1)
        for hd in range(NSA_KV_HEADS):
            v = t * NSA_KV_HEADS + hd
            gg = gg_all[hd * M:(hd + 1) * M]
            g0 = gg[:, :CMP_HIDDEN]
            g1 = gg[:, CMP_HIDDEN:]
            nxt = pltpu.roll(g1, M - 1, 0)
            nxt = jnp.where(row == M - 1, carry_sc[v][0:1, :], nxt)
            carry_sc[v] = g1[0:SUBLANES, :]
            hid = g0 + nxt + posterm
            out = _dot(jax.nn.gelu(hid).astype(BF16), w2_ref[t])
            o_ref[t, hd] = jnp.where(grp * M + row < n_cmp, out, 0.0)


def _compress(srcs, src_specs, n_in, rows, n_grp, n_cmp, B, w1, pos, w2, page_table=None):
    n_vec = 2 * NSA_KV_HEADS
    M = n_in * rows // CMP_STRIDE
    nch = n_grp * M
    paged = page_table is not None
    const3 = lambda *a: (0, 0, 0)
    in_specs = list(src_specs) + [
        pl.BlockSpec((2, CMP_STRIDE * NSA_HD, 2 * CMP_HIDDEN), const3),
        pl.BlockSpec((2, SUBLANES, CMP_STRIDE * NSA_HD), const3),
        pl.BlockSpec((2, CMP_HIDDEN, NSA_HD), const3),
    ]
    out_spec = pl.BlockSpec((None, 2, NSA_KV_HEADS, M, NSA_HD), lambda *a: (a[0], 0, 0, n_grp - 1 - a[1], 0))
    body = functools.partial(_cmp_body, n_in=n_in, rows=rows, n_grp=n_grp, n_cmp=n_cmp, paged=paged)
    out_shape = jax.ShapeDtypeStruct((B, 2, NSA_KV_HEADS, nch, NSA_HD), F32)
    scratch = [pltpu.VMEM((n_vec, SUBLANES, CMP_HIDDEN), F32),
               pltpu.VMEM((n_vec * M, CMP_STRIDE * NSA_HD), BF16)]
    sem = ("parallel", "arbitrary")
    if paged:
        grid_spec = pltpu.PrefetchScalarGridSpec(
            num_scalar_prefetch=1, grid=(B, n_grp), in_specs=in_specs, out_specs=out_spec,
            scratch_shapes=scratch)
        return pl.pallas_call(body, grid_spec=grid_spec, out_shape=out_shape,
                              compiler_params=_cparams(sem), name="nsa_compress_paged")(
            page_table, *srcs, w1, pos, w2)
    return pl.pallas_call(body, grid=(B, n_grp), in_specs=in_specs, out_specs=out_spec,
                          out_shape=out_shape, scratch_shapes=scratch,
                          compiler_params=_cparams(sem), name="nsa_compress")(*srcs, w1, pos, w2)


def _cover_matrix(n_cmp, n_slc, rows, cols):
    c_i = np.arange(n_cmp)[:, None]
    n_i = np.arange(n_slc)[None, :]
    cov = np.clip(np.minimum(c_i * CMP_STRIDE + CMP_BLOCK, (n_i + 1) * SLC_BLOCK)
                  - np.maximum(c_i * CMP_STRIDE, n_i * SLC_BLOCK), 0, None).astype(np.float32) / CMP_BLOCK
    out = np.zeros((rows, cols), np.float32)
    out[:n_cmp, :n_slc] = cov
    return out


def _nsa_prompt_body(q_ref, kc_ref, vc_ref, ks_ref, vs_ref, kw_ref, vw_ref, gate_ref, covt_ref, o_ref,
                     kcb_sc, vct_sc, ksb_sc, vst_sc, kwb_sc, vwt_sc, prio_sc, sel_sc, gt_sc,
                     m_sc, l_sc, acc_sc, out_sc, s_sc, *, tq, L, n_cmp, n_slc, n_sel, nbp):
    KV = NSA_KV_HEADS
    qi = pl.program_id(1)
    tk = tq
    nch = kc_ref.shape[1]
    scale = NSA_HD ** -0.5

    @pl.when(qi == 0)
    def _stage_kv():
        for kv in range(KV):
            cs = slice(kv * NSA_HD, (kv + 1) * NSA_HD)
            kcb_sc[kv] = kc_ref[kv].astype(BF16)
            vct_sc[kv] = vc_ref[kv].T.astype(BF16)
            ksb_sc[kv] = ks_ref[:, cs].astype(BF16)
            kwb_sc[kv] = kw_ref[:, cs].astype(BF16)
            for i in range(L // tk):
                vst_sc[kv, i] = vs_ref[i * tk:(i + 1) * tk, cs].T.astype(BF16)
                vwt_sc[kv, i] = vw_ref[i * tk:(i + 1) * tk, cs].T.astype(BF16)

    G = NSA_GROUP
    W = G * tq
    t0 = qi * tq
    t_row = t0 + lax.broadcasted_iota(jnp.int32, (1, tq), 1)
    lane = lax.broadcasted_iota(jnp.int32, (1, W), 1)
    t_all = t0 + (lane & (tq - 1))
    qcats = []
    for kv in range(KV):
        qc = jnp.concatenate([q_ref[:, (kv * G + g) * NSA_HD:(kv * G + g + 1) * NSA_HD] for g in range(G)], axis=0)
        qcats.append((qc * (scale * LOG2_E)).astype(BF16))
    gt_sc[...] = jax.nn.sigmoid(gate_ref[...]).T

    def gate_row(kv, br):
        return jnp.concatenate(
            [gt_sc[(kv * G + g) * N_BRANCH + br:(kv * G + g) * N_BRANCH + br + 1, :] for g in range(G)], axis=1)

    kp_l = lax.broadcasted_iota(jnp.int32, (tk, 1), 0)
    t_l = lane & (tq - 1)
    causal_bias = jnp.where(kp_l <= t_l, 0.0, NEG_BIG)
    far_bias = jnp.where(kp_l > t_l, 0.0, NEG_BIG)

    def reset(kv):
        m_sc[kv] = jnp.full((1, W), NEG_BIG, F32)
        l_sc[kv] = jnp.zeros((1, W), F32)
        acc_sc[kv] = jnp.zeros((NSA_HD, W), F32)

    def score(kb_sc, kv, kt, bias):
        k0 = pl.multiple_of(kt * tk, tk)
        s = _dot_nt(kb_sc[kv, pl.ds(k0, tk), :], qcats[kv])
        return s if bias is None else s + bias

    def update(vt_sc, kv, kt, s):
        m_old = m_sc[kv]
        m_new = jnp.maximum(m_old, jnp.max(s, axis=0, keepdims=True))
        p = jnp.exp2(s - m_new)
        alpha = jnp.exp2(m_old - m_new)
        l_sc[kv] = alpha * l_sc[kv] + jnp.sum(p, axis=0, keepdims=True)
        acc_sc[kv] = alpha * acc_sc[kv] + _dot(vt_sc[kv, kt], p.astype(BF16))
        m_sc[kv] = m_new

    def branch_out(kv, br):
        return (gate_row(kv, br) / l_sc[kv]) * acc_sc[kv]

    far = WINDOW // tk
    for kv in range(KV):
        reset(kv)
        tiles = []
        for back in range(far, -1, -1):
            bias = far_bias if back == far else (causal_bias if back == 0 else None)
            if back > 0:
                off = jnp.where(qi >= back, 0.0, NEG_BIG).astype(F32)
                bias = off if bias is None else bias + off
            kt = jnp.maximum(qi - back, 0)
            tiles.append((kt, score(kwb_sc, kv, kt, bias)))
        for kt, s_w in tiles:
            update(vwt_sc, kv, kt, s_w)
        out_sc[kv] = branch_out(kv, 2)

    c_i = lax.broadcasted_iota(jnp.int32, (nch, 1), 0)
    c_end = jnp.where(c_i < n_cmp, c_i * CMP_STRIDE + (CMP_BLOCK - 1), L)
    n_i = lax.broadcasted_iota(jnp.int32, (nbp, 1), 0)
    valid = (n_i * SLC_BLOCK <= t_row) & (n_i < n_slc)
    cur = t_row >> SLC_SHIFT
    forced = (n_i == 0) | (n_i == cur) | (n_i == cur - 1)
    for kv in range(KV):
        s = jnp.where(c_end <= t_all, _dot_nt(kcb_sc[kv], qcats[kv]), -jnp.inf)
        m = jnp.max(s, axis=0, keepdims=True)
        m = jnp.where(m > -jnp.inf, m, 0.0)
        e = jnp.exp2(s - m)
        d = jnp.sum(e, axis=0, keepdims=True)
        p = e / jnp.where(d > 0, d, 1.0)
        out_sc[kv] += gate_row(kv, 0) * _dot(vct_sc[kv], p.astype(BF16))
        psum = p[:, 0:tq]
        for g in range(1, G):
            psum = psum + p[:, g * tq:(g + 1) * tq]
        imp = jnp.dot(covt_ref[...], psum, preferred_element_type=F32, precision=lax.Precision.HIGHEST)
        prio = jnp.where(forced, jnp.inf, jnp.where(valid, imp, -jnp.inf))
        prio_sc[kv] = prio
        cnt = jnp.zeros((nbp, tq), jnp.int32)
        for mm in range(n_slc):
            pm = prio_sc[kv, mm:mm + 1, :]
            tie = jnp.where(n_i > mm, 1, 0)
            cnt = cnt + jnp.where(pm > prio, 1, jnp.where(pm == prio, tie, 0))
        selbias = jnp.where((cnt < n_sel) & valid, 0.0, NEG_BIG)
        sel_sc[kv] = jnp.concatenate([selbias] * G, axis=1)

    def sel_bias(kv, kt):
        per_tile = tk // SLC_BLOCK
        rows = [jnp.broadcast_to(sel_sc[kv, pl.ds(kt * per_tile + r, 1), :], (SLC_BLOCK, W))
                for r in range(per_tile)]
        return jnp.concatenate(rows, axis=0)

    for kv in range(KV):
        reset(kv)
        s_sc[kv] = score(ksb_sc, kv, 0, sel_bias(kv, 0))

    def slc_body(kt, carry):
        nxt = [score(ksb_sc, kv, kt + 1, sel_bias(kv, kt + 1)) for kv in range(KV)]
        for kv in range(KV):
            update(vst_sc, kv, kt, s_sc[kv])
        for kv in range(KV):
            s_sc[kv] = nxt[kv]
        return carry

    lax.fori_loop(0, qi, slc_body, 0)
    for kv in range(KV):
        update(vst_sc, kv, qi, s_sc[kv] + causal_bias)
        out_sc[kv] += branch_out(kv, 1)
        for g in range(G):
            h = kv * G + g
            o_ref[:, h * NSA_HD:(h + 1) * NSA_HD] = out_sc[kv, :, g * tq:(g + 1) * tq].T


def _nsa_prompt(P, CMP, B, L):
    tq = _pick_tile(L, (256, 128))
    assert L % tq == 0 and tq % SLC_BLOCK == 0 and WINDOW % tq == 0 and tq & (tq - 1) == 0
    nq = L // tq
    nch = CMP.shape[3]
    n_cmp = L // CMP_STRIDE - CMP_BLOCK // CMP_STRIDE + 1
    n_slc = -(-L // SLC_BLOCK)
    n_sel = min(N_SELECT, n_slc)
    nbp = _round_up(n_slc, SUBLANES)
    covt = jnp.asarray(_cover_matrix(n_cmp, n_slc, nch, nbp).T)
    qw = NSA_HEADS * NSA_HD
    kvw = NSA_KV_HEADS * NSA_HD
    KV = NSA_KV_HEADS
    W = NSA_GROUP * tq
    kv_col = lambda c: (lambda b, i: (b, (KV_CHUNK0 + c) // NSA_KV_HEADS))
    body = functools.partial(_nsa_prompt_body, tq=tq, L=L, n_cmp=n_cmp, n_slc=n_slc, n_sel=n_sel, nbp=nbp)
    return pl.pallas_call(
        body,
        grid=(B, nq),
        in_specs=[
            pl.BlockSpec((tq, qw), lambda b, i: (b * nq + i, 3)),
            pl.BlockSpec((None, None, KV, nch, NSA_HD), lambda b, i: (b, 0, 0, 0, 0)),
            pl.BlockSpec((None, None, KV, nch, NSA_HD), lambda b, i: (b, 1, 0, 0, 0)),
            pl.BlockSpec((L, kvw), kv_col(4)),
            pl.BlockSpec((L, kvw), kv_col(6)),
            pl.BlockSpec((L, kvw), kv_col(8)),
            pl.BlockSpec((L, kvw), kv_col(10)),
            pl.BlockSpec((tq, LANES), lambda b, i: (b * nq + i, GATE_CHUNK)),
            pl.BlockSpec((nbp, nch), lambda b, i: (0, 0)),
        ],
        out_specs=pl.BlockSpec((tq, qw), lambda b, i: (b * nq + i, 0)),
        out_shape=jax.ShapeDtypeStruct((B * L, qw), F32),
        scratch_shapes=[
            pltpu.VMEM((KV, nch, NSA_HD), BF16),
            pltpu.VMEM((KV, NSA_HD, nch), BF16),
            pltpu.VMEM((KV, L, NSA_HD), BF16),
            pltpu.VMEM((KV, L // tq, NSA_HD, tq), BF16),
            pltpu.VMEM((KV, L, NSA_HD), BF16),
            pltpu.VMEM((KV, L // tq, NSA_HD, tq), BF16),
            pltpu.VMEM((KV, nbp, tq), F32),
            pltpu.VMEM((KV, nbp, W), F32),
            pltpu.VMEM((LANES, tq), F32),
            pltpu.VMEM((KV, 1, W), F32),
            pltpu.VMEM((KV, 1, W), F32),
            pltpu.VMEM((KV, NSA_HD, W), F32),
            pltpu.VMEM((KV, NSA_HD, W), F32),
            pltpu.VMEM((KV, tq, W), F32),
        ],
        compiler_params=_cparams(("parallel", "arbitrary")),
        name="nsa_prompt",
    )(P, CMP, CMP, P, P, P, P, P, covt)


def _nsa_sample_body(*refs, nb, pg, n_steps, dl, q_off, w_buf, n_cmp, n_slc, n_sel, nbl, page):
    refs = refs[1:]
    q_ref, kvn_ref, wn_ref, gate_ref, kc_ref, vc_ref, cov_ref = refs[:7]
    cw_refs = refs[7:7 + nb]
    pg_refs = refs[7 + nb:7 + nb + nb * pg]
    o_ref, qb_sc, sel_sc, m_sc, l_sc, acc_sc, ocmp_sc = refs[7 + nb + nb * pg:]
    step = pl.program_id(1)
    KVH = NSA_KV_HEADS
    n_str = nb * KVH
    rows_h = NSA_GROUP * dl
    scale = NSA_HD ** -0.5
    nch = kc_ref.shape[2]

    def tok_pos(n):
        r = lax.broadcasted_iota(jnp.int32, (n, 1), 0)
        return q_off + r % dl

    @pl.when(step == 0)
    def _select():
        for jk in range(n_str):
            j, k = divmod(jk, KVH)
            for g in range(NSA_GROUP):
                h = k * NSA_GROUP + g
                qb_sc[pl.ds((jk * NSA_GROUP + g) * dl, dl), :] = (
                    q_ref[j * dl:(j + 1) * dl, h * NSA_HD:(h + 1) * NSA_HD] * scale).astype(BF16)
        t_h = tok_pos(rows_h)
        c_i = lax.broadcasted_iota(jnp.int32, (1, nch), 1)
        cmask = (c_i * CMP_STRIDE + (CMP_BLOCK - 1) <= t_h) & (c_i < n_cmp)
        psums = []
        for jk in range(n_str):
            j, k = divmod(jk, KVH)
            qk = qb_sc[pl.ds(jk * rows_h, rows_h), :]
            s = jnp.where(cmask, _dot_nt(qk, kc_ref[j, k].astype(BF16)), -jnp.inf)
            m = jnp.max(s, axis=1, keepdims=True)
            m = jnp.where(m > -jnp.inf, m, 0.0)
            e = jnp.exp(s - m)
            d = jnp.sum(e, axis=1, keepdims=True)
            p = e / jnp.where(d > 0, d, 1.0)
            ocmp_sc[pl.ds(jk * rows_h, rows_h), :] = _dot(p.astype(BF16), vc_ref[j, k].astype(BF16))
            psum = p[0:dl]
            for g in range(1, NSA_GROUP):
                psum = psum + p[g * dl:(g + 1) * dl]
            psums.append(psum)
        imp = jnp.dot(jnp.concatenate(psums, axis=0), cov_ref[...], preferred_element_type=F32,
                      precision=lax.Precision.HIGHEST)
        t_s = tok_pos(n_str * dl)
        n_i = lax.broadcasted_iota(jnp.int32, (1, nbl), 1)
        valid = (n_i * SLC_BLOCK <= t_s) & (n_i < n_slc)
        cur = t_s >> SLC_SHIFT
        forced = (n_i == 0) | (n_i == cur) | (n_i == cur - 1)
        prio = jnp.where(forced, jnp.inf, jnp.where(valid, imp, -jnp.inf))
        n_f = n_i.astype(F32)
        alive = jnp.broadcast_to(jnp.where(n_i < n_slc, 1.0, 0.0), prio.shape)
        sel = jnp.zeros(prio.shape, F32)
        for _ in range(n_sel):
            mx = jnp.max(jnp.where(alive > 0.5, prio, -jnp.inf), axis=1, keepdims=True)
            cand = (alive > 0.5) & (prio == mx)
            first = jnp.min(jnp.where(cand, n_f, float(nbl)), axis=1, keepdims=True)
            pick = n_f == first
            sel = jnp.where(pick, 1.0, sel)
            alive = jnp.where(pick, 0.0, alive)
        sel = jnp.where(valid, sel, 0.0)
        for jk in range(n_str):
            for g in range(NSA_GROUP):
                sel_sc[pl.ds((jk * NSA_GROUP + g) * dl, dl), :] = sel[jk * dl:(jk + 1) * dl]
        m_sc[...] = jnp.full(m_sc.shape, NEG_BIG, F32)
        l_sc[...] = jnp.zeros_like(l_sc)
        acc_sc[...] = jnp.zeros_like(acc_sc)

    selb = sel_sc[...].astype(BF16)
    n_col = lax.broadcasted_iota(jnp.int32, (nbl, 1), 0)

    def online_update(jk, s, mask, v_rows):
        rs = pl.ds(jk * rows_h, rows_h)
        m_old = m_sc[rs, :]
        m_new = jnp.maximum(m_old, jnp.max(s, axis=1, keepdims=True))
        p = jnp.where(mask, jnp.exp(s - m_new), 0.0)
        alpha = jnp.exp(m_old - m_new)
        l_sc[rs, :] = alpha * l_sc[rs, :] + jnp.sum(p, axis=1, keepdims=True)
        acc_sc[rs, :] = alpha * acc_sc[rs, :] + _dot(p.astype(BF16), v_rows)
        m_sc[rs, :] = m_new

    t_h = tok_pos(rows_h)

    def selected(kpos):
        blk = jnp.where(n_col == (kpos >> SLC_SHIFT), 1.0, 0.0).astype(BF16)
        return _dot(selb, blk)

    kpos = step * (pg * page) + lax.broadcasted_iota(jnp.int32, (1, pg * page), 1)
    sel_all = selected(kpos)
    for jk in range(n_str):
        j, k = divmod(jk, KVH)
        pages = pg_refs[j * pg:(j + 1) * pg]
        qk = qb_sc[pl.ds(jk * rows_h, rows_h), :]
        mask = (sel_all[jk * rows_h:(jk + 1) * rows_h] > 0.5) & (kpos <= t_h)
        k_rows = jnp.concatenate([_head_rows(r, k).astype(BF16) for r in pages], axis=0)
        v_rows = jnp.concatenate([_head_rows(r, KVH + k).astype(BF16) for r in pages], axis=0)
        s = jnp.where(mask, _dot_nt(qk, k_rows), NEG_BIG)
        online_update(jk, s, mask, v_rows)

    @pl.when(step == n_steps - 1)
    def _finish():
        padn = LANES - dl
        j_new = lax.broadcasted_iota(jnp.int32, (1, LANES), 1)
        kpos_n = q_off + j_new
        sel_n = selected(kpos_n)
        for jk in range(n_str):
            j, k = divmod(jk, KVH)
            tok = slice(j * dl, (j + 1) * dl)
            kn = jnp.concatenate([kvn_ref[tok, k * NSA_HD:(k + 1) * NSA_HD], jnp.zeros((padn, NSA_HD), F32)],
                                 axis=0)
            vn = jnp.concatenate([kvn_ref[tok, (KVH + k) * NSA_HD:(KVH + k + 1) * NSA_HD],
                                  jnp.zeros((padn, NSA_HD), F32)], axis=0)
            qk = qb_sc[pl.ds(jk * rows_h, rows_h), :]
            mask = (sel_n[jk * rows_h:(jk + 1) * rows_h] > 0.5) & (kpos_n <= t_h) & (j_new < dl)
            s = jnp.where(mask, _dot_nt(qk, kn.astype(BF16)), NEG_BIG)
            online_update(jk, s, mask, vn.astype(BF16))
        j_w = lax.broadcasted_iota(jnp.int32, (1, w_buf + LANES), 1)
        pos_w = q_off - w_buf + j_w
        dlt = t_h - pos_w
        wmask = (j_w < w_buf + dl) & (pos_w >= 0) & (dlt >= 0) & (dlt < WINDOW)
        gates = jax.nn.sigmoid(gate_ref[...])
        for jk in range(n_str):
            j, k = divmod(jk, KVH)
            tok = slice(j * dl, (j + 1) * dl)
            kw = jnp.concatenate([_head_rows(cw_refs[j], k),
                                  wn_ref[tok, k * NSA_HD:(k + 1) * NSA_HD], jnp.zeros((padn, NSA_HD), F32)], axis=0)
            vw = jnp.concatenate([_head_rows(cw_refs[j], KVH + k),
                                  wn_ref[tok, (KVH + k) * NSA_HD:(KVH + k + 1) * NSA_HD],
                                  jnp.zeros((padn, NSA_HD), F32)], axis=0)
            qk = qb_sc[pl.ds(jk * rows_h, rows_h), :]
            s = jnp.where(wmask, _dot_nt(qk, kw.astype(BF16)), -jnp.inf)
            m = jnp.max(s, axis=1, keepdims=True)
            m = jnp.where(m > -jnp.inf, m, 0.0)
            e = jnp.exp(s - m)
            d = jnp.sum(e, axis=1, keepdims=True)
            o_win = _dot((e / jnp.where(d > 0, d, 1.0)).astype(BF16), vw.astype(BF16))
            rs = pl.ds(jk * rows_h, rows_h)
            o_slc = acc_sc[rs, :] / l_sc[rs, :]
            o_cmp = ocmp_sc[rs, :]
            for g in range(NSA_GROUP):
                h = k * NSA_GROUP + g
                r = slice(g * dl, (g + 1) * dl)
                gc = gates[tok, h * N_BRANCH:h * N_BRANCH + 1]
                gs = gates[tok, h * N_BRANCH + 1:h * N_BRANCH + 2]
                gw = gates[tok, h * N_BRANCH + 2:h * N_BRANCH + 3]
                o_ref[tok, h * NSA_HD:(h + 1) * NSA_HD] = gc * o_cmp[r] + gs * o_slc[r] + gw * o_win[r]


def _nsa_sample(P, CMP, cache_kv, cache_win, layer, page_table, DB, DL, past_len, page):
    n_pages = page_table.shape[1]
    w_buf = cache_win.shape[2]
    lk = past_len + DL
    nch = CMP.shape[3]
    n_cmp = lk // CMP_STRIDE - CMP_BLOCK // CMP_STRIDE + 1
    n_slc = -(-lk // SLC_BLOCK)
    n_sel = min(N_SELECT, n_slc)
    nbl = _round_up(n_slc, LANES)
    pg = _pick_tile(n_pages, (32, 16, 8, 4, 2, 1))
    n_steps = n_pages // pg
    nb = _pick_tile(DB, (2, 1))
    assert DL % SUBLANES == 0 and DL <= LANES and page == LANES and past_len == n_pages * page
    cov = jnp.asarray(_cover_matrix(n_cmp, n_slc, nch, nbl))
    kvw = NSA_KV_HEADS * NSA_HD
    page_spec = lambda j, i: pl.BlockSpec((None, None, page, 2, NSA_KV_HEADS, NSA_HD),
                                          lambda b, s, pt: (layer, pt[b * nb + j, s * pg + i], 0, 1, 0, 0))
    win_spec = lambda j: pl.BlockSpec((None, None, w_buf, 2, NSA_KV_HEADS, NSA_HD),
                                      lambda b, s, pt: (layer, b * nb + j, 0, 0, 0, 0))
    tok = nb * DL
    in_specs = [
        pl.BlockSpec((tok, NSA_HEADS * NSA_HD), lambda b, s, pt: (b, 3)),
        pl.BlockSpec((tok, 2 * kvw), lambda b, s, pt: (b, 9)),
        pl.BlockSpec((tok, 2 * kvw), lambda b, s, pt: (b, 10)),
        pl.BlockSpec((tok, LANES), lambda b, s, pt: (b, GATE_CHUNK)),
        pl.BlockSpec((nb, None, NSA_KV_HEADS, nch, NSA_HD), lambda b, s, pt: (b, 0, 0, 0, 0)),
        pl.BlockSpec((nb, None, NSA_KV_HEADS, nch, NSA_HD), lambda b, s, pt: (b, 1, 0, 0, 0)),
        pl.BlockSpec((nch, nbl), lambda b, s, pt: (0, 0)),
    ] + [win_spec(j) for j in range(nb)] + [page_spec(j, i) for j in range(nb) for i in range(pg)]
    rows = nb * NSA_HEADS * DL
    grid_spec = pltpu.PrefetchScalarGridSpec(
        num_scalar_prefetch=1,
        grid=(DB // nb, n_steps),
        in_specs=in_specs,
        out_specs=pl.BlockSpec((tok, NSA_HEADS * NSA_HD), lambda b, s, pt: (b, 0)),
        scratch_shapes=[
            pltpu.VMEM((rows, NSA_HD), BF16),
            pltpu.VMEM((rows, nbl), F32),
            pltpu.VMEM((rows, 1), F32),
            pltpu.VMEM((rows, 1), F32),
            pltpu.VMEM((rows, NSA_HD), F32),
            pltpu.VMEM((rows, NSA_HD), F32),
        ],
    )
    body = functools.partial(_nsa_sample_body, nb=nb, pg=pg, n_steps=n_steps, dl=DL, q_off=past_len, w_buf=w_buf,
                             n_cmp=n_cmp, n_slc=n_slc, n_sel=n_sel, nbl=nbl, page=page)
    return pl.pallas_call(
        body, grid_spec=grid_spec,
        out_shape=jax.ShapeDtypeStruct((DB * DL, NSA_HEADS * NSA_HD), F32),
        compiler_params=_cparams(("parallel", "arbitrary")),
        name="nsa_sample",
    )(page_table, P, P, P, P, CMP, CMP, cov, *([cache_win] * nb), *([cache_kv] * (nb * pg)))


def _out_ln_body(h_ref, ro_ref, no_ref, wr_ref, wn_ref, g_ref, b_ref, o_ref, *, alpha):
    tm = h_ref.shape[0]
    half = tm // 4 if tm % (4 * BF16_ROWS) == 0 else tm
    for r0 in range(0, tm, half):
        rows = slice(r0, r0 + half)
        m = _dot(ro_ref[rows, :].astype(BF16), wr_ref[...]) + _dot(no_ref[rows, :].astype(BF16), wn_ref[...])
        o_ref[rows, :] = _layer_norm(alpha * h_ref[rows, :] + m, g_ref[...], b_ref[...])


def _out_ln(h, ro, no, w_out, g, b, alpha):
    T, D = h.shape
    kr = ro.shape[1]
    kn = no.shape[1]
    tm = _pick_tile(T, (512, 256, 128, 64, 32, 16, 8))
    return pl.pallas_call(
        functools.partial(_out_ln_body, alpha=alpha),
        grid=(T // tm,),
        in_specs=[
            pl.BlockSpec((tm, D), lambda i: (i, 0)),
            pl.BlockSpec((tm, kr), lambda i: (i, 0)),
            pl.BlockSpec((tm, kn), lambda i: (i, 0)),
            pl.BlockSpec((kr, D), lambda i: (0, 0)),
            pl.BlockSpec((kn, D), lambda i: (1, 0)),
            pl.BlockSpec((1, D), lambda i: (0, 0)),
            pl.BlockSpec((1, D), lambda i: (0, 0)),
        ],
        out_specs=pl.BlockSpec((tm, D), lambda i: (i, 0)),
        out_shape=jax.ShapeDtypeStruct((T, D), F32),
        compiler_params=_cparams(("parallel",)),
        name="out_ln",
    )(h, ro, no, w_out, w_out, g.reshape(1, D), b.reshape(1, D))


def _rope_tables(pos):
    half = NSA_HD // 2
    inv = ROPE_THETA ** (-jnp.arange(half, dtype=F32) / half)
    ang = pos.astype(F32)[:, None] * inv[None, :]
    cos = jnp.cos(ang)
    sin = jnp.sin(ang)
    return jnp.concatenate([cos, cos], -1), jnp.concatenate([-sin, sin], -1)


def _cmp_weights(w1, pos, w2):
    r = CMP_BLOCK // CMP_STRIDE
    w1r = w1.reshape(r, CMP_STRIDE * NSA_HD, CMP_HIDDEN)
    w1c = jnp.concatenate([w1r[i] for i in range(r)], axis=1).astype(BF16)
    posr = jnp.pad(pos.reshape(r, CMP_STRIDE * NSA_HD), ((0, SUBLANES - r), (0, 0))).astype(BF16)
    return w1c, posr, w2.astype(BF16)


def _layer_view(arr, l, shape):
    return arr.reshape(shape) if arr.shape[0] == 1 else arr[l].reshape(shape)


def _decoder_layer(x, B, L, q_off, s0, p, sample_ctx):
    alpha = p['alpha']
    h1, h1_bf = _ffn_ln(x, p['ffn1_w_up'], p['ffn1_w_down'], p['ln1_g'], p['ln1_b'], alpha, True)
    cos, sin = _rope_tables(q_off + jnp.arange(L, dtype=jnp.int32))
    P, kv_rows, win_rows = _proj(h1_bf, p['w_in'], p['rope_cols'], p['scale_cols'], cos, sin, L)
    ro, ret_s = _retention(P, s0, p['ret_gn_g'], p['ret_gn_b'], B, L)
    if sample_ctx is None:
        assert L % CMP_STRIDE == 0
        n_cmp = L // CMP_STRIDE - CMP_BLOCK // CMP_STRIDE + 1
        n_vec = 2 * NSA_KV_HEADS
        specs = [pl.BlockSpec((L, NSA_HD), (lambda v: (lambda b, g: (b, KV_CHUNK0 + v)))(v)) for v in range(n_vec)]
        CMP = _compress([P] * n_vec, specs, 1, L, 1, n_cmp, B, p['cmp_w1'], p['cmp_pos'], p['cmp_w2'])
        no = _nsa_prompt(P, CMP, B, L)
    else:
        cache_kv, cache_win, layer, page_table, past_len, page = sample_ctx
        n_pages = page_table.shape[1]
        lk = past_len + L
        assert (lk // CMP_STRIDE) * CMP_STRIDE <= past_len, "compression blocks must lie in the paged past"
        n_cmp = lk // CMP_STRIDE - CMP_BLOCK // CMP_STRIDE + 1
        n_in = _pick_tile(n_pages, (32, 16, 8, 4, 2, 1))
        n_grp = n_pages // n_in
        specs = [pl.BlockSpec((None, None, page, 2, NSA_KV_HEADS, NSA_HD),
                              (lambda i: (lambda b, g, pt: (layer, pt[b, (n_grp - 1 - g) * n_in + i], 0, 0, 0, 0)))(i))
                 for i in range(n_in)]
        CMP = _compress([cache_kv] * n_in, specs, n_in, page, n_grp, n_cmp, B,
                        p['cmp_w1'], p['cmp_pos'], p['cmp_w2'], page_table=page_table)
        no = _nsa_sample(P, CMP, cache_kv, cache_win, layer, page_table, B, L, past_len, page)
    x2 = _out_ln(h1, ro, no, p['w_out'], p['ln2_g'], p['ln2_b'], alpha)
    y, _ = _ffn_ln(x2, p['ffn2_w_up'], p['ffn2_w_down'], p['ln3_g'], p['ln3_b'], alpha, False)
    return y, ret_s, kv_rows, win_rows


def kernel(x_prompt, x_sample, state_ret, cache_nsa_kv, cache_win, page_table, ffn1_w_up, ffn1_w_down, ln1_g, ln1_b, w_in, w_out, ret_gn_g, ret_gn_b, cmp_pos_k, cmp_w1_k, cmp_w2_k, cmp_pos_v, cmp_w1_v, cmp_w2_v, ln2_g, ln2_b, ffn2_w_up, ffn2_w_down, ln3_g, ln3_b):
    B, L, D = x_prompt.shape
    DB, DL, _ = x_sample.shape
    depth = w_in.shape[0]
    n_pool, page = cache_nsa_kv.shape[1], cache_nsa_kv.shape[2]
    n_pages = page_table.shape[1]
    past_len = n_pages * page
    w_buf = cache_win.shape[2]
    alpha = (2.0 * depth) ** 0.25
    rope_np = np.zeros((N_IN_PAD // LANES, LANES), np.float32)
    rope_np[list(ROPE_CHUNKS)] = 1.0
    scale_np = np.ones((N_IN_PAD // LANES, LANES), np.float32)
    scale_np[list(KSCALE_CHUNKS)] = RET_DK ** -0.5
    rope_cols = jnp.asarray(rope_np.reshape(1, N_IN_PAD))
    scale_cols = jnp.asarray(scale_np.reshape(1, N_IN_PAD))

    yp = x_prompt.reshape(B * L, D)
    ys = x_sample.reshape(DB * DL, D)
    outs = [[] for _ in range(6)]
    for l in range(depth):
        k1, p1, k2 = _cmp_weights(cmp_w1_k[l], cmp_pos_k[l], cmp_w2_k[l])
        v1, q1, v2 = _cmp_weights(cmp_w1_v[l], cmp_pos_v[l], cmp_w2_v[l])
        p = {
            'alpha': alpha, 'rope_cols': rope_cols, 'scale_cols': scale_cols,
            'ffn1_w_up': ffn1_w_up[l].astype(BF16), 'ffn1_w_down': (0.5 * ffn1_w_down[l]).astype(BF16),
            'ln1_g': ln1_g[l], 'ln1_b': ln1_b[l],
            'w_in': jnp.pad(w_in[l], ((0, 0), (0, N_IN_PAD - N_IN))).astype(BF16),
            'w_out': w_out[l].astype(BF16),
            'ret_gn_g': ret_gn_g[l], 'ret_gn_b': ret_gn_b[l],
            'cmp_w1': jnp.stack([k1, v1]), 'cmp_pos': jnp.stack([p1, q1]), 'cmp_w2': jnp.stack([k2, v2]),
            'ln2_g': ln2_g[l], 'ln2_b': ln2_b[l],
            'ffn2_w_up': ffn2_w_up[l].astype(BF16), 'ffn2_w_down': (0.5 * ffn2_w_down[l]).astype(BF16),
            'ln3_g': ln3_g[l], 'ln3_b': ln3_b[l],
        }
        s0 = jnp.zeros((B, RET_HEADS, RET_DK, RET_DV), F32)
        yp, rs_p, kv_p, win_p = _decoder_layer(yp, B, L, 0, s0, p, None)
        ctx = (cache_nsa_kv, cache_win, l, page_table, past_len, page)
        ys, rs_s, kv_s, win_s = _decoder_layer(ys, DB, DL, past_len,
                                               _layer_view(state_ret, l, state_ret.shape[1:]), p, ctx)
        wl = min(WINDOW, L)
        outs[0].append(rs_p)
        outs[1].append(rs_s)
        outs[2].append(kv_p.reshape(B, L, 4, NSA_KV_HEADS, NSA_HD))
        outs[3].append(kv_s.reshape(DB, DL, 4, NSA_KV_HEADS, NSA_HD))
        outs[4].append(win_p.reshape(B, L, 2, NSA_KV_HEADS, NSA_HD)[:, L - wl:])
        win_s = win_s.reshape(DB, DL, 2, NSA_KV_HEADS, NSA_HD)
        outs[5].append(jnp.concatenate([cache_win[l], win_s], axis=1)[:, -w_buf:])
    return (yp.reshape(B, L, D), ys.reshape(DB, DL, D), jnp.stack(outs[0]), jnp.stack(outs[1]),
            jnp.stack(outs[2]), jnp.stack(outs[3]), jnp.stack(outs[4]), jnp.stack(outs[5]))
```

```python
import functools

import numpy as np
import jax
import jax.numpy as jnp
from jax import lax
from jax.experimental import pallas as pl
from jax.experimental.pallas import tpu as pltpu

F32 = jnp.float32
BF16 = jnp.bfloat16

LANES = 128
SUBLANES = 8
BF16_ROWS = 16
VMEM_LIMIT_BYTES = 56 * 1024 * 1024

RET_HEADS = 4
RET_DK = 128
RET_DV = 256
RET_CHUNK = 128
NSA_HEADS = 8
NSA_KV_HEADS = 2
NSA_HD = 128
NSA_GROUP = NSA_HEADS // NSA_KV_HEADS
CMP_BLOCK = 32
CMP_STRIDE = 16
CMP_HIDDEN = 2 * NSA_HD
SLC_BLOCK = 64
SLC_SHIFT = 6
N_SELECT = 16
WINDOW = 512
N_BRANCH = 3
ROPE_THETA = 10000.0
LN_EPS = 1e-5
NEG_BIG = -1e30
LOG2_E = 1.4426950408889634

N_IN = 5656
N_IN_PAD = 5760
KV_CHUNK0 = 32
WIN_CHUNK0 = 40
GATE_CHUNK = 44
ROPE_CHUNKS = tuple(range(0, 8)) + tuple(range(24, 32)) + (32, 33, 36, 37, 40, 41)
KSCALE_CHUNKS = tuple(range(4, 8))


def _cparams(sem):
    return pltpu.CompilerParams(dimension_semantics=sem, vmem_limit_bytes=VMEM_LIMIT_BYTES)


def _pick_tile(n, candidates):
    for c in candidates:
        if n % c == 0:
            return c
    return n


def _round_up(n, m):
    return (n + m - 1) // m * m


def _layer_norm(z, g, b):
    mu = jnp.mean(z, axis=-1, keepdims=True)
    zc = z - mu
    var = jnp.mean(zc * zc, axis=-1, keepdims=True)
    return zc * lax.rsqrt(var + LN_EPS) * g + b


def _dot(a, b):
    return jnp.dot(a, b, preferred_element_type=F32)


def _dot_nt(a, b):
    return lax.dot_general(a, b, (((1,), (1,)), ((), ())), preferred_element_type=F32)


def _ffn_ln_body(x_ref, wa_ref, wb_ref, wd_ref, g_ref, b_ref, *rest, nj, alpha, emit_bf16):
    if emit_bf16:
        o_ref, obf_ref, xbf_sc, acc_sc = rest
    else:
        o_ref, xbf_sc, acc_sc = rest
    j = pl.program_id(1)

    @pl.when(j == 0)
    def _init():
        xbf_sc[...] = x_ref[...].astype(BF16)
        acc_sc[...] = jnp.zeros_like(acc_sc)

    xb = xbf_sc[...]
    a = _dot(xb, wa_ref[...])
    b = _dot(xb, wb_ref[...])
    h = (a * jax.nn.sigmoid(a)) * b
    acc_sc[...] += _dot(h.astype(BF16), wd_ref[...])

    @pl.when(j == nj - 1)
    def _finish():
        z = alpha * x_ref[...] + acc_sc[...]
        y = _layer_norm(z, g_ref[...], b_ref[...])
        o_ref[...] = y
        if emit_bf16:
            obf_ref[...] = y.astype(BF16)


def _ffn_ln(x, w_up, w_down, g, b, alpha, emit_bf16):
    T, D = x.shape
    F = w_down.shape[0]
    tm = _pick_tile(T, (512, 256, 128, 64, 32, 16, 8))
    tf = _pick_tile(F, (512, 256, 128))
    nj = F // tf
    out_shape = [jax.ShapeDtypeStruct((T, D), F32)]
    out_specs = [pl.BlockSpec((tm, D), lambda i, j: (i, 0))]
    if emit_bf16:
        out_shape.append(jax.ShapeDtypeStruct((T, D), BF16))
        out_specs.append(pl.BlockSpec((tm, D), lambda i, j: (i, 0)))
    res = pl.pallas_call(
        functools.partial(_ffn_ln_body, nj=nj, alpha=alpha, emit_bf16=emit_bf16),
        grid=(T // tm, nj),
        in_specs=[
            pl.BlockSpec((tm, D), lambda i, j: (i, 0)),
            pl.BlockSpec((D, tf), lambda i, j: (0, j)),
            pl.BlockSpec((D, tf), lambda i, j: (0, nj + j)),
            pl.BlockSpec((tf, D), lambda i, j: (j, 0)),
            pl.BlockSpec((1, D), lambda i, j: (0, 0)),
            pl.BlockSpec((1, D), lambda i, j: (0, 0)),
        ],
        out_specs=out_specs,
        out_shape=out_shape,
        scratch_shapes=[pltpu.VMEM((tm, D), BF16), pltpu.VMEM((tm, D), F32)],
        compiler_params=_cparams(("parallel", "arbitrary")),
        name="ffn_ln",
    )(x, w_up, w_up, w_down, g.reshape(1, D), b.reshape(1, D))
    return res if emit_bf16 else (res[0], None)


def _rows_view(ref, j):
    rows = ref.shape[0]
    n = int(np.prod(ref.shape[1:-1]))
    return ref.reshape(n * rows, ref.shape[-1]), pl.ds(j, rows, stride=n)


def _head_rows(ref, j):
    view, idx = _rows_view(ref, j)
    return view[idx, :]


def _store_rows(ref, j, val):
    view, idx = _rows_view(ref, j)
    view[idx, :] = val


def _proj_body(x_ref, w_ref, cos_ref, sin_ref, rope_ref, scale_ref, o_ref, kv_ref, win_ref, *, n_chunk, j_rows):
    y = _dot(x_ref[...], w_ref[...])
    cos = cos_ref[...]
    sin = sin_ref[...]
    for c in range(n_chunk):
        sl = slice(c * LANES, (c + 1) * LANES)
        yc = y[:, sl]
        roped = yc * cos + pltpu.roll(yc, NSA_HD // 2, 1) * sin
        o_ref[:, sl] = jnp.where(rope_ref[:, sl] > 0.5, roped, yc) * scale_ref[:, sl]

    c0 = KV_CHUNK0 - j_rows * n_chunk
    for r in range(WIN_CHUNK0 - KV_CHUNK0):
        _store_rows(kv_ref, r, o_ref[:, (c0 + r) * LANES:(c0 + r + 1) * LANES])
    c0 = WIN_CHUNK0 - j_rows * n_chunk
    for r in range(GATE_CHUNK - WIN_CHUNK0):
        _store_rows(win_ref, r, o_ref[:, (c0 + r) * LANES:(c0 + r + 1) * LANES])


def _proj(x_bf, w_bf, rope_cols, scale_cols, cos, sin, rows_per_seq):
    T, D = x_bf.shape
    N = w_bf.shape[1]
    tm = _pick_tile(T, (512, 256, 128, 64, 32, 16, 8))
    tn = 1920
    assert N % tn == 0
    n_chunk = tn // LANES
    if rows_per_seq >= tm:
        assert rows_per_seq % tm == 0
        n_tab = rows_per_seq // tm
    else:
        assert tm % rows_per_seq == 0
        cos = jnp.tile(cos, (tm // rows_per_seq, 1))
        sin = jnp.tile(sin, (tm // rows_per_seq, 1))
        n_tab = 1
    j_rows = KV_CHUNK0 // n_chunk
    assert (GATE_CHUNK - 1) // n_chunk == j_rows == N // tn - 1, "cache-row columns must sit in the last column tile"
    kv_shape = (T, 4, NSA_KV_HEADS, NSA_HD)
    win_shape = (T, 2, NSA_KV_HEADS, NSA_HD)
    return pl.pallas_call(
        functools.partial(_proj_body, n_chunk=n_chunk, j_rows=j_rows),
        grid=(T // tm, N // tn),
        in_specs=[
            pl.BlockSpec((tm, D), lambda i, j: (i, 0)),
            pl.BlockSpec((D, tn), lambda i, j: (0, j)),
            pl.BlockSpec((tm, LANES), lambda i, j: (i % n_tab, 0)),
            pl.BlockSpec((tm, LANES), lambda i, j: (i % n_tab, 0)),
            pl.BlockSpec((1, tn), lambda i, j: (0, j)),
            pl.BlockSpec((1, tn), lambda i, j: (0, j)),
        ],
        out_specs=[
            pl.BlockSpec((tm, tn), lambda i, j: (i, j)),
            pl.BlockSpec((tm,) + kv_shape[1:], lambda i, j: (i, 0, 0, 0)),
            pl.BlockSpec((tm,) + win_shape[1:], lambda i, j: (i, 0, 0, 0)),
        ],
        out_shape=[
            jax.ShapeDtypeStruct((T, N), F32),
            jax.ShapeDtypeStruct(kv_shape, F32),
            jax.ShapeDtypeStruct(win_shape, F32),
        ],
        compiler_params=_cparams(("parallel", "arbitrary")),
        name="proj_rope",
    )(x_bf, w_bf, cos, sin, rope_cols, scale_cols)


def _ret_body(q_ref, k_ref, v_ref, g_ref, s0_ref, dm_ref, ind_ref, std_ref, cd_ref, gng_ref, gnb_ref,
              o_ref, sout_ref, s_sc, *, nc, rows, rows_pad, per_step):
    c = pl.program_id(1)

    @pl.when(c == 0)
    def _load_state():
        s_sc[...] = s0_ref[...]

    pad = rows_pad - rows
    for sub in range(per_step):
        rs = slice(sub * rows, (sub + 1) * rows)
        for h in range(RET_HEADS):
            ks = slice(h * RET_DK, (h + 1) * RET_DK)
            vs = slice(h * RET_DV, (h + 1) * RET_DV)
            q = q_ref[rs, ks]
            k = k_ref[rs, ks]
            v = v_ref[rs, vs]
            kd = k * std_ref[h]
            if pad:
                k = jnp.concatenate([k, jnp.zeros((pad, RET_DK), F32)], axis=0)
                kd = jnp.concatenate([kd, jnp.zeros((pad, RET_DK), F32)], axis=0)
                v = jnp.concatenate([v, jnp.zeros((pad, RET_DV), F32)], axis=0)
            s_old = s_sc[h]
            vb = v.astype(BF16)
            a = _dot_nt(q.astype(BF16), k.astype(BF16)) * dm_ref[h]
            o = _dot(a.astype(BF16), vb) + _dot((q * ind_ref[h]).astype(BF16), s_old.astype(BF16))
            s_sc[h] = s_old * cd_ref[h] + _dot(kd.T.astype(BF16), vb)
            mu = jnp.mean(o, axis=-1, keepdims=True)
            oc = o - mu
            var = jnp.mean(oc * oc, axis=-1, keepdims=True)
            on = oc * lax.rsqrt(var + LN_EPS) * gng_ref[:, vs] + gnb_ref[:, vs]
            gate = g_ref[rs, vs]
            o_ref[rs, vs] = (gate * jax.nn.sigmoid(gate)) * on

    @pl.when(c == nc - 1)
    def _store_state():
        sout_ref[...] = s_sc[...]


def _retention(P, s0, gn_g, gn_b, B, L):
    C = RET_CHUNK if L % RET_CHUNK == 0 else L
    nc = L // C
    CP = max(C, LANES)
    lg = jnp.log1p(-jnp.exp2(-5.0 - jnp.arange(RET_HEADS, dtype=F32)))
    i = jnp.arange(C, dtype=F32)
    diff = i[:, None] - i[None, :]
    dmask = jnp.where(diff >= 0, jnp.exp(lg[:, None, None] * jnp.maximum(diff, 0.0)), 0.0)
    dmask = jnp.pad(dmask, ((0, 0), (0, 0), (0, CP - C)))
    in_decay = jnp.broadcast_to(jnp.exp(lg[:, None] * (i + 1.0))[:, :, None], (RET_HEADS, C, RET_DK))
    st_decay = jnp.broadcast_to(jnp.exp(lg[:, None] * (C - 1.0 - i))[:, :, None], (RET_HEADS, C, RET_DK))
    chunk_decay = jnp.broadcast_to(jnp.exp(lg * C)[:, None, None], (RET_HEADS, 1, RET_DV))
    qw = RET_HEADS * RET_DK
    vw = RET_HEADS * RET_DV
    const3 = lambda b, c: (0, 0, 0)
    per_step = _pick_tile(nc, (8, 4, 2, 1))
    nc = nc // per_step
    CS = C * per_step
    return pl.pallas_call(
        functools.partial(_ret_body, nc=nc, rows=C, rows_pad=CP, per_step=per_step),
        grid=(B, nc),
        in_specs=[
            pl.BlockSpec((CS, qw), lambda b, c: (b * nc + c, 0)),
            pl.BlockSpec((CS, qw), lambda b, c: (b * nc + c, 1)),
            pl.BlockSpec((CS, vw), lambda b, c: (b * nc + c, 1)),
            pl.BlockSpec((CS, vw), lambda b, c: (b * nc + c, 2)),
            pl.BlockSpec((None, RET_HEADS, RET_DK, RET_DV), lambda b, c: (b, 0, 0, 0)),
            pl.BlockSpec((RET_HEADS, C, CP), const3),
            pl.BlockSpec((RET_HEADS, C, RET_DK), const3),
            pl.BlockSpec((RET_HEADS, C, RET_DK), const3),
            pl.BlockSpec((RET_HEADS, 1, RET_DV), const3),
            pl.BlockSpec((1, vw), lambda b, c: (0, 0)),
            pl.BlockSpec((1, vw), lambda b, c: (0, 0)),
        ],
        out_specs=[
            pl.BlockSpec((CS, vw), lambda b, c: (b * nc + c, 0)),
            pl.BlockSpec((None, RET_HEADS, RET_DK, RET_DV), lambda b, c: (b, 0, 0, 0)),
        ],
        out_shape=[
            jax.ShapeDtypeStruct((B * L, vw), F32),
            jax.ShapeDtypeStruct((B, RET_HEADS, RET_DK, RET_DV), F32),
        ],
        scratch_shapes=[pltpu.VMEM((RET_HEADS, RET_DK, RET_DV), F32)],
        compiler_params=_cparams(("parallel", "arbitrary")),
        name="retention",
    )(P, P, P, P, s0, dmask, in_decay, st_decay, chunk_decay, gn_g.reshape(1, vw), gn_b.reshape(1, vw))


def _cmp_body(*refs, n_in, rows, n_grp, n_cmp, paged):
    n_vec = 2 * NSA_KV_HEADS
    if paged:
        refs = refs[1:]
    n_src = n_in if paged else n_vec * n_in
    x_refs = refs[:n_src]
    w1_ref, pos_ref, w2_ref, o_ref, carry_sc, xc_sc = refs[n_src:]
    g = pl.program_id(1)
    grp = n_grp - 1 - g
    cpi = rows // CMP_STRIDE
    M = n_in * cpi

    @pl.when(g == 0)
    def _init():
        carry_sc[...] = jnp.zeros_like(carry_sc)

    row = lax.broadcasted_iota(jnp.int32, (M, 1), 0)
    grp_in = 1 if cpi % BF16_ROWS == 0 else BF16_ROWS // cpi
    assert n_in % grp_in == 0 and (grp_in * cpi) % BF16_ROWS == 0
    for t in range(2):
        w1 = w1_ref[t]
        gp = _dot(pos_ref[t], w1)
        posterm = gp[0:1, :CMP_HIDDEN] + gp[1:2, CMP_HIDDEN:]
        for hd in range(NSA_KV_HEADS):
            v = t * NSA_KV_HEADS + hd
            for i0 in range(0, n_in, grp_in):
                if paged:
                    parts = [pltpu.einshape("csd->scd",
                                            _head_rows(x_refs[i], v).reshape(cpi, CMP_STRIDE, NSA_HD))
                             for i in range(i0, i0 + grp_in)]
                    piece = lambda s: jnp.concatenate([xt[s] for xt in parts], axis=0)
                else:
                    refs_i = [x_refs[v * n_in + i] for i in range(i0, i0 + grp_in)]
                    piece = lambda s: jnp.concatenate(
                        [r[pl.ds(s, cpi, stride=CMP_STRIDE), :] for r in refs_i], axis=0)
                r0 = v * M + i0 * cpi
                for s in range(CMP_STRIDE):
                    xc_sc[r0:r0 + grp_in * cpi, s * NSA_HD:(s + 1) * NSA_HD] = piece(s).astype(BF16)
        t0 = t * NSA_KV_HEADS * M
        gg_all = _dot(xc_sc[t0:t0 + NSA_KV_HEADS * M, :], w1)
        for hd in range(NSA_KV_HEADS):
            v = t * NSA_KV_HEADS + hd
            gg = gg_all[hd * M:(hd + 1) * M]
            g0 = gg[:, :CMP_HIDDEN]
            g1 = gg[:, CMP_HIDDEN:]
            nxt = pltpu.roll(g1, M - 1, 0)
            nxt = jnp.where(row == M - 1, carry_sc[v][0:1, :], nxt)
            carry_sc[v] = g1[0:SUBLANES, :]
            hid = g0 + nxt + posterm
            out = _dot(jax.nn.gelu(hid).astype(BF16), w2_ref[t])
            o_ref[t, hd] = jnp.where(grp * M + row < n_cmp, out, 0.0)


def _compress(srcs, src_specs, n_in, rows, n_grp, n_cmp, B, w1, pos, w2, page_table=None):
    n_vec = 2 * NSA_KV_HEADS
    M = n_in * rows // CMP_STRIDE
    nch = n_grp * M
    paged = page_table is not None
    const3 = lambda *a: (0, 0, 0)
    in_specs = list(src_specs) + [
        pl.BlockSpec((2, CMP_STRIDE * NSA_HD, 2 * CMP_HIDDEN), const3),
        pl.BlockSpec((2, SUBLANES, CMP_STRIDE * NSA_HD), const3),
        pl.BlockSpec((2, CMP_HIDDEN, NSA_HD), const3),
    ]
    out_spec = pl.BlockSpec((None, 2, NSA_KV_HEADS, M, NSA_HD), lambda *a: (a[0], 0, 0, n_grp - 1 - a[1], 0))
    body = functools.partial(_cmp_body, n_in=n_in, rows=rows, n_grp=n_grp, n_cmp=n_cmp, paged=paged)
    out_shape = jax.ShapeDtypeStruct((B, 2, NSA_KV_HEADS, nch, NSA_HD), F32)
    scratch = [pltpu.VMEM((n_vec, SUBLANES, CMP_HIDDEN), F32),
               pltpu.VMEM((n_vec * M, CMP_STRIDE * NSA_HD), BF16)]
    sem = ("parallel", "arbitrary")
    if paged:
        grid_spec = pltpu.PrefetchScalarGridSpec(
            num_scalar_prefetch=1, grid=(B, n_grp), in_specs=in_specs, out_specs=out_spec,
            scratch_shapes=scratch)
        return pl.pallas_call(body, grid_spec=grid_spec, out_shape=out_shape,
                              compiler_params=_cparams(sem), name="nsa_compress_paged")(
            page_table, *srcs, w1, pos, w2)
    return pl.pallas_call(body, grid=(B, n_grp), in_specs=in_specs, out_specs=out_spec,
                          out_shape=out_shape, scratch_shapes=scratch,
                          compiler_params=_cparams(sem), name="nsa_compress")(*srcs, w1, pos, w2)


def _cover_matrix(n_cmp, n_slc, rows, cols):
    c_i = np.arange(n_cmp)[:, None]
    n_i = np.arange(n_slc)[None, :]
    cov = np.clip(np.minimum(c_i * CMP_STRIDE + CMP_BLOCK, (n_i + 1) * SLC_BLOCK)
                  - np.maximum(c_i * CMP_STRIDE, n_i * SLC_BLOCK), 0, None).astype(np.float32) / CMP_BLOCK
    out = np.zeros((rows, cols), np.float32)
    out[:n_cmp, :n_slc] = cov
    return out


def _nsa_prompt_body(q_ref, kc_ref, vc_ref, ks_ref, vs_ref, kw_ref, vw_ref, gate_ref, covt_ref, o_ref,
                     kcb_sc, vct_sc, ksb_sc, vst_sc, kwb_sc, vwt_sc, prio_sc, sel_sc, gt_sc,
                     m_sc, l_sc, acc_sc, out_sc, s_sc, *, tq, L, n_cmp, n_slc, n_sel, nbp):
    KV = NSA_KV_HEADS
    qi = pl.program_id(1)
    tk = tq
    nch = kc_ref.shape[1]
    scale = NSA_HD ** -0.5

    @pl.when(qi == 0)
    def _stage_kv():
        for kv in range(KV):
            cs = slice(kv * NSA_HD, (kv + 1) * NSA_HD)
            kcb_sc[kv] = kc_ref[kv].astype(BF16)
            vct_sc[kv] = vc_ref[kv].T.astype(BF16)
            ksb_sc[kv] = ks_ref[:, cs].astype(BF16)
            kwb_sc[kv] = kw_ref[:, cs].astype(BF16)
            for i in range(L // tk):
                vst_sc[kv, i] = vs_ref[i * tk:(i + 1) * tk, cs].T.astype(BF16)
                vwt_sc[kv, i] = vw_ref[i * tk:(i + 1) * tk, cs].T.astype(BF16)

    G = NSA_GROUP
    W = G * tq
    t0 = qi * tq
    t_row = t0 + lax.broadcasted_iota(jnp.int32, (1, tq), 1)
    lane = lax.broadcasted_iota(jnp.int32, (1, W), 1)
    t_all = t0 + (lane & (tq - 1))
    qcats = []
    for kv in range(KV):
        qc = jnp.concatenate([q_ref[:, (kv * G + g) * NSA_HD:(kv * G + g + 1) * NSA_HD] for g in range(G)], axis=0)
        qcats.append((qc * (scale * LOG2_E)).astype(BF16))
    gt_sc[...] = jax.nn.sigmoid(gate_ref[...]).T

    def gate_row(kv, br):
        return jnp.concatenate(
            [gt_sc[(kv * G + g) * N_BRANCH + br:(kv * G + g) * N_BRANCH + br + 1, :] for g in range(G)], axis=1)

    kp_l = lax.broadcasted_iota(jnp.int32, (tk, 1), 0)
    t_l = lane & (tq - 1)
    causal_bias = jnp.where(kp_l <= t_l, 0.0, NEG_BIG)
    far_bias = jnp.where(kp_l > t_l, 0.0, NEG_BIG)

    def reset(kv):
        m_sc[kv] = jnp.full((1, W), NEG_BIG, F32)
        l_sc[kv] = jnp.zeros((1, W), F32)
        acc_sc[kv] = jnp.zeros((NSA_HD, W), F32)

    def score(kb_sc, kv, kt, bias):
        k0 = pl.multiple_of(kt * tk, tk)
        s = _dot_nt(kb_sc[kv, pl.ds(k0, tk), :], qcats[kv])
        return s if bias is None else s + bias

    def update(vt_sc, kv, kt, s):
        m_old = m_sc[kv]
        m_new = jnp.maximum(m_old, jnp.max(s, axis=0, keepdims=True))
        p = jnp.exp2(s - m_new)
        alpha = jnp.exp2(m_old - m_new)
        l_sc[kv] = alpha * l_sc[kv] + jnp.sum(p, axis=0, keepdims=True)
        acc_sc[kv] = alpha * acc_sc[kv] + _dot(vt_sc[kv, kt], p.astype(BF16))
        m_sc[kv] = m_new

    def branch_out(kv, br):
        return (gate_row(kv, br) / l_sc[kv]) * acc_sc[kv]

    far = WINDOW // tk
    for kv in range(KV):
        reset(kv)
        tiles = []
        for back in range(far, -1, -1):
            bias = far_bias if back == far else (causal_bias if back == 0 else None)
            if back > 0:
                off = jnp.where(qi >= back, 0.0, NEG_BIG).astype(F32)
                bias = off if bias is None else bias + off
            kt = jnp.maximum(qi - back, 0)
            tiles.append((kt, score(kwb_sc, kv, kt, bias)))
        for kt, s_w in tiles:
            update(vwt_sc, kv, kt, s_w)
        out_sc[kv] = branch_out(kv, 2)

    c_i = lax.broadcasted_iota(jnp.int32, (nch, 1), 0)
    c_end = jnp.where(c_i < n_cmp, c_i * CMP_STRIDE + (CMP_BLOCK - 1), L)
    n_i = lax.broadcasted_iota(jnp.int32, (nbp, 1), 0)
    valid = (n_i * SLC_BLOCK <= t_row) & (n_i < n_slc)
    cur = t_row >> SLC_SHIFT
    forced = (n_i == 0) | (n_i == cur) | (n_i == cur - 1)
    for kv in range(KV):
        s = jnp.where(c_end <= t_all, _dot_nt(kcb_sc[kv], qcats[kv]), -jnp.inf)
        m = jnp.max(s, axis=0, keepdims=True)
        m = jnp.where(m > -jnp.inf, m, 0.0)
        e = jnp.exp2(s - m)
        d = jnp.sum(e, axis=0, keepdims=True)
        p = e / jnp.where(d > 0, d, 1.0)
        out_sc[kv] += gate_row(kv, 0) * _dot(vct_sc[kv], p.astype(BF16))
        psum = p[:, 0:tq]
        for g in range(1, G):
            psum = psum + p[:, g * tq:(g + 1) * tq]
        imp = jnp.dot(covt_ref[...], psum, preferred_element_type=F32, precision=lax.Precision.HIGHEST)
        prio = jnp.where(forced, jnp.inf, jnp.where(valid, imp, -jnp.inf))
        prio_sc[kv] = prio
        cnt = jnp.zeros((nbp, tq), jnp.int32)
        for mm in range(n_slc):
            pm = prio_sc[kv, mm:mm + 1, :]
            tie = jnp.where(n_i > mm, 1, 0)
            cnt = cnt + jnp.where(pm > prio, 1, jnp.where(pm == prio, tie, 0))
        selbias = jnp.where((cnt < n_sel) & valid, 0.0, NEG_BIG)
        sel_sc[kv] = jnp.concatenate([selbias] * G, axis=1)

    def sel_bias(kv, kt):
        per_tile = tk // SLC_BLOCK
        rows = [jnp.broadcast_to(sel_sc[kv, pl.ds(kt * per_tile + r, 1), :], (SLC_BLOCK, W))
                for r in range(per_tile)]
        return jnp.concatenate(rows, axis=0)

    for kv in range(KV):
        reset(kv)
        s_sc[kv] = score(ksb_sc, kv, 0, sel_bias(kv, 0))

    def slc_body(kt, carry):
        nxt = [score(ksb_sc, kv, kt + 1, sel_bias(kv, kt + 1)) for kv in range(KV)]
        for kv in range(KV):
            update(vst_sc, kv, kt, s_sc[kv])
        for kv in range(KV):
            s_sc[kv] = nxt[kv]
        return carry

    lax.fori_loop(0, qi, slc_body, 0)
    for kv in range(KV):
        update(vst_sc, kv, qi, s_sc[kv] + causal_bias)
        out_sc[kv] += branch_out(kv, 1)
        for g in range(G):
            h = kv * G + g
            o_ref[:, h * NSA_HD:(h + 1) * NSA_HD] = out_sc[kv, :, g * tq:(g + 1) * tq].T


def _nsa_prompt(P, CMP, B, L):
    tq = _pick_tile(L, (256, 128))
    assert L % tq == 0 and tq % SLC_BLOCK == 0 and WINDOW % tq == 0 and tq & (tq - 1) == 0
    nq = L // tq
    nch = CMP.shape[3]
    n_cmp = L // CMP_STRIDE - CMP_BLOCK // CMP_STRIDE + 1
    n_slc = -(-L // SLC_BLOCK)
    n_sel = min(N_SELECT, n_slc)
    nbp = _round_up(n_slc, SUBLANES)
    covt = jnp.asarray(_cover_matrix(n_cmp, n_slc, nch, nbp).T)
    qw = NSA_HEADS * NSA_HD
    kvw = NSA_KV_HEADS * NSA_HD
    KV = NSA_KV_HEADS
    W = NSA_GROUP * tq
    kv_col = lambda c: (lambda b, i: (b, (KV_CHUNK0 + c) // NSA_KV_HEADS))
    body = functools.partial(_nsa_prompt_body, tq=tq, L=L, n_cmp=n_cmp, n_slc=n_slc, n_sel=n_sel, nbp=nbp)
    return pl.pallas_call(
        body,
        grid=(B, nq),
        in_specs=[
            pl.BlockSpec((tq, qw), lambda b, i: (b * nq + i, 3)),
            pl.BlockSpec((None, None, KV, nch, NSA_HD), lambda b, i: (b, 0, 0, 0, 0)),
            pl.BlockSpec((None, None, KV, nch, NSA_HD), lambda b, i: (b, 1, 0, 0, 0)),
            pl.BlockSpec((L, kvw), kv_col(4)),
            pl.BlockSpec((L, kvw), kv_col(6)),
            pl.BlockSpec((L, kvw), kv_col(8)),
            pl.BlockSpec((L, kvw), kv_col(10)),
            pl.BlockSpec((tq, LANES), lambda b, i: (b * nq + i, GATE_CHUNK)),
            pl.BlockSpec((nbp, nch), lambda b, i: (0, 0)),
        ],
        out_specs=pl.BlockSpec((tq, qw), lambda b, i: (b * nq + i, 0)),
        out_shape=jax.ShapeDtypeStruct((B * L, qw), F32),
        scratch_shapes=[
            pltpu.VMEM((KV, nch, NSA_HD), BF16),
            pltpu.VMEM((KV, NSA_HD, nch), BF16),
            pltpu.VMEM((KV, L, NSA_HD), BF16),
            pltpu.VMEM((KV, L // tq, NSA_HD, tq), BF16),
            pltpu.VMEM((KV, L, NSA_HD), BF16),
            pltpu.VMEM((KV, L // tq, NSA_HD, tq), BF16),
            pltpu.VMEM((KV, nbp, tq), F32),
            pltpu.VMEM((KV, nbp, W), F32),
            pltpu.VMEM((LANES, tq), F32),
            pltpu.VMEM((KV, 1, W), F32),
            pltpu.VMEM((KV, 1, W), F32),
            pltpu.VMEM((KV, NSA_HD, W), F32),
            pltpu.VMEM((KV, NSA_HD, W), F32),
            pltpu.VMEM((KV, tq, W), F32),
        ],
        compiler_params=_cparams(("parallel", "arbitrary")),
        name="nsa_prompt",
    )(P, CMP, CMP, P, P, P, P, P, covt)


def _nsa_sample_body(*refs, nb, pg, n_steps, dl, q_off, w_buf, n_cmp, n_slc, n_sel, nbl, page):
    refs = refs[1:]
    q_ref, kvn_ref, wn_ref, gate_ref, kc_ref, vc_ref, cov_ref = refs[:7]
    cw_refs = refs[7:7 + nb]
    pg_refs = refs[7 + nb:7 + nb + nb * pg]
    o_ref, qb_sc, sel_sc, m_sc, l_sc, acc_sc, ocmp_sc = refs[7 + nb + nb * pg:]
    step = pl.program_id(1)
    KVH = NSA_KV_HEADS
    n_str = nb * KVH
    rows_h = NSA_GROUP * dl
    scale = NSA_HD ** -0.5
    nch = kc_ref.shape[2]

    def tok_pos(n):
        r = lax.broadcasted_iota(jnp.int32, (n, 1), 0)
        return q_off + r % dl

    @pl.when(step == 0)
    def _select():
        for jk in range(n_str):
            j, k = divmod(jk, KVH)
            for g in range(NSA_GROUP):
                h = k * NSA_GROUP + g
                qb_sc[pl.ds((jk * NSA_GROUP + g) * dl, dl), :] = (
                    q_ref[j * dl:(j + 1) * dl, h * NSA_HD:(h + 1) * NSA_HD] * scale).astype(BF16)
        t_h = tok_pos(rows_h)
        c_i = lax.broadcasted_iota(jnp.int32, (1, nch), 1)
        cmask = (c_i * CMP_STRIDE + (CMP_BLOCK - 1) <= t_h) & (c_i < n_cmp)
        psums = []
        for jk in range(n_str):
            j, k = divmod(jk, KVH)
            qk = qb_sc[pl.ds(jk * rows_h, rows_h), :]
            s = jnp.where(cmask, _dot_nt(qk, kc_ref[j, k].astype(BF16)), -jnp.inf)
            m = jnp.max(s, axis=1, keepdims=True)
            m = jnp.where(m > -jnp.inf, m, 0.0)
            e = jnp.exp(s - m)
            d = jnp.sum(e, axis=1, keepdims=True)
            p = e / jnp.where(d > 0, d, 1.0)
            ocmp_sc[pl.ds(jk * rows_h, rows_h), :] = _dot(p.astype(BF16), vc_ref[j, k].astype(BF16))
            psum = p[0:dl]
            for g in range(1, NSA_GROUP):
                psum = psum + p[g * dl:(g + 1) * dl]
            psums.append(psum)
        imp = jnp.dot(jnp.concatenate(psums, axis=0), cov_ref[...], preferred_element_type=F32,
                      precision=lax.Precision.HIGHEST)
        t_s = tok_pos(n_str * dl)
        n_i = lax.broadcasted_iota(jnp.int32, (1, nbl), 1)
        valid = (n_i * SLC_BLOCK <= t_s) & (n_i < n_slc)
        cur = t_s >> SLC_SHIFT
        forced = (n_i == 0) | (n_i == cur) | (n_i == cur - 1)
        prio = jnp.where(forced, jnp.inf, jnp.where(valid, imp, -jnp.inf))
        n_f = n_i.astype(F32)
        alive = jnp.broadcast_to(jnp.where(n_i < n_slc, 1.0, 0.0), prio.shape)
        sel = jnp.zeros(prio.shape, F32)
        for _ in range(n_sel):
            mx = jnp.max(jnp.where(alive > 0.5, prio, -jnp.inf), axis=1, keepdims=True)
            cand = (alive > 0.5) & (prio == mx)
            first = jnp.min(jnp.where(cand, n_f, float(nbl)), axis=1, keepdims=True)
            pick = n_f == first
            sel = jnp.where(pick, 1.0, sel)
            alive = jnp.where(pick, 0.0, alive)
        sel = jnp.where(valid, sel, 0.0)
        for jk in range(n_str):
            for g in range(NSA_GROUP):
                sel_sc[pl.ds((jk * NSA_GROUP + g) * dl, dl), :] = sel[jk * dl:(jk + 1) * dl]
        m_sc[...] = jnp.full(m_sc.shape, NEG_BIG, F32)
        l_sc[...] = jnp.zeros_like(l_sc)
        acc_sc[...] = jnp.zeros_like(acc_sc)

    selb = sel_sc[...].astype(BF16)
    n_col = lax.broadcasted_iota(jnp.int32, (nbl, 1), 0)

    def online_update(jk, s, mask, v_rows):
        rs = pl.ds(jk * rows_h, rows_h)
        m_old = m_sc[rs, :]
        m_new = jnp.maximum(m_old, jnp.max(s, axis=1, keepdims=True))
        p = jnp.where(mask, jnp.exp(s - m_new), 0.0)
        alpha = jnp.exp(m_old - m_new)
        l_sc[rs, :] = alpha * l_sc[rs, :] + jnp.sum(p, axis=1, keepdims=True)
        acc_sc[rs, :] = alpha * acc_sc[rs, :] + _dot(p.astype(BF16), v_rows)
        m_sc[rs, :] = m_new

    t_h = tok_pos(rows_h)

    def selected(kpos):
        blk = jnp.where(n_col == (kpos >> SLC_SHIFT), 1.0, 0.0).astype(BF16)
        return _dot(selb, blk)

    kpos = step * (pg * page) + lax.broadcasted_iota(jnp.int32, (1, pg * page), 1)
    sel_all = selected(kpos)
    for jk in range(n_str):
        j, k = divmod(jk, KVH)
        pages = pg_refs[j * pg:(j + 1) * pg]
        qk = qb_sc[pl.ds(jk * rows_h, rows_h), :]
        mask = (sel_all[jk * rows_h:(jk + 1) * rows_h] > 0.5) & (kpos <= t_h)
        k_rows = jnp.concatenate([_head_rows(r, k).astype(BF16) for r in pages], axis=0)
        v_rows = jnp.concatenate([_head_rows(r, KVH + k).astype(BF16) for r in pages], axis=0)
        s = jnp.where(mask, _dot_nt(qk, k_rows), NEG_BIG)
        online_update(jk, s, mask, v_rows)

    @pl.when(step == n_steps - 1)
    def _finish():
        padn = LANES - dl
        j_new = lax.broadcasted_iota(jnp.int32, (1, LANES), 1)
        kpos_n = q_off + j_new
        sel_n = selected(kpos_n)
        for jk in range(n_str):
            j, k = divmod(jk, KVH)
            tok = slice(j * dl, (j + 1) * dl)
            kn = jnp.concatenate([kvn_ref[tok, k * NSA_HD:(k + 1) * NSA_HD], jnp.zeros((padn, NSA_HD), F32)],
                                 axis=0)
            vn = jnp.concatenate([kvn_ref[tok, (KVH + k) * NSA_HD:(KVH + k + 1) * NSA_HD],
                                  jnp.zeros((padn, NSA_HD), F32)], axis=0)
            qk = qb_sc[pl.ds(jk * rows_h, rows_h), :]
            mask = (sel_n[jk * rows_h:(jk + 1) * rows_h] > 0.5) & (kpos_n <= t_h) & (j_new < dl)
            s = jnp.where(mask, _dot_nt(qk, kn.astype(BF16)), NEG_BIG)
            online_update(jk, s, mask, vn.astype(BF16))
        j_w = lax.broadcasted_iota(jnp.int32, (1, w_buf + LANES), 1)
        pos_w = q_off - w_buf + j_w
        dlt = t_h - pos_w
        wmask = (j_w < w_buf + dl) & (pos_w >= 0) & (dlt >= 0) & (dlt < WINDOW)
        gates = jax.nn.sigmoid(gate_ref[...])
        for jk in range(n_str):
            j, k = divmod(jk, KVH)
            tok = slice(j * dl, (j + 1) * dl)
            kw = jnp.concatenate([_head_rows(cw_refs[j], k),
                                  wn_ref[tok, k * NSA_HD:(k + 1) * NSA_HD], jnp.zeros((padn, NSA_HD), F32)], axis=0)
            vw = jnp.concatenate([_head_rows(cw_refs[j], KVH + k),
                                  wn_ref[tok, (KVH + k) * NSA_HD:(KVH + k + 1) * NSA_HD],
                                  jnp.zeros((padn, NSA_HD), F32)], axis=0)
            qk = qb_sc[pl.ds(jk * rows_h, rows_h), :]
            s = jnp.where(wmask, _dot_nt(qk, kw.astype(BF16)), -jnp.inf)
            m = jnp.max(s, axis=1, keepdims=True)
            m = jnp.where(m > -jnp.inf, m, 0.0)
            e = jnp.exp(s - m)
            d = jnp.sum(e, axis=1, keepdims=True)
            o_win = _dot((e / jnp.where(d > 0, d, 1.0)).astype(BF16), vw.astype(BF16))
            rs = pl.ds(jk * rows_h, rows_h)
            o_slc = acc_sc[rs, :] / l_sc[rs, :]
            o_cmp = ocmp_sc[rs, :]
            for g in range(NSA_GROUP):
                h = k * NSA_GROUP + g
                r = slice(g * dl, (g + 1) * dl)
                gc = gates[tok, h * N_BRANCH:h * N_BRANCH + 1]
                gs = gates[tok, h * N_BRANCH + 1:h * N_BRANCH + 2]
                gw = gates[tok, h * N_BRANCH + 2:h * N_BRANCH + 3]
                o_ref[tok, h * NSA_HD:(h + 1) * NSA_HD] = gc * o_cmp[r] + gs * o_slc[r] + gw * o_win[r]


def _nsa_sample(P, CMP, cache_kv, cache_win, layer, page_table, DB, DL, past_len, page):
    n_pages = page_table.shape[1]
    w_buf = cache_win.shape[2]
    lk = past_len + DL
    nch = CMP.shape[3]
    n_cmp = lk // CMP_STRIDE - CMP_BLOCK // CMP_STRIDE + 1
    n_slc = -(-lk // SLC_BLOCK)
    n_sel = min(N_SELECT, n_slc)
    nbl = _round_up(n_slc, LANES)
    pg = _pick_tile(n_pages, (32, 16, 8, 4, 2, 1))
    n_steps = n_pages // pg
    nb = _pick_tile(DB, (2, 1))
    assert DL % SUBLANES == 0 and DL <= LANES and page == LANES and past_len == n_pages * page
    cov = jnp.asarray(_cover_matrix(n_cmp, n_slc, nch, nbl))
    kvw = NSA_KV_HEADS * NSA_HD
    page_spec = lambda j, i: pl.BlockSpec((None, None, page, 2, NSA_KV_HEADS, NSA_HD),
                                          lambda b, s, pt: (layer, pt[b * nb + j, s * pg + i], 0, 1, 0, 0))
    win_spec = lambda j: pl.BlockSpec((None, None, w_buf, 2, NSA_KV_HEADS, NSA_HD),
                                      lambda b, s, pt: (layer, b * nb + j, 0, 0, 0, 0))
    tok = nb * DL
    in_specs = [
        pl.BlockSpec((tok, NSA_HEADS * NSA_HD), lambda b, s, pt: (b, 3)),
        pl.BlockSpec((tok, 2 * kvw), lambda b, s, pt: (b, 9)),
        pl.BlockSpec((tok, 2 * kvw), lambda b, s, pt: (b, 10)),
        pl.BlockSpec((tok, LANES), lambda b, s, pt: (b, GATE_CHUNK)),
        pl.BlockSpec((nb, None, NSA_KV_HEADS, nch, NSA_HD), lambda b, s, pt: (b, 0, 0, 0, 0)),
        pl.BlockSpec((nb, None, NSA_KV_HEADS, nch, NSA_HD), lambda b, s, pt: (b, 1, 0, 0, 0)),
        pl.BlockSpec((nch, nbl), lambda b, s, pt: (0, 0)),
    ] + [win_spec(j) for j in range(nb)] + [page_spec(j, i) for j in range(nb) for i in range(pg)]
    rows = nb * NSA_HEADS * DL
    grid_spec = pltpu.PrefetchScalarGridSpec(
        num_scalar_prefetch=1,
        grid=(DB // nb, n_steps),
        in_specs=in_specs,
        out_specs=pl.BlockSpec((tok, NSA_HEADS * NSA_HD), lambda b, s, pt: (b, 0)),
        scratch_shapes=[
            pltpu.VMEM((rows, NSA_HD), BF16),
            pltpu.VMEM((rows, nbl), F32),
            pltpu.VMEM((rows, 1), F32),
            pltpu.VMEM((rows, 1), F32),
            pltpu.VMEM((rows, NSA_HD), F32),
            pltpu.VMEM((rows, NSA_HD), F32),
        ],
    )
    body = functools.partial(_nsa_sample_body, nb=nb, pg=pg, n_steps=n_steps, dl=DL, q_off=past_len, w_buf=w_buf,
                             n_cmp=n_cmp, n_slc=n_slc, n_sel=n_sel, nbl=nbl, page=page)
    return pl.pallas_call(
        body, grid_spec=grid_spec,
        out_shape=jax.ShapeDtypeStruct((DB * DL, NSA_HEADS * NSA_HD), F32),
        compiler_params=_cparams(("parallel", "arbitrary")),
        name="nsa_sample",
    )(page_table, P, P, P, P, CMP, CMP, cov, *([cache_win] * nb), *([cache_kv] * (nb * pg)))


def _out_ln_body(h_ref, ro_ref, no_ref, wr_ref, wn_ref, g_ref, b_ref, o_ref, *, alpha):
    tm = h_ref.shape[0]
    part = tm // 4 if tm % (4 * BF16_ROWS) == 0 else tm
    for r0 in range(0, tm, part):
        rows = slice(r0, r0 + part)
        m = _dot(ro_ref[rows, :].astype(BF16), wr_ref[...]) + _dot(no_ref[rows, :].astype(BF16), wn_ref[...])
        o_ref[rows, :] = _layer_norm(alpha * h_ref[rows, :] + m, g_ref[...], b_ref[...])


def _out_ln(h, ro, no, w_out, g, b, alpha):
    T, D = h.shape
    kr = ro.shape[1]
    kn = no.shape[1]
    tm = _pick_tile(T, (512, 256, 128, 64, 32, 16, 8))
    return pl.pallas_call(
        functools.partial(_out_ln_body, alpha=alpha),
        grid=(T // tm,),
        in_specs=[
            pl.BlockSpec((tm, D), lambda i: (i, 0)),
            pl.BlockSpec((tm, kr), lambda i: (i, 0)),
            pl.BlockSpec((tm, kn), lambda i: (i, 0)),
            pl.BlockSpec((kr, D), lambda i: (0, 0)),
            pl.BlockSpec((kn, D), lambda i: (1, 0)),
            pl.BlockSpec((1, D), lambda i: (0, 0)),
            pl.BlockSpec((1, D), lambda i: (0, 0)),
        ],
        out_specs=pl.BlockSpec((tm, D), lambda i: (i, 0)),
        out_shape=jax.ShapeDtypeStruct((T, D), F32),
        compiler_params=_cparams(("parallel",)),
        name="out_ln",
    )(h, ro, no, w_out, w_out, g.reshape(1, D), b.reshape(1, D))


def _rope_tables(pos):
    half = NSA_HD // 2
    inv = ROPE_THETA ** (-jnp.arange(half, dtype=F32) / half)
    ang = pos.astype(F32)[:, None] * inv[None, :]
    cos = jnp.cos(ang)
    sin = jnp.sin(ang)
    return jnp.concatenate([cos, cos], -1), jnp.concatenate([-sin, sin], -1)


def _cmp_weights(w1, pos, w2):
    r = CMP_BLOCK // CMP_STRIDE
    w1r = w1.reshape(r, CMP_STRIDE * NSA_HD, CMP_HIDDEN)
    w1c = jnp.concatenate([w1r[i] for i in range(r)], axis=1).astype(BF16)
    posr = jnp.pad(pos.reshape(r, CMP_STRIDE * NSA_HD), ((0, SUBLANES - r), (0, 0))).astype(BF16)
    return w1c, posr, w2.astype(BF16)


def _layer_view(arr, l, shape):
    return arr.reshape(shape) if arr.shape[0] == 1 else arr[l].reshape(shape)


def _decoder_layer(x, B, L, q_off, s0, p, sample_ctx):
    alpha = p['alpha']
    h1, h1_bf = _ffn_ln(x, p['ffn1_w_up'], p['ffn1_w_down'], p['ln1_g'], p['ln1_b'], alpha, True)
    cos, sin = _rope_tables(q_off + jnp.arange(L, dtype=jnp.int32))
    P, kv_rows, win_rows = _proj(h1_bf, p['w_in'], p['rope_cols'], p['scale_cols'], cos, sin, L)
    ro, ret_s = _retention(P, s0, p['ret_gn_g'], p['ret_gn_b'], B, L)
    if sample_ctx is None:
        assert L % CMP_STRIDE == 0
        n_cmp = L // CMP_STRIDE - CMP_BLOCK // CMP_STRIDE + 1
        n_vec = 2 * NSA_KV_HEADS
        specs = [pl.BlockSpec((L, NSA_HD), (lambda v: (lambda b, g: (b, KV_CHUNK0 + v)))(v)) for v in range(n_vec)]
        CMP = _compress([P] * n_vec, specs, 1, L, 1, n_cmp, B, p['cmp_w1'], p['cmp_pos'], p['cmp_w2'])
        no = _nsa_prompt(P, CMP, B, L)
    else:
        cache_kv, cache_win, layer, page_table, past_len, page = sample_ctx
        n_pages = page_table.shape[1]
        lk = past_len + L
        assert (lk // CMP_STRIDE) * CMP_STRIDE <= past_len, "compression blocks must lie in the paged past"
        n_cmp = lk // CMP_STRIDE - CMP_BLOCK // CMP_STRIDE + 1
        n_in = _pick_tile(n_pages, (32, 16, 8, 4, 2, 1))
        n_grp = n_pages // n_in
        specs = [pl.BlockSpec((None, None, page, 2, NSA_KV_HEADS, NSA_HD),
                              (lambda i: (lambda b, g, pt: (layer, pt[b, (n_grp - 1 - g) * n_in + i], 0, 0, 0, 0)))(i))
                 for i in range(n_in)]
        CMP = _compress([cache_kv] * n_in, specs, n_in, page, n_grp, n_cmp, B,
                        p['cmp_w1'], p['cmp_pos'], p['cmp_w2'], page_table=page_table)
        no = _nsa_sample(P, CMP, cache_kv, cache_win, layer, page_table, B, L, past_len, page)
    x2 = _out_ln(h1, ro, no, p['w_out'], p['ln2_g'], p['ln2_b'], alpha)
    y, _ = _ffn_ln(x2, p['ffn2_w_up'], p['ffn2_w_down'], p['ln3_g'], p['ln3_b'], alpha, False)
    return y, ret_s, kv_rows, win_rows


def kernel(x_prompt, x_sample, state_ret, cache_nsa_kv, cache_win, page_table, ffn1_w_up, ffn1_w_down, ln1_g, ln1_b, w_in, w_out, ret_gn_g, ret_gn_b, cmp_pos_k, cmp_w1_k, cmp_w2_k, cmp_pos_v, cmp_w1_v, cmp_w2_v, ln2_g, ln2_b, ffn2_w_up, ffn2_w_down, ln3_g, ln3_b):
    B, L, D = x_prompt.shape
    DB, DL, _ = x_sample.shape
    depth = w_in.shape[0]
    n_pool, page = cache_nsa_kv.shape[1], cache_nsa_kv.shape[2]
    n_pages = page_table.shape[1]
    past_len = n_pages * page
    w_buf = cache_win.shape[2]
    alpha = (2.0 * depth) ** 0.25
    rope_np = np.zeros((N_IN_PAD // LANES, LANES), np.float32)
    rope_np[list(ROPE_CHUNKS)] = 1.0
    scale_np = np.ones((N_IN_PAD // LANES, LANES), np.float32)
    scale_np[list(KSCALE_CHUNKS)] = RET_DK ** -0.5
    rope_cols = jnp.asarray(rope_np.reshape(1, N_IN_PAD))
    scale_cols = jnp.asarray(scale_np.reshape(1, N_IN_PAD))

    yp = x_prompt.reshape(B * L, D)
    ys = x_sample.reshape(DB * DL, D)
    outs = [[] for _ in range(6)]
    for l in range(depth):
        k1, p1, k2 = _cmp_weights(cmp_w1_k[l], cmp_pos_k[l], cmp_w2_k[l])
        v1, q1, v2 = _cmp_weights(cmp_w1_v[l], cmp_pos_v[l], cmp_w2_v[l])
        p = {
            'alpha': alpha, 'rope_cols': rope_cols, 'scale_cols': scale_cols,
            'ffn1_w_up': ffn1_w_up[l].astype(BF16), 'ffn1_w_down': (0.5 * ffn1_w_down[l]).astype(BF16),
            'ln1_g': ln1_g[l], 'ln1_b': ln1_b[l],
            'w_in': jnp.pad(w_in[l], ((0, 0), (0, N_IN_PAD - N_IN))).astype(BF16),
            'w_out': w_out[l].astype(BF16),
            'ret_gn_g': ret_gn_g[l], 'ret_gn_b': ret_gn_b[l],
            'cmp_w1': jnp.stack([k1, v1]), 'cmp_pos': jnp.stack([p1, q1]), 'cmp_w2': jnp.stack([k2, v2]),
            'ln2_g': ln2_g[l], 'ln2_b': ln2_b[l],
            'ffn2_w_up': ffn2_w_up[l].astype(BF16), 'ffn2_w_down': (0.5 * ffn2_w_down[l]).astype(BF16),
            'ln3_g': ln3_g[l], 'ln3_b': ln3_b[l],
        }
        s0 = jnp.zeros((B, RET_HEADS, RET_DK, RET_DV), F32)
        yp, rs_p, kv_p, win_p = _decoder_layer(yp, B, L, 0, s0, p, None)
        ctx = (cache_nsa_kv, cache_win, l, page_table, past_len, page)
        ys, rs_s, kv_s, win_s = _decoder_layer(ys, DB, DL, past_len,
                                               _layer_view(state_ret, l, state_ret.shape[1:]), p, ctx)
        wl = min(WINDOW, L)
        outs[0].append(rs_p)
        outs[1].append(rs_s)
        outs[2].append(kv_p.reshape(B, L, 4, NSA_KV_HEADS, NSA_HD))
        outs[3].append(kv_s.reshape(DB, DL, 4, NSA_KV_HEADS, NSA_HD))
        outs[4].append(win_p.reshape(B, L, 2, NSA_KV_HEADS, NSA_HD)[:, L - wl:])
        win_s = win_s.reshape(DB, DL, 2, NSA_KV_HEADS, NSA_HD)
        outs[5].append(jnp.concatenate([cache_win[l], win_s], axis=1)[:, -w_buf:])
    return (yp.reshape(B, L, D), ys.reshape(DB, DL, D), jnp.stack(outs[0]), jnp.stack(outs[1]),
            jnp.stack(outs[2]), jnp.stack(outs[3]), jnp.stack(outs[4]), jnp.stack(outs[5]))
```

```python
import functools

import numpy as np
import jax
import jax.numpy as jnp
from jax import lax
from jax.experimental import pallas as pl
from jax.experimental.pallas import tpu as pltpu

F32 = jnp.float32
BF16 = jnp.bfloat16

LANES = 128
SUBLANES = 8
BF16_ROWS = 16
VMEM_LIMIT_BYTES = 56 * 1024 * 1024

RET_HEADS = 4
RET_DK = 128
RET_DV = 256
RET_CHUNK = 128
NSA_HEADS = 8
NSA_KV_HEADS = 2
NSA_HD = 128
NSA_GROUP = NSA_HEADS // NSA_KV_HEADS
CMP_BLOCK = 32
CMP_STRIDE = 16
CMP_HIDDEN = 2 * NSA_HD
SLC_BLOCK = 64
SLC_SHIFT = 6
N_SELECT = 16
WINDOW = 512
N_BRANCH = 3
ROPE_THETA = 10000.0
LN_EPS = 1e-5
NEG_BIG = -1e30
LOG2_E = 1.4426950408889634

N_IN = 5656
N_IN_PAD = 5760
KV_CHUNK0 = 32
WIN_CHUNK0 = 40
GATE_CHUNK = 44
ROPE_CHUNKS = tuple(range(0, 8)) + tuple(range(24, 32)) + (32, 33, 36, 37, 40, 41)
KSCALE_CHUNKS = tuple(range(4, 8))


def _cparams(sem):
    return pltpu.CompilerParams(dimension_semantics=sem, vmem_limit_bytes=VMEM_LIMIT_BYTES)


def _pick_tile(n, candidates):
    for c in candidates:
        if n % c == 0:
            return c
    return n


def _round_up(n, m):
    return (n + m - 1) // m * m


def _layer_norm(z, g, b):
    mu = jnp.mean(z, axis=-1, keepdims=True)
    zc = z - mu
    var = jnp.mean(zc * zc, axis=-1, keepdims=True)
    return zc * lax.rsqrt(var + LN_EPS) * g + b


def _dot(a, b):
    return jnp.dot(a, b, preferred_element_type=F32)


def _dot_nt(a, b):
    return lax.dot_general(a, b, (((1,), (1,)), ((), ())), preferred_element_type=F32)


def _ffn_ln_body(x_ref, wa_ref, wb_ref, wd_ref, g_ref, b_ref, *rest, nj, alpha, emit_bf16, has_xbf):
    if has_xbf:
        xbf, rest = rest[0], rest[1:]
    if emit_bf16:
        o_ref, obf_ref = rest[:2]
        rest = rest[2:]
    else:
        o_ref = rest[0]
        rest = rest[1:]
    if has_xbf:
        (acc_sc,) = rest
    else:
        xbf, acc_sc = rest
    j = pl.program_id(1)

    @pl.when(j == 0)
    def _init():
        if not has_xbf:
            xbf[...] = x_ref[...].astype(BF16)
        acc_sc[...] = jnp.zeros_like(acc_sc)

    xb = xbf[...]
    a = _dot(xb, wa_ref[...])
    b = _dot(xb, wb_ref[...])
    h = (a * jax.nn.sigmoid(a)) * b
    acc_sc[...] += _dot(h.astype(BF16), wd_ref[...])

    @pl.when(j == nj - 1)
    def _finish():
        z = alpha * x_ref[...] + acc_sc[...]
        y = _layer_norm(z, g_ref[...], b_ref[...])
        o_ref[...] = y
        if emit_bf16:
            obf_ref[...] = y.astype(BF16)


def _ffn_ln(x, w_up, w_down, g, b, alpha, emit_bf16, x_bf=None):
    T, D = x.shape
    F = w_down.shape[0]
    tm = _pick_tile(T, (512, 256, 128, 64, 32, 16, 8))
    tf = _pick_tile(F, (512, 256, 128))
    nj = F // tf
    out_shape = [jax.ShapeDtypeStruct((T, D), F32)]
    out_specs = [pl.BlockSpec((tm, D), lambda i, j: (i, 0))]
    if emit_bf16:
        out_shape.append(jax.ShapeDtypeStruct((T, D), BF16))
        out_specs.append(pl.BlockSpec((tm, D), lambda i, j: (i, 0)))
    has_xbf = x_bf is not None
    row_block = pl.BlockSpec((tm, D), lambda i, j: (i, 0))
    res = pl.pallas_call(
        functools.partial(_ffn_ln_body, nj=nj, alpha=alpha, emit_bf16=emit_bf16, has_xbf=has_xbf),
        grid=(T // tm, nj),
        in_specs=[
            row_block,
            pl.BlockSpec((D, tf), lambda i, j: (0, j)),
            pl.BlockSpec((D, tf), lambda i, j: (0, nj + j)),
            pl.BlockSpec((tf, D), lambda i, j: (j, 0)),
            pl.BlockSpec((1, D), lambda i, j: (0, 0)),
            pl.BlockSpec((1, D), lambda i, j: (0, 0)),
        ] + ([row_block] if has_xbf else []),
        out_specs=out_specs,
        out_shape=out_shape,
        scratch_shapes=([] if has_xbf else [pltpu.VMEM((tm, D), BF16)]) + [pltpu.VMEM((tm, D), F32)],
        compiler_params=_cparams(("parallel", "arbitrary")),
        name="ffn_ln",
    )(x, w_up, w_up, w_down, g.reshape(1, D), b.reshape(1, D), *([x_bf] if has_xbf else []))
    return res if emit_bf16 else (res[0], None)


def _rows_view(ref, j):
    rows = ref.shape[0]
    n = int(np.prod(ref.shape[1:-1]))
    return ref.reshape(n * rows, ref.shape[-1]), pl.ds(j, rows, stride=n)


def _head_rows(ref, j):
    view, idx = _rows_view(ref, j)
    return view[idx, :]


def _store_rows(ref, j, val):
    view, idx = _rows_view(ref, j)
    view[idx, :] = val


def _proj_body(x_ref, w_ref, cos_ref, sin_ref, rope_ref, scale_ref, o_ref, kv_ref, win_ref, *, n_chunk, j_rows):
    y = _dot(x_ref[...], w_ref[...])
    cos = cos_ref[...]
    sin = sin_ref[...]
    for c in range(n_chunk):
        sl = slice(c * LANES, (c + 1) * LANES)
        yc = y[:, sl]
        roped = yc * cos + pltpu.roll(yc, NSA_HD // 2, 1) * sin
        o_ref[:, sl] = jnp.where(rope_ref[:, sl] > 0.5, roped, yc) * scale_ref[:, sl]

    c0 = KV_CHUNK0 - j_rows * n_chunk
    for r in range(WIN_CHUNK0 - KV_CHUNK0):
        _store_rows(kv_ref, r, o_ref[:, (c0 + r) * LANES:(c0 + r + 1) * LANES])
    c0 = WIN_CHUNK0 - j_rows * n_chunk
    for r in range(GATE_CHUNK - WIN_CHUNK0):
        _store_rows(win_ref, r, o_ref[:, (c0 + r) * LANES:(c0 + r + 1) * LANES])


def _proj(x_bf, w_bf, rope_cols, scale_cols, cos, sin, rows_per_seq):
    T, D = x_bf.shape
    N = w_bf.shape[1]
    tm = _pick_tile(T, (512, 256, 128, 64, 32, 16, 8))
    tn = 1920
    assert N % tn == 0
    n_chunk = tn // LANES
    if rows_per_seq >= tm:
        assert rows_per_seq % tm == 0
        n_tab = rows_per_seq // tm
    else:
        assert tm % rows_per_seq == 0
        cos = jnp.tile(cos, (tm // rows_per_seq, 1))
        sin = jnp.tile(sin, (tm // rows_per_seq, 1))
        n_tab = 1
    j_rows = KV_CHUNK0 // n_chunk
    assert (GATE_CHUNK - 1) // n_chunk == j_rows == N // tn - 1, "cache-row columns must sit in the last column tile"
    kv_shape = (T, 4, NSA_KV_HEADS, NSA_HD)
    win_shape = (T, 2, NSA_KV_HEADS, NSA_HD)
    return pl.pallas_call(
        functools.partial(_proj_body, n_chunk=n_chunk, j_rows=j_rows),
        grid=(T // tm, N // tn),
        in_specs=[
            pl.BlockSpec((tm, D), lambda i, j: (i, 0)),
            pl.BlockSpec((D, tn), lambda i, j: (0, j)),
            pl.BlockSpec((tm, LANES), lambda i, j: (i % n_tab, 0)),
            pl.BlockSpec((tm, LANES), lambda i, j: (i % n_tab, 0)),
            pl.BlockSpec((1, tn), lambda i, j: (0, j)),
            pl.BlockSpec((1, tn), lambda i, j: (0, j)),
        ],
        out_specs=[
            pl.BlockSpec((tm, tn), lambda i, j: (i, j)),
            pl.BlockSpec((tm,) + kv_shape[1:], lambda i, j: (i, 0, 0, 0)),
            pl.BlockSpec((tm,) + win_shape[1:], lambda i, j: (i, 0, 0, 0)),
        ],
        out_shape=[
            jax.ShapeDtypeStruct((T, N), F32),
            jax.ShapeDtypeStruct(kv_shape, F32),
            jax.ShapeDtypeStruct(win_shape, F32),
        ],
        compiler_params=_cparams(("parallel", "arbitrary")),
        name="proj_rope",
    )(x_bf, w_bf, cos, sin, rope_cols, scale_cols)


def _ret_body(q_ref, k_ref, v_ref, g_ref, s0_ref, dm_ref, ind_ref, std_ref, cd_ref, gng_ref, gnb_ref,
              o_ref, sout_ref, s_sc, *, nc, rows, rows_pad, per_step):
    c = pl.program_id(1)

    @pl.when(c == 0)
    def _load_state():
        s_sc[...] = s0_ref[...]

    pad = rows_pad - rows
    for sub in range(per_step):
        rs = slice(sub * rows, (sub + 1) * rows)
        for h in range(RET_HEADS):
            ks = slice(h * RET_DK, (h + 1) * RET_DK)
            vs = slice(h * RET_DV, (h + 1) * RET_DV)
            q = q_ref[rs, ks]
            k = k_ref[rs, ks]
            v = v_ref[rs, vs]
            kd = k * std_ref[h]
            if pad:
                k = jnp.concatenate([k, jnp.zeros((pad, RET_DK), F32)], axis=0)
                kd = jnp.concatenate([kd, jnp.zeros((pad, RET_DK), F32)], axis=0)
                v = jnp.concatenate([v, jnp.zeros((pad, RET_DV), F32)], axis=0)
            s_old = s_sc[h]
            vb = v.astype(BF16)
            a = _dot_nt(q.astype(BF16), k.astype(BF16)) * dm_ref[h]
            o = _dot(a.astype(BF16), vb) + _dot((q * ind_ref[h]).astype(BF16), s_old.astype(BF16))
            s_sc[h] = s_old * cd_ref[h] + _dot(kd.T.astype(BF16), vb)
            mu = jnp.mean(o, axis=-1, keepdims=True)
            oc = o - mu
            var = jnp.mean(oc * oc, axis=-1, keepdims=True)
            on = oc * lax.rsqrt(var + LN_EPS) * gng_ref[:, vs] + gnb_ref[:, vs]
            gate = g_ref[rs, vs]
            o_ref[rs, vs] = (gate * jax.nn.sigmoid(gate)) * on

    @pl.when(c == nc - 1)
    def _store_state():
        sout_ref[...] = s_sc[...]


def _retention(P, s0, gn_g, gn_b, B, L):
    C = RET_CHUNK if L % RET_CHUNK == 0 else L
    nc = L // C
    CP = max(C, LANES)
    lg = jnp.log1p(-jnp.exp2(-5.0 - jnp.arange(RET_HEADS, dtype=F32)))
    i = jnp.arange(C, dtype=F32)
    diff = i[:, None] - i[None, :]
    dmask = jnp.where(diff >= 0, jnp.exp(lg[:, None, None] * jnp.maximum(diff, 0.0)), 0.0)
    dmask = jnp.pad(dmask, ((0, 0), (0, 0), (0, CP - C)))
    in_decay = jnp.broadcast_to(jnp.exp(lg[:, None] * (i + 1.0))[:, :, None], (RET_HEADS, C, RET_DK))
    st_decay = jnp.broadcast_to(jnp.exp(lg[:, None] * (C - 1.0 - i))[:, :, None], (RET_HEADS, C, RET_DK))
    chunk_decay = jnp.broadcast_to(jnp.exp(lg * C)[:, None, None], (RET_HEADS, 1, RET_DV))
    qw = RET_HEADS * RET_DK
    vw = RET_HEADS * RET_DV
    const3 = lambda b, c: (0, 0, 0)
    per_step = _pick_tile(nc, (8, 4, 2, 1))
    nc = nc // per_step
    CS = C * per_step
    return pl.pallas_call(
        functools.partial(_ret_body, nc=nc, rows=C, rows_pad=CP, per_step=per_step),
        grid=(B, nc),
        in_specs=[
            pl.BlockSpec((CS, qw), lambda b, c: (b * nc + c, 0)),
            pl.BlockSpec((CS, qw), lambda b, c: (b * nc + c, 1)),
            pl.BlockSpec((CS, vw), lambda b, c: (b * nc + c, 1)),
            pl.BlockSpec((CS, vw), lambda b, c: (b * nc + c, 2)),
            pl.BlockSpec((None, RET_HEADS, RET_DK, RET_DV), lambda b, c: (b, 0, 0, 0)),
            pl.BlockSpec((RET_HEADS, C, CP), const3),
            pl.BlockSpec((RET_HEADS, C, RET_DK), const3),
            pl.BlockSpec((RET_HEADS, C, RET_DK), const3),
            pl.BlockSpec((RET_HEADS, 1, RET_DV), const3),
            pl.BlockSpec((1, vw), lambda b, c: (0, 0)),
            pl.BlockSpec((1, vw), lambda b, c: (0, 0)),
        ],
        out_specs=[
            pl.BlockSpec((CS, vw), lambda b, c: (b * nc + c, 0)),
            pl.BlockSpec((None, RET_HEADS, RET_DK, RET_DV), lambda b, c: (b, 0, 0, 0)),
        ],
        out_shape=[
            jax.ShapeDtypeStruct((B * L, vw), F32),
            jax.ShapeDtypeStruct((B, RET_HEADS, RET_DK, RET_DV), F32),
        ],
        scratch_shapes=[pltpu.VMEM((RET_HEADS, RET_DK, RET_DV), F32)],
        compiler_params=_cparams(("parallel", "arbitrary")),
        name="retention",
    )(P, P, P, P, s0, dmask, in_decay, st_decay, chunk_decay, gn_g.reshape(1, vw), gn_b.reshape(1, vw))


def _cmp_body(*refs, n_in, rows, n_grp, n_cmp, paged):
    n_vec = 2 * NSA_KV_HEADS
    if paged:
        refs = refs[1:]
    n_src = n_in if paged else n_vec * n_in
    x_refs = refs[:n_src]
    w1_ref, pos_ref, w2_ref, o_ref, carry_sc, xc_sc = refs[n_src:]
    g = pl.program_id(1)
    grp = n_grp - 1 - g
    cpi = rows // CMP_STRIDE
    M = n_in * cpi

    @pl.when(g == 0)
    def _init():
        carry_sc[...] = jnp.zeros_like(carry_sc)

    row = lax.broadcasted_iota(jnp.int32, (M, 1), 0)
    grp_in = 1 if cpi % BF16_ROWS == 0 else BF16_ROWS // cpi
    assert n_in % grp_in == 0 and (grp_in * cpi) % BF16_ROWS == 0
    for t in range(2):
        w1 = w1_ref[t]
        gp = _dot(pos_ref[t], w1)
        posterm = gp[0:1, :CMP_HIDDEN] + gp[1:2, CMP_HIDDEN:]
        for hd in range(NSA_KV_HEADS):
            v = t * NSA_KV_HEADS + hd
            for i0 in range(0, n_in, grp_in):
                if paged:
                    parts = [pltpu.einshape("csd->scd",
                                            _head_rows(x_refs[i], v).reshape(cpi, CMP_STRIDE, NSA_HD))
                             for i in range(i0, i0 + grp_in)]
                    piece = lambda s: jnp.concatenate([xt[s] for xt in parts], axis=0)
                else:
                    refs_i = [x_refs[v * n_in + i] for i in range(i0, i0 + grp_in)]
                    piece = lambda s: jnp.concatenate(
                        [r[pl.ds(s, cpi, stride=CMP_STRIDE), :] for r in refs_i], axis=0)
                r0 = v * M + i0 * cpi
                for s in range(CMP_STRIDE):
                    xc_sc[r0:r0 + grp_in * cpi, s * NSA_HD:(s + 1) * NSA_HD] = piece(s).astype(BF16)
        t0 = t * NSA_KV_HEADS * M
        gg_all = _dot(xc_sc[t0:t0 + NSA_KV_HEADS * M, :], w1)
        for hd in range(NSA_KV_HEADS):
            v = t * NSA_KV_HEADS + hd
            gg = gg_all[hd * M:(hd + 1) * M]
            g0 = gg[:, :CMP_HIDDEN]
            g1 = gg[:, CMP_HIDDEN:]
            nxt = pltpu.roll(g1, M - 1, 0)
            nxt = jnp.where(row == M - 1, carry_sc[v][0:1, :], nxt)
            carry_sc[v] = g1[0:SUBLANES, :]
            hid = g0 + nxt + posterm
            out = _dot(jax.nn.gelu(hid).astype(BF16), w2_ref[t])
            o_ref[t, hd] = jnp.where(grp * M + row < n_cmp, out, 0.0)


def _compress(srcs, src_specs, n_in, rows, n_grp, n_cmp, B, w1, pos, w2, page_table=None):
    n_vec = 2 * NSA_KV_HEADS
    M = n_in * rows // CMP_STRIDE
    nch = n_grp * M
    paged = page_table is not None
    const3 = lambda *a: (0, 0, 0)
    in_specs = list(src_specs) + [
        pl.BlockSpec((2, CMP_STRIDE * NSA_HD, 2 * CMP_HIDDEN), const3),
        pl.BlockSpec((2, SUBLANES, CMP_STRIDE * NSA_HD), const3),
        pl.BlockSpec((2, CMP_HIDDEN, NSA_HD), const3),
    ]
    out_spec = pl.BlockSpec((None, 2, NSA_KV_HEADS, M, NSA_HD), lambda *a: (a[0], 0, 0, n_grp - 1 - a[1], 0))
    body = functools.partial(_cmp_body, n_in=n_in, rows=rows, n_grp=n_grp, n_cmp=n_cmp, paged=paged)
    out_shape = jax.ShapeDtypeStruct((B, 2, NSA_KV_HEADS, nch, NSA_HD), F32)
    scratch = [pltpu.VMEM((n_vec, SUBLANES, CMP_HIDDEN), F32),
               pltpu.VMEM((n_vec * M, CMP_STRIDE * NSA_HD), BF16)]
    sem = ("parallel", "arbitrary")
    if paged:
        grid_spec = pltpu.PrefetchScalarGridSpec(
            num_scalar_prefetch=1, grid=(B, n_grp), in_specs=in_specs, out_specs=out_spec,
            scratch_shapes=scratch)
        return pl.pallas_call(body, grid_spec=grid_spec, out_shape=out_shape,
                              compiler_params=_cparams(sem), name="nsa_compress_paged")(
            page_table, *srcs, w1, pos, w2)
    return pl.pallas_call(body, grid=(B, n_grp), in_specs=in_specs, out_specs=out_spec,
                          out_shape=out_shape, scratch_shapes=scratch,
                          compiler_params=_cparams(sem), name="nsa_compress")(*srcs, w1, pos, w2)


def _cover_matrix(n_cmp, n_slc, rows, cols):
    c_i = np.arange(n_cmp)[:, None]
    n_i = np.arange(n_slc)[None, :]
    cov = np.clip(np.minimum(c_i * CMP_STRIDE + CMP_BLOCK, (n_i + 1) * SLC_BLOCK)
                  - np.maximum(c_i * CMP_STRIDE, n_i * SLC_BLOCK), 0, None).astype(np.float32) / CMP_BLOCK
    out = np.zeros((rows, cols), np.float32)
    out[:n_cmp, :n_slc] = cov
    return out


def _nsa_prompt_body(q_ref, kc_ref, vc_ref, ks_ref, vs_ref, kw_ref, vw_ref, gate_ref, covt_ref, o_ref,
                     kcb_sc, vct_sc, ksb_sc, vst_sc, kwb_sc, vwt_sc, prio_sc, sel_sc, gt_sc,
                     m_sc, l_sc, acc_sc, out_sc, s_sc, *, tq, L, n_cmp, n_slc, n_sel, nbp):
    KV = NSA_KV_HEADS
    qi = pl.program_id(1)
    tk = tq
    nch = kc_ref.shape[1]
    scale = NSA_HD ** -0.5

    @pl.when(qi == 0)
    def _stage_kv():
        for kv in range(KV):
            cs = slice(kv * NSA_HD, (kv + 1) * NSA_HD)
            kcb_sc[kv] = kc_ref[kv].astype(BF16)
            vct_sc[kv] = vc_ref[kv].T.astype(BF16)
            ksb_sc[kv] = ks_ref[:, cs].astype(BF16)
            kwb_sc[kv] = kw_ref[:, cs].astype(BF16)
            for i in range(L // tk):
                vst_sc[kv, i] = vs_ref[i * tk:(i + 1) * tk, cs].T.astype(BF16)
                vwt_sc[kv, i] = vw_ref[i * tk:(i + 1) * tk, cs].T.astype(BF16)

    G = NSA_GROUP
    W = G * tq
    t0 = qi * tq
    t_row = t0 + lax.broadcasted_iota(jnp.int32, (1, tq), 1)
    lane = lax.broadcasted_iota(jnp.int32, (1, W), 1)
    t_all = t0 + (lane & (tq - 1))
    qcats = []
    for kv in range(KV):
        qc = jnp.concatenate([q_ref[:, (kv * G + g) * NSA_HD:(kv * G + g + 1) * NSA_HD] for g in range(G)], axis=0)
        qcats.append((qc * (scale * LOG2_E)).astype(BF16))
    gt_sc[...] = jax.nn.sigmoid(gate_ref[...]).T

    def gate_row(kv, br):
        return jnp.concatenate(
            [gt_sc[(kv * G + g) * N_BRANCH + br:(kv * G + g) * N_BRANCH + br + 1, :] for g in range(G)], axis=1)

    kp_l = lax.broadcasted_iota(jnp.int32, (tk, 1), 0)
    t_l = lane & (tq - 1)
    causal_bias = jnp.where(kp_l <= t_l, 0.0, NEG_BIG)
    far_bias = jnp.where(kp_l > t_l, 0.0, NEG_BIG)

    def reset(kv):
        m_sc[kv] = jnp.full((1, W), NEG_BIG, F32)
        l_sc[kv] = jnp.zeros((1, W), F32)
        acc_sc[kv] = jnp.zeros((NSA_HD, W), F32)

    def score(kb_sc, kv, kt, bias):
        k0 = pl.multiple_of(kt * tk, tk)
        s = _dot_nt(kb_sc[kv, pl.ds(k0, tk), :], qcats[kv])
        return s if bias is None else s + bias

    def update(vt_sc, kv, kt, s):
        m_old = m_sc[kv]
        m_new = jnp.maximum(m_old, jnp.max(s, axis=0, keepdims=True))
        p = jnp.exp2(s - m_new)
        alpha = jnp.exp2(m_old - m_new)
        l_sc[kv] = alpha * l_sc[kv] + jnp.sum(p, axis=0, keepdims=True)
        acc_sc[kv] = alpha * acc_sc[kv] + _dot(vt_sc[kv, kt], p.astype(BF16))
        m_sc[kv] = m_new

    def branch_out(kv, br):
        return (gate_row(kv, br) / l_sc[kv]) * acc_sc[kv]

    far = WINDOW // tk
    for kv in range(KV):
        reset(kv)
        tiles = []
        for back in range(far, -1, -1):
            bias = far_bias if back == far else (causal_bias if back == 0 else None)
            if back > 0:
                off = jnp.where(qi >= back, 0.0, NEG_BIG).astype(F32)
                bias = off if bias is None else bias + off
            kt = jnp.maximum(qi - back, 0)
            tiles.append((kt, score(kwb_sc, kv, kt, bias)))
        for kt, s_w in tiles:
            update(vwt_sc, kv, kt, s_w)
        out_sc[kv] = branch_out(kv, 2)

    c_i = lax.broadcasted_iota(jnp.int32, (nch, 1), 0)
    c_end = jnp.where(c_i < n_cmp, c_i * CMP_STRIDE + (CMP_BLOCK - 1), L)
    n_i = lax.broadcasted_iota(jnp.int32, (nbp, 1), 0)
    valid = (n_i * SLC_BLOCK <= t_row) & (n_i < n_slc)
    cur = t_row >> SLC_SHIFT
    forced = (n_i == 0) | (n_i == cur) | (n_i == cur - 1)
    for kv in range(KV):
        s = jnp.where(c_end <= t_all, _dot_nt(kcb_sc[kv], qcats[kv]), -jnp.inf)
        m = jnp.max(s, axis=0, keepdims=True)
        m = jnp.where(m > -jnp.inf, m, 0.0)
        e = jnp.exp2(s - m)
        d = jnp.sum(e, axis=0, keepdims=True)
        p = e / jnp.where(d > 0, d, 1.0)
        out_sc[kv] += gate_row(kv, 0) * _dot(vct_sc[kv], p.astype(BF16))
        psum = p[:, 0:tq]
        for g in range(1, G):
            psum = psum + p[:, g * tq:(g + 1) * tq]
        imp = jnp.dot(covt_ref[...], psum, preferred_element_type=F32, precision=lax.Precision.HIGHEST)
        prio = jnp.where(forced, jnp.inf, jnp.where(valid, imp, -jnp.inf))
        prio_sc[kv] = prio
        cnt = jnp.zeros((nbp, tq), jnp.int32)
        for mm in range(n_slc):
            pm = prio_sc[kv, mm:mm + 1, :]
            tie = jnp.where(n_i > mm, 1, 0)
            cnt = cnt + jnp.where(pm > prio, 1, jnp.where(pm == prio, tie, 0))
        selbias = jnp.where((cnt < n_sel) & valid, 0.0, NEG_BIG)
        sel_sc[kv] = jnp.concatenate([selbias] * G, axis=1)

    def sel_bias(kv, kt):
        per_tile = tk // SLC_BLOCK
        rows = [jnp.broadcast_to(sel_sc[kv, pl.ds(kt * per_tile + r, 1), :], (SLC_BLOCK, W))
                for r in range(per_tile)]
        return jnp.concatenate(rows, axis=0)

    for kv in range(KV):
        reset(kv)
        s_sc[kv] = score(ksb_sc, kv, 0, sel_bias(kv, 0))

    def slc_body(kt, carry):
        nxt = [score(ksb_sc, kv, kt + 1, sel_bias(kv, kt + 1)) for kv in range(KV)]
        for kv in range(KV):
            update(vst_sc, kv, kt, s_sc[kv])
        for kv in range(KV):
            s_sc[kv] = nxt[kv]
        return carry

    lax.fori_loop(0, qi, slc_body, 0)
    for kv in range(KV):
        update(vst_sc, kv, qi, s_sc[kv] + causal_bias)
        out_sc[kv] += branch_out(kv, 1)
        for g in range(G):
            h = kv * G + g
            o_ref[:, h * NSA_HD:(h + 1) * NSA_HD] = out_sc[kv, :, g * tq:(g + 1) * tq].T


def _nsa_prompt(P, CMP, B, L):
    tq = _pick_tile(L, (256, 128))
    assert L % tq == 0 and tq % SLC_BLOCK == 0 and WINDOW % tq == 0 and tq & (tq - 1) == 0
    nq = L // tq
    nch = CMP.shape[3]
    n_cmp = L // CMP_STRIDE - CMP_BLOCK // CMP_STRIDE + 1
    n_slc = -(-L // SLC_BLOCK)
    n_sel = min(N_SELECT, n_slc)
    nbp = _round_up(n_slc, SUBLANES)
    covt = jnp.asarray(_cover_matrix(n_cmp, n_slc, nch, nbp).T)
    qw = NSA_HEADS * NSA_HD
    kvw = NSA_KV_HEADS * NSA_HD
    KV = NSA_KV_HEADS
    W = NSA_GROUP * tq
    kv_col = lambda c: (lambda b, i: (b, (KV_CHUNK0 + c) // NSA_KV_HEADS))
    body = functools.partial(_nsa_prompt_body, tq=tq, L=L, n_cmp=n_cmp, n_slc=n_slc, n_sel=n_sel, nbp=nbp)
    return pl.pallas_call(
        body,
        grid=(B, nq),
        in_specs=[
            pl.BlockSpec((tq, qw), lambda b, i: (b * nq + i, 3)),
            pl.BlockSpec((None, None, KV, nch, NSA_HD), lambda b, i: (b, 0, 0, 0, 0)),
            pl.BlockSpec((None, None, KV, nch, NSA_HD), lambda b, i: (b, 1, 0, 0, 0)),
            pl.BlockSpec((L, kvw), kv_col(4)),
            pl.BlockSpec((L, kvw), kv_col(6)),
            pl.BlockSpec((L, kvw), kv_col(8)),
            pl.BlockSpec((L, kvw), kv_col(10)),
            pl.BlockSpec((tq, LANES), lambda b, i: (b * nq + i, GATE_CHUNK)),
            pl.BlockSpec((nbp, nch), lambda b, i: (0, 0)),
        ],
        out_specs=pl.BlockSpec((tq, qw), lambda b, i: (b * nq + i, 0)),
        out_shape=jax.ShapeDtypeStruct((B * L, qw), F32),
        scratch_shapes=[
            pltpu.VMEM((KV, nch, NSA_HD), BF16),
            pltpu.VMEM((KV, NSA_HD, nch), BF16),
            pltpu.VMEM((KV, L, NSA_HD), BF16),
            pltpu.VMEM((KV, L // tq, NSA_HD, tq), BF16),
            pltpu.VMEM((KV, L, NSA_HD), BF16),
            pltpu.VMEM((KV, L // tq, NSA_HD, tq), BF16),
            pltpu.VMEM((KV, nbp, tq), F32),
            pltpu.VMEM((KV, nbp, W), F32),
            pltpu.VMEM((LANES, tq), F32),
            pltpu.VMEM((KV, 1, W), F32),
            pltpu.VMEM((KV, 1, W), F32),
            pltpu.VMEM((KV, NSA_HD, W), F32),
            pltpu.VMEM((KV, NSA_HD, W), F32),
            pltpu.VMEM((KV, tq, W), F32),
        ],
        compiler_params=_cparams(("parallel", "arbitrary")),
        name="nsa_prompt",
    )(P, CMP, CMP, P, P, P, P, P, covt)


def _nsa_sample_body(*refs, nb, pg, n_steps, dl, q_off, w_buf, n_cmp, n_slc, n_sel, nbl, page):
    refs = refs[1:]
    q_ref, kvn_ref, wn_ref, gate_ref, kc_ref, vc_ref, cov_ref = refs[:7]
    cw_refs = refs[7:7 + nb]
    pg_refs = refs[7 + nb:7 + nb + nb * pg]
    o_ref, qb_sc, sel_sc, m_sc, l_sc, acc_sc, ocmp_sc = refs[7 + nb + nb * pg:]
    step = pl.program_id(1)
    KVH = NSA_KV_HEADS
    n_str = nb * KVH
    rows_h = NSA_GROUP * dl
    scale = NSA_HD ** -0.5
    nch = kc_ref.shape[2]

    def tok_pos(n):
        r = lax.broadcasted_iota(jnp.int32, (n, 1), 0)
        return q_off + r % dl

    @pl.when(step == 0)
    def _select():
        for jk in range(n_str):
            j, k = divmod(jk, KVH)
            for g in range(NSA_GROUP):
                h = k * NSA_GROUP + g
                qb_sc[pl.ds((jk * NSA_GROUP + g) * dl, dl), :] = (
                    q_ref[j * dl:(j + 1) * dl, h * NSA_HD:(h + 1) * NSA_HD] * scale).astype(BF16)
        t_h = tok_pos(rows_h)
        c_i = lax.broadcasted_iota(jnp.int32, (1, nch), 1)
        cmask = (c_i * CMP_STRIDE + (CMP_BLOCK - 1) <= t_h) & (c_i < n_cmp)
        psums = []
        for jk in range(n_str):
            j, k = divmod(jk, KVH)
            qk = qb_sc[pl.ds(jk * rows_h, rows_h), :]
            s = jnp.where(cmask, _dot_nt(qk, kc_ref[j, k].astype(BF16)), -jnp.inf)
            m = jnp.max(s, axis=1, keepdims=True)
            m = jnp.where(m > -jnp.inf, m, 0.0)
            e = jnp.exp(s - m)
            d = jnp.sum(e, axis=1, keepdims=True)
            p = e / jnp.where(d > 0, d, 1.0)
            ocmp_sc[pl.ds(jk * rows_h, rows_h), :] = _dot(p.astype(BF16), vc_ref[j, k].astype(BF16))
            psum = p[0:dl]
            for g in range(1, NSA_GROUP):
                psum = psum + p[g * dl:(g + 1) * dl]
            psums.append(psum)
        imp = jnp.dot(jnp.concatenate(psums, axis=0), cov_ref[...], preferred_element_type=F32,
                      precision=lax.Precision.HIGHEST)
        t_s = tok_pos(n_str * dl)
        n_i = lax.broadcasted_iota(jnp.int32, (1, nbl), 1)
        valid = (n_i * SLC_BLOCK <= t_s) & (n_i < n_slc)
        cur = t_s >> SLC_SHIFT
        forced = (n_i == 0) | (n_i == cur) | (n_i == cur - 1)
        prio = jnp.where(forced, jnp.inf, jnp.where(valid, imp, -jnp.inf))
        n_f = n_i.astype(F32)
        alive = jnp.broadcast_to(jnp.where(n_i < n_slc, 1.0, 0.0), prio.shape)
        sel = jnp.zeros(prio.shape, F32)
        for _ in range(n_sel):
            mx = jnp.max(jnp.where(alive > 0.5, prio, -jnp.inf), axis=1, keepdims=True)
            cand = (alive > 0.5) & (prio == mx)
            first = jnp.min(jnp.where(cand, n_f, float(nbl)), axis=1, keepdims=True)
            pick = n_f == first
            sel = jnp.where(pick, 1.0, sel)
            alive = jnp.where(pick, 0.0, alive)
        sel = jnp.where(valid, sel, 0.0)
        for jk in range(n_str):
            for g in range(NSA_GROUP):
                sel_sc[pl.ds((jk * NSA_GROUP + g) * dl, dl), :] = sel[jk * dl:(jk + 1) * dl]
        m_sc[...] = jnp.full(m_sc.shape, NEG_BIG, F32)
        l_sc[...] = jnp.zeros_like(l_sc)
        acc_sc[...] = jnp.zeros_like(acc_sc)

    selb = sel_sc[...].astype(BF16)
    n_col = lax.broadcasted_iota(jnp.int32, (nbl, 1), 0)

    def online_update(jk, s, mask, v_rows):
        rs = pl.ds(jk * rows_h, rows_h)
        m_old = m_sc[rs, :]
        m_new = jnp.maximum(m_old, jnp.max(s, axis=1, keepdims=True))
        p = jnp.where(mask, jnp.exp(s - m_new), 0.0)
        alpha = jnp.exp(m_old - m_new)
        l_sc[rs, :] = alpha * l_sc[rs, :] + jnp.sum(p, axis=1, keepdims=True)
        acc_sc[rs, :] = alpha * acc_sc[rs, :] + _dot(p.astype(BF16), v_rows)
        m_sc[rs, :] = m_new

    t_h = tok_pos(rows_h)

    def selected(kpos):
        blk = jnp.where(n_col == (kpos >> SLC_SHIFT), 1.0, 0.0).astype(BF16)
        return _dot(selb, blk)

    kpos = step * (pg * page) + lax.broadcasted_iota(jnp.int32, (1, pg * page), 1)
    sel_all = selected(kpos)
    for jk in range(n_str):
        j, k = divmod(jk, KVH)
        pages = pg_refs[j * pg:(j + 1) * pg]
        qk = qb_sc[pl.ds(jk * rows_h, rows_h), :]
        mask = (sel_all[jk * rows_h:(jk + 1) * rows_h] > 0.5) & (kpos <= t_h)
        k_rows = jnp.concatenate([_head_rows(r, k).astype(BF16) for r in pages], axis=0)
        v_rows = jnp.concatenate([_head_rows(r, KVH + k).astype(BF16) for r in pages], axis=0)
        s = jnp.where(mask, _dot_nt(qk, k_rows), NEG_BIG)
        online_update(jk, s, mask, v_rows)

    @pl.when(step == n_steps - 1)
    def _finish():
        padn = LANES - dl
        j_new = lax.broadcasted_iota(jnp.int32, (1, LANES), 1)
        kpos_n = q_off + j_new
        sel_n = selected(kpos_n)
        for jk in range(n_str):
            j, k = divmod(jk, KVH)
            tok = slice(j * dl, (j + 1) * dl)
            kn = jnp.concatenate([kvn_ref[tok, k * NSA_HD:(k + 1) * NSA_HD], jnp.zeros((padn, NSA_HD), F32)],
                                 axis=0)
            vn = jnp.concatenate([kvn_ref[tok, (KVH + k) * NSA_HD:(KVH + k + 1) * NSA_HD],
                                  jnp.zeros((padn, NSA_HD), F32)], axis=0)
            qk = qb_sc[pl.ds(jk * rows_h, rows_h), :]
            mask = (sel_n[jk * rows_h:(jk + 1) * rows_h] > 0.5) & (kpos_n <= t_h) & (j_new < dl)
            s = jnp.where(mask, _dot_nt(qk, kn.astype(BF16)), NEG_BIG)
            online_update(jk, s, mask, vn.astype(BF16))
        j_w = lax.broadcasted_iota(jnp.int32, (1, w_buf + LANES), 1)
        pos_w = q_off - w_buf + j_w
        dlt = t_h - pos_w
        wmask = (j_w < w_buf + dl) & (pos_w >= 0) & (dlt >= 0) & (dlt < WINDOW)
        gates = jax.nn.sigmoid(gate_ref[...])
        for jk in range(n_str):
            j, k = divmod(jk, KVH)
            tok = slice(j * dl, (j + 1) * dl)
            kw = jnp.concatenate([_head_rows(cw_refs[j], k),
                                  wn_ref[tok, k * NSA_HD:(k + 1) * NSA_HD], jnp.zeros((padn, NSA_HD), F32)], axis=0)
            vw = jnp.concatenate([_head_rows(cw_refs[j], KVH + k),
                                  wn_ref[tok, (KVH + k) * NSA_HD:(KVH + k + 1) * NSA_HD],
                                  jnp.zeros((padn, NSA_HD), F32)], axis=0)
            qk = qb_sc[pl.ds(jk * rows_h, rows_h), :]
            s = jnp.where(wmask, _dot_nt(qk, kw.astype(BF16)), -jnp.inf)
            m = jnp.max(s, axis=1, keepdims=True)
            m = jnp.where(m > -jnp.inf, m, 0.0)
            e = jnp.exp(s - m)
            d = jnp.sum(e, axis=1, keepdims=True)
            o_win = _dot((e / jnp.where(d > 0, d, 1.0)).astype(BF16), vw.astype(BF16))
            rs = pl.ds(jk * rows_h, rows_h)
            o_slc = acc_sc[rs, :] / l_sc[rs, :]
            o_cmp = ocmp_sc[rs, :]
            for g in range(NSA_GROUP):
                h = k * NSA_GROUP + g
                r = slice(g * dl, (g + 1) * dl)
                gc = gates[tok, h * N_BRANCH:h * N_BRANCH + 1]
                gs = gates[tok, h * N_BRANCH + 1:h * N_BRANCH + 2]
                gw = gates[tok, h * N_BRANCH + 2:h * N_BRANCH + 3]
                o_ref[tok, h * NSA_HD:(h + 1) * NSA_HD] = gc * o_cmp[r] + gs * o_slc[r] + gw * o_win[r]


def _nsa_sample(P, CMP, cache_kv, cache_win, layer, page_table, DB, DL, past_len, page):
    n_pages = page_table.shape[1]
    w_buf = cache_win.shape[2]
    lk = past_len + DL
    nch = CMP.shape[3]
    n_cmp = lk // CMP_STRIDE - CMP_BLOCK // CMP_STRIDE + 1
    n_slc = -(-lk // SLC_BLOCK)
    n_sel = min(N_SELECT, n_slc)
    nbl = _round_up(n_slc, LANES)
    pg = _pick_tile(n_pages, (32, 16, 8, 4, 2, 1))
    n_steps = n_pages // pg
    nb = _pick_tile(DB, (2, 1))
    assert DL % SUBLANES == 0 and DL <= LANES and page == LANES and past_len == n_pages * page
    cov = jnp.asarray(_cover_matrix(n_cmp, n_slc, nch, nbl))
    kvw = NSA_KV_HEADS * NSA_HD
    page_spec = lambda j, i: pl.BlockSpec((None, None, page, 2, NSA_KV_HEADS, NSA_HD),
                                          lambda b, s, pt: (layer, pt[b * nb + j, s * pg + i], 0, 1, 0, 0))
    win_spec = lambda j: pl.BlockSpec((None, None, w_buf, 2, NSA_KV_HEADS, NSA_HD),
                                      lambda b, s, pt: (layer, b * nb + j, 0, 0, 0, 0))
    tok = nb * DL
    in_specs = [
        pl.BlockSpec((tok, NSA_HEADS * NSA_HD), lambda b, s, pt: (b, 3)),
        pl.BlockSpec((tok, 2 * kvw), lambda b, s, pt: (b, 9)),
        pl.BlockSpec((tok, 2 * kvw), lambda b, s, pt: (b, 10)),
        pl.BlockSpec((tok, LANES), lambda b, s, pt: (b, GATE_CHUNK)),
        pl.BlockSpec((nb, None, NSA_KV_HEADS, nch, NSA_HD), lambda b, s, pt: (b, 0, 0, 0, 0)),
        pl.BlockSpec((nb, None, NSA_KV_HEADS, nch, NSA_HD), lambda b, s, pt: (b, 1, 0, 0, 0)),
        pl.BlockSpec((nch, nbl), lambda b, s, pt: (0, 0)),
    ] + [win_spec(j) for j in range(nb)] + [page_spec(j, i) for j in range(nb) for i in range(pg)]
    rows = nb * NSA_HEADS * DL
    grid_spec = pltpu.PrefetchScalarGridSpec(
        num_scalar_prefetch=1,
        grid=(DB // nb, n_steps),
        in_specs=in_specs,
        out_specs=pl.BlockSpec((tok, NSA_HEADS * NSA_HD), lambda b, s, pt: (b, 0)),
        scratch_shapes=[
            pltpu.VMEM((rows, NSA_HD), BF16),
            pltpu.VMEM((rows, nbl), F32),
            pltpu.VMEM((rows, 1), F32),
            pltpu.VMEM((rows, 1), F32),
            pltpu.VMEM((rows, NSA_HD), F32),
            pltpu.VMEM((rows, NSA_HD), F32),
        ],
    )
    body = functools.partial(_nsa_sample_body, nb=nb, pg=pg, n_steps=n_steps, dl=DL, q_off=past_len, w_buf=w_buf,
                             n_cmp=n_cmp, n_slc=n_slc, n_sel=n_sel, nbl=nbl, page=page)
    return pl.pallas_call(
        body, grid_spec=grid_spec,
        out_shape=jax.ShapeDtypeStruct((DB * DL, NSA_HEADS * NSA_HD), F32),
        compiler_params=_cparams(("parallel", "arbitrary")),
        name="nsa_sample",
    )(page_table, P, P, P, P, CMP, CMP, cov, *([cache_win] * nb), *([cache_kv] * (nb * pg)))


def _out_ln_body(h_ref, ro_ref, no_ref, wr_ref, wn_ref, g_ref, b_ref, o_ref, obf_ref, *, alpha):
    tm = h_ref.shape[0]
    part = tm // 4 if tm % (4 * BF16_ROWS) == 0 else tm
    for r0 in range(0, tm, part):
        rows = slice(r0, r0 + part)
        m = _dot(ro_ref[rows, :].astype(BF16), wr_ref[...]) + _dot(no_ref[rows, :].astype(BF16), wn_ref[...])
        y = _layer_norm(alpha * h_ref[rows, :] + m, g_ref[...], b_ref[...])
        o_ref[rows, :] = y
        obf_ref[rows, :] = y.astype(BF16)


def _out_ln(h, ro, no, w_out, g, b, alpha):
    T, D = h.shape
    kr = ro.shape[1]
    kn = no.shape[1]
    tm = _pick_tile(T, (512, 256, 128, 64, 32, 16, 8))
    return pl.pallas_call(
        functools.partial(_out_ln_body, alpha=alpha),
        grid=(T // tm,),
        in_specs=[
            pl.BlockSpec((tm, D), lambda i: (i, 0)),
            pl.BlockSpec((tm, kr), lambda i: (i, 0)),
            pl.BlockSpec((tm, kn), lambda i: (i, 0)),
            pl.BlockSpec((kr, D), lambda i: (0, 0)),
            pl.BlockSpec((kn, D), lambda i: (1, 0)),
            pl.BlockSpec((1, D), lambda i: (0, 0)),
            pl.BlockSpec((1, D), lambda i: (0, 0)),
        ],
        out_specs=[pl.BlockSpec((tm, D), lambda i: (i, 0)), pl.BlockSpec((tm, D), lambda i: (i, 0))],
        out_shape=[jax.ShapeDtypeStruct((T, D), F32), jax.ShapeDtypeStruct((T, D), BF16)],
        compiler_params=_cparams(("parallel",)),
        name="out_ln",
    )(h, ro, no, w_out, w_out, g.reshape(1, D), b.reshape(1, D))


def _rope_tables(pos):
    half = NSA_HD // 2
    inv = ROPE_THETA ** (-jnp.arange(half, dtype=F32) / half)
    ang = pos.astype(F32)[:, None] * inv[None, :]
    cos = jnp.cos(ang)
    sin = jnp.sin(ang)
    return jnp.concatenate([cos, cos], -1), jnp.concatenate([-sin, sin], -1)


def _cmp_weights(w1, pos, w2):
    r = CMP_BLOCK // CMP_STRIDE
    w1r = w1.reshape(r, CMP_STRIDE * NSA_HD, CMP_HIDDEN)
    w1c = jnp.concatenate([w1r[i] for i in range(r)], axis=1).astype(BF16)
    posr = jnp.pad(pos.reshape(r, CMP_STRIDE * NSA_HD), ((0, SUBLANES - r), (0, 0))).astype(BF16)
    return w1c, posr, w2.astype(BF16)


def _layer_view(arr, l, shape):
    return arr.reshape(shape) if arr.shape[0] == 1 else arr[l].reshape(shape)


def _decoder_layer(x, B, L, q_off, s0, p, sample_ctx):
    alpha = p['alpha']
    h1, h1_bf = _ffn_ln(x, p['ffn1_w_up'], p['ffn1_w_down'], p['ln1_g'], p['ln1_b'], alpha, True)
    cos, sin = _rope_tables(q_off + jnp.arange(L, dtype=jnp.int32))
    P, kv_rows, win_rows = _proj(h1_bf, p['w_in'], p['rope_cols'], p['scale_cols'], cos, sin, L)
    ro, ret_s = _retention(P, s0, p['ret_gn_g'], p['ret_gn_b'], B, L)
    if sample_ctx is None:
        assert L % CMP_STRIDE == 0
        n_cmp = L // CMP_STRIDE - CMP_BLOCK // CMP_STRIDE + 1
        n_vec = 2 * NSA_KV_HEADS
        specs = [pl.BlockSpec((L, NSA_HD), (lambda v: (lambda b, g: (b, KV_CHUNK0 + v)))(v)) for v in range(n_vec)]
        CMP = _compress([P] * n_vec, specs, 1, L, 1, n_cmp, B, p['cmp_w1'], p['cmp_pos'], p['cmp_w2'])
        no = _nsa_prompt(P, CMP, B, L)
    else:
        cache_kv, cache_win, layer, page_table, past_len, page = sample_ctx
        n_pages = page_table.shape[1]
        lk = past_len + L
        assert (lk // CMP_STRIDE) * CMP_STRIDE <= past_len, "compression blocks must lie in the paged past"
        n_cmp = lk // CMP_STRIDE - CMP_BLOCK // CMP_STRIDE + 1
        n_in = _pick_tile(n_pages, (32, 16, 8, 4, 2, 1))
        n_grp = n_pages // n_in
        specs = [pl.BlockSpec((None, None, page, 2, NSA_KV_HEADS, NSA_HD),
                              (lambda i: (lambda b, g, pt: (layer, pt[b, (n_grp - 1 - g) * n_in + i], 0, 0, 0, 0)))(i))
                 for i in range(n_in)]
        CMP = _compress([cache_kv] * n_in, specs, n_in, page, n_grp, n_cmp, B,
                        p['cmp_w1'], p['cmp_pos'], p['cmp_w2'], page_table=page_table)
        no = _nsa_sample(P, CMP, cache_kv, cache_win, layer, page_table, B, L, past_len, page)
    x2, x2_bf = _out_ln(h1, ro, no, p['w_out'], p['ln2_g'], p['ln2_b'], alpha)
    y, _ = _ffn_ln(x2, p['ffn2_w_up'], p['ffn2_w_down'], p['ln3_g'], p['ln3_b'], alpha, False, x_bf=x2_bf)
    return y, ret_s, kv_rows, win_rows


def kernel(x_prompt, x_sample, state_ret, cache_nsa_kv, cache_win, page_table, ffn1_w_up, ffn1_w_down, ln1_g, ln1_b, w_in, w_out, ret_gn_g, ret_gn_b, cmp_pos_k, cmp_w1_k, cmp_w2_k, cmp_pos_v, cmp_w1_v, cmp_w2_v, ln2_g, ln2_b, ffn2_w_up, ffn2_w_down, ln3_g, ln3_b):
    B, L, D = x_prompt.shape
    DB, DL, _ = x_sample.shape
    depth = w_in.shape[0]
    n_pool, page = cache_nsa_kv.shape[1], cache_nsa_kv.shape[2]
    n_pages = page_table.shape[1]
    past_len = n_pages * page
    w_buf = cache_win.shape[2]
    alpha = (2.0 * depth) ** 0.25
    rope_np = np.zeros((N_IN_PAD // LANES, LANES), np.float32)
    rope_np[list(ROPE_CHUNKS)] = 1.0
    scale_np = np.ones((N_IN_PAD // LANES, LANES), np.float32)
    scale_np[list(KSCALE_CHUNKS)] = RET_DK ** -0.5
    rope_cols = jnp.asarray(rope_np.reshape(1, N_IN_PAD))
    scale_cols = jnp.asarray(scale_np.reshape(1, N_IN_PAD))

    yp = x_prompt.reshape(B * L, D)
    ys = x_sample.reshape(DB * DL, D)
    outs = [[] for _ in range(6)]
    for l in range(depth):
        k1, p1, k2 = _cmp_weights(cmp_w1_k[l], cmp_pos_k[l], cmp_w2_k[l])
        v1, q1, v2 = _cmp_weights(cmp_w1_v[l], cmp_pos_v[l], cmp_w2_v[l])
        p = {
            'alpha': alpha, 'rope_cols': rope_cols, 'scale_cols': scale_cols,
            'ffn1_w_up': ffn1_w_up[l].astype(BF16), 'ffn1_w_down': (0.5 * ffn1_w_down[l]).astype(BF16),
            'ln1_g': ln1_g[l], 'ln1_b': ln1_b[l],
            'w_in': jnp.pad(w_in[l], ((0, 0), (0, N_IN_PAD - N_IN))).astype(BF16),
            'w_out': w_out[l].astype(BF16),
            'ret_gn_g': ret_gn_g[l], 'ret_gn_b': ret_gn_b[l],
            'cmp_w1': jnp.stack([k1, v1]), 'cmp_pos': jnp.stack([p1, q1]), 'cmp_w2': jnp.stack([k2, v2]),
            'ln2_g': ln2_g[l], 'ln2_b': ln2_b[l],
            'ffn2_w_up': ffn2_w_up[l].astype(BF16), 'ffn2_w_down': (0.5 * ffn2_w_down[l]).astype(BF16),
            'ln3_g': ln3_g[l], 'ln3_b': ln3_b[l],
        }
        s0 = jnp.zeros((B, RET_HEADS, RET_DK, RET_DV), F32)
        yp, rs_p, kv_p, win_p = _decoder_layer(yp, B, L, 0, s0, p, None)
        ctx = (cache_nsa_kv, cache_win, l, page_table, past_len, page)
        ys, rs_s, kv_s, win_s = _decoder_layer(ys, DB, DL, past_len,
                                               _layer_view(state_ret, l, state_ret.shape[1:]), p, ctx)
        wl = min(WINDOW, L)
        outs[0].append(rs_p)
        outs[1].append(rs_s)
        outs[2].append(kv_p.reshape(B, L, 4, NSA_KV_HEADS, NSA_HD))
        outs[3].append(kv_s.reshape(DB, DL, 4, NSA_KV_HEADS, NSA_HD))
        outs[4].append(win_p.reshape(B, L, 2, NSA_KV_HEADS, NSA_HD)[:, L - wl:])
        win_s = win_s.reshape(DB, DL, 2, NSA_KV_HEADS, NSA_HD)
        outs[5].append(jnp.concatenate([cache_win[l], win_s], axis=1)[:, -w_buf:])
    return (yp.reshape(B, L, D), ys.reshape(DB, DL, D), jnp.stack(outs[0]), jnp.stack(outs[1]),
            jnp.stack(outs[2]), jnp.stack(outs[3]), jnp.stack(outs[4]), jnp.stack(outs[5]))
```
